```python
import jax, jax.numpy as jnp
from jax import lax
import numpy as np

D_MODEL = 1024
BATCH = 8
SEQ = 8192
DEPTH = 1

ATTN_WIDTH = D_MODEL // 2
HEAD_DIM = 64
N_HEADS = ATTN_WIDTH // HEAD_DIM
CONV_WIDTH_CH = D_MODEL - ATTN_WIDTH
CONV_GROUPS = CONV_WIDTH_CH // HEAD_DIM
CONV_KERNEL = 31
D_FF = 2816
Q_BLOCK = 128
N_SUBLAYERS = 3
MIX_IN = 3 * ATTN_WIDTH + 2 * CONV_WIDTH_CH
RMS_EPS = 1e-6
LN_EPS = 1e-5

kernel_name = "hybrid_stickbreak_conformer_macaron_block"


def rms_norm(x, g, eps=RMS_EPS):
    xf = x.astype(jnp.float32)
    y = xf * lax.rsqrt(jnp.mean(xf * xf, axis=-1, keepdims=True) + eps)
    return (y * g.astype(jnp.float32)).astype(x.dtype)


def layer_norm(x, g, b, eps=LN_EPS):
    xf = x.astype(jnp.float32)
    mu = jnp.mean(xf, axis=-1, keepdims=True)
    var = jnp.mean(jnp.square(xf - mu), axis=-1, keepdims=True)
    y = (xf - mu) * lax.rsqrt(var + eps)
    return (y * g.astype(jnp.float32) + b.astype(jnp.float32)).astype(x.dtype)


def modulate(h, shift, scale):
    return h * (1.0 + scale[:, None, :]) + shift[:, None, :]


def swiglu_ffn(h, w_in, w_out):
    gate, up = jnp.split(h @ w_in, 2, axis=-1)
    return (jax.nn.silu(gate) * up) @ w_out


def stick_breaking_attention(q, k, v):
    seq = q.shape[2]
    scale = HEAD_DIM ** -0.5
    qf = q.astype(jnp.float32) * scale
    kf = k.astype(jnp.float32)
    vf = v.astype(jnp.float32)
    outs = []
    for start in range(0, seq, Q_BLOCK):
        end = start + Q_BLOCK
        q_blk = qf[:, :, start:end]
        k_ctx = kf[:, :, :end]
        v_ctx = vf[:, :, :end]
        z = jnp.einsum('bhqd,bhkd->bhqk', q_blk, k_ctx)
        q_pos = jnp.arange(start, end)[:, None]
        k_pos = jnp.arange(end)[None, :]
        strict = k_pos < q_pos
        log_one_minus = jnp.where(strict, jax.nn.log_sigmoid(-z), 0.0)
        after = lax.cumsum(log_one_minus, axis=3, reverse=True) - log_one_minus
        log_w = jax.nn.log_sigmoid(z) + after
        w = jnp.where(strict, jnp.exp(log_w), 0.0)
        outs.append(jnp.einsum('bhqk,bhkd->bhqd', w, v_ctx))
    return jnp.concatenate(outs, axis=2).astype(q.dtype)


def causal_depthwise_conv(u, w, b):
    y = lax.conv_general_dilated(
        u, w[:, None, :].astype(u.dtype), window_strides=(1,),
        padding=[(CONV_KERNEL - 1, 0)],
        dimension_numbers=('NWC', 'WIO', 'NWC'),
        feature_group_count=u.shape[-1])
    return y + b


def hybrid_mixer(h, w_in_mix, g_attn_out, conv_w, conv_b, conv_ln_g, conv_ln_b, w_out_mix):
    bsz, seq, _ = h.shape
    proj = h @ w_in_mix
    q, k, v, cv, cg = jnp.split(
        proj, [ATTN_WIDTH, 2 * ATTN_WIDTH, 3 * ATTN_WIDTH, 3 * ATTN_WIDTH + CONV_WIDTH_CH], axis=-1)

    def heads(t):
        return t.reshape(bsz, seq, N_HEADS, HEAD_DIM).transpose(0, 2, 1, 3)

    a = stick_breaking_attention(heads(q), heads(k), heads(v))
    a = rms_norm(a, g_attn_out[:, None, :])
    a = a.transpose(0, 2, 1, 3).reshape(bsz, seq, ATTN_WIDTH)

    u = cv * jax.nn.sigmoid(cg)
    u = causal_depthwise_conv(u, conv_w, conv_b)
    u = jax.nn.silu(layer_norm(u, conv_ln_g, conv_ln_b))

    return jnp.concatenate([a, u], axis=-1) @ w_out_mix


def sandwich_sublayer(x, g_pre, g_post, shift, scale, gate, res_w, fn):
    h = modulate(rms_norm(x, g_pre), shift, scale)
    y = rms_norm(fn(h), g_post)
    return x + res_w * (1.0 + gate[:, None, :]) * y


def _fwd_setup_inputs(seed: int = 0) -> dict:
    key = jax.random.key(seed)
    ks = jax.random.split(key, 24)
    f32 = jnp.float32

    def nrm(k, shape, s):
        return jax.random.normal(k, shape, f32) * s

    def gain(k, n):
        return 1.0 + 0.02 * jax.random.normal(k, (n,), f32)

    return {
        "x": jax.random.normal(ks[0], (BATCH, SEQ, D_MODEL), f32),
        "c": jax.random.normal(ks[1], (BATCH, D_MODEL), f32),
        "w_ada": nrm(ks[2], (D_MODEL, 3 * N_SUBLAYERS * D_MODEL), 0.1 * D_MODEL ** -0.5),
        "b_ada": nrm(ks[3], (3 * N_SUBLAYERS * D_MODEL,), 0.02),
        "g_pre_ff1": gain(ks[4], D_MODEL),
        "g_post_ff1": gain(ks[5], D_MODEL),
        "ff1_w_in": nrm(ks[6], (D_MODEL, 2 * D_FF), D_MODEL ** -0.5),
        "ff1_w_out": nrm(ks[7], (D_FF, D_MODEL), D_FF ** -0.5),
        "g_pre_mix": gain(ks[8], D_MODEL),
        "g_post_mix": gain(ks[9], D_MODEL),
        "w_in_mix": nrm(ks[10], (D_MODEL, MIX_IN), D_MODEL ** -0.5),
        "g_attn_out": 1.0 + 0.02 * jax.random.normal(ks[11], (N_HEADS, HEAD_DIM), f32),
        "conv_w": nrm(ks[12], (CONV_KERNEL, CONV_WIDTH_CH), CONV_KERNEL ** -0.5),
        "conv_b": nrm(ks[13], (CONV_WIDTH_CH,), 0.02),
        "conv_ln_g": gain(ks[14], CONV_WIDTH_CH),
        "conv_ln_b": nrm(ks[15], (CONV_WIDTH_CH,), 0.02),
        "w_out_mix": nrm(ks[16], (D_MODEL, D_MODEL), D_MODEL ** -0.5),
        "g_pre_ff2": gain(ks[17], D_MODEL),
        "g_post_ff2": gain(ks[18], D_MODEL),
        "ff2_w_in": nrm(ks[19], (D_MODEL, 2 * D_FF), D_MODEL ** -0.5),
        "ff2_w_out": nrm(ks[20], (D_FF, D_MODEL), D_FF ** -0.5),
    }


def _fwd_reference(x, c, w_ada, b_ada, g_pre_ff1, g_post_ff1, ff1_w_in, ff1_w_out,
              g_pre_mix, g_post_mix, w_in_mix, g_attn_out, conv_w, conv_b,
              conv_ln_g, conv_ln_b, w_out_mix, g_pre_ff2, g_post_ff2,
              ff2_w_in, ff2_w_out):
    mod = (jax.nn.silu(c) @ w_ada + b_ada).reshape(c.shape[0], N_SUBLAYERS, 3, D_MODEL)
    h = x
    for _layer in range(DEPTH):
        h = sandwich_sublayer(
            h, g_pre_ff1, g_post_ff1, mod[:, 0, 0], mod[:, 0, 1], mod[:, 0, 2], 0.5,
            lambda t: swiglu_ffn(t, ff1_w_in, ff1_w_out))
        h = sandwich_sublayer(
            h, g_pre_mix, g_post_mix, mod[:, 1, 0], mod[:, 1, 1], mod[:, 1, 2], 1.0,
            lambda t: hybrid_mixer(t, w_in_mix, g_attn_out, conv_w, conv_b,
                                   conv_ln_g, conv_ln_b, w_out_mix))
        h = sandwich_sublayer(
            h, g_pre_ff2, g_post_ff2, mod[:, 2, 0], mod[:, 2, 1], mod[:, 2, 2], 0.5,
            lambda t: swiglu_ffn(t, ff2_w_in, ff2_w_out))
    return h


import jax as _jax
import jax.numpy as _jnp

TWIN_FORMAT = 'train_step'
FWD_PARAMS = ['x', 'c', 'w_ada', 'b_ada', 'g_pre_ff1', 'g_post_ff1', 'ff1_w_in', 'ff1_w_out', 'g_pre_mix', 'g_post_mix', 'w_in_mix', 'g_attn_out', 'conv_w', 'conv_b', 'conv_ln_g', 'conv_ln_b', 'w_out_mix', 'g_pre_ff2', 'g_post_ff2', 'ff2_w_in', 'ff2_w_out']
TWIN_WEIGHTS = ['w_ada', 'b_ada', 'g_pre_ff1', 'g_post_ff1', 'ff1_w_in', 'ff1_w_out', 'g_pre_mix', 'g_post_mix', 'w_in_mix', 'g_attn_out', 'conv_w', 'conv_b', 'conv_ln_g', 'conv_ln_b', 'w_out_mix', 'g_pre_ff2', 'g_post_ff2', 'ff2_w_in', 'ff2_w_out']
TWIN_DIFF_INPUT = 'x'
TWIN_INPUTS = ['x', 'c', 'w_ada', 'b_ada', 'g_pre_ff1', 'g_post_ff1', 'ff1_w_in', 'ff1_w_out', 'g_pre_mix', 'g_post_mix', 'w_in_mix', 'g_attn_out', 'conv_w', 'conv_b', 'conv_ln_g', 'conv_ln_b', 'w_out_mix', 'g_pre_ff2', 'g_post_ff2', 'ff2_w_in', 'ff2_w_out', 'loss_target', 'm_w_ada', 'm_b_ada', 'm_g_pre_ff1', 'm_g_post_ff1', 'm_ff1_w_in', 'm_ff1_w_out', 'm_g_pre_mix', 'm_g_post_mix', 'm_w_in_mix', 'm_g_attn_out', 'm_conv_w', 'm_conv_b', 'm_conv_ln_g', 'm_conv_ln_b', 'm_w_out_mix', 'm_g_pre_ff2', 'm_g_post_ff2', 'm_ff2_w_in', 'm_ff2_w_out', 'v_w_ada', 'v_b_ada', 'v_g_pre_ff1', 'v_g_post_ff1', 'v_ff1_w_in', 'v_ff1_w_out', 'v_g_pre_mix', 'v_g_post_mix', 'v_w_in_mix', 'v_g_attn_out', 'v_conv_w', 'v_conv_b', 'v_conv_ln_g', 'v_conv_ln_b', 'v_w_out_mix', 'v_g_pre_ff2', 'v_g_post_ff2', 'v_ff2_w_in', 'v_ff2_w_out']
TWIN_OUTPUTS = ['loss', 'grad_x', 'grad_w_ada', 'grad_b_ada', 'grad_g_pre_ff1', 'grad_g_post_ff1', 'grad_ff1_w_in', 'grad_ff1_w_out', 'grad_g_pre_mix', 'grad_g_post_mix', 'grad_w_in_mix', 'grad_g_attn_out', 'grad_conv_w', 'grad_conv_b', 'grad_conv_ln_g', 'grad_conv_ln_b', 'grad_w_out_mix', 'grad_g_pre_ff2', 'grad_g_post_ff2', 'grad_ff2_w_in', 'grad_ff2_w_out', 'delta_w_ada', 'delta_b_ada', 'delta_g_pre_ff1', 'delta_g_post_ff1', 'delta_ff1_w_in', 'delta_ff1_w_out', 'delta_g_pre_mix', 'delta_g_post_mix', 'delta_w_in_mix', 'delta_g_attn_out', 'delta_conv_w', 'delta_conv_b', 'delta_conv_ln_g', 'delta_conv_ln_b', 'delta_w_out_mix', 'delta_g_pre_ff2', 'delta_g_post_ff2', 'delta_ff2_w_in', 'delta_ff2_w_out', 'new_m_w_ada', 'new_m_b_ada', 'new_m_g_pre_ff1', 'new_m_g_post_ff1', 'new_m_ff1_w_in', 'new_m_ff1_w_out', 'new_m_g_pre_mix', 'new_m_g_post_mix', 'new_m_w_in_mix', 'new_m_g_attn_out', 'new_m_conv_w', 'new_m_conv_b', 'new_m_conv_ln_g', 'new_m_conv_ln_b', 'new_m_w_out_mix', 'new_m_g_pre_ff2', 'new_m_g_post_ff2', 'new_m_ff2_w_in', 'new_m_ff2_w_out', 'new_v_w_ada', 'new_v_b_ada', 'new_v_g_pre_ff1', 'new_v_g_post_ff1', 'new_v_ff1_w_in', 'new_v_ff1_w_out', 'new_v_g_pre_mix', 'new_v_g_post_mix', 'new_v_w_in_mix', 'new_v_g_attn_out', 'new_v_conv_w', 'new_v_conv_b', 'new_v_conv_ln_g', 'new_v_conv_ln_b', 'new_v_w_out_mix', 'new_v_g_pre_ff2', 'new_v_g_post_ff2', 'new_v_ff2_w_in', 'new_v_ff2_w_out']
TWIN_LEAF_KINDS = {'loss': 'loss', 'grad_x': 'grad_x', 'grad_w_ada': 'grad_w', 'grad_b_ada': 'grad_w', 'grad_g_pre_ff1': 'grad_w', 'grad_g_post_ff1': 'grad_w', 'grad_ff1_w_in': 'grad_w', 'grad_ff1_w_out': 'grad_w', 'grad_g_pre_mix': 'grad_w', 'grad_g_post_mix': 'grad_w', 'grad_w_in_mix': 'grad_w', 'grad_g_attn_out': 'grad_w', 'grad_conv_w': 'grad_w', 'grad_conv_b': 'grad_w', 'grad_conv_ln_g': 'grad_w', 'grad_conv_ln_b': 'grad_w', 'grad_w_out_mix': 'grad_w', 'grad_g_pre_ff2': 'grad_w', 'grad_g_post_ff2': 'grad_w', 'grad_ff2_w_in': 'grad_w', 'grad_ff2_w_out': 'grad_w', 'delta_w_ada': 'delta_w', 'delta_b_ada': 'delta_w', 'delta_g_pre_ff1': 'delta_w', 'delta_g_post_ff1': 'delta_w', 'delta_ff1_w_in': 'delta_w', 'delta_ff1_w_out': 'delta_w', 'delta_g_pre_mix': 'delta_w', 'delta_g_post_mix': 'delta_w', 'delta_w_in_mix': 'delta_w', 'delta_g_attn_out': 'delta_w', 'delta_conv_w': 'delta_w', 'delta_conv_b': 'delta_w', 'delta_conv_ln_g': 'delta_w', 'delta_conv_ln_b': 'delta_w', 'delta_w_out_mix': 'delta_w', 'delta_g_pre_ff2': 'delta_w', 'delta_g_post_ff2': 'delta_w', 'delta_ff2_w_in': 'delta_w', 'delta_ff2_w_out': 'delta_w', 'new_m_w_ada': 'new_m', 'new_m_b_ada': 'new_m', 'new_m_g_pre_ff1': 'new_m', 'new_m_g_post_ff1': 'new_m', 'new_m_ff1_w_in': 'new_m', 'new_m_ff1_w_out': 'new_m', 'new_m_g_pre_mix': 'new_m', 'new_m_g_post_mix': 'new_m', 'new_m_w_in_mix': 'new_m', 'new_m_g_attn_out': 'new_m', 'new_m_conv_w': 'new_m', 'new_m_conv_b': 'new_m', 'new_m_conv_ln_g': 'new_m', 'new_m_conv_ln_b': 'new_m', 'new_m_w_out_mix': 'new_m', 'new_m_g_pre_ff2': 'new_m', 'new_m_g_post_ff2': 'new_m', 'new_m_ff2_w_in': 'new_m', 'new_m_ff2_w_out': 'new_m', 'new_v_w_ada': 'new_v', 'new_v_b_ada': 'new_v', 'new_v_g_pre_ff1': 'new_v', 'new_v_g_post_ff1': 'new_v', 'new_v_ff1_w_in': 'new_v', 'new_v_ff1_w_out': 'new_v', 'new_v_g_pre_mix': 'new_v', 'new_v_g_post_mix': 'new_v', 'new_v_w_in_mix': 'new_v', 'new_v_g_attn_out': 'new_v', 'new_v_conv_w': 'new_v', 'new_v_conv_b': 'new_v', 'new_v_conv_ln_g': 'new_v', 'new_v_conv_ln_b': 'new_v', 'new_v_w_out_mix': 'new_v', 'new_v_g_pre_ff2': 'new_v', 'new_v_g_post_ff2': 'new_v', 'new_v_ff2_w_in': 'new_v', 'new_v_ff2_w_out': 'new_v'}


def _forward(args):
    return _fwd_reference(*[args[k] for k in FWD_PARAMS])


def _output_shape():
    def fwd():
        inp = _fwd_setup_inputs(0)
        return _fwd_reference(*[inp[k] for k in FWD_PARAMS])
    out = _jax.eval_shape(fwd)
    return out.shape, out.dtype

N_MICROBATCH = 1
ADAM_LR = 0.001
ADAM_B1 = 0.9
ADAM_B2 = 0.999
ADAM_EPS = 1e-08
ADAM_WD = 0.01
ADAM_STEP = 10
PER_EXAMPLE_BATCH_AXIS = {'x': 0, 'c': 0, 'loss_target': 0}
SHARED_INPUTS = []
_WEIGHT_DTYPES = {'w_ada': _jnp.float32, 'b_ada': _jnp.float32, 'g_pre_ff1': _jnp.float32, 'g_post_ff1': _jnp.float32, 'ff1_w_in': _jnp.float32, 'ff1_w_out': _jnp.float32, 'g_pre_mix': _jnp.float32, 'g_post_mix': _jnp.float32, 'w_in_mix': _jnp.float32, 'g_attn_out': _jnp.float32, 'conv_w': _jnp.float32, 'conv_b': _jnp.float32, 'conv_ln_g': _jnp.float32, 'conv_ln_b': _jnp.float32, 'w_out_mix': _jnp.float32, 'g_pre_ff2': _jnp.float32, 'g_post_ff2': _jnp.float32, 'ff2_w_in': _jnp.float32, 'ff2_w_out': _jnp.float32}
MOMENT_SCALE = {'w_ada': 6.428908e+00, 'b_ada': 2.272118e+01, 'g_pre_ff1': 5.711288e-01, 'g_post_ff1': 1.600078e+01, 'ff1_w_in': 2.350379e-01, 'ff1_w_out': 4.110108e-01, 'g_pre_mix': 5.525569e-01, 'g_post_mix': 6.462789e+01, 'w_in_mix': 3.620415e-01, 'g_attn_out': 7.117275e-01, 'conv_w': 3.917668e-01, 'conv_b': 2.415538e+00, 'conv_ln_g': 1.133753e+00, 'conv_ln_b': 1.558198e+00, 'w_out_mix': 6.392565e-01, 'g_pre_ff2': 6.055845e-01, 'g_post_ff2': 1.609000e+01, 'ff2_w_in': 2.526139e-01, 'ff2_w_out': 5.312729e-01}


def _to_microbatches(a, axis):
    t = _jnp.moveaxis(a, axis, 0)
    t = t.reshape((N_MICROBATCH, t.shape[0] // N_MICROBATCH) + t.shape[1:])
    return _jnp.moveaxis(t, 1, axis + 1)


def setup_inputs(seed: int = 0) -> dict:
    inp = _fwd_setup_inputs(seed)
    key = _jax.random.fold_in(_jax.random.key(seed), 7919)
    shape, _ = _output_shape()
    out = dict(inp)
    out["loss_target"] = _jax.random.normal(_jax.random.fold_in(key, 0), shape, _jnp.float32)
    for i, name in enumerate(TWIN_WEIGHTS):
        w = inp[name].astype(_jnp.float32)
        if MOMENT_SCALE is None:
            s = _jnp.sqrt(_jnp.mean(_jnp.square(w)) + 1e-30)
        else:
            s = MOMENT_SCALE[name]
        km, kv = _jax.random.split(_jax.random.fold_in(key, i + 1))
        out[name] = w
        out["m_" + name] = s * _jax.random.normal(km, w.shape, _jnp.float32)
        out["v_" + name] = (s * s) * _jax.random.uniform(kv, w.shape, _jnp.float32, 0.5, 1.5)
    if N_MICROBATCH > 1:
        for name, axis in PER_EXAMPLE_BATCH_AXIS.items():
            out[name] = _to_microbatches(out[name], axis)
    return {'x': out['x'], 'c': out['c'], 'w_ada': out['w_ada'], 'b_ada': out['b_ada'], 'g_pre_ff1': out['g_pre_ff1'], 'g_post_ff1': out['g_post_ff1'], 'ff1_w_in': out['ff1_w_in'], 'ff1_w_out': out['ff1_w_out'], 'g_pre_mix': out['g_pre_mix'], 'g_post_mix': out['g_post_mix'], 'w_in_mix': out['w_in_mix'], 'g_attn_out': out['g_attn_out'], 'conv_w': out['conv_w'], 'conv_b': out['conv_b'], 'conv_ln_g': out['conv_ln_g'], 'conv_ln_b': out['conv_ln_b'], 'w_out_mix': out['w_out_mix'], 'g_pre_ff2': out['g_pre_ff2'], 'g_post_ff2': out['g_post_ff2'], 'ff2_w_in': out['ff2_w_in'], 'ff2_w_out': out['ff2_w_out'], 'loss_target': out['loss_target'], 'm_w_ada': out['m_w_ada'], 'm_b_ada': out['m_b_ada'], 'm_g_pre_ff1': out['m_g_pre_ff1'], 'm_g_post_ff1': out['m_g_post_ff1'], 'm_ff1_w_in': out['m_ff1_w_in'], 'm_ff1_w_out': out['m_ff1_w_out'], 'm_g_pre_mix': out['m_g_pre_mix'], 'm_g_post_mix': out['m_g_post_mix'], 'm_w_in_mix': out['m_w_in_mix'], 'm_g_attn_out': out['m_g_attn_out'], 'm_conv_w': out['m_conv_w'], 'm_conv_b': out['m_conv_b'], 'm_conv_ln_g': out['m_conv_ln_g'], 'm_conv_ln_b': out['m_conv_ln_b'], 'm_w_out_mix': out['m_w_out_mix'], 'm_g_pre_ff2': out['m_g_pre_ff2'], 'm_g_post_ff2': out['m_g_post_ff2'], 'm_ff2_w_in': out['m_ff2_w_in'], 'm_ff2_w_out': out['m_ff2_w_out'], 'v_w_ada': out['v_w_ada'], 'v_b_ada': out['v_b_ada'], 'v_g_pre_ff1': out['v_g_pre_ff1'], 'v_g_post_ff1': out['v_g_post_ff1'], 'v_ff1_w_in': out['v_ff1_w_in'], 'v_ff1_w_out': out['v_ff1_w_out'], 'v_g_pre_mix': out['v_g_pre_mix'], 'v_g_post_mix': out['v_g_post_mix'], 'v_w_in_mix': out['v_w_in_mix'], 'v_g_attn_out': out['v_g_attn_out'], 'v_conv_w': out['v_conv_w'], 'v_conv_b': out['v_conv_b'], 'v_conv_ln_g': out['v_conv_ln_g'], 'v_conv_ln_b': out['v_conv_ln_b'], 'v_w_out_mix': out['v_w_out_mix'], 'v_g_pre_ff2': out['v_g_pre_ff2'], 'v_g_post_ff2': out['v_g_post_ff2'], 'v_ff2_w_in': out['v_ff2_w_in'], 'v_ff2_w_out': out['v_ff2_w_out']}


def _loss(weights, diff, rest, loss_target):
    with _jax.named_scope("forward"):
        args = {**rest, TWIN_DIFF_INPUT: diff, **{k: w.astype(_WEIGHT_DTYPES[k]) for k, w in weights.items()}}
        y = _forward(args)
    with _jax.named_scope("loss_head"):
        err = _jnp.square(y.astype(_jnp.float32) - loss_target)
        return 0.5 * _jnp.sum(_jnp.mean(err, axis=-1)) if err.ndim else 0.5 * err


def _adamw(w, g, m, v):
    m = ADAM_B1 * m + (1.0 - ADAM_B1) * g
    v = ADAM_B2 * v + (1.0 - ADAM_B2) * _jnp.square(g)
    m_hat = m / (1.0 - ADAM_B1 ** ADAM_STEP)
    v_hat = v / (1.0 - ADAM_B2 ** ADAM_STEP)
    delta = -ADAM_LR * (m_hat / (_jnp.sqrt(v_hat) + ADAM_EPS) + ADAM_WD * w)
    return delta, m, v


def reference(x, c, w_ada, b_ada, g_pre_ff1, g_post_ff1, ff1_w_in, ff1_w_out, g_pre_mix, g_post_mix, w_in_mix, g_attn_out, conv_w, conv_b, conv_ln_g, conv_ln_b, w_out_mix, g_pre_ff2, g_post_ff2, ff2_w_in, ff2_w_out, loss_target, m_w_ada, m_b_ada, m_g_pre_ff1, m_g_post_ff1, m_ff1_w_in, m_ff1_w_out, m_g_pre_mix, m_g_post_mix, m_w_in_mix, m_g_attn_out, m_conv_w, m_conv_b, m_conv_ln_g, m_conv_ln_b, m_w_out_mix, m_g_pre_ff2, m_g_post_ff2, m_ff2_w_in, m_ff2_w_out, v_w_ada, v_b_ada, v_g_pre_ff1, v_g_post_ff1, v_ff1_w_in, v_ff1_w_out, v_g_pre_mix, v_g_post_mix, v_w_in_mix, v_g_attn_out, v_conv_w, v_conv_b, v_conv_ln_g, v_conv_ln_b, v_w_out_mix, v_g_pre_ff2, v_g_post_ff2, v_ff2_w_in, v_ff2_w_out):
    given = dict(x=x, c=c, w_ada=w_ada, b_ada=b_ada, g_pre_ff1=g_pre_ff1, g_post_ff1=g_post_ff1, ff1_w_in=ff1_w_in, ff1_w_out=ff1_w_out, g_pre_mix=g_pre_mix, g_post_mix=g_post_mix, w_in_mix=w_in_mix, g_attn_out=g_attn_out, conv_w=conv_w, conv_b=conv_b, conv_ln_g=conv_ln_g, conv_ln_b=conv_ln_b, w_out_mix=w_out_mix, g_pre_ff2=g_pre_ff2, g_post_ff2=g_post_ff2, ff2_w_in=ff2_w_in, ff2_w_out=ff2_w_out, loss_target=loss_target, m_w_ada=m_w_ada, m_b_ada=m_b_ada, m_g_pre_ff1=m_g_pre_ff1, m_g_post_ff1=m_g_post_ff1, m_ff1_w_in=m_ff1_w_in, m_ff1_w_out=m_ff1_w_out, m_g_pre_mix=m_g_pre_mix, m_g_post_mix=m_g_post_mix, m_w_in_mix=m_w_in_mix, m_g_attn_out=m_g_attn_out, m_conv_w=m_conv_w, m_conv_b=m_conv_b, m_conv_ln_g=m_conv_ln_g, m_conv_ln_b=m_conv_ln_b, m_w_out_mix=m_w_out_mix, m_g_pre_ff2=m_g_pre_ff2, m_g_post_ff2=m_g_post_ff2, m_ff2_w_in=m_ff2_w_in, m_ff2_w_out=m_ff2_w_out, v_w_ada=v_w_ada, v_b_ada=v_b_ada, v_g_pre_ff1=v_g_pre_ff1, v_g_post_ff1=v_g_post_ff1, v_ff1_w_in=v_ff1_w_in, v_ff1_w_out=v_ff1_w_out, v_g_pre_mix=v_g_pre_mix, v_g_post_mix=v_g_post_mix, v_w_in_mix=v_w_in_mix, v_g_attn_out=v_g_attn_out, v_conv_w=v_conv_w, v_conv_b=v_conv_b, v_conv_ln_g=v_conv_ln_g, v_conv_ln_b=v_conv_ln_b, v_w_out_mix=v_w_out_mix, v_g_pre_ff2=v_g_pre_ff2, v_g_post_ff2=v_g_post_ff2, v_ff2_w_in=v_ff2_w_in, v_ff2_w_out=v_ff2_w_out)
    weights = {n: given[n] for n in TWIN_WEIGHTS}
    shared = {n: given[n] for n in SHARED_INPUTS}
    per_example = {n: given[n] for n in ['x', 'c']}
    grad_fn = _jax.value_and_grad(_loss, argnums=(0, 1))

    def one_microbatch(ex, loss_target):
        ex = dict(ex)
        diff = ex.pop(TWIN_DIFF_INPUT)
        return grad_fn(weights, diff, {**shared, **ex}, loss_target)

    if N_MICROBATCH == 1:
        loss, (grad_w, grad_x) = one_microbatch(per_example, given["loss_target"])
    else:
        def body(carry, xs):
            loss_sum, grad_sum = carry
            l_k, (gw_k, gx_k) = one_microbatch(xs[0], xs[1])
            with _jax.named_scope("update"):
                return (loss_sum + l_k, _jax.tree.map(_jnp.add, grad_sum, gw_k)), gx_k

        init = (_jnp.zeros((), _jnp.float32), _jax.tree.map(_jnp.zeros_like, weights))
        (loss, grad_w), grad_x = _jax.lax.scan(body, init, (per_example, given["loss_target"]))
    with _jax.named_scope("update"):
        delta_w, new_m, new_v = {}, {}, {}
        for n in TWIN_WEIGHTS:
            delta_w[n], new_m[n], new_v[n] = _adamw(weights[n], grad_w[n], given["m_" + n], given["v_" + n])
    return (loss, grad_x, *[grad_w[n] for n in TWIN_WEIGHTS], *[delta_w[n] for n in TWIN_WEIGHTS],
            *[new_m[n] for n in TWIN_WEIGHTS], *[new_v[n] for n in TWIN_WEIGHTS])
```

```python
import functools

import jax
import jax.numpy as jnp
from jax import lax
from jax.experimental import pallas as pl
from jax.experimental.pallas import tpu as pltpu

F32 = jnp.float32
BF16 = jnp.bfloat16
D = 1024
DFF = 2816
AW = 512
HD = 64
NH = 8
CW = 512
CK = 31
HALO = 32
MIXIN = 2560
NDEV = 8
NMOD = 9 * D
RMS_EPS = 1e-6
LN_EPS = 1e-5
QK_SCALE = HD ** -0.5
W_ZERO_BELOW = -104.0
ADAM_LR, ADAM_B1, ADAM_B2, ADAM_EPS, ADAM_WD, ADAM_STEP = 0.001, 0.9, 0.999, 1e-08, 0.01, 10
MIB = 1024 * 1024
MESH = pl.DeviceIdType.MESH

PACK_ROWS = (704, 352, 320, 128, 704, 352)
PACK_R = sum(PACK_ROWS)
SMALL_N = NMOD + 6 * D + 4 * 512 + CK * CW
SMALL_R = 40


def _pcall(body, name, **kw):
    return pl.pallas_call(body, name=name, **kw)


def _cp(sem=None, vmem_mib=48):
    if sem is None:
        return pltpu.CompilerParams(vmem_limit_bytes=vmem_mib * MIB)
    return pltpu.CompilerParams(dimension_semantics=sem, vmem_limit_bytes=vmem_mib * MIB)


def _dot(a, b):
    return jnp.dot(a, b, preferred_element_type=F32)


def _dot_nt(a, b):
    return lax.dot_general(a, b, (((1,), (1,)), ((), ())), preferred_element_type=F32)


def _dot_tn(a, b):
    return lax.dot_general(a, b, (((0,), (0,)), ((), ())), preferred_element_type=F32)


def _sigmoid(x):
    return 1.0 / (1.0 + jnp.exp(-x))


def _split2(x):
    hi = x.astype(BF16)
    mid = (x - hi.astype(F32)).astype(BF16)
    return hi, mid


def _all_gather(x, name, in_vmem):
    R, C = x.shape

    def body(x_ref, out_ref, send_sems, recv_sems, local_sem):
        mx, my, mc = lax.axis_index("x"), lax.axis_index("y"), lax.axis_index("c")
        me, sibling = (mx, my, mc), (mx, my, 1 - mc)
        chips = [(1 - mx, my), (mx, 1 - my), (1 - mx, 1 - my)]

        def slab(px, py, pc):
            return out_ref.at[4 * px + 2 * py + pc]

        def copy(k, block, to, src=None):
            return pltpu.make_async_remote_copy(
                src_ref=slab(*block) if src is None else src, dst_ref=slab(*block),
                send_sem=send_sems.at[k], recv_sem=recv_sems.at[k], device_id=to, device_id_type=MESH)

        mine = pltpu.make_async_copy(x_ref, slab(*me), local_sem)
        mine.start()
        first = [copy(0, me, sibling, src=x_ref)]
        first += [copy(1 + j, me, (*chip, mc), src=x_ref) for j, chip in enumerate(chips)]
        for cp in first:
            cp.start()
        passed = [copy(4 + j, (*chip, mc), sibling) for j, chip in enumerate(chips)]
        for j, chip in enumerate(chips):
            copy(1 + j, (*chip, mc), me).wait_recv()
            passed[j].start()
        copy(0, sibling, me).wait_recv()
        for j, chip in enumerate(chips):
            copy(4 + j, (*chip, 1 - mc), me).wait_recv()
        for cp in first + passed:
            cp.wait_send()
        mine.wait()

    space = pltpu.VMEM if in_vmem else pl.ANY
    return _pcall(
        body, name,
        out_shape=jax.ShapeDtypeStruct((NDEV, R, C), x.dtype),
        in_specs=[pl.BlockSpec(memory_space=space)],
        out_specs=pl.BlockSpec(memory_space=space),
        scratch_shapes=[pltpu.SemaphoreType.DMA((7,)), pltpu.SemaphoreType.DMA((7,)), pltpu.SemaphoreType.DMA],
    )(x)


def _all_to_all(send, name):
    _, R, C = send.shape

    def body(send_ref, recv_ref, send_sems, recv_sems, local_sem):
        mx, my, mc = lax.axis_index("x"), lax.axis_index("y"), lax.axis_index("c")
        me = 4 * mx + 2 * my + mc
        mine = pltpu.make_async_copy(send_ref.at[me], recv_ref.at[me], local_sem)
        mine.start()
        copies = []
        for r in range(1, NDEV):
            px = 1 - mx if r & 4 else mx
            py = 1 - my if r & 2 else my
            pc = 1 - mc if r & 1 else mc
            peer = 4 * px + 2 * py + pc
            copies.append(pltpu.make_async_remote_copy(
                src_ref=send_ref.at[peer], dst_ref=recv_ref.at[me],
                send_sem=send_sems.at[r - 1], recv_sem=recv_sems.at[r - 1],
                device_id=(px, py, pc), device_id_type=MESH))
        for cp in copies:
            cp.start()
        for cp in copies:
            cp.wait_recv()
        for cp in copies:
            cp.wait_send()
        mine.wait()

    return _pcall(
        body, name,
        out_shape=jax.ShapeDtypeStruct(send.shape, send.dtype),
        in_specs=[pl.BlockSpec(memory_space=pl.ANY)],
        out_specs=pl.BlockSpec(memory_space=pl.ANY),
        scratch_shapes=[pltpu.SemaphoreType.DMA((7,)), pltpu.SemaphoreType.DMA((7,)), pltpu.SemaphoreType.DMA],
    )(send)


def _ada_fwd(c_all, w, b):
    n = w.shape[1]

    def body(c_ref, w_ref, b_ref, o_ref):
        c = c_ref[...]
        s = c * _sigmoid(c)
        o_ref[...] = jnp.dot(s, w_ref[...], preferred_element_type=F32, precision=lax.Precision.HIGHEST) + b_ref[...]

    return _pcall(body, "ada_fwd", out_shape=jax.ShapeDtypeStruct((NDEV, n), F32), compiler_params=_cp())(c_all, w, b)


def _ada_bwd(c_all_t, dmod):
    n = dmod.shape[1]

    def body(ct_ref, d_ref, o_ref):
        ct = ct_ref[...]
        s = ct * _sigmoid(ct)
        acc = s[:, 0:1] * d_ref[0:1, :]
        for b in range(1, NDEV):
            acc = acc + s[:, b:b + 1] * d_ref[b:b + 1, :]
        o_ref[...] = acc

    return _pcall(body, "ada_bwd", out_shape=jax.ShapeDtypeStruct((D, n), F32), compiler_params=_cp())(c_all_t, dmod)


def _pre_fwd(x, g, shift, scale, name, tb=512):
    T = x.shape[0]

    def body(x_ref, g_ref, sh_ref, sc_ref, h_ref):
        xv = x_ref[...]
        r = lax.rsqrt(jnp.mean(xv * xv, axis=-1, keepdims=True) + RMS_EPS)
        h_ref[...] = ((xv * r) * g_ref[...] * (1.0 + sc_ref[...]) + sh_ref[...]).astype(BF16)

    row = pl.BlockSpec((tb, D), lambda i: (i, 0))
    vec = pl.BlockSpec((1, D), lambda i: (0, 0))
    return _pcall(body, name, grid=(T // tb,), in_specs=[row, vec, vec, vec], out_specs=row,
                  out_shape=jax.ShapeDtypeStruct((T, D), BF16), compiler_params=_cp(("parallel",)))(x, g, shift, scale)


def _ffn_in(h, wg, wu, name, tm=512, tn=1408):
    T = h.shape[0]

    def body(h_ref, wg_ref, wu_ref, g_ref, u_ref, a_ref):
        hv = h_ref[...]
        g = _dot(hv, wg_ref[...])
        u = _dot(hv, wu_ref[...])
        g_ref[...] = g
        u_ref[...] = u
        a_ref[...] = (g * _sigmoid(g) * u).astype(BF16)

    hs = pl.BlockSpec((tm, D), lambda j, i: (i, 0))
    ws = pl.BlockSpec((D, tn), lambda j, i: (0, j))
    os = pl.BlockSpec((tm, tn), lambda j, i: (i, j))
    return _pcall(body, name, grid=(DFF // tn, T // tm), in_specs=[hs, ws, ws], out_specs=[os, os, os],
                  out_shape=[jax.ShapeDtypeStruct((T, DFF), F32), jax.ShapeDtypeStruct((T, DFF), F32),
                             jax.ShapeDtypeStruct((T, DFF), BF16)],
                  compiler_params=_cp(("parallel", "parallel")))(h, wg, wu)


def _mm_nn(a, w, name, out_dtype=F32, tm=512):
    T, K = a.shape
    N = w.shape[1]

    def body(a_ref, w_ref, o_ref):
        o_ref[...] = _dot(a_ref[...], w_ref[...]).astype(out_dtype)

    return _pcall(body, name, grid=(T // tm,),
                  in_specs=[pl.BlockSpec((tm, K), lambda i: (i, 0)), pl.BlockSpec((K, N), lambda i: (0, 0))],
                  out_specs=pl.BlockSpec((tm, N), lambda i: (i, 0)),
                  out_shape=jax.ShapeDtypeStruct((T, N), out_dtype), compiler_params=_cp(("parallel",)))(a, w)


def _mm_post(a, w, x, g_post, gate, res_w, name, tm=512):
    T, K = a.shape

    def body(a_ref, w_ref, x_ref, g_ref, gt_ref, f_ref, o_ref):
        f = _dot(a_ref[...], w_ref[...])
        f_ref[...] = f
        r = lax.rsqrt(jnp.mean(f * f, axis=-1, keepdims=True) + RMS_EPS)
        y = (f * r) * g_ref[...]
        o_ref[...] = x_ref[...] + (res_w * (1.0 + gt_ref[...])) * y

    row = pl.BlockSpec((tm, D), lambda i: (i, 0))
    vec = pl.BlockSpec((1, D), lambda i: (0, 0))
    return _pcall(body, name, grid=(T // tm,),
                  in_specs=[pl.BlockSpec((tm, K), lambda i: (i, 0)), pl.BlockSpec((K, D), lambda i: (0, 0)), row, vec, vec],
                  out_specs=[row, row],
                  out_shape=[jax.ShapeDtypeStruct((T, D), F32), jax.ShapeDtypeStruct((T, D), F32)],
                  compiler_params=_cp(("parallel",)))(a, w, x, g_post, gate)


def _post_bwd(dout, f, g_post, gate, res_w, name, tb=512):
    T = f.shape[0]

    def body(do_ref, f_ref, g_ref, gt_ref, df_ref, dgate_ref, dg_ref):
        @pl.when(pl.program_id(0) == 0)
        def _():
            dgate_ref[...] = jnp.zeros_like(dgate_ref)
            dg_ref[...] = jnp.zeros_like(dg_ref)

        do = do_ref[...]
        f = f_ref[...]
        r = lax.rsqrt(jnp.mean(f * f, axis=-1, keepdims=True) + RMS_EPS)
        fn = f * r
        dgate_ref[...] += jnp.sum((res_w * do) * (fn * g_ref[...]), axis=0, keepdims=True)
        dy = (res_w * (1.0 + gt_ref[...])) * do
        dg_ref[...] += jnp.sum(dy * fn, axis=0, keepdims=True)
        dyg = dy * g_ref[...]
        df = r * (dyg - fn * jnp.mean(dyg * fn, axis=-1, keepdims=True))
        df_ref[...] = df.astype(BF16)

    row = pl.BlockSpec((tb, D), lambda i: (i, 0))
    vec = pl.BlockSpec((1, D), lambda i: (0, 0))
    return _pcall(body, name, grid=(T // tb,), in_specs=[row, row, vec, vec], out_specs=[row, vec, vec],
                  out_shape=[jax.ShapeDtypeStruct((T, D), BF16), jax.ShapeDtypeStruct((1, D), F32),
                             jax.ShapeDtypeStruct((1, D), F32)],
                  compiler_params=_cp(("arbitrary",)))(dout, f, g_post, gate)


def _ffn_out_bwd(df, w_out_t, g, u, name, tm=512, tn=1408):
    T = df.shape[0]

    def body(df_ref, w_ref, g_ref, u_ref, dg_ref, du_ref):
        da = _dot(df_ref[...], w_ref[...])
        gv = g_ref[...]
        s = _sigmoid(gv)
        gs = gv * s
        dg_ref[...] = (da * u_ref[...] * (s + gs * (1.0 - s))).astype(BF16)
        du_ref[...] = (da * gs).astype(BF16)

    ds = pl.BlockSpec((tm, D), lambda j, i: (i, 0))
    ws = pl.BlockSpec((D, tn), lambda j, i: (0, j))
    os = pl.BlockSpec((tm, tn), lambda j, i: (i, j))
    return _pcall(body, name, grid=(DFF // tn, T // tm), in_specs=[ds, ws, os, os], out_specs=[os, os],
                  out_shape=[jax.ShapeDtypeStruct((T, DFF), BF16), jax.ShapeDtypeStruct((T, DFF), BF16)],
                  compiler_params=_cp(("parallel", "parallel")))(df, w_out_t, g, u)


def _mm_tn(a, b, name, tn, tm=512):
    T, K = a.shape
    N = b.shape[1]

    def body(a_ref, b_ref, o_ref):
        @pl.when(pl.program_id(1) == 0)
        def _():
            o_ref[...] = jnp.zeros_like(o_ref)

        o_ref[...] += _dot_tn(a_ref[...], b_ref[...])

    return _pcall(body, name, grid=(N // tn, T // tm),
                  in_specs=[pl.BlockSpec((tm, K), lambda j, i: (i, 0)), pl.BlockSpec((tm, tn), lambda j, i: (i, j))],
                  out_specs=pl.BlockSpec((K, tn), lambda j, i: (0, j)),
                  out_shape=jax.ShapeDtypeStruct((K, N), F32),
                  compiler_params=_cp(("parallel", "arbitrary")))(a, b)


def _mm_nn_prebwd(ops, x, dout, g_pre, scale, name, tm=256):
    T = x.shape[0]
    n = len(ops)

    def body(*refs):
        a_refs = refs[0:2 * n:2]
        w_refs = refs[1:2 * n:2]
        x_ref, do_ref, g_ref, sc_ref, dx_ref, dsh_ref, dsc_ref, dg_ref = refs[2 * n:]

        @pl.when(pl.program_id(0) == 0)
        def _():
            dsh_ref[...] = jnp.zeros_like(dsh_ref)
            dsc_ref[...] = jnp.zeros_like(dsc_ref)
            dg_ref[...] = jnp.zeros_like(dg_ref)

        dh = _dot(a_refs[0][...], w_refs[0][...])
        for k in range(1, n):
            dh = dh + _dot(a_refs[k][...], w_refs[k][...])
        xv = x_ref[...]
        r = lax.rsqrt(jnp.mean(xv * xv, axis=-1, keepdims=True) + RMS_EPS)
        xn = xv * r
        dsh_ref[...] += jnp.sum(dh, axis=0, keepdims=True)
        dsc_ref[...] += jnp.sum(dh * (xn * g_ref[...]), axis=0, keepdims=True)
        dn = dh * (1.0 + sc_ref[...])
        dg_ref[...] += jnp.sum(dn * xn, axis=0, keepdims=True)
        dng = dn * g_ref[...]
        dx_ref[...] = do_ref[...] + r * (dng - xn * jnp.mean(dng * xn, axis=-1, keepdims=True))

    row = pl.BlockSpec((tm, D), lambda i: (i, 0))
    vec = pl.BlockSpec((1, D), lambda i: (0, 0))
    in_specs, args = [], []
    for a, w in ops:
        in_specs += [pl.BlockSpec((tm, a.shape[1]), lambda i: (i, 0)), pl.BlockSpec(w.shape, lambda i: (0, 0))]
        args += [a, w]
    return _pcall(body, name, grid=(T // tm,), in_specs=in_specs + [row, row, vec, vec],
                  out_specs=[row, vec, vec, vec],
                  out_shape=[jax.ShapeDtypeStruct((T, D), F32)] + [jax.ShapeDtypeStruct((1, D), F32)] * 3,
                  compiler_params=_cp(("arbitrary",), 56))(*args, x, dout, g_pre, scale)


def _loss_head(y, tgt, name, tb=512):
    T = y.shape[0]

    def body(y_ref, t_ref, dy_ref, l_ref):
        @pl.when(pl.program_id(0) == 0)
        def _():
            l_ref[...] = jnp.zeros_like(l_ref)

        e = y_ref[...] - t_ref[...]
        dy_ref[...] = e * (1.0 / D)
        l_ref[...] += 0.5 * jnp.sum(jnp.mean(e * e, axis=-1, keepdims=True), axis=0, keepdims=True)

    row = pl.BlockSpec((tb, D), lambda i: (i, 0))
    return _pcall(body, name, grid=(T // tb,), in_specs=[row, row],
                  out_specs=[row, pl.BlockSpec((1, 1), lambda i: (0, 0))],
                  out_shape=[jax.ShapeDtypeStruct((T, D), F32), jax.ShapeDtypeStruct((1, 1), F32)],
                  compiler_params=_cp(("arbitrary",)))(y, tgt)


def _softplus_parts(z):
    l = jnp.log(1.0 + jnp.exp(-jnp.abs(z)))
    return jnp.minimum(z, 0.0) - l, jnp.minimum(-z, 0.0) - l


def _attn_fwd(q, k, v, g_attn, tq):
    H, T, _ = q.shape

    def body(q_ref, k_ref, v_ref, g_ref, o_ref, an_ref):
        i = pl.program_id(1)
        qb = q_ref[0]
        rows = lax.broadcasted_iota(jnp.int32, (tq, tq), 0)
        cols = lax.broadcasted_iota(jnp.int32, (tq, tq), 1)
        tri = (rows > cols).astype(BF16)
        strict = cols < rows

        def tile(j, R, acc, masked):
            start = pl.multiple_of(j * tq, tq)
            kb = k_ref[0, pl.ds(start, tq), :]
            vb = v_ref[0, pl.ds(start, tq), :]
            z = _dot_nt(qb, kb) * QK_SCALE
            ls, lsm = _softplus_parts(z)
            if masked:
                lsm = jnp.where(strict, lsm, 0.0)
            hi, mid = _split2(lsm)
            after = _dot(hi, tri) + _dot(mid, tri)
            w = jnp.exp(ls + after + R)
            if masked:
                w = jnp.where(strict, w, 0.0)
            acc = acc + _dot(w.astype(BF16), vb)
            R = R + after[:, 0:1] + lsm[:, 0:1]
            return R, acc

        R, acc = tile(i, jnp.zeros((tq, 1), F32), jnp.zeros((tq, HD), F32), True)

        def more(c):
            return jnp.logical_and(c[0] < i, jnp.max(c[1]) > W_ZERO_BELOW)

        def step(c):
            R, acc = tile(i - 1 - c[0], c[1], c[2], False)
            return c[0] + 1, R, acc

        _, R, acc = lax.while_loop(more, step, (jnp.int32(0), R, acc))
        o_ref[0] = acc
        r = lax.rsqrt(jnp.mean(acc * acc, axis=-1, keepdims=True) + RMS_EPS)
        an_ref[0] = ((acc * r) * g_ref[0]).astype(BF16)

    qs = pl.BlockSpec((1, tq, HD), lambda h, i: (h, i, 0))
    ks = pl.BlockSpec((1, T, HD), lambda h, i: (h, 0, 0))
    gs = pl.BlockSpec((1, 1, HD), lambda h, i: (h, 0, 0))
    return _pcall(body, "attn_fwd", grid=(H, T // tq), in_specs=[qs, ks, ks, gs], out_specs=[qs, qs],
                  out_shape=[jax.ShapeDtypeStruct((H, T, HD), F32), jax.ShapeDtypeStruct((H, T, HD), BF16)],
                  compiler_params=_cp(("parallel", "parallel")))(q, k, v, g_attn)


def _attn_bwd(q, k, v, o, dan, g_attn, tq):
    H, T, _ = q.shape

    def body(q_ref, k_ref, v_ref, o_ref, dan_ref, g_ref, dq_ref, dk_ref, dv_ref, dg_ref):
        i = pl.program_id(1)

        @pl.when(i == 0)
        def _():
            dk_ref[...] = jnp.zeros_like(dk_ref)
            dv_ref[...] = jnp.zeros_like(dv_ref)
            dg_ref[...] = jnp.zeros_like(dg_ref)

        qb = q_ref[0]
        o = o_ref[0]
        dan = dan_ref[0]
        g = g_ref[0]
        r = lax.rsqrt(jnp.mean(o * o, axis=-1, keepdims=True) + RMS_EPS)
        on = o * r
        dg_ref[0] += jnp.sum(dan * on, axis=0, keepdims=True)
        dyg = dan * g
        dO = r * (dyg - on * jnp.mean(dyg * on, axis=-1, keepdims=True))
        dOb = dO.astype(BF16)
        prod = dOb.astype(F32) * o
        p1 = prod.astype(BF16)
        rem = prod - p1.astype(F32)
        p2 = rem.astype(BF16)
        p3 = (rem - p2.astype(F32)).astype(BF16)
        ones = jnp.ones((8, HD), BF16)
        Drow = (_dot_nt(ones, p1) + _dot_nt(ones, p2) + _dot_nt(ones, p3))[0:1, :]

        rows = lax.broadcasted_iota(jnp.int32, (tq, tq), 0)
        cols = lax.broadcasted_iota(jnp.int32, (tq, tq), 1)
        tri_after = (cols > rows).astype(BF16)
        tri_incl = (cols >= rows).astype(BF16)
        strict = rows < cols

        def tile(j, R, G, dq, masked):
            start = pl.multiple_of(j * tq, tq)
            kb = k_ref[0, pl.ds(start, tq), :]
            vb = v_ref[0, pl.ds(start, tq), :]
            z = _dot_nt(kb, qb) * QK_SCALE
            ls, lsm = _softplus_parts(z)
            if masked:
                lsm = jnp.where(strict, lsm, 0.0)
            hi, mid = _split2(lsm)
            after = _dot(tri_after, hi) + _dot(tri_after, mid)
            w = jnp.exp(ls + after + R)
            if masked:
                w = jnp.where(strict, w, 0.0)
            wb = w.astype(BF16)
            dlw = _dot_nt(vb, dOb) * wb.astype(F32)
            hi2, mid2 = _split2(dlw)
            C = _dot(tri_incl, hi2) + _dot(tri_incl, mid2)
            dlsm = Drow - G - C
            if masked:
                dlsm = jnp.where(strict, dlsm, 0.0)
            p = jnp.exp(ls)
            dz = ((dlw * (1.0 - p) - dlsm * p) * QK_SCALE).astype(BF16)
            dk_ref[0, pl.ds(start, tq), :] += _dot(dz, qb)
            dv_ref[0, pl.ds(start, tq), :] += _dot(wb, dOb)
            dq = dq + _dot_tn(dz, kb)
            R = R + after[0:1, :] + lsm[0:1, :]
            G = G + C[0:1, :]
            return R, G, dq

        zrow = jnp.zeros((1, tq), F32)
        R, G, dq = tile(i, zrow, zrow, jnp.zeros((tq, HD), F32), True)

        def more(c):
            return jnp.logical_and(c[0] < i, jnp.max(c[1]) > W_ZERO_BELOW)

        def step(c):
            R, G, dq = tile(i - 1 - c[0], c[1], c[2], c[3], False)
            return c[0] + 1, R, G, dq

        _, R, G, dq = lax.while_loop(more, step, (jnp.int32(0), R, G, dq))
        dq_ref[0] = dq

    qs = pl.BlockSpec((1, tq, HD), lambda h, i: (h, i, 0))
    ks = pl.BlockSpec((1, T, HD), lambda h, i: (h, 0, 0))
    gs = pl.BlockSpec((1, 1, HD), lambda h, i: (h, 0, 0))
    full = jax.ShapeDtypeStruct((H, T, HD), F32)
    return _pcall(body, "attn_bwd", grid=(H, T // tq), in_specs=[qs, ks, ks, qs, qs, gs],
                  out_specs=[qs, ks, ks, gs],
                  out_shape=[full, full, full, jax.ShapeDtypeStruct((H, 1, HD), F32)],
                  compiler_params=_cp(("parallel", "arbitrary")))(q, k, v, o, dan, g_attn)


def _conv_fwd(proj, conv_w, conv_b, ln_g, ln_b, tb=512):
    T = proj.shape[0]
    hb = tb // HALO

    def body(cv_ref, cg_ref, cvp_ref, cgp_ref, w_ref, b_ref, g_ref, be_ref, u0_ref, u1_ref, u3_ref, pad_ref):
        i = pl.program_id(0)
        u0 = cv_ref[...] * _sigmoid(cg_ref[...])
        prev = cvp_ref[...] * _sigmoid(cgp_ref[...])
        pad_ref[0:HALO, :] = jnp.where(i > 0, prev, 0.0)
        pad_ref[HALO:HALO + tb, :] = u0
        u0_ref[...] = u0
        acc = jnp.zeros((tb, CW), F32) + b_ref[...]
        for kk in range(CK):
            off = HALO - (CK - 1) + kk
            acc = acc + w_ref[kk:kk + 1, :] * pad_ref[off:off + tb, :]
        u1_ref[...] = acc
        mu = jnp.mean(acc, axis=-1, keepdims=True)
        xc = acc - mu
        var = jnp.mean(xc * xc, axis=-1, keepdims=True)
        u2 = (xc * lax.rsqrt(var + LN_EPS)) * g_ref[...] + be_ref[...]
        u3_ref[...] = (u2 * _sigmoid(u2)).astype(BF16)

    cur = lambda col: pl.BlockSpec((tb, CW), lambda i: (i, col))
    prv = lambda col: pl.BlockSpec((HALO, CW), lambda i: (jnp.maximum(i * hb - 1, 0), col))
    vec = pl.BlockSpec((1, CW), lambda i: (0, 0))
    out = pl.BlockSpec((tb, CW), lambda i: (i, 0))
    return _pcall(body, "conv_fwd", grid=(T // tb,),
                  in_specs=[cur(3), cur(4), prv(3), prv(4), pl.BlockSpec((HALO, CW), lambda i: (0, 0)), vec, vec, vec],
                  out_specs=[out, out, out],
                  out_shape=[jax.ShapeDtypeStruct((T, CW), F32), jax.ShapeDtypeStruct((T, CW), F32),
                             jax.ShapeDtypeStruct((T, CW), BF16)],
                  scratch_shapes=[pltpu.VMEM((tb + HALO, CW), F32)],
                  compiler_params=_cp(("parallel",)))(proj, proj, proj, proj, conv_w, conv_b, ln_g, ln_b)


def _conv_bwd1(dcat, u1, u0, ln_g, ln_b, tb=512):
    T = u1.shape[0]
    hb = tb // HALO

    def body(d3_ref, u1_ref, u0_ref, u0p_ref, g_ref, be_ref, du1_ref, dw_ref, db_ref, dlg_ref, dlb_ref, pad_ref):
        i = pl.program_id(0)

        @pl.when(i == 0)
        def _():
            dw_ref[...] = jnp.zeros_like(dw_ref)
            db_ref[...] = jnp.zeros_like(db_ref)
            dlg_ref[...] = jnp.zeros_like(dlg_ref)
            dlb_ref[...] = jnp.zeros_like(dlb_ref)

        u1 = u1_ref[...]
        mu = jnp.mean(u1, axis=-1, keepdims=True)
        xc = u1 - mu
        rstd = lax.rsqrt(jnp.mean(xc * xc, axis=-1, keepdims=True) + LN_EPS)
        xh = xc * rstd
        u2 = xh * g_ref[...] + be_ref[...]
        s = _sigmoid(u2)
        du2 = d3_ref[...] * (s + u2 * s * (1.0 - s))
        dlg_ref[...] += jnp.sum(du2 * xh, axis=0, keepdims=True)
        dlb_ref[...] += jnp.sum(du2, axis=0, keepdims=True)
        dxh = du2 * g_ref[...]
        du1 = rstd * (dxh - jnp.mean(dxh, axis=-1, keepdims=True) - xh * jnp.mean(dxh * xh, axis=-1, keepdims=True))
        du1_ref[...] = du1
        db_ref[...] += jnp.sum(du1, axis=0, keepdims=True)
        pad_ref[0:HALO, :] = jnp.where(i > 0, u0p_ref[...], 0.0)
        pad_ref[HALO:HALO + tb, :] = u0_ref[...]
        for kk in range(CK):
            off = HALO - (CK - 1) + kk
            dw_ref[kk:kk + 1, :] += jnp.sum(du1 * pad_ref[off:off + tb, :], axis=0, keepdims=True)

    cur = pl.BlockSpec((tb, CW), lambda i: (i, 0))
    vec = pl.BlockSpec((1, CW), lambda i: (0, 0))
    return _pcall(body, "conv_bwd1", grid=(T // tb,),
                  in_specs=[pl.BlockSpec((tb, CW), lambda i: (i, 1)), cur, cur,
                            pl.BlockSpec((HALO, CW), lambda i: (jnp.maximum(i * hb - 1, 0), 0)), vec, vec],
                  out_specs=[cur, pl.BlockSpec((HALO, CW), lambda i: (0, 0)), vec, vec, vec],
                  out_shape=[jax.ShapeDtypeStruct((T, CW), F32), jax.ShapeDtypeStruct((HALO, CW), F32)]
                  + [jax.ShapeDtypeStruct((1, CW), F32)] * 3,
                  scratch_shapes=[pltpu.VMEM((tb + HALO, CW), F32)],
                  compiler_params=_cp(("arbitrary",)))(dcat, u1, u0, u0, ln_g, ln_b)


def _conv_bwd2(du1, proj, conv_w, tb=512):
    T = du1.shape[0]
    hb = tb // HALO
    last = T // HALO - 1

    def body(d_ref, dn_ref, cv_ref, cg_ref, w_ref, dcv_ref, dcg_ref, pad_ref):
        i = pl.program_id(0)
        pad_ref[0:tb, :] = d_ref[...]
        pad_ref[tb:tb + HALO, :] = jnp.where(i < pl.num_programs(0) - 1, dn_ref[...], 0.0)
        acc = jnp.zeros((tb, CW), F32)
        for kk in range(CK):
            off = CK - 1 - kk
            acc = acc + w_ref[kk:kk + 1, :] * pad_ref[off:off + tb, :]
        sg = _sigmoid(cg_ref[...])
        dcv_ref[...] = (acc * sg).astype(BF16)
        dcg_ref[...] = (acc * cv_ref[...] * sg * (1.0 - sg)).astype(BF16)

    cur = pl.BlockSpec((tb, CW), lambda i: (i, 0))
    return _pcall(body, "conv_bwd2", grid=(T // tb,),
                  in_specs=[cur, pl.BlockSpec((HALO, CW), lambda i: (jnp.minimum((i + 1) * hb, last), 0)),
                            pl.BlockSpec((tb, CW), lambda i: (i, 3)), pl.BlockSpec((tb, CW), lambda i: (i, 4)),
                            pl.BlockSpec((HALO, CW), lambda i: (0, 0))],
                  out_specs=[cur, cur],
                  out_shape=[jax.ShapeDtypeStruct((T, CW), BF16), jax.ShapeDtypeStruct((T, CW), BF16)],
                  scratch_shapes=[pltpu.VMEM((tb + HALO, CW), F32)],
                  compiler_params=_cp(("parallel",)))(du1, du1, proj, proj, conv_w)


def _adamw(w, gslots, m, v, name, tb):
    R, C = w.shape
    S = gslots.shape[0]
    c1 = 1.0 / (1.0 - ADAM_B1 ** ADAM_STEP)
    c2 = 1.0 / (1.0 - ADAM_B2 ** ADAM_STEP)

    def body(w_ref, gs_ref, m_ref, v_ref, g_ref, d_ref, nm_ref, nv_ref):
        g = gs_ref[0].astype(F32)
        for s in range(1, S):
            g = g + gs_ref[s].astype(F32)
        g_ref[...] = g
        nm = ADAM_B1 * m_ref[...] + (1.0 - ADAM_B1) * g
        nv = ADAM_B2 * v_ref[...] + (1.0 - ADAM_B2) * (g * g)
        nm_ref[...] = nm
        nv_ref[...] = nv
        d_ref[...] = -ADAM_LR * ((nm * c1) / (jnp.sqrt(nv * c2) + ADAM_EPS) + ADAM_WD * w_ref[...])

    blk = pl.BlockSpec((tb, C), lambda i: (i, 0))
    return _pcall(body, name, grid=(R // tb,),
                  in_specs=[blk, pl.BlockSpec((S, tb, C), lambda i: (0, i, 0)), blk, blk],
                  out_specs=[blk] * 4, out_shape=[jax.ShapeDtypeStruct((R, C), F32)] * 4,
                  compiler_params=_cp(("parallel",)))(w, gslots, m, v)


def _pack_shards(w_in1, w_out1, w_inm, w_outm, w_in2, w_out2):
    return jnp.concatenate([w_in1.reshape(704, D), w_out1, w_inm.reshape(320, D), w_outm,
                            w_in2.reshape(704, D), w_out2], axis=0)


def _unpack_shards(p):
    offs = [0]
    for r in PACK_ROWS:
        offs.append(offs[-1] + r)
    parts = [p[offs[k]:offs[k + 1]] for k in range(6)]
    return (parts[0].reshape(D, 704), parts[1], parts[2].reshape(D, 320), parts[3],
            parts[4].reshape(D, 704), parts[5])


def _unpack_gathered(G):
    offs = [0]
    for r in PACK_ROWS:
        offs.append(offs[-1] + r)

    def cols(k, n):
        return G[:, offs[k]:offs[k + 1]].reshape(NDEV, D, n).transpose(1, 0, 2).reshape(D, NDEV * n)

    def rows(k, n):
        return G[:, offs[k]:offs[k + 1]].reshape(NDEV * n, D)

    return cols(0, 704), rows(1, 352), cols(2, 320), rows(3, 128), cols(4, 704), rows(5, 352)


def _pack_grads(dw_in1, dw_out1, dw_inm, dw_outm, dw_in2, dw_out2):
    def cols(g, n):
        return g.reshape(D, NDEV, n).transpose(1, 0, 2).reshape(NDEV, n, D)

    def rows(g, n):
        return g.reshape(NDEV, n, D)

    return jnp.concatenate([cols(dw_in1, 704), rows(dw_out1, 352), cols(dw_inm, 320), rows(dw_outm, 128),
                            cols(dw_in2, 704), rows(dw_out2, 352)], axis=1).astype(BF16)


def _small_pack(b_ada, gs6, g_attn, conv_b, ln_g, ln_b, conv_w_full):
    flat = jnp.concatenate([b_ada.reshape(-1)] + [g.reshape(-1) for g in gs6]
                           + [g_attn.reshape(-1), conv_b.reshape(-1), ln_g.reshape(-1), ln_b.reshape(-1),
                              conv_w_full.reshape(-1)])
    return jnp.pad(flat, (0, SMALL_R * D - SMALL_N)).reshape(SMALL_R, D)


def _small_unpack(p):
    flat = p.reshape(-1)
    o = 0
    out = []
    for n, shape in [(NMOD, (NMOD,))] + [(D, (D,))] * 6 + [(512, (NH, HD)), (512, (512,)), (512, (512,)),
                                                           (512, (512,)), (CK * CW, (CK, CW))]:
        out.append(flat[o:o + n].reshape(shape))
        o += n
    return out


def _ffn_fwd(x, g_pre, g_post, shift, scale, gate, wg, wu, w_out, tag):
    h = _pre_fwd(x, g_pre, shift, scale, "pre_fwd_" + tag)
    g, u, a = _ffn_in(h, wg, wu, "ffn_in_" + tag)
    f, out = _mm_post(a, w_out, x, g_post, gate, 0.5, "ffn_out_" + tag)
    return out, (x, h, g, u, a, f)


def _ffn_bwd(dout, saved, g_pre, g_post, scale, gate, wg_t, wu_t, w_out_t, tag):
    x, h, g, u, a, f = saved
    df, dgate, dg_post = _post_bwd(dout, f, g_post, gate, 0.5, "post_bwd_" + tag)
    dg, du = _ffn_out_bwd(df, w_out_t, g, u, "ffn_out_bwd_" + tag)
    dw_out = _mm_tn(a, df, "dw_out_" + tag, tn=D, tm=512)
    dwg = _mm_tn(h, dg, "dwg_" + tag, tn=1408)
    dwu = _mm_tn(h, du, "dwu_" + tag, tn=1408)
    dx, dshift, dscale, dg_pre = _mm_nn_prebwd([(dg, wg_t), (du, wu_t)], x, dout, g_pre, scale, "ffn_in_bwd_" + tag)
    dw_in = jnp.concatenate([dwg, dwu], axis=1)
    return dx, dw_in, dw_out, dg_pre, dg_post, (dshift, dscale, dgate)


def _heads(t):
    return t.reshape(t.shape[0], NH, HD).transpose(1, 0, 2)


def _unheads(t):
    return t.transpose(1, 0, 2).reshape(t.shape[1], AW)


def kernel(x, c, w_ada, b_ada, g_pre_ff1, g_post_ff1, ff1_w_in, ff1_w_out, g_pre_mix, g_post_mix, w_in_mix, g_attn_out, conv_w, conv_b, conv_ln_g, conv_ln_b, w_out_mix, g_pre_ff2, g_post_ff2, ff2_w_in, ff2_w_out, loss_target, m_w_ada, m_b_ada, m_g_pre_ff1, m_g_post_ff1, m_ff1_w_in, m_ff1_w_out, m_g_pre_mix, m_g_post_mix, m_w_in_mix, m_g_attn_out, m_conv_w, m_conv_b, m_conv_ln_g, m_conv_ln_b, m_w_out_mix, m_g_pre_ff2, m_g_post_ff2, m_ff2_w_in, m_ff2_w_out, v_w_ada, v_b_ada, v_g_pre_ff1, v_g_post_ff1, v_ff1_w_in, v_ff1_w_out, v_g_pre_mix, v_g_post_mix, v_w_in_mix, v_g_attn_out, v_conv_w, v_conv_b, v_conv_ln_g, v_conv_ln_b, v_w_out_mix, v_g_pre_ff2, v_g_post_ff2, v_ff2_w_in, v_ff2_w_out):
    me = 4 * lax.axis_index("x") + 2 * lax.axis_index("y") + lax.axis_index("c")
    T = x.shape[1]
    tq = min(256, T)
    x0 = x.reshape(T, D)
    tgt = loss_target.reshape(T, D)
    row = lambda a: a.reshape(1, -1)

    small_in = jnp.concatenate([c.reshape(-1), jnp.pad(conv_w.reshape(-1), (0, 2 * D - CK * 64)),
                                jnp.zeros((5 * D,), F32)]).reshape(8, D)
    small_all = _all_gather(small_in, "gather_c_convw", True)
    c_all = small_all[:, 0, :]
    conv_w_full = small_all[:, 1:3, :].reshape(NDEV, 2 * D)[:, :CK * 64].reshape(NDEV, CK, 64)
    conv_w_full = conv_w_full.transpose(1, 0, 2).reshape(CK, CW)
    conv_w_pad = jnp.pad(conv_w_full, ((0, HALO - CK), (0, 0)))

    wpack = _pack_shards(ff1_w_in, ff1_w_out, w_in_mix, w_out_mix, ff2_w_in, ff2_w_out).astype(BF16)
    wall = _all_gather(wpack, "gather_weights", False)
    w_in1, w_out1, w_inm, w_outm, w_in2, w_out2 = _unpack_gathered(wall)

    b_cols = lax.dynamic_slice(b_ada, (me * 1152,), (1152,)).reshape(1, 1152)
    mod_cols = _ada_fwd(c_all, w_ada, b_cols)
    mod_all = _all_gather(mod_cols, "gather_mod", True)
    mod = lax.dynamic_slice(mod_all, (0, me, 0), (NDEV, 1, 1152)).reshape(9, D)
    sh = lambda s: mod[3 * s:3 * s + 1]
    sc = lambda s: mod[3 * s + 1:3 * s + 2]
    gt = lambda s: mod[3 * s + 2:3 * s + 3]

    x1, sv1 = _ffn_fwd(x0, row(g_pre_ff1), row(g_post_ff1), sh(0), sc(0), gt(0),
                       w_in1[:, :DFF], w_in1[:, DFF:], w_out1, "ff1")
    hm = _pre_fwd(x1, row(g_pre_mix), sh(1), sc(1), "pre_fwd_mix")
    proj = _mm_nn(hm, w_inm, "mix_in")
    qh = _heads(proj[:, 0:AW]).astype(BF16)
    kh = _heads(proj[:, AW:2 * AW]).astype(BF16)
    vh = _heads(proj[:, 2 * AW:3 * AW]).astype(BF16)
    g_attn3 = g_attn_out.reshape(NH, 1, HD)
    o_att, an = _attn_fwd(qh, kh, vh, g_attn3, tq)
    u0, u1, u3 = _conv_fwd(proj, conv_w_pad, row(conv_b), row(conv_ln_g), row(conv_ln_b))
    cat = jnp.concatenate([_unheads(an), u3], axis=1)
    fm, x2 = _mm_post(cat, w_outm, x1, row(g_post_mix), gt(1), 1.0, "mix_out")
    x3, sv2 = _ffn_fwd(x2, row(g_pre_ff2), row(g_post_ff2), sh(2), sc(2), gt(2),
                       w_in2[:, :DFF], w_in2[:, DFF:], w_out2, "ff2")
    dy, loss_part = _loss_head(x3, tgt, "loss_head")
    loss = lax.psum(loss_part[0, 0], ("x", "y", "c"))

    dx2, dw_in2, dw_out2, dgpre2, dgpost2, dmod2 = _ffn_bwd(
        dy, sv2, row(g_pre_ff2), row(g_post_ff2), sc(2), gt(2),
        w_in2[:, :DFF].T, w_in2[:, DFF:].T, w_out2.T, "ff2")

    dfm, dgate1, dgpostm = _post_bwd(dx2, fm, row(g_post_mix), gt(1), 1.0, "post_bwd_mix")
    dcat = _mm_nn(dfm, w_outm.T, "mix_out_bwd")
    dw_outm = _mm_tn(cat, dfm, "dw_out_mix", tn=D)
    dan = _heads(dcat[:, 0:AW])
    dq, dk, dv, dg_attn = _attn_bwd(qh, kh, vh, o_att, dan, g_attn3, tq)
    du1, dconv_w, dconv_b, dln_g, dln_b = _conv_bwd1(dcat, u1, u0, row(conv_ln_g), row(conv_ln_b))
    dcv, dcg = _conv_bwd2(du1, proj, conv_w_pad)
    dproj = jnp.concatenate([_unheads(dq).astype(BF16), _unheads(dk).astype(BF16), _unheads(dv).astype(BF16),
                             dcv, dcg], axis=1)
    dw_inm = _mm_tn(hm, dproj, "dw_in_mix", tn=1280)
    dx1, dshift1, dscale1, dgprem = _mm_nn_prebwd([(dproj, w_inm.T)], x1, dx2, row(g_pre_mix), sc(1), "mix_in_bwd")

    dx0, dw_in1, dw_out1, dgpre1, dgpost1, dmod0 = _ffn_bwd(
        dx1, sv1, row(g_pre_ff1), row(g_post_ff1), sc(0), gt(0),
        w_in1[:, :DFF].T, w_in1[:, DFF:].T, w_out1.T, "ff1")

    dmod = jnp.concatenate(list(dmod0) + [dshift1, dscale1, dgate1] + list(dmod2), axis=0)
    small_g = _small_pack(dmod, [dgpre1, dgpost1, dgprem, dgpostm, dgpre2, dgpost2], dg_attn,
                          dconv_b, dln_g, dln_b, dconv_w[:CK])
    small_g_all = _all_gather(small_g, "gather_small_grads", True)
    gsend = _pack_grads(dw_in1, dw_out1, dw_inm, dw_outm, dw_in2, dw_out2)
    grecv = _all_to_all(gsend, "exchange_weight_grads")

    dmod_all = small_g_all[:, 0:9, :].reshape(NDEV, NMOD)
    dmod_cols = lax.dynamic_slice(dmod_all, (0, me * 1152), (NDEV, 1152))
    g_w_ada = _ada_bwd(c_all.T, dmod_cols)

    def scatter_conv(a):
        return lax.dynamic_update_slice(jnp.zeros((CK, CW), F32), a, (0, me * 64))

    def small_of(b, g6, ga, cb, lg, lb, cw):
        return _small_pack(b, g6, ga, cb, lg, lb, scatter_conv(cw))

    sw = small_of(b_ada, [g_pre_ff1, g_post_ff1, g_pre_mix, g_post_mix, g_pre_ff2, g_post_ff2], g_attn_out,
                  conv_b, conv_ln_g, conv_ln_b, conv_w)
    sm = small_of(m_b_ada, [m_g_pre_ff1, m_g_post_ff1, m_g_pre_mix, m_g_post_mix, m_g_pre_ff2, m_g_post_ff2],
                  m_g_attn_out, m_conv_b, m_conv_ln_g, m_conv_ln_b, m_conv_w)
    sv = small_of(v_b_ada, [v_g_pre_ff1, v_g_post_ff1, v_g_pre_mix, v_g_post_mix, v_g_pre_ff2, v_g_post_ff2],
                  v_g_attn_out, v_conv_b, v_conv_ln_g, v_conv_ln_b, v_conv_w)
    s_out = [_small_unpack(p) for p in _adamw(sw, small_g_all, sm, sv, "adamw_small", SMALL_R)]
    for o in s_out:
        o[11] = lax.dynamic_slice(o[11], (0, me * 64), (CK, 64))

    bw = _pack_shards(ff1_w_in, ff1_w_out, w_in_mix, w_out_mix, ff2_w_in, ff2_w_out)
    bm = _pack_shards(m_ff1_w_in, m_ff1_w_out, m_w_in_mix, m_w_out_mix, m_ff2_w_in, m_ff2_w_out)
    bv = _pack_shards(v_ff1_w_in, v_ff1_w_out, v_w_in_mix, v_w_out_mix, v_ff2_w_in, v_ff2_w_out)
    b_out = [_unpack_shards(p) for p in _adamw(bw, grecv, bm, bv, "adamw_big", 256)]

    a_out = _adamw(w_ada, g_w_ada.reshape(1, D, 1152), m_w_ada, v_w_ada, "adamw_ada", 256)

    def leaves(k):
        s, b = s_out[k], b_out[k]
        return [a_out[k], s[0], s[1], s[2], b[0], b[1], s[3], s[4], b[2], s[7], s[11], s[8], s[9], s[10], b[3],
                s[5], s[6], b[4], b[5]]

    return (loss, dx0.reshape(1, T, D), *leaves(0), *leaves(1), *leaves(2), *leaves(3))
```

```python
import functools

import jax
import jax.numpy as jnp
from jax import lax
from jax.experimental import pallas as pl
from jax.experimental.pallas import tpu as pltpu

F32 = jnp.float32
BF16 = jnp.bfloat16
D = 1024
DFF = 2816
SL = 704
NSL = DFF // SL
AW = 512
HD = 64
CW = 512
CK = 31
HALO = 32
MIXIN = 2560
NDEV = 8
NMOD = 9 * D
ADA_COLS = NMOD // NDEV
RMS_EPS = 1e-6
LN_EPS = 1e-5
QK_SCALE = HD ** -0.5
W_ZERO_BELOW = -104.0
ADAM_LR, ADAM_B1, ADAM_B2, ADAM_EPS, ADAM_WD, ADAM_STEP = 0.001, 0.9, 0.999, 1e-08, 0.01, 10
ADAM_C1 = 1.0 / (1.0 - ADAM_B1 ** ADAM_STEP)
ADAM_C2 = 1.0 / (1.0 - ADAM_B2 ** ADAM_STEP)
MIB = 1024 * 1024
MESH = pl.DeviceIdType.MESH

ROW_GAINS = 9
ROW_ATTN_CB = 15
ROW_LN = 16
ROW_CONVW = 17
CONVW_ROWS = 16
ROW_LOSS = 33
SMALL_R = 40


def _pcall(body, name, **kw):
    return pl.pallas_call(body, name=name, **kw)


def _cp(sem=None, vmem_mib=48):
    if sem is None:
        return pltpu.CompilerParams(vmem_limit_bytes=vmem_mib * MIB)
    return pltpu.CompilerParams(dimension_semantics=sem, vmem_limit_bytes=vmem_mib * MIB)


def _dot(a, b):
    return jnp.dot(a, b, preferred_element_type=F32)


def _dot_nt(a, b):
    return lax.dot_general(a, b, (((1,), (1,)), ((), ())), preferred_element_type=F32)


def _dot_tn(a, b):
    return lax.dot_general(a, b, (((0,), (0,)), ((), ())), preferred_element_type=F32)


def _sigmoid(x):
    return 1.0 / (1.0 + jnp.exp(-x))


def _split2(x):
    hi = x.astype(BF16)
    mid = (x - hi.astype(F32)).astype(BF16)
    return hi, mid


def _mat(ref):
    lead = len(ref.shape) - 2
    return ref[(0,) * lead] if lead else ref[...]


def _all_gather(xs, name, in_vmem):
    n = len(xs)

    def body(*refs):
        x_refs, out_refs = refs[:n], refs[n:2 * n]
        send_sems, recv_sems, local_sems = refs[2 * n:]
        mx, my, mc = lax.axis_index("x"), lax.axis_index("y"), lax.axis_index("c")
        me, sibling = (mx, my, mc), (mx, my, 1 - mc)
        chips = [(1 - mx, my), (mx, 1 - my), (1 - mx, 1 - my)]

        def slab(a, px, py, pc):
            return out_refs[a].at[4 * px + 2 * py + pc]

        def copy(a, k, block, to, src=None):
            return pltpu.make_async_remote_copy(
                src_ref=slab(a, *block) if src is None else src, dst_ref=slab(a, *block),
                send_sem=send_sems.at[a, k], recv_sem=recv_sems.at[a, k], device_id=to, device_id_type=MESH)

        mine = [pltpu.make_async_copy(x_refs[a], slab(a, *me), local_sems.at[a]) for a in range(n)]
        for cp in mine:
            cp.start()
        first = []
        for a in range(n):
            first.append(copy(a, 0, me, sibling, src=x_refs[a]))
            first += [copy(a, 1 + j, me, (*chip, mc), src=x_refs[a]) for j, chip in enumerate(chips)]
        for cp in first:
            cp.start()
        passed = []
        for j, chip in enumerate(chips):
            for a in range(n):
                copy(a, 1 + j, (*chip, mc), me).wait_recv()
                passed.append(copy(a, 4 + j, (*chip, mc), sibling))
                passed[-1].start()
        for a in range(n):
            copy(a, 0, sibling, me).wait_recv()
            for j, chip in enumerate(chips):
                copy(a, 4 + j, (*chip, 1 - mc), me).wait_recv()
        for cp in first + passed:
            cp.wait_send()
        for cp in mine:
            cp.wait()

    space = pltpu.VMEM if in_vmem else pl.ANY
    return _pcall(
        body, name,
        out_shape=[jax.ShapeDtypeStruct((NDEV,) + x.shape, x.dtype) for x in xs],
        in_specs=[pl.BlockSpec(memory_space=space)] * n,
        out_specs=[pl.BlockSpec(memory_space=space)] * n,
        scratch_shapes=[pltpu.SemaphoreType.DMA((n, 7)), pltpu.SemaphoreType.DMA((n, 7)),
                        pltpu.SemaphoreType.DMA((n,))],
    )(*xs)


def _all_to_all(sends, name):
    n = len(sends)

    def body(*refs):
        send_refs, recv_refs = refs[:n], refs[n:2 * n]
        send_sems, recv_sems, local_sems = refs[2 * n:]
        mx, my, mc = lax.axis_index("x"), lax.axis_index("y"), lax.axis_index("c")
        me = 4 * mx + 2 * my + mc
        mine = [pltpu.make_async_copy(send_refs[a].at[me], recv_refs[a].at[me], local_sems.at[a]) for a in range(n)]
        for cp in mine:
            cp.start()
        copies = []
        for r in range(1, NDEV):
            px = 1 - mx if r & 4 else mx
            py = 1 - my if r & 2 else my
            pc = 1 - mc if r & 1 else mc
            peer = 4 * px + 2 * py + pc
            for a in range(n):
                copies.append(pltpu.make_async_remote_copy(
                    src_ref=send_refs[a].at[peer], dst_ref=recv_refs[a].at[me],
                    send_sem=send_sems.at[a, r - 1], recv_sem=recv_sems.at[a, r - 1],
                    device_id=(px, py, pc), device_id_type=MESH))
        for cp in copies:
            cp.start()
        for cp in copies:
            cp.wait_recv()
        for cp in copies:
            cp.wait_send()
        for cp in mine:
            cp.wait()

    return _pcall(
        body, name,
        out_shape=[jax.ShapeDtypeStruct(s.shape, s.dtype) for s in sends],
        in_specs=[pl.BlockSpec(memory_space=pl.ANY)] * n,
        out_specs=[pl.BlockSpec(memory_space=pl.ANY)] * n,
        scratch_shapes=[pltpu.SemaphoreType.DMA((n, 7)), pltpu.SemaphoreType.DMA((n, 7)),
                        pltpu.SemaphoreType.DMA((n,))],
    )(*sends)


def _ada_fwd(c_all, w, b):
    n = w.shape[1]

    def body(c_ref, w_ref, b_ref, o_ref):
        c = c_ref[...]
        s = c * _sigmoid(c)
        o_ref[...] = jnp.dot(s, w_ref[...], preferred_element_type=F32, precision=lax.Precision.HIGHEST) + b_ref[...]

    return _pcall(body, "ada_fwd", out_shape=jax.ShapeDtypeStruct((NDEV, n), F32), compiler_params=_cp())(c_all, w, b)


def _ada_bwd(c_all_t, dmod):
    n = dmod.shape[1]

    def body(ct_ref, d_ref, o_ref):
        ct = ct_ref[...]
        s = ct * _sigmoid(ct)
        acc = s[:, 0:1] * d_ref[0:1, :]
        for b in range(1, NDEV):
            acc = acc + s[:, b:b + 1] * d_ref[b:b + 1, :]
        o_ref[...] = acc

    return _pcall(body, "ada_bwd", out_shape=jax.ShapeDtypeStruct((D, n), F32), compiler_params=_cp())(c_all_t, dmod)


def _pre_fwd(x, g, shift, scale, name, tb=512):
    T = x.shape[0]

    def body(x_ref, g_ref, sh_ref, sc_ref, h_ref):
        xv = x_ref[...]
        r = lax.rsqrt(jnp.mean(xv * xv, axis=-1, keepdims=True) + RMS_EPS)
        h_ref[...] = ((xv * r) * g_ref[...] * (1.0 + sc_ref[...]) + sh_ref[...]).astype(BF16)

    row = pl.BlockSpec((tb, D), lambda i: (i, 0))
    vec = pl.BlockSpec((1, D), lambda i: (0, 0))
    return _pcall(body, name, grid=(T // tb,), in_specs=[row, vec, vec, vec], out_specs=row,
                  out_shape=jax.ShapeDtypeStruct((T, D), BF16), compiler_params=_cp(("parallel",)))(x, g, shift, scale)


def _ffn_in(h, w_in, name, tm=512):
    T = h.shape[0]

    def body(h_ref, wg_ref, wu_ref, gu_ref, a_ref):
        hv = h_ref[...]
        g = _dot(hv, wg_ref[0])
        u = _dot(hv, wu_ref[0])
        gu_ref[0, 0] = g
        gu_ref[0, 1] = u
        a_ref[0] = (g * _sigmoid(g) * u).astype(BF16)

    return _pcall(body, name, grid=(NSL, T // tm),
                  in_specs=[pl.BlockSpec((tm, D), lambda j, i: (i, 0)),
                            pl.BlockSpec((1, D, SL), lambda j, i: (j, 0, 0)),
                            pl.BlockSpec((1, D, SL), lambda j, i: (j + NSL, 0, 0))],
                  out_specs=[pl.BlockSpec((1, 2, tm, SL), lambda j, i: (j, 0, i, 0)),
                             pl.BlockSpec((1, tm, SL), lambda j, i: (j, i, 0))],
                  out_shape=[jax.ShapeDtypeStruct((NSL, 2, T, SL), F32), jax.ShapeDtypeStruct((NSL, T, SL), BF16)],
                  compiler_params=_cp(("parallel", "parallel")))(h, w_in, w_in)


def _mm_post(a_list, a_specs, w_list, w_specs, x, g_post, gate, res_w, name, tm):
    T = x.shape[0]
    n = len(a_list)

    def body(*refs):
        a_refs, w_refs = refs[:n], refs[n:2 * n]
        x_ref, g_ref, gt_ref, f_ref, o_ref = refs[2 * n:]
        f = None
        for a_ref, w_ref in zip(a_refs, w_refs):
            if len(a_ref.shape) == 3:
                terms = [_dot(a_ref[j], w_ref[j]) for j in range(a_ref.shape[0])]
            else:
                terms = [_dot(a_ref[...], w_ref[...])]
            for t in terms:
                f = t if f is None else f + t
        f_ref[...] = f
        r = lax.rsqrt(jnp.mean(f * f, axis=-1, keepdims=True) + RMS_EPS)
        y = (f * r) * g_ref[...]
        o_ref[...] = x_ref[...] + (res_w * (1.0 + gt_ref[...])) * y

    row = pl.BlockSpec((tm, D), lambda i: (i, 0))
    vec = pl.BlockSpec((1, D), lambda i: (0, 0))
    return _pcall(body, name, grid=(T // tm,),
                  in_specs=list(a_specs) + list(w_specs) + [row, vec, vec], out_specs=[row, row],
                  out_shape=[jax.ShapeDtypeStruct((T, D), F32), jax.ShapeDtypeStruct((T, D), F32)],
                  compiler_params=_cp(("parallel",)))(*a_list, *w_list, x, g_post, gate)


def _post_bwd(dout, f, g_post, gate, res_w, name, tb=512):
    T = f.shape[0]

    def body(do_ref, f_ref, g_ref, gt_ref, df_ref, dgate_ref, dg_ref):
        @pl.when(pl.program_id(0) == 0)
        def _():
            dgate_ref[...] = jnp.zeros_like(dgate_ref)
            dg_ref[...] = jnp.zeros_like(dg_ref)

        do = do_ref[...]
        f = f_ref[...]
        r = lax.rsqrt(jnp.mean(f * f, axis=-1, keepdims=True) + RMS_EPS)
        fn = f * r
        dgate_ref[...] += jnp.sum((res_w * do) * (fn * g_ref[...]), axis=0, keepdims=True)
        dy = (res_w * (1.0 + gt_ref[...])) * do
        dg_ref[...] += jnp.sum(dy * fn, axis=0, keepdims=True)
        dyg = dy * g_ref[...]
        df = r * (dyg - fn * jnp.mean(dyg * fn, axis=-1, keepdims=True))
        df_ref[...] = df.astype(BF16)

    row = pl.BlockSpec((tb, D), lambda i: (i, 0))
    vec = pl.BlockSpec((1, D), lambda i: (0, 0))
    return _pcall(body, name, grid=(T // tb,), in_specs=[row, row, vec, vec], out_specs=[row, vec, vec],
                  out_shape=[jax.ShapeDtypeStruct((T, D), BF16), jax.ShapeDtypeStruct((1, D), F32),
                             jax.ShapeDtypeStruct((1, D), F32)],
                  compiler_params=_cp(("arbitrary",)))(dout, f, g_post, gate)


def _ffn_out_bwd(df, w_out4, gu, name, tm=512):
    T = df.shape[0]

    def body(df_ref, w_ref, gu_ref, dgu_ref):
        da = _dot_nt(df_ref[...], w_ref[0])
        gv = gu_ref[0, 0]
        s = _sigmoid(gv)
        gs = gv * s
        dgu_ref[0, 0] = (da * gu_ref[0, 1] * (s + gs * (1.0 - s))).astype(BF16)
        dgu_ref[0, 1] = (da * gs).astype(BF16)

    gus = pl.BlockSpec((1, 2, tm, SL), lambda j, i: (j, 0, i, 0))
    return _pcall(body, name, grid=(NSL, T // tm),
                  in_specs=[pl.BlockSpec((tm, D), lambda j, i: (i, 0)), pl.BlockSpec((1, SL, D), lambda j, i: (j, 0, 0)), gus],
                  out_specs=gus, out_shape=jax.ShapeDtypeStruct((NSL, 2, T, SL), BF16),
                  compiler_params=_cp(("parallel", "parallel")))(df, w_out4, gu)


def _mm_tn(a, a_spec, b, b_spec, out_shape, out_spec, grid, name):
    k, nn = out_spec.block_shape[-2:]

    def body(a_ref, b_ref, o_ref, acc_ref):
        i = pl.program_id(1)

        @pl.when(i == 0)
        def _():
            acc_ref[...] = jnp.zeros_like(acc_ref)

        acc_ref[...] += _dot_tn(_mat(a_ref), _mat(b_ref))

        @pl.when(i == pl.num_programs(1) - 1)
        def _():
            lead = len(o_ref.shape) - 2
            o_ref[(0,) * lead if lead else ...] = acc_ref[...].astype(BF16)

    return _pcall(body, name, grid=grid, in_specs=[a_spec, b_spec], out_specs=out_spec,
                  out_shape=jax.ShapeDtypeStruct(out_shape, BF16),
                  scratch_shapes=[pltpu.VMEM((k, nn), F32)],
                  compiler_params=_cp(("parallel", "arbitrary")))(a, b)


def _mm_prebwd(a, a_spec, w, w_spec, dh_fn, x, dout, g_pre, scale, name, tm=256):
    T = x.shape[0]

    def body(a_ref, w_ref, x_ref, do_ref, g_ref, sc_ref, dx_ref, dsh_ref, dsc_ref, dg_ref):
        @pl.when(pl.program_id(0) == 0)
        def _():
            dsh_ref[...] = jnp.zeros_like(dsh_ref)
            dsc_ref[...] = jnp.zeros_like(dsc_ref)
            dg_ref[...] = jnp.zeros_like(dg_ref)

        dh = dh_fn(a_ref, w_ref)
        xv = x_ref[...]
        r = lax.rsqrt(jnp.mean(xv * xv, axis=-1, keepdims=True) + RMS_EPS)
        xn = xv * r
        dsh_ref[...] += jnp.sum(dh, axis=0, keepdims=True)
        dsc_ref[...] += jnp.sum(dh * (xn * g_ref[...]), axis=0, keepdims=True)
        dn = dh * (1.0 + sc_ref[...])
        dg_ref[...] += jnp.sum(dn * xn, axis=0, keepdims=True)
        dng = dn * g_ref[...]
        dx_ref[...] = do_ref[...] + r * (dng - xn * jnp.mean(dng * xn, axis=-1, keepdims=True))

    row = pl.BlockSpec((tm, D), lambda i: (i, 0))
    vec = pl.BlockSpec((1, D), lambda i: (0, 0))
    return _pcall(body, name, grid=(T // tm,), in_specs=[a_spec, w_spec, row, row, vec, vec],
                  out_specs=[row, vec, vec, vec],
                  out_shape=[jax.ShapeDtypeStruct((T, D), F32)] + [jax.ShapeDtypeStruct((1, D), F32)] * 3,
                  compiler_params=_cp(("arbitrary",), 56))(a, w, x, dout, g_pre, scale)


def _loss_head(y, tgt, name, tb=512):
    T = y.shape[0]

    def body(y_ref, t_ref, dy_ref, l_ref):
        @pl.when(pl.program_id(0) == 0)
        def _():
            l_ref[...] = jnp.zeros_like(l_ref)

        e = y_ref[...] - t_ref[...]
        dy_ref[...] = e * (1.0 / D)
        l_ref[...] += 0.5 * jnp.sum(jnp.mean(e * e, axis=-1, keepdims=True), axis=0, keepdims=True)

    row = pl.BlockSpec((tb, D), lambda i: (i, 0))
    return _pcall(body, name, grid=(T // tb,), in_specs=[row, row],
                  out_specs=[row, pl.BlockSpec((1, 1), lambda i: (0, 0))],
                  out_shape=[jax.ShapeDtypeStruct((T, D), F32), jax.ShapeDtypeStruct((1, 1), F32)],
                  compiler_params=_cp(("arbitrary",)))(y, tgt)


def _mix_in(h, w, name, tm=512):
    T = h.shape[0]

    def body(h_ref, w_ref, qkv_ref, cvg_ref):
        p = _dot(h_ref[...], w_ref[...])
        qkv_ref[...] = p[:, :3 * AW].astype(BF16)
        cvg_ref[...] = p[:, 3 * AW:]

    return _pcall(body, name, grid=(T // tm,),
                  in_specs=[pl.BlockSpec((tm, D), lambda i: (i, 0)), pl.BlockSpec((D, MIXIN), lambda i: (0, 0))],
                  out_specs=[pl.BlockSpec((tm, 3 * AW), lambda i: (i, 0)), pl.BlockSpec((tm, 2 * CW), lambda i: (i, 0))],
                  out_shape=[jax.ShapeDtypeStruct((T, 3 * AW), BF16), jax.ShapeDtypeStruct((T, 2 * CW), F32)],
                  compiler_params=_cp(("parallel",)))(h, w)


def _mm_nt(a, w, name, tm=512):
    T, K = a.shape
    N = w.shape[0]

    def body(a_ref, w_ref, o_ref):
        o_ref[...] = _dot_nt(a_ref[...], w_ref[...])

    return _pcall(body, name, grid=(T // tm,),
                  in_specs=[pl.BlockSpec((tm, K), lambda i: (i, 0)), pl.BlockSpec((N, K), lambda i: (0, 0))],
                  out_specs=pl.BlockSpec((tm, N), lambda i: (i, 0)),
                  out_shape=jax.ShapeDtypeStruct((T, N), F32), compiler_params=_cp(("parallel",)))(a, w)


def _softplus_parts(z):
    l = jnp.log(1.0 + jnp.exp(-jnp.abs(z)))
    return jnp.minimum(z, 0.0) - l, jnp.minimum(-z, 0.0) - l


def _head_sum(x, first):
    sa = jnp.sum(jnp.where(first, x, 0.0), axis=-1, keepdims=True)
    sb = jnp.sum(jnp.where(first, 0.0, x), axis=-1, keepdims=True)
    return jnp.where(first, sa, sb)


def _attn_specs(T, tq):
    qs = pl.BlockSpec((tq, 128), lambda p, i: (i, p))
    ks = pl.BlockSpec((T, 128), lambda p, i: (0, 4 + p))
    vs = pl.BlockSpec((T, 128), lambda p, i: (0, 8 + p))
    gs = pl.BlockSpec((1, 128), lambda p, i: (0, p))
    return qs, ks, vs, gs


def _attn_fwd(qkv, g_attn, tq):
    T = qkv.shape[0]

    def body(q_ref, k_ref, v_ref, g_ref, o_ref, an_ref):
        i = pl.program_id(1)
        first = lax.broadcasted_iota(jnp.int32, (tq, 128), 1) < HD
        q = q_ref[...]
        zq = jnp.zeros_like(q)
        qs = (jnp.where(first, q, zq), jnp.where(first, zq, q))
        rows = lax.broadcasted_iota(jnp.int32, (tq, tq), 0)
        cols = lax.broadcasted_iota(jnp.int32, (tq, tq), 1)
        tri = (rows > cols).astype(BF16)
        strict = cols < rows

        def tile(j, Rs, acc, masked):
            start = pl.multiple_of(j * tq, tq)
            kb = k_ref[pl.ds(start, tq), :]
            vb = v_ref[pl.ds(start, tq), :]
            outs, new_r = [], []
            for hh in range(2):
                z = _dot_nt(qs[hh], kb) * QK_SCALE
                ls, lsm = _softplus_parts(z)
                if masked:
                    lsm = jnp.where(strict, lsm, 0.0)
                hi, mid = _split2(lsm)
                after = _dot(hi, tri) + _dot(mid, tri)
                w = jnp.exp(ls + after + Rs[hh])
                if masked:
                    w = jnp.where(strict, w, 0.0)
                outs.append(_dot(w.astype(BF16), vb))
                new_r.append(Rs[hh] + after[:, 0:1] + lsm[:, 0:1])
            return new_r[0], new_r[1], acc + jnp.where(first, outs[0], outs[1])

        zr = jnp.zeros((tq, 1), F32)
        ra, rb, acc = tile(i, (zr, zr), jnp.zeros((tq, 128), F32), True)

        def more(c):
            return jnp.logical_and(c[0] < i, jnp.maximum(jnp.max(c[1]), jnp.max(c[2])) > W_ZERO_BELOW)

        def step(c):
            ra, rb, acc = tile(i - 1 - c[0], (c[1], c[2]), c[3], False)
            return c[0] + 1, ra, rb, acc

        _, ra, rb, acc = lax.while_loop(more, step, (jnp.int32(0), ra, rb, acc))
        o_ref[...] = acc
        r = lax.rsqrt(_head_sum(acc * acc, first) * (1.0 / HD) + RMS_EPS)
        an_ref[...] = ((acc * r) * g_ref[...]).astype(BF16)

    qs, ks, vs, gs = _attn_specs(T, tq)
    return _pcall(body, "attn_fwd", grid=(AW // 128, T // tq), in_specs=[qs, ks, vs, gs], out_specs=[qs, qs],
                  out_shape=[jax.ShapeDtypeStruct((T, AW), F32), jax.ShapeDtypeStruct((T, AW), BF16)],
                  compiler_params=_cp(("parallel", "parallel")))(qkv, qkv, qkv, g_attn)


def _attn_bwd(qkv, o, dcat, g_attn, tq):
    T = qkv.shape[0]

    def body(q_ref, k_ref, v_ref, o_ref, dan_ref, g_ref, dq_ref, dk_ref, dv_ref, dg_ref):
        i = pl.program_id(1)

        @pl.when(i == 0)
        def _():
            dk_ref[...] = jnp.zeros_like(dk_ref)
            dv_ref[...] = jnp.zeros_like(dv_ref)
            dg_ref[...] = jnp.zeros_like(dg_ref)

        first = lax.broadcasted_iota(jnp.int32, (tq, 128), 1) < HD
        q = q_ref[...]
        zq = jnp.zeros_like(q)
        qs = (jnp.where(first, q, zq), jnp.where(first, zq, q))
        o = o_ref[...]
        dan = dan_ref[...]
        r = lax.rsqrt(_head_sum(o * o, first) * (1.0 / HD) + RMS_EPS)
        on = o * r
        dg_ref[...] += jnp.sum(dan * on, axis=0, keepdims=True)
        dyg = dan * g_ref[...]
        dO = r * (dyg - on * (_head_sum(dyg * on, first) * (1.0 / HD)))
        dOb = dO.astype(BF16)
        dOs = (jnp.where(first, dOb, zq), jnp.where(first, zq, dOb))
        ones = jnp.ones((8, 128), BF16)
        Ds = []
        for hh in range(2):
            prod = dOs[hh].astype(F32) * o
            p1 = prod.astype(BF16)
            rem = prod - p1.astype(F32)
            p2 = rem.astype(BF16)
            p3 = (rem - p2.astype(F32)).astype(BF16)
            Ds.append((_dot_nt(ones, p1) + _dot_nt(ones, p2) + _dot_nt(ones, p3))[0:1, :])

        rows = lax.broadcasted_iota(jnp.int32, (tq, tq), 0)
        cols = lax.broadcasted_iota(jnp.int32, (tq, tq), 1)
        tri_after = (cols > rows).astype(BF16)
        tri_incl = (cols >= rows).astype(BF16)
        strict = rows < cols

        def tile(j, Rs, Gs, dq, masked):
            start = pl.multiple_of(j * tq, tq)
            kb = k_ref[pl.ds(start, tq), :]
            vb = v_ref[pl.ds(start, tq), :]
            new_r, new_g = [], []
            dkp = jnp.zeros((tq, 128), F32)
            dvp = jnp.zeros((tq, 128), F32)
            for hh in range(2):
                z = _dot_nt(kb, qs[hh]) * QK_SCALE
                ls, lsm = _softplus_parts(z)
                if masked:
                    lsm = jnp.where(strict, lsm, 0.0)
                hi, mid = _split2(lsm)
                after = _dot(tri_after, hi) + _dot(tri_after, mid)
                w = jnp.exp(ls + after + Rs[hh])
                if masked:
                    w = jnp.where(strict, w, 0.0)
                wb = w.astype(BF16)
                dlw = _dot_nt(vb, dOs[hh]) * wb.astype(F32)
                hi2, mid2 = _split2(dlw)
                C = _dot(tri_incl, hi2) + _dot(tri_incl, mid2)
                dlsm = Ds[hh] - Gs[hh] - C
                if masked:
                    dlsm = jnp.where(strict, dlsm, 0.0)
                p = jnp.exp(ls)
                dz = ((dlw * (1.0 - p) - dlsm * p) * QK_SCALE).astype(BF16)
                dkp = dkp + _dot(dz, qs[hh])
                dvp = dvp + _dot(wb, dOs[hh])
                dqh = _dot_tn(dz, kb)
                dq = dq + (jnp.where(first, dqh, 0.0) if hh == 0 else jnp.where(first, 0.0, dqh))
                new_r.append(Rs[hh] + after[0:1, :] + lsm[0:1, :])
                new_g.append(Gs[hh] + C[0:1, :])
            dk_ref[pl.ds(start, tq), :] += dkp
            dv_ref[pl.ds(start, tq), :] += dvp
            return new_r[0], new_r[1], new_g[0], new_g[1], dq

        zrow = jnp.zeros((1, tq), F32)
        st = tile(i, (zrow, zrow), (zrow, zrow), jnp.zeros((tq, 128), F32), True)

        def more(c):
            return jnp.logical_and(c[0] < i, jnp.maximum(jnp.max(c[1]), jnp.max(c[2])) > W_ZERO_BELOW)

        def step(c):
            return (c[0] + 1,) + tile(i - 1 - c[0], (c[1], c[2]), (c[3], c[4]), c[5], False)

        out = lax.while_loop(more, step, (jnp.int32(0),) + st)
        dq_ref[...] = out[5].astype(BF16)

    qs, ks, vs, gs = _attn_specs(T, tq)
    kacc = pl.BlockSpec((T, 128), lambda p, i: (0, p))
    return _pcall(body, "attn_bwd", grid=(AW // 128, T // tq), in_specs=[qs, ks, vs, qs, qs, gs],
                  out_specs=[qs, kacc, kacc, gs],
                  out_shape=[jax.ShapeDtypeStruct((T, AW), BF16), jax.ShapeDtypeStruct((T, AW), F32),
                             jax.ShapeDtypeStruct((T, AW), F32), jax.ShapeDtypeStruct((1, AW), F32)],
                  compiler_params=_cp(("parallel", "arbitrary")))(qkv, qkv, qkv, o, dcat, g_attn)


def _conv_fwd(cvg, conv_w, conv_b, ln_g, ln_b, tb=512):
    T = cvg.shape[0]
    hb = tb // HALO

    def body(cv_ref, cg_ref, cvp_ref, cgp_ref, w_ref, b_ref, g_ref, be_ref, u0_ref, u1_ref, u3_ref, pad_ref):
        i = pl.program_id(0)
        u0 = cv_ref[...] * _sigmoid(cg_ref[...])
        prev = cvp_ref[...] * _sigmoid(cgp_ref[...])
        pad_ref[0:HALO, :] = jnp.where(i > 0, prev, 0.0)
        pad_ref[HALO:HALO + tb, :] = u0
        u0_ref[...] = u0
        acc = jnp.zeros((tb, CW), F32) + b_ref[...]
        for kk in range(CK):
            off = HALO - (CK - 1) + kk
            acc = acc + w_ref[kk:kk + 1, :] * pad_ref[off:off + tb, :]
        u1_ref[...] = acc
        mu = jnp.mean(acc, axis=-1, keepdims=True)
        xc = acc - mu
        var = jnp.mean(xc * xc, axis=-1, keepdims=True)
        u2 = (xc * lax.rsqrt(var + LN_EPS)) * g_ref[...] + be_ref[...]
        u3_ref[...] = (u2 * _sigmoid(u2)).astype(BF16)

    cur = lambda col: pl.BlockSpec((tb, CW), lambda i: (i, col))
    prv = lambda col: pl.BlockSpec((HALO, CW), lambda i: (jnp.maximum(i * hb - 1, 0), col))
    vec = pl.BlockSpec((1, CW), lambda i: (0, 0))
    out = pl.BlockSpec((tb, CW), lambda i: (i, 0))
    return _pcall(body, "conv_fwd", grid=(T // tb,),
                  in_specs=[cur(0), cur(1), prv(0), prv(1), pl.BlockSpec((HALO, CW), lambda i: (0, 0)), vec, vec, vec],
                  out_specs=[out, out, out],
                  out_shape=[jax.ShapeDtypeStruct((T, CW), F32), jax.ShapeDtypeStruct((T, CW), F32),
                             jax.ShapeDtypeStruct((T, CW), BF16)],
                  scratch_shapes=[pltpu.VMEM((tb + HALO, CW), F32)],
                  compiler_params=_cp(("parallel",)))(cvg, cvg, cvg, cvg, conv_w, conv_b, ln_g, ln_b)


def _conv_bwd1(dcat, u1, u0, ln_g, ln_b, tb=512):
    T = u1.shape[0]
    hb = tb // HALO

    def body(d3_ref, u1_ref, u0_ref, u0p_ref, g_ref, be_ref, du1_ref, dw_ref, db_ref, dlg_ref, dlb_ref, pad_ref):
        i = pl.program_id(0)

        @pl.when(i == 0)
        def _():
            dw_ref[...] = jnp.zeros_like(dw_ref)
            db_ref[...] = jnp.zeros_like(db_ref)
            dlg_ref[...] = jnp.zeros_like(dlg_ref)
            dlb_ref[...] = jnp.zeros_like(dlb_ref)

        u1 = u1_ref[...]
        mu = jnp.mean(u1, axis=-1, keepdims=True)
        xc = u1 - mu
        rstd = lax.rsqrt(jnp.mean(xc * xc, axis=-1, keepdims=True) + LN_EPS)
        xh = xc * rstd
        u2 = xh * g_ref[...] + be_ref[...]
        s = _sigmoid(u2)
        du2 = d3_ref[...] * (s + u2 * s * (1.0 - s))
        dlg_ref[...] += jnp.sum(du2 * xh, axis=0, keepdims=True)
        dlb_ref[...] += jnp.sum(du2, axis=0, keepdims=True)
        dxh = du2 * g_ref[...]
        du1 = rstd * (dxh - jnp.mean(dxh, axis=-1, keepdims=True) - xh * jnp.mean(dxh * xh, axis=-1, keepdims=True))
        du1_ref[...] = du1
        db_ref[...] += jnp.sum(du1, axis=0, keepdims=True)
        pad_ref[0:HALO, :] = jnp.where(i > 0, u0p_ref[...], 0.0)
        pad_ref[HALO:HALO + tb, :] = u0_ref[...]
        for kk in range(CK):
            off = HALO - (CK - 1) + kk
            dw_ref[kk:kk + 1, :] += jnp.sum(du1 * pad_ref[off:off + tb, :], axis=0, keepdims=True)

    cur = pl.BlockSpec((tb, CW), lambda i: (i, 0))
    vec = pl.BlockSpec((1, CW), lambda i: (0, 0))
    return _pcall(body, "conv_bwd1", grid=(T // tb,),
                  in_specs=[pl.BlockSpec((tb, CW), lambda i: (i, 1)), cur, cur,
                            pl.BlockSpec((HALO, CW), lambda i: (jnp.maximum(i * hb - 1, 0), 0)), vec, vec],
                  out_specs=[cur, pl.BlockSpec((HALO, CW), lambda i: (0, 0)), vec, vec, vec],
                  out_shape=[jax.ShapeDtypeStruct((T, CW), F32), jax.ShapeDtypeStruct((HALO, CW), F32)]
                  + [jax.ShapeDtypeStruct((1, CW), F32)] * 3,
                  scratch_shapes=[pltpu.VMEM((tb + HALO, CW), F32)],
                  compiler_params=_cp(("arbitrary",)))(dcat, u1, u0, u0, ln_g, ln_b)


def _conv_bwd2(du1, cvg, conv_w, tb=512):
    T = du1.shape[0]
    hb = tb // HALO
    last = T // HALO - 1
    nblk = T // tb

    def body(d_ref, dn_ref, cv_ref, cg_ref, w_ref, o_ref, pad_ref):
        i = pl.program_id(0)
        pad_ref[0:tb, :] = d_ref[...]
        pad_ref[tb:tb + HALO, :] = jnp.where(i < nblk - 1, dn_ref[...], 0.0)
        acc = jnp.zeros((tb, CW), F32)
        for kk in range(CK):
            off = CK - 1 - kk
            acc = acc + w_ref[kk:kk + 1, :] * pad_ref[off:off + tb, :]
        sg = _sigmoid(cg_ref[...])
        o_ref[:, 0:CW] = (acc * sg).astype(BF16)
        o_ref[:, CW:2 * CW] = (acc * cv_ref[...] * sg * (1.0 - sg)).astype(BF16)

    cur = pl.BlockSpec((tb, CW), lambda i: (i, 0))
    return _pcall(body, "conv_bwd2", grid=(nblk,),
                  in_specs=[cur, pl.BlockSpec((HALO, CW), lambda i: (jnp.minimum((i + 1) * hb, last), 0)),
                            pl.BlockSpec((tb, CW), lambda i: (i, 0)), pl.BlockSpec((tb, CW), lambda i: (i, 1)),
                            pl.BlockSpec((HALO, CW), lambda i: (0, 0))],
                  out_specs=pl.BlockSpec((tb, 2 * CW), lambda i: (i, 0)),
                  out_shape=jax.ShapeDtypeStruct((T, 2 * CW), BF16),
                  scratch_shapes=[pltpu.VMEM((tb + HALO, CW), F32)],
                  compiler_params=_cp(("parallel",)))(du1, du1, cvg, cvg, conv_w)


def _adam_math(w, g, m, v):
    nm = ADAM_B1 * m + (1.0 - ADAM_B1) * g
    nv = ADAM_B2 * v + (1.0 - ADAM_B2) * (g * g)
    delta = -ADAM_LR * ((nm * ADAM_C1) / (jnp.sqrt(nv * ADAM_C2) + ADAM_EPS) + ADAM_WD * w)
    return delta, nm, nv


def _adamw(w, gslots, m, v, name, tb):
    R, C = w.shape
    S = gslots.shape[0]

    def body(w_ref, gs_ref, m_ref, v_ref, g_ref, d_ref, nm_ref, nv_ref):
        g = gs_ref[0].astype(F32)
        for s in range(1, S):
            g = g + gs_ref[s].astype(F32)
        g_ref[...] = g
        d_ref[...], nm_ref[...], nv_ref[...] = _adam_math(w_ref[...], g, m_ref[...], v_ref[...])

    blk = pl.BlockSpec((tb, C), lambda i: (i, 0))
    return _pcall(body, name, grid=(R // tb,),
                  in_specs=[blk, pl.BlockSpec((S, tb, C), lambda i: (0, i, 0)), blk, blk],
                  out_specs=[blk] * 4, out_shape=[jax.ShapeDtypeStruct((R, C), F32)] * 4,
                  compiler_params=_cp(("parallel",)))(w, gslots, m, v)


def _adamw_small(gall, gattn, gconvw, ws, ms, vs):
    n = len(ws)

    def body(*refs):
        gall_ref, gattn_ref, gconvw_ref = refs[:3]
        w_refs, m_refs, v_refs = refs[3:3 + n], refs[3 + n:3 + 2 * n], refs[3 + 2 * n:3 + 3 * n]
        loss_ref = refs[3 + 3 * n]
        outs = refs[4 + 3 * n:]
        g_refs, d_refs, nm_refs, nv_refs = outs[:n], outs[n:2 * n], outs[2 * n:3 * n], outs[3 * n:]

        def total(ref):
            t = ref[0]
            for dev in range(1, NDEV):
                t = t + ref[dev]
            return t

        tot = total(gall_ref)
        grads = [tot[0:9, :]] + [tot[ROW_GAINS + k:ROW_GAINS + k + 1, :] for k in range(6)]
        grads += [total(gattn_ref), tot[ROW_ATTN_CB:ROW_ATTN_CB + 1, CW:2 * CW], tot[ROW_LN:ROW_LN + 1, 0:CW],
                  tot[ROW_LN:ROW_LN + 1, CW:2 * CW], total(gconvw_ref)]
        loss_ref[...] = tot[ROW_LOSS:ROW_LOSS + 1, 0:1]
        for k in range(n):
            g_refs[k][...] = grads[k]
            d_refs[k][...], nm_refs[k][...], nv_refs[k][...] = _adam_math(w_refs[k][...], grads[k], m_refs[k][...],
                                                                          v_refs[k][...])

    shapes = [jax.ShapeDtypeStruct(w.shape, F32) for w in ws]
    res = _pcall(body, "adamw_small", out_shape=[jax.ShapeDtypeStruct((1, 1), F32)] + shapes * 4,
                 compiler_params=_cp())(gall, gattn, gconvw, *ws, *ms, *vs)
    return res[0], [res[1 + k * n:1 + (k + 1) * n] for k in range(4)]


def _ffn_fwd(x, g_pre, g_post, shift, scale, gate, w_in, w_out4, tag, tm):
    h = _pre_fwd(x, g_pre, shift, scale, "pre_fwd_" + tag)
    gu, a = _ffn_in(h, w_in, "ffn_in_" + tag)
    f, out = _mm_post([a], [pl.BlockSpec((NSL, tm, SL), lambda i: (0, i, 0))],
                      [w_out4], [pl.BlockSpec((NSL, SL, D), lambda i: (0, 0, 0))],
                      x, g_post, gate, 0.5, "ffn_out_" + tag, tm)
    return out, (x, h, gu, a, f)


def _ffn_bwd(dout, saved, g_pre, g_post, scale, gate, w_in, w_out4, tag, tm, tmb):
    x, h, gu, a, f = saved
    T = x.shape[0]
    df, dgate, dg_post = _post_bwd(dout, f, g_post, gate, 0.5, "post_bwd_" + tag)
    dgu = _ffn_out_bwd(df, w_out4, gu, "ffn_out_bwd_" + tag)
    dw_out = _mm_tn(a, pl.BlockSpec((1, tm, SL), lambda j, i: (j, i, 0)),
                    df, pl.BlockSpec((tm, D), lambda j, i: (i, 0)),
                    (NSL, SL, D), pl.BlockSpec((1, SL, D), lambda j, i: (j, 0, 0)), (NSL, T // tm), "dw_out_" + tag)
    dw_in = _mm_tn(h, pl.BlockSpec((tm, D), lambda s, i: (i, 0)),
                   dgu, pl.BlockSpec((1, 1, tm, SL), lambda s, i: (s % NSL, s // NSL, i, 0)),
                   (NDEV, D, SL), pl.BlockSpec((1, D, SL), lambda s, i: (s, 0, 0)), (NDEV, T // tm), "dw_in_" + tag)

    def dh_fn(a_ref, w_ref):
        dh = None
        for p in range(2):
            for j in range(NSL):
                t = _dot_nt(a_ref[j, p], w_ref[NSL * p + j])
                dh = t if dh is None else dh + t
        return dh

    dx, dshift, dscale, dg_pre = _mm_prebwd(
        dgu, pl.BlockSpec((NSL, 2, tmb, SL), lambda i: (0, 0, i, 0)),
        w_in, pl.BlockSpec((NDEV, D, SL), lambda i: (0, 0, 0)),
        dh_fn, x, dout, g_pre, scale, "ffn_in_bwd_" + tag, tmb)
    return dx, dw_in, dw_out, dg_pre, dg_post, (dshift, dscale, dgate)


def kernel(x, c, w_ada, b_ada, g_pre_ff1, g_post_ff1, ff1_w_in, ff1_w_out, g_pre_mix, g_post_mix, w_in_mix, g_attn_out, conv_w, conv_b, conv_ln_g, conv_ln_b, w_out_mix, g_pre_ff2, g_post_ff2, ff2_w_in, ff2_w_out, loss_target, m_w_ada, m_b_ada, m_g_pre_ff1, m_g_post_ff1, m_ff1_w_in, m_ff1_w_out, m_g_pre_mix, m_g_post_mix, m_w_in_mix, m_g_attn_out, m_conv_w, m_conv_b, m_conv_ln_g, m_conv_ln_b, m_w_out_mix, m_g_pre_ff2, m_g_post_ff2, m_ff2_w_in, m_ff2_w_out, v_w_ada, v_b_ada, v_g_pre_ff1, v_g_post_ff1, v_ff1_w_in, v_ff1_w_out, v_g_pre_mix, v_g_post_mix, v_w_in_mix, v_g_attn_out, v_conv_w, v_conv_b, v_conv_ln_g, v_conv_ln_b, v_w_out_mix, v_g_pre_ff2, v_g_post_ff2, v_ff2_w_in, v_ff2_w_out):
    me = 4 * lax.axis_index("x") + 2 * lax.axis_index("y") + lax.axis_index("c")
    T = x.shape[1]
    tq = min(256, T)
    tm = 512
    tmb = 256
    x0 = x.reshape(T, D)
    tgt = loss_target.reshape(T, D)
    row = lambda a: a.reshape(1, -1)

    small_in = jnp.concatenate([c.reshape(-1), jnp.pad(conv_w.reshape(-1), (0, 2 * D - CK * 64)),
                                jnp.zeros((5 * D,), F32)]).reshape(8, D)
    small_all, = _all_gather([small_in], "gather_c_convw", True)
    c_all = small_all[:, 0, :]
    conv_w_full = small_all[:, 1:3, :].reshape(NDEV, 2 * D)[:, :CK * 64].reshape(NDEV, CK, 64)
    conv_w_full = conv_w_full.transpose(1, 0, 2).reshape(CK, CW)
    conv_w_pad = jnp.pad(conv_w_full, ((0, HALO - CK), (0, 0)))

    big = [ff1_w_in, ff1_w_out, w_in_mix, w_out_mix, ff2_w_in, ff2_w_out]
    w_in1, w_out1, w_inm_s, w_outm_s, w_in2, w_out2 = _all_gather([w.astype(BF16) for w in big], "gather_weights", False)
    w_out1_4 = w_out1.reshape(NSL, SL, D)
    w_out2_4 = w_out2.reshape(NSL, SL, D)
    w_inm = w_inm_s.transpose(1, 0, 2).reshape(D, MIXIN)
    w_outm = w_outm_s.reshape(D, D)

    b_cols = lax.dynamic_slice(b_ada, (me * ADA_COLS,), (ADA_COLS,)).reshape(1, ADA_COLS)
    mod_cols = _ada_fwd(c_all, w_ada, b_cols)
    mod_all, = _all_gather([mod_cols], "gather_mod", True)
    mod = lax.dynamic_slice(mod_all, (0, me, 0), (NDEV, 1, ADA_COLS)).reshape(9, D)
    sh = lambda s: mod[3 * s:3 * s + 1]
    sc = lambda s: mod[3 * s + 1:3 * s + 2]
    gt = lambda s: mod[3 * s + 2:3 * s + 3]

    x1, sv1 = _ffn_fwd(x0, row(g_pre_ff1), row(g_post_ff1), sh(0), sc(0), gt(0), w_in1, w_out1_4, "ff1", tm)
    hm = _pre_fwd(x1, row(g_pre_mix), sh(1), sc(1), "pre_fwd_mix")
    qkv, cvg = _mix_in(hm, w_inm, "mix_in")
    g_attn_row = row(g_attn_out)
    o_att, an = _attn_fwd(qkv, g_attn_row, tq)
    u0, u1, u3 = _conv_fwd(cvg, conv_w_pad, row(conv_b), row(conv_ln_g), row(conv_ln_b))
    half = lambda k: pl.BlockSpec((AW, D), lambda i: (k, 0))
    act = pl.BlockSpec((tm, AW), lambda i: (i, 0))
    fm, x2 = _mm_post([an, u3], [act, act], [w_outm, w_outm], [half(0), half(1)],
                      x1, row(g_post_mix), gt(1), 1.0, "mix_out", tm)
    x3, sv2 = _ffn_fwd(x2, row(g_pre_ff2), row(g_post_ff2), sh(2), sc(2), gt(2), w_in2, w_out2_4, "ff2", tm)
    dy, loss_part = _loss_head(x3, tgt, "loss_head")

    dx2, dw_in2, dw_out2, dgpre2, dgpost2, dmod2 = _ffn_bwd(
        dy, sv2, row(g_pre_ff2), row(g_post_ff2), sc(2), gt(2), w_in2, w_out2_4, "ff2", tm, tmb)

    dfm, dgate1, dgpostm = _post_bwd(dx2, fm, row(g_post_mix), gt(1), 1.0, "post_bwd_mix")
    dcat = _mm_nt(dfm, w_outm, "mix_out_bwd")
    tok = pl.BlockSpec((tm, AW), lambda j, i: (i, 0))
    tokd = pl.BlockSpec((tm, D), lambda j, i: (i, 0))
    whole = pl.BlockSpec((AW, D), lambda j, i: (0, 0))
    dw_outm = jnp.concatenate([_mm_tn(an, tok, dfm, tokd, (AW, D), whole, (1, T // tm), "dw_out_mix_a"),
                               _mm_tn(u3, tok, dfm, tokd, (AW, D), whole, (1, T // tm), "dw_out_mix_c")], axis=0)
    dq, dk, dv, dg_attn = _attn_bwd(qkv, o_att, dcat, g_attn_row, tq)
    du1, dconv_w, dconv_b, dln_g, dln_b = _conv_bwd1(dcat, u1, u0, row(conv_ln_g), row(conv_ln_b))
    dcvg = _conv_bwd2(du1, cvg, conv_w_pad)
    dproj = jnp.concatenate([dq, dk.astype(BF16), dv.astype(BF16), dcvg], axis=1)
    dw_inm = _mm_tn(hm, pl.BlockSpec((tm, D), lambda j, i: (i, 0)),
                    dproj, pl.BlockSpec((tm, MIXIN // 2), lambda j, i: (i, j)),
                    (D, MIXIN), pl.BlockSpec((D, MIXIN // 2), lambda j, i: (0, j)), (2, T // tm), "dw_in_mix")
    dx1, dshift1, dscale1, dgprem = _mm_prebwd(
        dproj, pl.BlockSpec((tmb, MIXIN), lambda i: (i, 0)), w_inm, pl.BlockSpec((D, MIXIN), lambda i: (0, 0)),
        lambda a_ref, w_ref: _dot_nt(a_ref[...], w_ref[...]), x1, dx2, row(g_pre_mix), sc(1), "mix_in_bwd", tmb)

    dx0, dw_in1, dw_out1, dgpre1, dgpost1, dmod0 = _ffn_bwd(
        dx1, sv1, row(g_pre_ff1), row(g_post_ff1), sc(0), gt(0), w_in1, w_out1_4, "ff1", tm, tmb)

    zrow = jnp.zeros((1, D), F32)
    small_g = jnp.concatenate(
        list(dmod0) + [dshift1, dscale1, dgate1] + list(dmod2)
        + [dgpre1, dgpost1, dgprem, dgpostm, dgpre2, dgpost2]
        + [jnp.concatenate([dg_attn, dconv_b], axis=1), jnp.concatenate([dln_g, dln_b], axis=1),
           jnp.pad(dconv_w[:CK].reshape(-1), (0, CONVW_ROWS * D - CK * CW)).reshape(CONVW_ROWS, D),
           jnp.pad(loss_part, ((0, 0), (0, D - 1)))] + [zrow] * (SMALL_R - ROW_LOSS - 1), axis=0)
    small_g_all, = _all_gather([small_g], "gather_small_grads", True)
    sends = [dw_in1, dw_out1.reshape(NDEV, 352, D), dw_inm.reshape(D, NDEV, 320).transpose(1, 0, 2),
             dw_outm.reshape(NDEV, 128, D), dw_in2, dw_out2.reshape(NDEV, 352, D)]
    recvs = _all_to_all(sends, "exchange_weight_grads")

    dmod_all = small_g_all[:, 0:9, :].reshape(NDEV, NMOD)
    dmod_cols = lax.dynamic_slice(dmod_all, (0, me * ADA_COLS), (NDEV, ADA_COLS))
    g_w_ada = _ada_bwd(c_all.T, dmod_cols)

    gattn = small_g_all[:, ROW_ATTN_CB, 0:AW].reshape(NDEV, 8, HD)
    gconvw = small_g_all[:, ROW_CONVW:ROW_CONVW + CONVW_ROWS, :].reshape(NDEV, CONVW_ROWS * D)[:, :CK * CW]
    gconvw = lax.dynamic_slice(gconvw.reshape(NDEV, CK, CW), (0, 0, me * 64), (NDEV, CK, 64))

    def small_list(b, g6, ga, cb, lg, lb, cw):
        return [b.reshape(9, D)] + [row(g) for g in g6] + [ga, row(cb), row(lg), row(lb), cw]

    sw = small_list(b_ada, [g_pre_ff1, g_post_ff1, g_pre_mix, g_post_mix, g_pre_ff2, g_post_ff2], g_attn_out,
                    conv_b, conv_ln_g, conv_ln_b, conv_w)
    sm = small_list(m_b_ada, [m_g_pre_ff1, m_g_post_ff1, m_g_pre_mix, m_g_post_mix, m_g_pre_ff2, m_g_post_ff2],
                    m_g_attn_out, m_conv_b, m_conv_ln_g, m_conv_ln_b, m_conv_w)
    sv = small_list(v_b_ada, [v_g_pre_ff1, v_g_post_ff1, v_g_pre_mix, v_g_post_mix, v_g_pre_ff2, v_g_post_ff2],
                    v_g_attn_out, v_conv_b, v_conv_ln_g, v_conv_ln_b, v_conv_w)
    loss, s_out = _adamw_small(small_g_all, gattn, gconvw, sw, sm, sv)
    s_out = [[o.reshape(w.shape) for o, w in zip(outs, [b_ada, g_pre_ff1, g_post_ff1, g_pre_mix, g_post_mix,
                                                        g_pre_ff2, g_post_ff2, g_attn_out, conv_b, conv_ln_g,
                                                        conv_ln_b, conv_w])] for outs in s_out]

    big_m = [m_ff1_w_in, m_ff1_w_out, m_w_in_mix, m_w_out_mix, m_ff2_w_in, m_ff2_w_out]
    big_v = [v_ff1_w_in, v_ff1_w_out, v_w_in_mix, v_w_out_mix, v_ff2_w_in, v_ff2_w_out]
    tbs = [256, 176, 256, 128, 256, 176]
    tags = ["ff1_w_in", "ff1_w_out", "w_in_mix", "w_out_mix", "ff2_w_in", "ff2_w_out"]
    b_out = [_adamw(big[k], recvs[k], big_m[k], big_v[k], "adamw_" + tags[k], tbs[k]) for k in range(6)]
    a_out = _adamw(w_ada, g_w_ada.reshape(1, D, ADA_COLS), m_w_ada, v_w_ada, "adamw_ada", 256)

    def leaves(k):
        s = s_out[k]
        b = [o[k] for o in b_out]
        return [a_out[k], s[0], s[1], s[2], b[0], b[1], s[3], s[4], b[2], s[7], s[11], s[8], s[9], s[10], b[3],
                s[5], s[6], b[4], b[5]]

    return (loss.reshape(()), dx0.reshape(1, T, D), *leaves(0), *leaves(1), *leaves(2), *leaves(3))
```

```python
import functools

import jax
import jax.numpy as jnp
from jax import lax
from jax.experimental import pallas as pl
from jax.experimental.pallas import tpu as pltpu

F32 = jnp.float32
BF16 = jnp.bfloat16
D = 1024
DFF = 2816
SL = 704
NSL = DFF // SL
AW = 512
HD = 64
CW = 512
CK = 31
HALO = 32
MIXIN = 2560
NDEV = 8
NMOD = 9 * D
ADA_COLS = NMOD // NDEV
RMS_EPS = 1e-6
LN_EPS = 1e-5
QK_SCALE = HD ** -0.5
W_ZERO_BELOW = -104.0
ADAM_LR, ADAM_B1, ADAM_B2, ADAM_EPS, ADAM_WD, ADAM_STEP = 0.001, 0.9, 0.999, 1e-08, 0.01, 10
ADAM_C1 = 1.0 / (1.0 - ADAM_B1 ** ADAM_STEP)
ADAM_C2 = 1.0 / (1.0 - ADAM_B2 ** ADAM_STEP)
MIB = 1024 * 1024
MESH = pl.DeviceIdType.MESH

ROW_GAINS = 9
ROW_ATTN_CB = 15
ROW_LN = 16
ROW_CONVW = 17
CONVW_ROWS = 16
ROW_LOSS = 33
SMALL_R = 40


def _pcall(body, name, **kw):
    return pl.pallas_call(body, name=name, **kw)


def _cp(sem=None, vmem_mib=48):
    if sem is None:
        return pltpu.CompilerParams(vmem_limit_bytes=vmem_mib * MIB)
    return pltpu.CompilerParams(dimension_semantics=sem, vmem_limit_bytes=vmem_mib * MIB)


def _dot(a, b):
    return jnp.dot(a, b, preferred_element_type=F32)


def _dot_nt(a, b):
    return lax.dot_general(a, b, (((1,), (1,)), ((), ())), preferred_element_type=F32)


def _dot_tn(a, b):
    return lax.dot_general(a, b, (((0,), (0,)), ((), ())), preferred_element_type=F32)


def _sigmoid(x):
    return 1.0 / (1.0 + jnp.exp(-x))


def _split2(x):
    hi = x.astype(BF16)
    mid = (x - hi.astype(F32)).astype(BF16)
    return hi, mid


def _mat(ref):
    lead = len(ref.shape) - 2
    return ref[(0,) * lead] if lead else ref[...]


def _all_gather(xs, name, in_vmem):
    n = len(xs)

    def body(*refs):
        x_refs, out_refs = refs[:n], refs[n:2 * n]
        send_sems, recv_sems, local_sems = refs[2 * n:]
        mx, my, mc = lax.axis_index("x"), lax.axis_index("y"), lax.axis_index("c")
        me, sibling = (mx, my, mc), (mx, my, 1 - mc)
        chips = [(1 - mx, my), (mx, 1 - my), (1 - mx, 1 - my)]

        def slab(a, px, py, pc):
            return out_refs[a].at[4 * px + 2 * py + pc]

        def copy(a, k, block, to, src=None):
            return pltpu.make_async_remote_copy(
                src_ref=slab(a, *block) if src is None else src, dst_ref=slab(a, *block),
                send_sem=send_sems.at[a, k], recv_sem=recv_sems.at[a, k], device_id=to, device_id_type=MESH)

        mine = [pltpu.make_async_copy(x_refs[a], slab(a, *me), local_sems.at[a]) for a in range(n)]
        for cp in mine:
            cp.start()
        first = []
        for a in range(n):
            first.append(copy(a, 0, me, sibling, src=x_refs[a]))
            first += [copy(a, 1 + j, me, (*chip, mc), src=x_refs[a]) for j, chip in enumerate(chips)]
        for cp in first:
            cp.start()
        passed = []
        for j, chip in enumerate(chips):
            for a in range(n):
                copy(a, 1 + j, (*chip, mc), me).wait_recv()
                passed.append(copy(a, 4 + j, (*chip, mc), sibling))
                passed[-1].start()
        for a in range(n):
            copy(a, 0, sibling, me).wait_recv()
            for j, chip in enumerate(chips):
                copy(a, 4 + j, (*chip, 1 - mc), me).wait_recv()
        for cp in first + passed:
            cp.wait_send()
        for cp in mine:
            cp.wait()

    space = pltpu.VMEM if in_vmem else pl.ANY
    return _pcall(
        body, name,
        out_shape=[jax.ShapeDtypeStruct((NDEV,) + x.shape, x.dtype) for x in xs],
        in_specs=[pl.BlockSpec(memory_space=space)] * n,
        out_specs=[pl.BlockSpec(memory_space=space)] * n,
        scratch_shapes=[pltpu.SemaphoreType.DMA((n, 7)), pltpu.SemaphoreType.DMA((n, 7)),
                        pltpu.SemaphoreType.DMA((n,))],
    )(*xs)


def _exchange_copies(kind, src, dst, send_sems, recv_sems, local_sems):
    mx, my, mc = lax.axis_index("x"), lax.axis_index("y"), lax.axis_index("c")
    me = 4 * mx + 2 * my + mc
    n = len(src)
    pick = (lambda a, p: src[a].at[p]) if kind == "a2a" else (lambda a, p: src[a])
    mine = [pltpu.make_async_copy(pick(a, me), dst[a].at[me], local_sems.at[a]) for a in range(n)]
    copies = []
    for r in range(1, NDEV):
        px = 1 - mx if r & 4 else mx
        py = 1 - my if r & 2 else my
        pc = 1 - mc if r & 1 else mc
        for a in range(n):
            copies.append(pltpu.make_async_remote_copy(
                src_ref=pick(a, 4 * px + 2 * py + pc), dst_ref=dst[a].at[me],
                send_sem=send_sems.at[a, r - 1], recv_sem=recv_sems.at[a, r - 1],
                device_id=(px, py, pc), device_id_type=MESH))
    return mine, copies


def _hosted_call(body, name, comm, grid, in_specs, out_specs, out_shape, scratch_shapes, sem, vmem_mib, args):
    if comm is None:
        outs = _pcall(body, name, grid=grid, in_specs=in_specs, out_specs=out_specs, out_shape=out_shape,
                      scratch_shapes=scratch_shapes, compiler_params=_cp(sem, vmem_mib))(*args)
        return outs, []
    kind, arrs = comm
    nc, n_in, n_out, n_scr = len(arrs), len(in_specs), len(out_specs), len(scratch_shapes)
    rank = len(grid)

    def wrapped(*refs):
        ins, csrc = refs[:n_in], refs[n_in:n_in + nc]
        outs, cdst = refs[n_in + nc:n_in + nc + n_out], refs[n_in + nc + n_out:n_in + 2 * nc + n_out]
        rest = refs[n_in + 2 * nc + n_out:]
        scr, sems = rest[:n_scr], rest[n_scr:]
        first = functools.reduce(jnp.logical_and, [pl.program_id(d) == 0 for d in range(rank)])
        last = functools.reduce(jnp.logical_and, [pl.program_id(d) == grid[d] - 1 for d in range(rank)])

        @pl.when(first)
        def _():
            mine, copies = _exchange_copies(kind, csrc, cdst, *sems)
            for cp in mine + copies:
                cp.start()

        body(*ins, *outs, *scr)

        @pl.when(last)
        def _():
            mine, copies = _exchange_copies(kind, csrc, cdst, *sems)
            for cp in copies:
                cp.wait_recv()
            for cp in copies:
                cp.wait_send()
            for cp in mine:
                cp.wait()

    hbm = pl.BlockSpec(memory_space=pl.ANY)
    cshape = [jax.ShapeDtypeStruct(a.shape if kind == "a2a" else (NDEV,) + a.shape, a.dtype) for a in arrs]
    res = _pcall(wrapped, name, grid=grid, in_specs=list(in_specs) + [hbm] * nc,
                 out_specs=list(out_specs) + [hbm] * nc, out_shape=list(out_shape) + cshape,
                 scratch_shapes=list(scratch_shapes) + [pltpu.SemaphoreType.DMA((nc, 7)),
                                                        pltpu.SemaphoreType.DMA((nc, 7)),
                                                        pltpu.SemaphoreType.DMA((nc,))],
                 compiler_params=_cp(("arbitrary",) * rank, vmem_mib))(*args, *arrs)
    return res[:n_out], res[n_out:]


def _ada_fwd(c_all, w, b):
    n = w.shape[1]

    def body(c_ref, w_ref, b_ref, o_ref):
        c = c_ref[...]
        s = c * _sigmoid(c)
        o_ref[...] = jnp.dot(s, w_ref[...], preferred_element_type=F32, precision=lax.Precision.HIGHEST) + b_ref[...]

    return _pcall(body, "ada_fwd", out_shape=jax.ShapeDtypeStruct((NDEV, n), F32), compiler_params=_cp())(c_all, w, b)


def _ada_bwd(c_all_t, dmod):
    n = dmod.shape[1]

    def body(ct_ref, d_ref, o_ref):
        ct = ct_ref[...]
        s = ct * _sigmoid(ct)
        acc = s[:, 0:1] * d_ref[0:1, :]
        for b in range(1, NDEV):
            acc = acc + s[:, b:b + 1] * d_ref[b:b + 1, :]
        o_ref[...] = acc

    return _pcall(body, "ada_bwd", out_shape=jax.ShapeDtypeStruct((D, n), F32), compiler_params=_cp())(c_all_t, dmod)


def _pre_fwd(x, g, shift, scale, name, tb=512):
    T = x.shape[0]

    def body(x_ref, g_ref, sh_ref, sc_ref, h_ref):
        xv = x_ref[...]
        r = lax.rsqrt(jnp.mean(xv * xv, axis=-1, keepdims=True) + RMS_EPS)
        h_ref[...] = ((xv * r) * g_ref[...] * (1.0 + sc_ref[...]) + sh_ref[...]).astype(BF16)

    row = pl.BlockSpec((tb, D), lambda i: (i, 0))
    vec = pl.BlockSpec((1, D), lambda i: (0, 0))
    return _pcall(body, name, grid=(T // tb,), in_specs=[row, vec, vec, vec], out_specs=row,
                  out_shape=jax.ShapeDtypeStruct((T, D), BF16), compiler_params=_cp(("parallel",)))(x, g, shift, scale)


def _ffn_in(h, w_in, name, comm=None, tm=512):
    T = h.shape[0]

    def body(h_ref, wg_ref, wu_ref, gu_ref, a_ref):
        hv = h_ref[...]
        g = _dot(hv, wg_ref[0])
        u = _dot(hv, wu_ref[0])
        gu_ref[0, 0] = g.astype(BF16)
        gu_ref[0, 1] = u.astype(BF16)
        a_ref[0] = (g * _sigmoid(g) * u).astype(BF16)

    (gu, a), got = _hosted_call(
        body, name, comm, (NSL, T // tm),
        [pl.BlockSpec((tm, D), lambda j, i: (i, 0)), pl.BlockSpec((1, D, SL), lambda j, i: (j, 0, 0)),
         pl.BlockSpec((1, D, SL), lambda j, i: (j + NSL, 0, 0))],
        [pl.BlockSpec((1, 2, tm, SL), lambda j, i: (j, 0, i, 0)), pl.BlockSpec((1, tm, SL), lambda j, i: (j, i, 0))],
        [jax.ShapeDtypeStruct((NSL, 2, T, SL), BF16), jax.ShapeDtypeStruct((NSL, T, SL), BF16)],
        [], ("parallel", "parallel"), 48, (h, w_in, w_in))
    return gu, a, got


def _mm_post(a_list, a_specs, w_list, w_specs, x, g_post, gate, res_w, name, tm):
    T = x.shape[0]
    n = len(a_list)

    def body(*refs):
        a_refs, w_refs = refs[:n], refs[n:2 * n]
        x_ref, g_ref, gt_ref, f_ref, o_ref = refs[2 * n:]
        f = None
        for a_ref, w_ref in zip(a_refs, w_refs):
            if len(a_ref.shape) == 3:
                terms = [_dot(a_ref[j], w_ref[j]) for j in range(a_ref.shape[0])]
            else:
                terms = [_dot(a_ref[...], w_ref[...])]
            for t in terms:
                f = t if f is None else f + t
        f_ref[...] = f
        r = lax.rsqrt(jnp.mean(f * f, axis=-1, keepdims=True) + RMS_EPS)
        y = (f * r) * g_ref[...]
        o_ref[...] = x_ref[...] + (res_w * (1.0 + gt_ref[...])) * y

    row = pl.BlockSpec((tm, D), lambda i: (i, 0))
    vec = pl.BlockSpec((1, D), lambda i: (0, 0))
    return _pcall(body, name, grid=(T // tm,),
                  in_specs=list(a_specs) + list(w_specs) + [row, vec, vec], out_specs=[row, row],
                  out_shape=[jax.ShapeDtypeStruct((T, D), F32), jax.ShapeDtypeStruct((T, D), F32)],
                  compiler_params=_cp(("parallel",)))(*a_list, *w_list, x, g_post, gate)


def _post_bwd(dout, f, g_post, gate, res_w, name, tb=512):
    T = f.shape[0]

    def body(do_ref, f_ref, g_ref, gt_ref, df_ref, dgate_ref, dg_ref):
        @pl.when(pl.program_id(0) == 0)
        def _():
            dgate_ref[...] = jnp.zeros_like(dgate_ref)
            dg_ref[...] = jnp.zeros_like(dg_ref)

        do = do_ref[...]
        f = f_ref[...]
        r = lax.rsqrt(jnp.mean(f * f, axis=-1, keepdims=True) + RMS_EPS)
        fn = f * r
        dgate_ref[...] += jnp.sum((res_w * do) * (fn * g_ref[...]), axis=0, keepdims=True)
        dy = (res_w * (1.0 + gt_ref[...])) * do
        dg_ref[...] += jnp.sum(dy * fn, axis=0, keepdims=True)
        dyg = dy * g_ref[...]
        df = r * (dyg - fn * jnp.mean(dyg * fn, axis=-1, keepdims=True))
        df_ref[...] = df.astype(BF16)

    row = pl.BlockSpec((tb, D), lambda i: (i, 0))
    vec = pl.BlockSpec((1, D), lambda i: (0, 0))
    return _pcall(body, name, grid=(T // tb,), in_specs=[row, row, vec, vec], out_specs=[row, vec, vec],
                  out_shape=[jax.ShapeDtypeStruct((T, D), BF16), jax.ShapeDtypeStruct((1, D), F32),
                             jax.ShapeDtypeStruct((1, D), F32)],
                  compiler_params=_cp(("arbitrary",)))(dout, f, g_post, gate)


def _ffn_out_bwd(df, w_out4, gu, name, tm=512):
    T = df.shape[0]

    def body(df_ref, w_ref, gu_ref, dgu_ref):
        da = _dot_nt(df_ref[...], w_ref[0])
        gv = gu_ref[0, 0].astype(F32)
        s = _sigmoid(gv)
        gs = gv * s
        dgu_ref[0, 0] = (da * gu_ref[0, 1].astype(F32) * (s + gs * (1.0 - s))).astype(BF16)
        dgu_ref[0, 1] = (da * gs).astype(BF16)

    gus = pl.BlockSpec((1, 2, tm, SL), lambda j, i: (j, 0, i, 0))
    return _pcall(body, name, grid=(NSL, T // tm),
                  in_specs=[pl.BlockSpec((tm, D), lambda j, i: (i, 0)), pl.BlockSpec((1, SL, D), lambda j, i: (j, 0, 0)), gus],
                  out_specs=gus, out_shape=jax.ShapeDtypeStruct((NSL, 2, T, SL), BF16),
                  compiler_params=_cp(("parallel", "parallel")))(df, w_out4, gu)


def _mm_tn(a, a_spec, b, b_spec, out_shape, out_spec, grid, name, comm=None):
    k, nn = out_spec.block_shape[-2:]
    steps = grid[1]

    def body(a_ref, b_ref, o_ref, acc_ref):
        i = pl.program_id(1)

        @pl.when(i == 0)
        def _():
            acc_ref[...] = jnp.zeros_like(acc_ref)

        acc_ref[...] += _dot_tn(_mat(a_ref), _mat(b_ref))

        @pl.when(i == steps - 1)
        def _():
            lead = len(o_ref.shape) - 2
            o_ref[(0,) * lead if lead else ...] = acc_ref[...].astype(BF16)

    (out,), got = _hosted_call(body, name, comm, grid, [a_spec, b_spec], [out_spec],
                               [jax.ShapeDtypeStruct(out_shape, BF16)], [pltpu.VMEM((k, nn), F32)],
                               ("parallel", "arbitrary"), 48, (a, b))
    return out, got


def _mm_prebwd(a, a_spec, w, w_spec, dh_fn, x, dout, g_pre, scale, name, tm=256, comm=None):
    T = x.shape[0]

    def body(a_ref, w_ref, x_ref, do_ref, g_ref, sc_ref, dx_ref, dsh_ref, dsc_ref, dg_ref):
        @pl.when(pl.program_id(0) == 0)
        def _():
            dsh_ref[...] = jnp.zeros_like(dsh_ref)
            dsc_ref[...] = jnp.zeros_like(dsc_ref)
            dg_ref[...] = jnp.zeros_like(dg_ref)

        dh = dh_fn(a_ref, w_ref)
        xv = x_ref[...]
        r = lax.rsqrt(jnp.mean(xv * xv, axis=-1, keepdims=True) + RMS_EPS)
        xn = xv * r
        dsh_ref[...] += jnp.sum(dh, axis=0, keepdims=True)
        dsc_ref[...] += jnp.sum(dh * (xn * g_ref[...]), axis=0, keepdims=True)
        dn = dh * (1.0 + sc_ref[...])
        dg_ref[...] += jnp.sum(dn * xn, axis=0, keepdims=True)
        dng = dn * g_ref[...]
        dx_ref[...] = do_ref[...] + r * (dng - xn * jnp.mean(dng * xn, axis=-1, keepdims=True))

    row = pl.BlockSpec((tm, D), lambda i: (i, 0))
    vec = pl.BlockSpec((1, D), lambda i: (0, 0))
    return _hosted_call(body, name, comm, (T // tm,), [a_spec, w_spec, row, row, vec, vec], [row, vec, vec, vec],
                        [jax.ShapeDtypeStruct((T, D), F32)] + [jax.ShapeDtypeStruct((1, D), F32)] * 3,
                        [], ("arbitrary",), 56, (a, w, x, dout, g_pre, scale))


def _loss_head(y, tgt, name, tb=512):
    T = y.shape[0]

    def body(y_ref, t_ref, dy_ref, l_ref):
        @pl.when(pl.program_id(0) == 0)
        def _():
            l_ref[...] = jnp.zeros_like(l_ref)

        e = y_ref[...] - t_ref[...]
        dy_ref[...] = e * (1.0 / D)
        l_ref[...] += 0.5 * jnp.sum(jnp.mean(e * e, axis=-1, keepdims=True), axis=0, keepdims=True)

    row = pl.BlockSpec((tb, D), lambda i: (i, 0))
    return _pcall(body, name, grid=(T // tb,), in_specs=[row, row],
                  out_specs=[row, pl.BlockSpec((1, 1), lambda i: (0, 0))],
                  out_shape=[jax.ShapeDtypeStruct((T, D), F32), jax.ShapeDtypeStruct((1, 1), F32)],
                  compiler_params=_cp(("arbitrary",)))(y, tgt)


def _mix_in(h, w, name, tm=512):
    T = h.shape[0]

    def body(h_ref, w_ref, qkv_ref, cvg_ref):
        p = _dot(h_ref[...], w_ref[...])
        qkv_ref[...] = p[:, :3 * AW].astype(BF16)
        cvg_ref[...] = p[:, 3 * AW:]

    return _pcall(body, name, grid=(T // tm,),
                  in_specs=[pl.BlockSpec((tm, D), lambda i: (i, 0)), pl.BlockSpec((D, MIXIN), lambda i: (0, 0))],
                  out_specs=[pl.BlockSpec((tm, 3 * AW), lambda i: (i, 0)), pl.BlockSpec((tm, 2 * CW), lambda i: (i, 0))],
                  out_shape=[jax.ShapeDtypeStruct((T, 3 * AW), BF16), jax.ShapeDtypeStruct((T, 2 * CW), F32)],
                  compiler_params=_cp(("parallel",)))(h, w)


def _mm_nt(a, w, name, tm=512):
    T, K = a.shape
    N = w.shape[0]

    def body(a_ref, w_ref, o_ref):
        o_ref[...] = _dot_nt(a_ref[...], w_ref[...])

    return _pcall(body, name, grid=(T // tm,),
                  in_specs=[pl.BlockSpec((tm, K), lambda i: (i, 0)), pl.BlockSpec((N, K), lambda i: (0, 0))],
                  out_specs=pl.BlockSpec((tm, N), lambda i: (i, 0)),
                  out_shape=jax.ShapeDtypeStruct((T, N), F32), compiler_params=_cp(("parallel",)))(a, w)


def _softplus_parts(z):
    l = jnp.log(1.0 + jnp.exp(-jnp.abs(z)))
    return jnp.minimum(z, 0.0) - l, jnp.minimum(-z, 0.0) - l


def _head_sum(x, first):
    sa = jnp.sum(jnp.where(first, x, 0.0), axis=-1, keepdims=True)
    sb = jnp.sum(jnp.where(first, 0.0, x), axis=-1, keepdims=True)
    return jnp.where(first, sa, sb)


def _attn_specs(T, tq):
    qs = pl.BlockSpec((tq, 128), lambda p, i: (i, p))
    ks = pl.BlockSpec((T, 128), lambda p, i: (0, 4 + p))
    vs = pl.BlockSpec((T, 128), lambda p, i: (0, 8 + p))
    gs = pl.BlockSpec((1, 128), lambda p, i: (0, p))
    return qs, ks, vs, gs


def _attn_fwd(qkv, g_attn, tq, comm=None):
    T = qkv.shape[0]

    def body(q_ref, k_ref, v_ref, g_ref, o_ref, an_ref):
        i = pl.program_id(1)
        first = lax.broadcasted_iota(jnp.int32, (tq, 128), 1) < HD
        q = q_ref[...]
        zq = jnp.zeros_like(q)
        qs = (jnp.where(first, q, zq), jnp.where(first, zq, q))
        rows = lax.broadcasted_iota(jnp.int32, (tq, tq), 0)
        cols = lax.broadcasted_iota(jnp.int32, (tq, tq), 1)
        tri = (rows > cols).astype(BF16)
        strict = cols < rows

        def tile(j, Rs, acc, masked):
            start = pl.multiple_of(j * tq, tq)
            kb = k_ref[pl.ds(start, tq), :]
            vb = v_ref[pl.ds(start, tq), :]
            outs, new_r = [], []
            for hh in range(2):
                z = _dot_nt(qs[hh], kb) * QK_SCALE
                ls, lsm = _softplus_parts(z)
                if masked:
                    lsm = jnp.where(strict, lsm, 0.0)
                hi, mid = _split2(lsm)
                after = _dot(hi, tri) + _dot(mid, tri)
                w = jnp.exp(ls + after + Rs[hh])
                if masked:
                    w = jnp.where(strict, w, 0.0)
                outs.append(_dot(w.astype(BF16), vb))
                new_r.append(Rs[hh] + after[:, 0:1] + lsm[:, 0:1])
            return new_r[0], new_r[1], acc + jnp.where(first, outs[0], outs[1])

        zr = jnp.zeros((tq, 1), F32)
        ra, rb, acc = tile(i, (zr, zr), jnp.zeros((tq, 128), F32), True)

        def more(c):
            return jnp.logical_and(c[0] < i, jnp.maximum(jnp.max(c[1]), jnp.max(c[2])) > W_ZERO_BELOW)

        def step(c):
            ra, rb, acc = tile(i - 1 - c[0], (c[1], c[2]), c[3], False)
            return c[0] + 1, ra, rb, acc

        _, ra, rb, acc = lax.while_loop(more, step, (jnp.int32(0), ra, rb, acc))
        o_ref[...] = acc
        r = lax.rsqrt(_head_sum(acc * acc, first) * (1.0 / HD) + RMS_EPS)
        an_ref[...] = ((acc * r) * g_ref[...]).astype(BF16)

    qs, ks, vs, gs = _attn_specs(T, tq)
    (o, an), got = _hosted_call(body, "attn_fwd", comm, (AW // 128, T // tq), [qs, ks, vs, gs], [qs, qs],
                                [jax.ShapeDtypeStruct((T, AW), F32), jax.ShapeDtypeStruct((T, AW), BF16)],
                                [], ("parallel", "parallel"), 48, (qkv, qkv, qkv, g_attn))
    return o, an, got


def _attn_bwd(qkv, o, dcat, g_attn, tq):
    T = qkv.shape[0]

    def body(q_ref, k_ref, v_ref, o_ref, dan_ref, g_ref, dq_ref, dk_ref, dv_ref, dg_ref):
        i = pl.program_id(1)

        @pl.when(i == 0)
        def _():
            dk_ref[...] = jnp.zeros_like(dk_ref)
            dv_ref[...] = jnp.zeros_like(dv_ref)
            dg_ref[...] = jnp.zeros_like(dg_ref)

        first = lax.broadcasted_iota(jnp.int32, (tq, 128), 1) < HD
        q = q_ref[...]
        zq = jnp.zeros_like(q)
        qs = (jnp.where(first, q, zq), jnp.where(first, zq, q))
        o = o_ref[...]
        dan = dan_ref[...]
        r = lax.rsqrt(_head_sum(o * o, first) * (1.0 / HD) + RMS_EPS)
        on = o * r
        dg_ref[...] += jnp.sum(dan * on, axis=0, keepdims=True)
        dyg = dan * g_ref[...]
        dO = r * (dyg - on * (_head_sum(dyg * on, first) * (1.0 / HD)))
        dOb = dO.astype(BF16)
        dOs = (jnp.where(first, dOb, zq), jnp.where(first, zq, dOb))
        ones = jnp.ones((8, 128), BF16)
        Ds = []
        for hh in range(2):
            prod = dOs[hh].astype(F32) * o
            p1 = prod.astype(BF16)
            rem = prod - p1.astype(F32)
            p2 = rem.astype(BF16)
            p3 = (rem - p2.astype(F32)).astype(BF16)
            Ds.append((_dot_nt(ones, p1) + _dot_nt(ones, p2) + _dot_nt(ones, p3))[0:1, :])

        rows = lax.broadcasted_iota(jnp.int32, (tq, tq), 0)
        cols = lax.broadcasted_iota(jnp.int32, (tq, tq), 1)
        tri_after = (cols > rows).astype(BF16)
        tri_incl = (cols >= rows).astype(BF16)
        strict = rows < cols

        def tile(j, Rs, Gs, dq, masked):
            start = pl.multiple_of(j * tq, tq)
            kb = k_ref[pl.ds(start, tq), :]
            vb = v_ref[pl.ds(start, tq), :]
            new_r, new_g = [], []
            dkp = jnp.zeros((tq, 128), F32)
            dvp = jnp.zeros((tq, 128), F32)
            for hh in range(2):
                z = _dot_nt(kb, qs[hh]) * QK_SCALE
                ls, lsm = _softplus_parts(z)
                if masked:
                    lsm = jnp.where(strict, lsm, 0.0)
                hi, mid = _split2(lsm)
                after = _dot(tri_after, hi) + _dot(tri_after, mid)
                w = jnp.exp(ls + after + Rs[hh])
                if masked:
                    w = jnp.where(strict, w, 0.0)
                wb = w.astype(BF16)
                dlw = _dot_nt(vb, dOs[hh]) * wb.astype(F32)
                hi2, mid2 = _split2(dlw)
                C = _dot(tri_incl, hi2) + _dot(tri_incl, mid2)
                dlsm = Ds[hh] - Gs[hh] - C
                if masked:
                    dlsm = jnp.where(strict, dlsm, 0.0)
                p = jnp.exp(ls)
                dz = ((dlw * (1.0 - p) - dlsm * p) * QK_SCALE).astype(BF16)
                dkp = dkp + _dot(dz, qs[hh])
                dvp = dvp + _dot(wb, dOs[hh])
                dqh = _dot_tn(dz, kb)
                dq = dq + (jnp.where(first, dqh, 0.0) if hh == 0 else jnp.where(first, 0.0, dqh))
                new_r.append(Rs[hh] + after[0:1, :] + lsm[0:1, :])
                new_g.append(Gs[hh] + C[0:1, :])
            dk_ref[pl.ds(start, tq), :] += dkp
            dv_ref[pl.ds(start, tq), :] += dvp
            return new_r[0], new_r[1], new_g[0], new_g[1], dq

        zrow = jnp.zeros((1, tq), F32)
        st = tile(i, (zrow, zrow), (zrow, zrow), jnp.zeros((tq, 128), F32), True)

        def more(c):
            return jnp.logical_and(c[0] < i, jnp.maximum(jnp.max(c[1]), jnp.max(c[2])) > W_ZERO_BELOW)

        def step(c):
            return (c[0] + 1,) + tile(i - 1 - c[0], (c[1], c[2]), (c[3], c[4]), c[5], False)

        out = lax.while_loop(more, step, (jnp.int32(0),) + st)
        dq_ref[...] = out[5].astype(BF16)

    qs, ks, vs, gs = _attn_specs(T, tq)
    kacc = pl.BlockSpec((T, 128), lambda p, i: (0, p))
    return _pcall(body, "attn_bwd", grid=(AW // 128, T // tq), in_specs=[qs, ks, vs, qs, qs, gs],
                  out_specs=[qs, kacc, kacc, gs],
                  out_shape=[jax.ShapeDtypeStruct((T, AW), BF16), jax.ShapeDtypeStruct((T, AW), F32),
                             jax.ShapeDtypeStruct((T, AW), F32), jax.ShapeDtypeStruct((1, AW), F32)],
                  compiler_params=_cp(("parallel", "arbitrary")))(qkv, qkv, qkv, o, dcat, g_attn)


def _conv_fwd(cvg, conv_w, conv_b, ln_g, ln_b, tb=512):
    T = cvg.shape[0]
    hb = tb // HALO

    def body(cv_ref, cg_ref, cvp_ref, cgp_ref, w_ref, b_ref, g_ref, be_ref, u0_ref, u1_ref, u3_ref, pad_ref):
        i = pl.program_id(0)
        u0 = cv_ref[...] * _sigmoid(cg_ref[...])
        prev = cvp_ref[...] * _sigmoid(cgp_ref[...])
        pad_ref[0:HALO, :] = jnp.where(i > 0, prev, 0.0)
        pad_ref[HALO:HALO + tb, :] = u0
        u0_ref[...] = u0
        acc = jnp.zeros((tb, CW), F32) + b_ref[...]
        for kk in range(CK):
            off = HALO - (CK - 1) + kk
            acc = acc + w_ref[kk:kk + 1, :] * pad_ref[off:off + tb, :]
        u1_ref[...] = acc
        mu = jnp.mean(acc, axis=-1, keepdims=True)
        xc = acc - mu
        var = jnp.mean(xc * xc, axis=-1, keepdims=True)
        u2 = (xc * lax.rsqrt(var + LN_EPS)) * g_ref[...] + be_ref[...]
        u3_ref[...] = (u2 * _sigmoid(u2)).astype(BF16)

    cur = lambda col: pl.BlockSpec((tb, CW), lambda i: (i, col))
    prv = lambda col: pl.BlockSpec((HALO, CW), lambda i: (jnp.maximum(i * hb - 1, 0), col))
    vec = pl.BlockSpec((1, CW), lambda i: (0, 0))
    out = pl.BlockSpec((tb, CW), lambda i: (i, 0))
    return _pcall(body, "conv_fwd", grid=(T // tb,),
                  in_specs=[cur(0), cur(1), prv(0), prv(1), pl.BlockSpec((HALO, CW), lambda i: (0, 0)), vec, vec, vec],
                  out_specs=[out, out, out],
                  out_shape=[jax.ShapeDtypeStruct((T, CW), F32), jax.ShapeDtypeStruct((T, CW), F32),
                             jax.ShapeDtypeStruct((T, CW), BF16)],
                  scratch_shapes=[pltpu.VMEM((tb + HALO, CW), F32)],
                  compiler_params=_cp(("parallel",)))(cvg, cvg, cvg, cvg, conv_w, conv_b, ln_g, ln_b)


def _conv_bwd1(dcat, u1, u0, ln_g, ln_b, tb=512):
    T = u1.shape[0]
    hb = tb // HALO

    def body(d3_ref, u1_ref, u0_ref, u0p_ref, g_ref, be_ref, du1_ref, dw_ref, db_ref, dlg_ref, dlb_ref, pad_ref):
        i = pl.program_id(0)

        @pl.when(i == 0)
        def _():
            dw_ref[...] = jnp.zeros_like(dw_ref)
            db_ref[...] = jnp.zeros_like(db_ref)
            dlg_ref[...] = jnp.zeros_like(dlg_ref)
            dlb_ref[...] = jnp.zeros_like(dlb_ref)

        u1 = u1_ref[...]
        mu = jnp.mean(u1, axis=-1, keepdims=True)
        xc = u1 - mu
        rstd = lax.rsqrt(jnp.mean(xc * xc, axis=-1, keepdims=True) + LN_EPS)
        xh = xc * rstd
        u2 = xh * g_ref[...] + be_ref[...]
        s = _sigmoid(u2)
        du2 = d3_ref[...] * (s + u2 * s * (1.0 - s))
        dlg_ref[...] += jnp.sum(du2 * xh, axis=0, keepdims=True)
        dlb_ref[...] += jnp.sum(du2, axis=0, keepdims=True)
        dxh = du2 * g_ref[...]
        du1 = rstd * (dxh - jnp.mean(dxh, axis=-1, keepdims=True) - xh * jnp.mean(dxh * xh, axis=-1, keepdims=True))
        du1_ref[...] = du1
        db_ref[...] += jnp.sum(du1, axis=0, keepdims=True)
        pad_ref[0:HALO, :] = jnp.where(i > 0, u0p_ref[...], 0.0)
        pad_ref[HALO:HALO + tb, :] = u0_ref[...]
        for kk in range(CK):
            off = HALO - (CK - 1) + kk
            dw_ref[kk:kk + 1, :] += jnp.sum(du1 * pad_ref[off:off + tb, :], axis=0, keepdims=True)

    cur = pl.BlockSpec((tb, CW), lambda i: (i, 0))
    vec = pl.BlockSpec((1, CW), lambda i: (0, 0))
    return _pcall(body, "conv_bwd1", grid=(T // tb,),
                  in_specs=[pl.BlockSpec((tb, CW), lambda i: (i, 1)), cur, cur,
                            pl.BlockSpec((HALO, CW), lambda i: (jnp.maximum(i * hb - 1, 0), 0)), vec, vec],
                  out_specs=[cur, pl.BlockSpec((HALO, CW), lambda i: (0, 0)), vec, vec, vec],
                  out_shape=[jax.ShapeDtypeStruct((T, CW), F32), jax.ShapeDtypeStruct((HALO, CW), F32)]
                  + [jax.ShapeDtypeStruct((1, CW), F32)] * 3,
                  scratch_shapes=[pltpu.VMEM((tb + HALO, CW), F32)],
                  compiler_params=_cp(("arbitrary",)))(dcat, u1, u0, u0, ln_g, ln_b)


def _conv_bwd2(du1, cvg, conv_w, tb=512):
    T = du1.shape[0]
    hb = tb // HALO
    last = T // HALO - 1
    nblk = T // tb

    def body(d_ref, dn_ref, cv_ref, cg_ref, w_ref, o_ref, pad_ref):
        i = pl.program_id(0)
        pad_ref[0:tb, :] = d_ref[...]
        pad_ref[tb:tb + HALO, :] = jnp.where(i < nblk - 1, dn_ref[...], 0.0)
        acc = jnp.zeros((tb, CW), F32)
        for kk in range(CK):
            off = CK - 1 - kk
            acc = acc + w_ref[kk:kk + 1, :] * pad_ref[off:off + tb, :]
        sg = _sigmoid(cg_ref[...])
        o_ref[:, 0:CW] = (acc * sg).astype(BF16)
        o_ref[:, CW:2 * CW] = (acc * cv_ref[...] * sg * (1.0 - sg)).astype(BF16)

    cur = pl.BlockSpec((tb, CW), lambda i: (i, 0))
    return _pcall(body, "conv_bwd2", grid=(nblk,),
                  in_specs=[cur, pl.BlockSpec((HALO, CW), lambda i: (jnp.minimum((i + 1) * hb, last), 0)),
                            pl.BlockSpec((tb, CW), lambda i: (i, 0)), pl.BlockSpec((tb, CW), lambda i: (i, 1)),
                            pl.BlockSpec((HALO, CW), lambda i: (0, 0))],
                  out_specs=pl.BlockSpec((tb, 2 * CW), lambda i: (i, 0)),
                  out_shape=jax.ShapeDtypeStruct((T, 2 * CW), BF16),
                  scratch_shapes=[pltpu.VMEM((tb + HALO, CW), F32)],
                  compiler_params=_cp(("parallel",)))(du1, du1, cvg, cvg, conv_w)


def _adam_math(w, g, m, v):
    nm = ADAM_B1 * m + (1.0 - ADAM_B1) * g
    nv = ADAM_B2 * v + (1.0 - ADAM_B2) * (g * g)
    delta = -ADAM_LR * ((nm * ADAM_C1) / (jnp.sqrt(nv * ADAM_C2) + ADAM_EPS) + ADAM_WD * w)
    return delta, nm, nv


def _adamw(w, gslots, m, v, name, tb):
    R, C = w.shape
    S = gslots.shape[0]

    def body(w_ref, gs_ref, m_ref, v_ref, g_ref, d_ref, nm_ref, nv_ref):
        g = gs_ref[0].astype(F32)
        for s in range(1, S):
            g = g + gs_ref[s].astype(F32)
        g_ref[...] = g
        d_ref[...], nm_ref[...], nv_ref[...] = _adam_math(w_ref[...], g, m_ref[...], v_ref[...])

    blk = pl.BlockSpec((tb, C), lambda i: (i, 0))
    return _pcall(body, name, grid=(R // tb,),
                  in_specs=[blk, pl.BlockSpec((S, tb, C), lambda i: (0, i, 0)), blk, blk],
                  out_specs=[blk] * 4, out_shape=[jax.ShapeDtypeStruct((R, C), F32)] * 4,
                  compiler_params=_cp(("parallel",)))(w, gslots, m, v)


def _adamw_small(gall, gattn, gconvw, ws, ms, vs):
    n = len(ws)

    def body(*refs):
        gall_ref, gattn_ref, gconvw_ref = refs[:3]
        w_refs, m_refs, v_refs = refs[3:3 + n], refs[3 + n:3 + 2 * n], refs[3 + 2 * n:3 + 3 * n]
        loss_ref = refs[3 + 3 * n]
        outs = refs[4 + 3 * n:]
        g_refs, d_refs, nm_refs, nv_refs = outs[:n], outs[n:2 * n], outs[2 * n:3 * n], outs[3 * n:]

        def total(ref):
            t = ref[0]
            for dev in range(1, NDEV):
                t = t + ref[dev]
            return t

        tot = total(gall_ref)
        grads = [tot[0:9, :]] + [tot[ROW_GAINS + k:ROW_GAINS + k + 1, :] for k in range(6)]
        grads += [total(gattn_ref), tot[ROW_ATTN_CB:ROW_ATTN_CB + 1, CW:2 * CW], tot[ROW_LN:ROW_LN + 1, 0:CW],
                  tot[ROW_LN:ROW_LN + 1, CW:2 * CW], total(gconvw_ref)]
        loss_ref[...] = tot[ROW_LOSS:ROW_LOSS + 1, 0:1]
        for k in range(n):
            g_refs[k][...] = grads[k]
            d_refs[k][...], nm_refs[k][...], nv_refs[k][...] = _adam_math(w_refs[k][...], grads[k], m_refs[k][...],
                                                                          v_refs[k][...])

    shapes = [jax.ShapeDtypeStruct(w.shape, F32) for w in ws]
    res = _pcall(body, "adamw_small", out_shape=[jax.ShapeDtypeStruct((1, 1), F32)] + shapes * 4,
                 compiler_params=_cp())(gall, gattn, gconvw, *ws, *ms, *vs)
    return res[0], [res[1 + k * n:1 + (k + 1) * n] for k in range(4)]


def _ffn_fwd(x, g_pre, g_post, shift, scale, gate, w_in, w_out4, tag, tm, comm=None):
    h = _pre_fwd(x, g_pre, shift, scale, "pre_fwd_" + tag)
    gu, a, got = _ffn_in(h, w_in, "ffn_in_" + tag, comm)
    f, out = _mm_post([a], [pl.BlockSpec((NSL, tm, SL), lambda i: (0, i, 0))],
                      [w_out4], [pl.BlockSpec((NSL, SL, D), lambda i: (0, 0, 0))],
                      x, g_post, gate, 0.5, "ffn_out_" + tag, tm)
    return out, (x, h, gu, a, f), got


def _ffn_bwd(dout, saved, g_pre, g_post, scale, gate, w_in, w_out4, tag, tm, tmb):
    x, h, gu, a, f = saved
    T = x.shape[0]
    df, dgate, dg_post = _post_bwd(dout, f, g_post, gate, 0.5, "post_bwd_" + tag)
    dgu = _ffn_out_bwd(df, w_out4, gu, "ffn_out_bwd_" + tag)
    dw_out, _ = _mm_tn(a, pl.BlockSpec((1, tm, SL), lambda j, i: (j, i, 0)),
                       df, pl.BlockSpec((tm, D), lambda j, i: (i, 0)),
                       (NSL, SL, D), pl.BlockSpec((1, SL, D), lambda j, i: (j, 0, 0)), (NSL, T // tm), "dw_out_" + tag)
    dw_in, (r_out,) = _mm_tn(h, pl.BlockSpec((tm, D), lambda s, i: (i, 0)),
                             dgu, pl.BlockSpec((1, 1, tm, SL), lambda s, i: (s % NSL, s // NSL, i, 0)),
                             (NDEV, D, SL), pl.BlockSpec((1, D, SL), lambda s, i: (s, 0, 0)), (NDEV, T // tm),
                             "dw_in_" + tag, comm=("a2a", [dw_out.reshape(NDEV, SL // 2, D)]))

    def dh_fn(a_ref, w_ref):
        dh = None
        for p in range(2):
            for j in range(NSL):
                t = _dot_nt(a_ref[j, p], w_ref[NSL * p + j])
                dh = t if dh is None else dh + t
        return dh

    (dx, dshift, dscale, dg_pre), (r_in,) = _mm_prebwd(
        dgu, pl.BlockSpec((NSL, 2, tmb, SL), lambda i: (0, 0, i, 0)),
        w_in, pl.BlockSpec((NDEV, D, SL), lambda i: (0, 0, 0)),
        dh_fn, x, dout, g_pre, scale, "ffn_in_bwd_" + tag, tmb, comm=("a2a", [dw_in]))
    return dx, r_in, r_out, dg_pre, dg_post, (dshift, dscale, dgate)


def kernel(x, c, w_ada, b_ada, g_pre_ff1, g_post_ff1, ff1_w_in, ff1_w_out, g_pre_mix, g_post_mix, w_in_mix, g_attn_out, conv_w, conv_b, conv_ln_g, conv_ln_b, w_out_mix, g_pre_ff2, g_post_ff2, ff2_w_in, ff2_w_out, loss_target, m_w_ada, m_b_ada, m_g_pre_ff1, m_g_post_ff1, m_ff1_w_in, m_ff1_w_out, m_g_pre_mix, m_g_post_mix, m_w_in_mix, m_g_attn_out, m_conv_w, m_conv_b, m_conv_ln_g, m_conv_ln_b, m_w_out_mix, m_g_pre_ff2, m_g_post_ff2, m_ff2_w_in, m_ff2_w_out, v_w_ada, v_b_ada, v_g_pre_ff1, v_g_post_ff1, v_ff1_w_in, v_ff1_w_out, v_g_pre_mix, v_g_post_mix, v_w_in_mix, v_g_attn_out, v_conv_w, v_conv_b, v_conv_ln_g, v_conv_ln_b, v_w_out_mix, v_g_pre_ff2, v_g_post_ff2, v_ff2_w_in, v_ff2_w_out):
    me = 4 * lax.axis_index("x") + 2 * lax.axis_index("y") + lax.axis_index("c")
    T = x.shape[1]
    tq = min(256, T)
    tm = 512
    tmb = 256
    x0 = x.reshape(T, D)
    tgt = loss_target.reshape(T, D)
    row = lambda a: a.reshape(1, -1)

    small_in = jnp.concatenate([c.reshape(-1), jnp.pad(conv_w.reshape(-1), (0, 2 * D - CK * 64)),
                                jnp.zeros((5 * D,), F32)]).reshape(8, D)
    small_all, = _all_gather([small_in], "gather_c_convw", True)
    c_all = small_all[:, 0, :]
    conv_w_full = small_all[:, 1:3, :].reshape(NDEV, 2 * D)[:, :CK * 64].reshape(NDEV, CK, 64)
    conv_w_full = conv_w_full.transpose(1, 0, 2).reshape(CK, CW)
    conv_w_pad = jnp.pad(conv_w_full, ((0, HALO - CK), (0, 0)))

    big = [ff1_w_in, ff1_w_out, w_in_mix, w_out_mix, ff2_w_in, ff2_w_out]
    shards = [w.astype(BF16) for w in big]
    w_in1, w_out1 = _all_gather(shards[0:2], "gather_weights_ff1", False)
    w_out1_4 = w_out1.reshape(NSL, SL, D)

    b_cols = lax.dynamic_slice(b_ada, (me * ADA_COLS,), (ADA_COLS,)).reshape(1, ADA_COLS)
    mod_cols = _ada_fwd(c_all, w_ada, b_cols)
    mod_all, = _all_gather([mod_cols], "gather_mod", True)
    mod = lax.dynamic_slice(mod_all, (0, me, 0), (NDEV, 1, ADA_COLS)).reshape(9, D)
    sh = lambda s: mod[3 * s:3 * s + 1]
    sc = lambda s: mod[3 * s + 1:3 * s + 2]
    gt = lambda s: mod[3 * s + 2:3 * s + 3]

    x1, sv1, (w_inm_s, w_outm_s) = _ffn_fwd(x0, row(g_pre_ff1), row(g_post_ff1), sh(0), sc(0), gt(0), w_in1, w_out1_4,
                                            "ff1", tm, comm=("gather", shards[2:4]))
    w_inm = w_inm_s.transpose(1, 0, 2).reshape(D, MIXIN)
    w_outm = w_outm_s.reshape(D, D)
    hm = _pre_fwd(x1, row(g_pre_mix), sh(1), sc(1), "pre_fwd_mix")
    qkv, cvg = _mix_in(hm, w_inm, "mix_in")
    g_attn_row = row(g_attn_out)
    o_att, an, (w_in2, w_out2) = _attn_fwd(qkv, g_attn_row, tq, comm=("gather", shards[4:6]))
    w_out2_4 = w_out2.reshape(NSL, SL, D)
    u0, u1, u3 = _conv_fwd(cvg, conv_w_pad, row(conv_b), row(conv_ln_g), row(conv_ln_b))
    half = lambda k: pl.BlockSpec((AW, D), lambda i: (k, 0))
    act = pl.BlockSpec((tm, AW), lambda i: (i, 0))
    fm, x2 = _mm_post([an, u3], [act, act], [w_outm, w_outm], [half(0), half(1)],
                      x1, row(g_post_mix), gt(1), 1.0, "mix_out", tm)
    x3, sv2, _ = _ffn_fwd(x2, row(g_pre_ff2), row(g_post_ff2), sh(2), sc(2), gt(2), w_in2, w_out2_4, "ff2", tm)
    dy, loss_part = _loss_head(x3, tgt, "loss_head")

    dx2, r_in2, r_out2, dgpre2, dgpost2, dmod2 = _ffn_bwd(
        dy, sv2, row(g_pre_ff2), row(g_post_ff2), sc(2), gt(2), w_in2, w_out2_4, "ff2", tm, tmb)

    dfm, dgate1, dgpostm = _post_bwd(dx2, fm, row(g_post_mix), gt(1), 1.0, "post_bwd_mix")
    dcat = _mm_nt(dfm, w_outm, "mix_out_bwd")
    tok = pl.BlockSpec((tm, AW), lambda j, i: (i, 0))
    tokd = pl.BlockSpec((tm, D), lambda j, i: (i, 0))
    whole = pl.BlockSpec((AW, D), lambda j, i: (0, 0))
    dw_outm = jnp.concatenate([_mm_tn(an, tok, dfm, tokd, (AW, D), whole, (1, T // tm), "dw_out_mix_a")[0],
                               _mm_tn(u3, tok, dfm, tokd, (AW, D), whole, (1, T // tm), "dw_out_mix_c")[0]], axis=0)
    dq, dk, dv, dg_attn = _attn_bwd(qkv, o_att, dcat, g_attn_row, tq)
    du1, dconv_w, dconv_b, dln_g, dln_b = _conv_bwd1(dcat, u1, u0, row(conv_ln_g), row(conv_ln_b))
    dcvg = _conv_bwd2(du1, cvg, conv_w_pad)
    dproj = jnp.concatenate([dq, dk.astype(BF16), dv.astype(BF16), dcvg], axis=1)
    dw_inm, _ = _mm_tn(hm, pl.BlockSpec((tm, D), lambda j, i: (i, 0)),
                       dproj, pl.BlockSpec((tm, MIXIN // 2), lambda j, i: (i, j)),
                       (D, MIXIN), pl.BlockSpec((D, MIXIN // 2), lambda j, i: (0, j)), (2, T // tm), "dw_in_mix")
    (dx1, dshift1, dscale1, dgprem), (r_inm, r_outm) = _mm_prebwd(
        dproj, pl.BlockSpec((tmb, MIXIN), lambda i: (i, 0)), w_inm, pl.BlockSpec((D, MIXIN), lambda i: (0, 0)),
        lambda a_ref, w_ref: _dot_nt(a_ref[...], w_ref[...]), x1, dx2, row(g_pre_mix), sc(1), "mix_in_bwd", tmb,
        comm=("a2a", [dw_inm.reshape(D, NDEV, 320).transpose(1, 0, 2), dw_outm.reshape(NDEV, 128, D)]))

    dx0, r_in1, r_out1, dgpre1, dgpost1, dmod0 = _ffn_bwd(
        dx1, sv1, row(g_pre_ff1), row(g_post_ff1), sc(0), gt(0), w_in1, w_out1_4, "ff1", tm, tmb)
    recvs = [r_in1, r_out1, r_inm, r_outm, r_in2, r_out2]

    zrow = jnp.zeros((1, D), F32)
    small_g = jnp.concatenate(
        list(dmod0) + [dshift1, dscale1, dgate1] + list(dmod2)
        + [dgpre1, dgpost1, dgprem, dgpostm, dgpre2, dgpost2]
        + [jnp.concatenate([dg_attn, dconv_b], axis=1), jnp.concatenate([dln_g, dln_b], axis=1),
           jnp.pad(dconv_w[:CK].reshape(-1), (0, CONVW_ROWS * D - CK * CW)).reshape(CONVW_ROWS, D),
           jnp.pad(loss_part, ((0, 0), (0, D - 1)))] + [zrow] * (SMALL_R - ROW_LOSS - 1), axis=0)
    small_g_all, = _all_gather([small_g], "gather_small_grads", True)

    dmod_all = small_g_all[:, 0:9, :].reshape(NDEV, NMOD)
    dmod_cols = lax.dynamic_slice(dmod_all, (0, me * ADA_COLS), (NDEV, ADA_COLS))
    g_w_ada = _ada_bwd(c_all.T, dmod_cols)

    gattn = small_g_all[:, ROW_ATTN_CB, 0:AW].reshape(NDEV, 8, HD)
    gconvw = small_g_all[:, ROW_CONVW:ROW_CONVW + CONVW_ROWS, :].reshape(NDEV, CONVW_ROWS * D)[:, :CK * CW]
    gconvw = lax.dynamic_slice(gconvw.reshape(NDEV, CK, CW), (0, 0, me * 64), (NDEV, CK, 64))

    def small_list(b, g6, ga, cb, lg, lb, cw):
        return [b.reshape(9, D)] + [row(g) for g in g6] + [ga, row(cb), row(lg), row(lb), cw]

    sw = small_list(b_ada, [g_pre_ff1, g_post_ff1, g_pre_mix, g_post_mix, g_pre_ff2, g_post_ff2], g_attn_out,
                    conv_b, conv_ln_g, conv_ln_b, conv_w)
    sm = small_list(m_b_ada, [m_g_pre_ff1, m_g_post_ff1, m_g_pre_mix, m_g_post_mix, m_g_pre_ff2, m_g_post_ff2],
                    m_g_attn_out, m_conv_b, m_conv_ln_g, m_conv_ln_b, m_conv_w)
    sv = small_list(v_b_ada, [v_g_pre_ff1, v_g_post_ff1, v_g_pre_mix, v_g_post_mix, v_g_pre_ff2, v_g_post_ff2],
                    v_g_attn_out, v_conv_b, v_conv_ln_g, v_conv_ln_b, v_conv_w)
    loss, s_out = _adamw_small(small_g_all, gattn, gconvw, sw, sm, sv)
    s_out = [[o.reshape(w.shape) for o, w in zip(outs, [b_ada, g_pre_ff1, g_post_ff1, g_pre_mix, g_post_mix,
                                                        g_pre_ff2, g_post_ff2, g_attn_out, conv_b, conv_ln_g,
                                                        conv_ln_b, conv_w])] for outs in s_out]

    big_m = [m_ff1_w_in, m_ff1_w_out, m_w_in_mix, m_w_out_mix, m_ff2_w_in, m_ff2_w_out]
    big_v = [v_ff1_w_in, v_ff1_w_out, v_w_in_mix, v_w_out_mix, v_ff2_w_in, v_ff2_w_out]
    tbs = [256, 176, 256, 128, 256, 176]
    tags = ["ff1_w_in", "ff1_w_out", "w_in_mix", "w_out_mix", "ff2_w_in", "ff2_w_out"]
    b_out = [_adamw(big[k], recvs[k], big_m[k], big_v[k], "adamw_" + tags[k], tbs[k]) for k in range(6)]
    a_out = _adamw(w_ada, g_w_ada.reshape(1, D, ADA_COLS), m_w_ada, v_w_ada, "adamw_ada", 256)

    def leaves(k):
        s = s_out[k]
        b = [o[k] for o in b_out]
        return [a_out[k], s[0], s[1], s[2], b[0], b[1], s[3], s[4], b[2], s[7], s[11], s[8], s[9], s[10], b[3],
                s[5], s[6], b[4], b[5]]

    return (loss.reshape(()), dx0.reshape(1, T, D), *leaves(0), *leaves(1), *leaves(2), *leaves(3))
```

```python
import functools

import jax
import jax.numpy as jnp
from jax import lax
from jax.experimental import pallas as pl
from jax.experimental.pallas import tpu as pltpu

F32 = jnp.float32
BF16 = jnp.bfloat16
D = 1024
DFF = 2816
SL = 704
NSL = DFF // SL
AW = 512
HD = 64
CW = 512
CK = 31
HALO = 32
MIXIN = 2560
NDEV = 8
NMOD = 9 * D
ADA_COLS = NMOD // NDEV
RMS_EPS = 1e-6
LN_EPS = 1e-5
QK_SCALE = HD ** -0.5
W_ZERO_BELOW = -104.0
ADAM_LR, ADAM_B1, ADAM_B2, ADAM_EPS, ADAM_WD, ADAM_STEP = 0.001, 0.9, 0.999, 1e-08, 0.01, 10
ADAM_C1 = 1.0 / (1.0 - ADAM_B1 ** ADAM_STEP)
ADAM_C2 = 1.0 / (1.0 - ADAM_B2 ** ADAM_STEP)
MIB = 1024 * 1024
MESH = pl.DeviceIdType.MESH

ROW_GAINS = 9
ROW_ATTN_CB = 15
ROW_LN = 16
ROW_CONVW = 17
CONVW_ROWS = 16
ROW_LOSS = 33
SMALL_R = 40


def _pcall(body, name, **kw):
    return pl.pallas_call(body, name=name, **kw)


def _cp(sem=None, vmem_mib=48):
    if sem is None:
        return pltpu.CompilerParams(vmem_limit_bytes=vmem_mib * MIB)
    return pltpu.CompilerParams(dimension_semantics=sem, vmem_limit_bytes=vmem_mib * MIB)


def _dot(a, b):
    return jnp.dot(a, b, preferred_element_type=F32)


def _dot_nt(a, b):
    return lax.dot_general(a, b, (((1,), (1,)), ((), ())), preferred_element_type=F32)


def _dot_tn(a, b):
    return lax.dot_general(a, b, (((0,), (0,)), ((), ())), preferred_element_type=F32)


def _sigmoid(x):
    return 0.5 * jnp.tanh(0.5 * x) + 0.5


def _split2(x):
    hi = x.astype(BF16)
    mid = (x - hi.astype(F32)).astype(BF16)
    return hi, mid


def _mat(ref):
    lead = len(ref.shape) - 2
    return ref[(0,) * lead] if lead else ref[...]


def _all_gather(xs, name, in_vmem):
    n = len(xs)

    def body(*refs):
        x_refs, out_refs = refs[:n], refs[n:2 * n]
        send_sems, recv_sems, local_sems = refs[2 * n:]
        mx, my, mc = lax.axis_index("x"), lax.axis_index("y"), lax.axis_index("c")
        me, sibling = (mx, my, mc), (mx, my, 1 - mc)
        chips = [(1 - mx, my), (mx, 1 - my), (1 - mx, 1 - my)]

        def slab(a, px, py, pc):
            return out_refs[a].at[4 * px + 2 * py + pc]

        def copy(a, k, block, to, src=None):
            return pltpu.make_async_remote_copy(
                src_ref=slab(a, *block) if src is None else src, dst_ref=slab(a, *block),
                send_sem=send_sems.at[a, k], recv_sem=recv_sems.at[a, k], device_id=to, device_id_type=MESH)

        mine = [pltpu.make_async_copy(x_refs[a], slab(a, *me), local_sems.at[a]) for a in range(n)]
        for cp in mine:
            cp.start()
        first = []
        for a in range(n):
            first.append(copy(a, 0, me, sibling, src=x_refs[a]))
            first += [copy(a, 1 + j, me, (*chip, mc), src=x_refs[a]) for j, chip in enumerate(chips)]
        for cp in first:
            cp.start()
        passed = []
        for j, chip in enumerate(chips):
            for a in range(n):
                copy(a, 1 + j, (*chip, mc), me).wait_recv()
                passed.append(copy(a, 4 + j, (*chip, mc), sibling))
                passed[-1].start()
        for a in range(n):
            copy(a, 0, sibling, me).wait_recv()
            for j, chip in enumerate(chips):
                copy(a, 4 + j, (*chip, 1 - mc), me).wait_recv()
        for cp in first + passed:
            cp.wait_send()
        for cp in mine:
            cp.wait()

    space = pltpu.VMEM if in_vmem else pl.ANY
    return _pcall(
        body, name,
        out_shape=[jax.ShapeDtypeStruct((NDEV,) + x.shape, x.dtype) for x in xs],
        in_specs=[pl.BlockSpec(memory_space=space)] * n,
        out_specs=[pl.BlockSpec(memory_space=space)] * n,
        scratch_shapes=[pltpu.SemaphoreType.DMA((n, 7)), pltpu.SemaphoreType.DMA((n, 7)),
                        pltpu.SemaphoreType.DMA((n,))],
    )(*xs)


def _exchange_copies(kind, src, dst, send_sems, recv_sems, local_sems):
    mx, my, mc = lax.axis_index("x"), lax.axis_index("y"), lax.axis_index("c")
    me = 4 * mx + 2 * my + mc
    n = len(src)
    pick = (lambda a, p: src[a].at[p]) if kind == "a2a" else (lambda a, p: src[a])
    mine = [pltpu.make_async_copy(pick(a, me), dst[a].at[me], local_sems.at[a]) for a in range(n)]
    copies = []
    for r in range(1, NDEV):
        px = 1 - mx if r & 4 else mx
        py = 1 - my if r & 2 else my
        pc = 1 - mc if r & 1 else mc
        for a in range(n):
            copies.append(pltpu.make_async_remote_copy(
                src_ref=pick(a, 4 * px + 2 * py + pc), dst_ref=dst[a].at[me],
                send_sem=send_sems.at[a, r - 1], recv_sem=recv_sems.at[a, r - 1],
                device_id=(px, py, pc), device_id_type=MESH))
    return mine, copies


def _hosted_call(body, name, comm, grid, in_specs, out_specs, out_shape, scratch_shapes, sem, vmem_mib, args):
    if comm is None:
        outs = _pcall(body, name, grid=grid, in_specs=in_specs, out_specs=out_specs, out_shape=out_shape,
                      scratch_shapes=scratch_shapes, compiler_params=_cp(sem, vmem_mib))(*args)
        return outs, []
    kind, arrs = comm
    nc, n_in, n_out, n_scr = len(arrs), len(in_specs), len(out_specs), len(scratch_shapes)
    rank = len(grid)

    def wrapped(*refs):
        ins, csrc = refs[:n_in], refs[n_in:n_in + nc]
        outs, cdst = refs[n_in + nc:n_in + nc + n_out], refs[n_in + nc + n_out:n_in + 2 * nc + n_out]
        rest = refs[n_in + 2 * nc + n_out:]
        scr, sems = rest[:n_scr], rest[n_scr:]
        first = functools.reduce(jnp.logical_and, [pl.program_id(d) == 0 for d in range(rank)])
        last = functools.reduce(jnp.logical_and, [pl.program_id(d) == grid[d] - 1 for d in range(rank)])

        @pl.when(first)
        def _():
            mine, copies = _exchange_copies(kind, csrc, cdst, *sems)
            for cp in mine + copies:
                cp.start()

        body(*ins, *outs, *scr)

        @pl.when(last)
        def _():
            mine, copies = _exchange_copies(kind, csrc, cdst, *sems)
            for cp in copies:
                cp.wait_recv()
            for cp in copies:
                cp.wait_send()
            for cp in mine:
                cp.wait()

    hbm = pl.BlockSpec(memory_space=pl.ANY)
    cshape = [jax.ShapeDtypeStruct(a.shape if kind == "a2a" else (NDEV,) + a.shape, a.dtype) for a in arrs]
    res = _pcall(wrapped, name, grid=grid, in_specs=list(in_specs) + [hbm] * nc,
                 out_specs=list(out_specs) + [hbm] * nc, out_shape=list(out_shape) + cshape,
                 scratch_shapes=list(scratch_shapes) + [pltpu.SemaphoreType.DMA((nc, 7)),
                                                        pltpu.SemaphoreType.DMA((nc, 7)),
                                                        pltpu.SemaphoreType.DMA((nc,))],
                 compiler_params=_cp(("arbitrary",) * rank, vmem_mib))(*args, *arrs)
    return res[:n_out], res[n_out:]


def _ada_fwd(c_all, w, b):
    n = w.shape[1]

    def body(c_ref, w_ref, b_ref, o_ref):
        c = c_ref[...]
        s = c * _sigmoid(c)
        o_ref[...] = jnp.dot(s, w_ref[...], preferred_element_type=F32, precision=lax.Precision.HIGHEST) + b_ref[...]

    return _pcall(body, "ada_fwd", out_shape=jax.ShapeDtypeStruct((NDEV, n), F32), compiler_params=_cp())(c_all, w, b)


def _ada_bwd(c_all_t, dmod):
    n = dmod.shape[1]

    def body(ct_ref, d_ref, o_ref):
        ct = ct_ref[...]
        s = ct * _sigmoid(ct)
        acc = s[:, 0:1] * d_ref[0:1, :]
        for b in range(1, NDEV):
            acc = acc + s[:, b:b + 1] * d_ref[b:b + 1, :]
        o_ref[...] = acc

    return _pcall(body, "ada_bwd", out_shape=jax.ShapeDtypeStruct((D, n), F32), compiler_params=_cp())(c_all_t, dmod)


def _pre_fwd(x, g, shift, scale, name, tb=512):
    T = x.shape[0]

    def body(x_ref, g_ref, sh_ref, sc_ref, h_ref):
        xv = x_ref[...]
        r = lax.rsqrt(jnp.mean(xv * xv, axis=-1, keepdims=True) + RMS_EPS)
        h_ref[...] = ((xv * r) * g_ref[...] * (1.0 + sc_ref[...]) + sh_ref[...]).astype(BF16)

    row = pl.BlockSpec((tb, D), lambda i: (i, 0))
    vec = pl.BlockSpec((1, D), lambda i: (0, 0))
    return _pcall(body, name, grid=(T // tb,), in_specs=[row, vec, vec, vec], out_specs=row,
                  out_shape=jax.ShapeDtypeStruct((T, D), BF16), compiler_params=_cp(("parallel",)))(x, g, shift, scale)


def _ffn_in(h, w_in, name, comm=None, tm=512):
    T = h.shape[0]

    def body(h_ref, wg_ref, wu_ref, gu_ref, a_ref):
        hv = h_ref[...]
        g = _dot(hv, wg_ref[0])
        u = _dot(hv, wu_ref[0])
        gu_ref[0, 0] = g.astype(BF16)
        gu_ref[0, 1] = u.astype(BF16)
        a_ref[0] = (g * _sigmoid(g) * u).astype(BF16)

    (gu, a), got = _hosted_call(
        body, name, comm, (NSL, T // tm),
        [pl.BlockSpec((tm, D), lambda j, i: (i, 0)), pl.BlockSpec((1, D, SL), lambda j, i: (j, 0, 0)),
         pl.BlockSpec((1, D, SL), lambda j, i: (j + NSL, 0, 0))],
        [pl.BlockSpec((1, 2, tm, SL), lambda j, i: (j, 0, i, 0)), pl.BlockSpec((1, tm, SL), lambda j, i: (j, i, 0))],
        [jax.ShapeDtypeStruct((NSL, 2, T, SL), BF16), jax.ShapeDtypeStruct((NSL, T, SL), BF16)],
        [], ("parallel", "parallel"), 48, (h, w_in, w_in))
    return gu, a, got


def _mm_post(a_list, a_specs, w_list, w_specs, x, g_post, gate, res_w, name, tm):
    T = x.shape[0]
    n = len(a_list)

    def body(*refs):
        a_refs, w_refs = refs[:n], refs[n:2 * n]
        x_ref, g_ref, gt_ref, f_ref, o_ref = refs[2 * n:]
        f = None
        for a_ref, w_ref in zip(a_refs, w_refs):
            if len(a_ref.shape) == 3:
                terms = [_dot(a_ref[j], w_ref[j]) for j in range(a_ref.shape[0])]
            else:
                terms = [_dot(a_ref[...], w_ref[...])]
            for t in terms:
                f = t if f is None else f + t
        f_ref[...] = f
        r = lax.rsqrt(jnp.mean(f * f, axis=-1, keepdims=True) + RMS_EPS)
        y = (f * r) * g_ref[...]
        o_ref[...] = x_ref[...] + (res_w * (1.0 + gt_ref[...])) * y

    row = pl.BlockSpec((tm, D), lambda i: (i, 0))
    vec = pl.BlockSpec((1, D), lambda i: (0, 0))
    return _pcall(body, name, grid=(T // tm,),
                  in_specs=list(a_specs) + list(w_specs) + [row, vec, vec], out_specs=[row, row],
                  out_shape=[jax.ShapeDtypeStruct((T, D), F32), jax.ShapeDtypeStruct((T, D), F32)],
                  compiler_params=_cp(("parallel",)))(*a_list, *w_list, x, g_post, gate)


def _post_bwd(dout, f, g_post, gate, res_w, name, tb=512):
    T = f.shape[0]

    def body(do_ref, f_ref, g_ref, gt_ref, df_ref, dgate_ref, dg_ref):
        @pl.when(pl.program_id(0) == 0)
        def _():
            dgate_ref[...] = jnp.zeros_like(dgate_ref)
            dg_ref[...] = jnp.zeros_like(dg_ref)

        do = do_ref[...]
        f = f_ref[...]
        r = lax.rsqrt(jnp.mean(f * f, axis=-1, keepdims=True) + RMS_EPS)
        fn = f * r
        dgate_ref[...] += jnp.sum((res_w * do) * (fn * g_ref[...]), axis=0, keepdims=True)
        dy = (res_w * (1.0 + gt_ref[...])) * do
        dg_ref[...] += jnp.sum(dy * fn, axis=0, keepdims=True)
        dyg = dy * g_ref[...]
        df = r * (dyg - fn * jnp.mean(dyg * fn, axis=-1, keepdims=True))
        df_ref[...] = df.astype(BF16)

    row = pl.BlockSpec((tb, D), lambda i: (i, 0))
    vec = pl.BlockSpec((1, D), lambda i: (0, 0))
    return _pcall(body, name, grid=(T // tb,), in_specs=[row, row, vec, vec], out_specs=[row, vec, vec],
                  out_shape=[jax.ShapeDtypeStruct((T, D), BF16), jax.ShapeDtypeStruct((1, D), F32),
                             jax.ShapeDtypeStruct((1, D), F32)],
                  compiler_params=_cp(("arbitrary",)))(dout, f, g_post, gate)


def _ffn_out_bwd(df, w_out4, gu, name, tm=512):
    T = df.shape[0]

    def body(df_ref, w_ref, gu_ref, dgu_ref):
        da = _dot_nt(df_ref[...], w_ref[0])
        gv = gu_ref[0, 0].astype(F32)
        s = _sigmoid(gv)
        gs = gv * s
        dgu_ref[0, 0] = (da * gu_ref[0, 1].astype(F32) * (s + gs * (1.0 - s))).astype(BF16)
        dgu_ref[0, 1] = (da * gs).astype(BF16)

    gus = pl.BlockSpec((1, 2, tm, SL), lambda j, i: (j, 0, i, 0))
    return _pcall(body, name, grid=(NSL, T // tm),
                  in_specs=[pl.BlockSpec((tm, D), lambda j, i: (i, 0)), pl.BlockSpec((1, SL, D), lambda j, i: (j, 0, 0)), gus],
                  out_specs=gus, out_shape=jax.ShapeDtypeStruct((NSL, 2, T, SL), BF16),
                  compiler_params=_cp(("parallel", "parallel")))(df, w_out4, gu)


def _mm_tn(a, a_spec, b, b_spec, out_shape, out_spec, grid, name, comm=None):
    k, nn = out_spec.block_shape[-2:]
    steps = grid[1]

    def body(a_ref, b_ref, o_ref, acc_ref):
        i = pl.program_id(1)

        @pl.when(i == 0)
        def _():
            acc_ref[...] = jnp.zeros_like(acc_ref)

        acc_ref[...] += _dot_tn(_mat(a_ref), _mat(b_ref))

        @pl.when(i == steps - 1)
        def _():
            lead = len(o_ref.shape) - 2
            o_ref[(0,) * lead if lead else ...] = acc_ref[...].astype(BF16)

    (out,), got = _hosted_call(body, name, comm, grid, [a_spec, b_spec], [out_spec],
                               [jax.ShapeDtypeStruct(out_shape, BF16)], [pltpu.VMEM((k, nn), F32)],
                               ("parallel", "arbitrary"), 48, (a, b))
    return out, got


def _mm_prebwd(a, a_spec, w, w_spec, dh_fn, x, dout, g_pre, scale, name, tm=256, comm=None):
    T = x.shape[0]

    def body(a_ref, w_ref, x_ref, do_ref, g_ref, sc_ref, dx_ref, dsh_ref, dsc_ref, dg_ref):
        @pl.when(pl.program_id(0) == 0)
        def _():
            dsh_ref[...] = jnp.zeros_like(dsh_ref)
            dsc_ref[...] = jnp.zeros_like(dsc_ref)
            dg_ref[...] = jnp.zeros_like(dg_ref)

        dh = dh_fn(a_ref, w_ref)
        xv = x_ref[...]
        r = lax.rsqrt(jnp.mean(xv * xv, axis=-1, keepdims=True) + RMS_EPS)
        xn = xv * r
        dsh_ref[...] += jnp.sum(dh, axis=0, keepdims=True)
        dsc_ref[...] += jnp.sum(dh * (xn * g_ref[...]), axis=0, keepdims=True)
        dn = dh * (1.0 + sc_ref[...])
        dg_ref[...] += jnp.sum(dn * xn, axis=0, keepdims=True)
        dng = dn * g_ref[...]
        dx_ref[...] = do_ref[...] + r * (dng - xn * jnp.mean(dng * xn, axis=-1, keepdims=True))

    row = pl.BlockSpec((tm, D), lambda i: (i, 0))
    vec = pl.BlockSpec((1, D), lambda i: (0, 0))
    return _hosted_call(body, name, comm, (T // tm,), [a_spec, w_spec, row, row, vec, vec], [row, vec, vec, vec],
                        [jax.ShapeDtypeStruct((T, D), F32)] + [jax.ShapeDtypeStruct((1, D), F32)] * 3,
                        [], ("arbitrary",), 56, (a, w, x, dout, g_pre, scale))


def _loss_head(y, tgt, name, tb=512):
    T = y.shape[0]

    def body(y_ref, t_ref, dy_ref, l_ref):
        @pl.when(pl.program_id(0) == 0)
        def _():
            l_ref[...] = jnp.zeros_like(l_ref)

        e = y_ref[...] - t_ref[...]
        dy_ref[...] = e * (1.0 / D)
        l_ref[...] += 0.5 * jnp.sum(jnp.mean(e * e, axis=-1, keepdims=True), axis=0, keepdims=True)

    row = pl.BlockSpec((tb, D), lambda i: (i, 0))
    return _pcall(body, name, grid=(T // tb,), in_specs=[row, row],
                  out_specs=[row, pl.BlockSpec((1, 1), lambda i: (0, 0))],
                  out_shape=[jax.ShapeDtypeStruct((T, D), F32), jax.ShapeDtypeStruct((1, 1), F32)],
                  compiler_params=_cp(("arbitrary",)))(y, tgt)


def _mix_in(h, w, name, tm=512):
    T = h.shape[0]

    def body(h_ref, w_ref, qkv_ref, cvg_ref):
        p = _dot(h_ref[...], w_ref[...])
        qkv_ref[...] = p[:, :3 * AW].astype(BF16)
        cvg_ref[...] = p[:, 3 * AW:]

    return _pcall(body, name, grid=(T // tm,),
                  in_specs=[pl.BlockSpec((tm, D), lambda i: (i, 0)), pl.BlockSpec((D, MIXIN), lambda i: (0, 0))],
                  out_specs=[pl.BlockSpec((tm, 3 * AW), lambda i: (i, 0)), pl.BlockSpec((tm, 2 * CW), lambda i: (i, 0))],
                  out_shape=[jax.ShapeDtypeStruct((T, 3 * AW), BF16), jax.ShapeDtypeStruct((T, 2 * CW), F32)],
                  compiler_params=_cp(("parallel",)))(h, w)


def _mm_nt(a, w, name, tm=512):
    T, K = a.shape
    N = w.shape[0]

    def body(a_ref, w_ref, o_ref):
        o_ref[...] = _dot_nt(a_ref[...], w_ref[...])

    return _pcall(body, name, grid=(T // tm,),
                  in_specs=[pl.BlockSpec((tm, K), lambda i: (i, 0)), pl.BlockSpec((N, K), lambda i: (0, 0))],
                  out_specs=pl.BlockSpec((tm, N), lambda i: (i, 0)),
                  out_shape=jax.ShapeDtypeStruct((T, N), F32), compiler_params=_cp(("parallel",)))(a, w)


def _softplus_parts(z):
    l = jnp.log(1.0 + jnp.exp(-jnp.abs(z)))
    return jnp.minimum(z, 0.0) - l, jnp.minimum(-z, 0.0) - l


def _head_sum(x, first):
    sa = jnp.sum(jnp.where(first, x, 0.0), axis=-1, keepdims=True)
    sb = jnp.sum(jnp.where(first, 0.0, x), axis=-1, keepdims=True)
    return jnp.where(first, sa, sb)


def _attn_specs(T, tq):
    qs = pl.BlockSpec((tq, 128), lambda p, i: (i, p))
    ks = pl.BlockSpec((T, 128), lambda p, i: (0, 4 + p))
    vs = pl.BlockSpec((T, 128), lambda p, i: (0, 8 + p))
    gs = pl.BlockSpec((1, 128), lambda p, i: (0, p))
    return qs, ks, vs, gs


def _attn_fwd(qkv, g_attn, tq, comm=None):
    T = qkv.shape[0]

    def body(q_ref, k_ref, v_ref, g_ref, o_ref, an_ref):
        i = pl.program_id(1)
        first = lax.broadcasted_iota(jnp.int32, (tq, 128), 1) < HD
        q = q_ref[...]
        zq = jnp.zeros_like(q)
        qs = (jnp.where(first, q, zq), jnp.where(first, zq, q))
        rows = lax.broadcasted_iota(jnp.int32, (tq, tq), 0)
        cols = lax.broadcasted_iota(jnp.int32, (tq, tq), 1)
        tri = (rows > cols).astype(BF16)
        strict = cols < rows

        def tile(j, Rs, acc, masked):
            start = j * tq if isinstance(j, int) else pl.multiple_of(j * tq, tq)
            kb = k_ref[pl.ds(start, tq), :]
            vb = v_ref[pl.ds(start, tq), :]
            outs, new_r = [], []
            for hh in range(2):
                z = _dot_nt(qs[hh], kb) * QK_SCALE
                ls, lsm = _softplus_parts(z)
                if masked:
                    lsm = jnp.where(strict, lsm, 0.0)
                hi, mid = _split2(lsm)
                after = _dot(hi, tri) + _dot(mid, tri)
                w = jnp.exp(ls + after + Rs[hh])
                if masked:
                    w = jnp.where(strict, w, 0.0)
                outs.append(_dot(w.astype(BF16), vb))
                new_r.append(Rs[hh] + after[:, 0:1] + lsm[:, 0:1])
            return new_r[0], new_r[1], acc + jnp.where(first, outs[0], outs[1])

        zr = jnp.zeros((tq, 1), F32)

        def finish(acc):
            o_ref[...] = acc
            r = lax.rsqrt(_head_sum(acc * acc, first) * (1.0 / HD) + RMS_EPS)
            an_ref[...] = ((acc * r) * g_ref[...]).astype(BF16)

        @pl.when(i == 0)
        def _():
            finish(tile(0, (zr, zr), jnp.zeros((tq, 128), F32), True)[2])

        @pl.when(i > 0)
        def _():
            ra, rb, acc = tile(i, (zr, zr), jnp.zeros((tq, 128), F32), True)
            ra, rb, acc = tile(i - 1, (ra, rb), acc, False)

            def more(c):
                return jnp.logical_and(c[0] < i, jnp.maximum(jnp.max(c[1]), jnp.max(c[2])) > W_ZERO_BELOW)

            def step(c):
                ra, rb, acc = tile(i - 1 - c[0], (c[1], c[2]), c[3], False)
                return c[0] + 1, ra, rb, acc

            finish(lax.while_loop(more, step, (jnp.int32(1), ra, rb, acc))[3])

    qs, ks, vs, gs = _attn_specs(T, tq)
    (o, an), got = _hosted_call(body, "attn_fwd", comm, (AW // 128, T // tq), [qs, ks, vs, gs], [qs, qs],
                                [jax.ShapeDtypeStruct((T, AW), F32), jax.ShapeDtypeStruct((T, AW), BF16)],
                                [], ("parallel", "parallel"), 48, (qkv, qkv, qkv, g_attn))
    return o, an, got


def _attn_bwd(qkv, o, dcat, g_attn, tq):
    T = qkv.shape[0]

    def body(q_ref, k_ref, v_ref, o_ref, dan_ref, g_ref, dq_ref, dk_ref, dv_ref, dg_ref):
        i = pl.program_id(1)

        @pl.when(i == 0)
        def _():
            dk_ref[...] = jnp.zeros_like(dk_ref)
            dv_ref[...] = jnp.zeros_like(dv_ref)
            dg_ref[...] = jnp.zeros_like(dg_ref)

        first = lax.broadcasted_iota(jnp.int32, (tq, 128), 1) < HD
        q = q_ref[...]
        zq = jnp.zeros_like(q)
        qs = (jnp.where(first, q, zq), jnp.where(first, zq, q))
        o = o_ref[...]
        dan = dan_ref[...]
        r = lax.rsqrt(_head_sum(o * o, first) * (1.0 / HD) + RMS_EPS)
        on = o * r
        dg_ref[...] += jnp.sum(dan * on, axis=0, keepdims=True)
        dyg = dan * g_ref[...]
        dO = r * (dyg - on * (_head_sum(dyg * on, first) * (1.0 / HD)))
        dOb = dO.astype(BF16)
        dOs = (jnp.where(first, dOb, zq), jnp.where(first, zq, dOb))
        ones = jnp.ones((8, 128), BF16)
        Ds = []
        for hh in range(2):
            prod = dOs[hh].astype(F32) * o
            p1 = prod.astype(BF16)
            rem = prod - p1.astype(F32)
            p2 = rem.astype(BF16)
            p3 = (rem - p2.astype(F32)).astype(BF16)
            Ds.append((_dot_nt(ones, p1) + _dot_nt(ones, p2) + _dot_nt(ones, p3))[0:1, :])

        rows = lax.broadcasted_iota(jnp.int32, (tq, tq), 0)
        cols = lax.broadcasted_iota(jnp.int32, (tq, tq), 1)
        tri_after = (cols > rows).astype(BF16)
        tri_incl = (cols >= rows).astype(BF16)
        strict = rows < cols

        def tile(j, Rs, Gs, dq, masked):
            start = j * tq if isinstance(j, int) else pl.multiple_of(j * tq, tq)
            kb = k_ref[pl.ds(start, tq), :]
            vb = v_ref[pl.ds(start, tq), :]
            new_r, new_g = [], []
            dkp = jnp.zeros((tq, 128), F32)
            dvp = jnp.zeros((tq, 128), F32)
            for hh in range(2):
                z = _dot_nt(kb, qs[hh]) * QK_SCALE
                ls, lsm = _softplus_parts(z)
                if masked:
                    lsm = jnp.where(strict, lsm, 0.0)
                hi, mid = _split2(lsm)
                after = _dot(tri_after, hi) + _dot(tri_after, mid)
                w = jnp.exp(ls + after + Rs[hh])
                if masked:
                    w = jnp.where(strict, w, 0.0)
                wb = w.astype(BF16)
                dlw = _dot_nt(vb, dOs[hh]) * wb.astype(F32)
                hi2, mid2 = _split2(dlw)
                C = _dot(tri_incl, hi2) + _dot(tri_incl, mid2)
                dlsm = Ds[hh] - Gs[hh] - C
                if masked:
                    dlsm = jnp.where(strict, dlsm, 0.0)
                p = jnp.exp(ls)
                dz = ((dlw * (1.0 - p) - dlsm * p) * QK_SCALE).astype(BF16)
                dkp = dkp + _dot(dz, qs[hh])
                dvp = dvp + _dot(wb, dOs[hh])
                dqh = _dot_tn(dz, kb)
                dq = dq + (jnp.where(first, dqh, 0.0) if hh == 0 else jnp.where(first, 0.0, dqh))
                new_r.append(Rs[hh] + after[0:1, :] + lsm[0:1, :])
                new_g.append(Gs[hh] + C[0:1, :])
            dk_ref[pl.ds(start, tq), :] += dkp
            dv_ref[pl.ds(start, tq), :] += dvp
            return new_r[0], new_r[1], new_g[0], new_g[1], dq

        zrow = jnp.zeros((1, tq), F32)

        @pl.when(i == 0)
        def _():
            dq_ref[...] = tile(0, (zrow, zrow), (zrow, zrow), jnp.zeros((tq, 128), F32), True)[4].astype(BF16)

        @pl.when(i > 0)
        def _():
            st = tile(i, (zrow, zrow), (zrow, zrow), jnp.zeros((tq, 128), F32), True)
            st = tile(i - 1, st[0:2], st[2:4], st[4], False)

            def more(c):
                return jnp.logical_and(c[0] < i, jnp.maximum(jnp.max(c[1]), jnp.max(c[2])) > W_ZERO_BELOW)

            def step(c):
                return (c[0] + 1,) + tile(i - 1 - c[0], (c[1], c[2]), (c[3], c[4]), c[5], False)

            dq_ref[...] = lax.while_loop(more, step, (jnp.int32(1),) + st)[5].astype(BF16)

    qs, ks, vs, gs = _attn_specs(T, tq)
    kacc = pl.BlockSpec((T, 128), lambda p, i: (0, p))
    return _pcall(body, "attn_bwd", grid=(AW // 128, T // tq), in_specs=[qs, ks, vs, qs, qs, gs],
                  out_specs=[qs, kacc, kacc, gs],
                  out_shape=[jax.ShapeDtypeStruct((T, AW), BF16), jax.ShapeDtypeStruct((T, AW), F32),
                             jax.ShapeDtypeStruct((T, AW), F32), jax.ShapeDtypeStruct((1, AW), F32)],
                  compiler_params=_cp(("parallel", "arbitrary")))(qkv, qkv, qkv, o, dcat, g_attn)


def _conv_fwd(cvg, conv_w, conv_b, ln_g, ln_b, tb=512):
    T = cvg.shape[0]
    hb = tb // HALO

    def body(cv_ref, cg_ref, cvp_ref, cgp_ref, w_ref, b_ref, g_ref, be_ref, u0_ref, u1_ref, u3_ref, pad_ref):
        i = pl.program_id(0)
        u0 = cv_ref[...] * _sigmoid(cg_ref[...])
        prev = cvp_ref[...] * _sigmoid(cgp_ref[...])
        pad_ref[0:HALO, :] = jnp.where(i > 0, prev, 0.0)
        pad_ref[HALO:HALO + tb, :] = u0
        u0_ref[...] = u0
        acc = jnp.zeros((tb, CW), F32) + b_ref[...]
        for kk in range(CK):
            off = HALO - (CK - 1) + kk
            acc = acc + w_ref[kk:kk + 1, :] * pad_ref[off:off + tb, :]
        u1_ref[...] = acc
        mu = jnp.mean(acc, axis=-1, keepdims=True)
        xc = acc - mu
        var = jnp.mean(xc * xc, axis=-1, keepdims=True)
        u2 = (xc * lax.rsqrt(var + LN_EPS)) * g_ref[...] + be_ref[...]
        u3_ref[...] = (u2 * _sigmoid(u2)).astype(BF16)

    cur = lambda col: pl.BlockSpec((tb, CW), lambda i: (i, col))
    prv = lambda col: pl.BlockSpec((HALO, CW), lambda i: (jnp.maximum(i * hb - 1, 0), col))
    vec = pl.BlockSpec((1, CW), lambda i: (0, 0))
    out = pl.BlockSpec((tb, CW), lambda i: (i, 0))
    return _pcall(body, "conv_fwd", grid=(T // tb,),
                  in_specs=[cur(0), cur(1), prv(0), prv(1), pl.BlockSpec((HALO, CW), lambda i: (0, 0)), vec, vec, vec],
                  out_specs=[out, out, out],
                  out_shape=[jax.ShapeDtypeStruct((T, CW), F32), jax.ShapeDtypeStruct((T, CW), F32),
                             jax.ShapeDtypeStruct((T, CW), BF16)],
                  scratch_shapes=[pltpu.VMEM((tb + HALO, CW), F32)],
                  compiler_params=_cp(("parallel",)))(cvg, cvg, cvg, cvg, conv_w, conv_b, ln_g, ln_b)


def _conv_bwd1(dcat, u1, u0, ln_g, ln_b, tb=512):
    T = u1.shape[0]
    hb = tb // HALO

    def body(d3_ref, u1_ref, u0_ref, u0p_ref, g_ref, be_ref, du1_ref, dw_ref, db_ref, dlg_ref, dlb_ref, pad_ref):
        i = pl.program_id(0)

        @pl.when(i == 0)
        def _():
            dw_ref[...] = jnp.zeros_like(dw_ref)
            db_ref[...] = jnp.zeros_like(db_ref)
            dlg_ref[...] = jnp.zeros_like(dlg_ref)
            dlb_ref[...] = jnp.zeros_like(dlb_ref)

        u1 = u1_ref[...]
        mu = jnp.mean(u1, axis=-1, keepdims=True)
        xc = u1 - mu
        rstd = lax.rsqrt(jnp.mean(xc * xc, axis=-1, keepdims=True) + LN_EPS)
        xh = xc * rstd
        u2 = xh * g_ref[...] + be_ref[...]
        s = _sigmoid(u2)
        du2 = d3_ref[...] * (s + u2 * s * (1.0 - s))
        dlg_ref[...] += jnp.sum(du2 * xh, axis=0, keepdims=True)
        dlb_ref[...] += jnp.sum(du2, axis=0, keepdims=True)
        dxh = du2 * g_ref[...]
        du1 = rstd * (dxh - jnp.mean(dxh, axis=-1, keepdims=True) - xh * jnp.mean(dxh * xh, axis=-1, keepdims=True))
        du1_ref[...] = du1
        db_ref[...] += jnp.sum(du1, axis=0, keepdims=True)
        pad_ref[0:HALO, :] = jnp.where(i > 0, u0p_ref[...], 0.0)
        pad_ref[HALO:HALO + tb, :] = u0_ref[...]
        for kk in range(CK):
            off = HALO - (CK - 1) + kk
            dw_ref[kk:kk + 1, :] += jnp.sum(du1 * pad_ref[off:off + tb, :], axis=0, keepdims=True)

    cur = pl.BlockSpec((tb, CW), lambda i: (i, 0))
    vec = pl.BlockSpec((1, CW), lambda i: (0, 0))
    return _pcall(body, "conv_bwd1", grid=(T // tb,),
                  in_specs=[pl.BlockSpec((tb, CW), lambda i: (i, 1)), cur, cur,
                            pl.BlockSpec((HALO, CW), lambda i: (jnp.maximum(i * hb - 1, 0), 0)), vec, vec],
                  out_specs=[cur, pl.BlockSpec((HALO, CW), lambda i: (0, 0)), vec, vec, vec],
                  out_shape=[jax.ShapeDtypeStruct((T, CW), F32), jax.ShapeDtypeStruct((HALO, CW), F32)]
                  + [jax.ShapeDtypeStruct((1, CW), F32)] * 3,
                  scratch_shapes=[pltpu.VMEM((tb + HALO, CW), F32)],
                  compiler_params=_cp(("arbitrary",)))(dcat, u1, u0, u0, ln_g, ln_b)


def _conv_bwd2(du1, cvg, conv_w, tb=512):
    T = du1.shape[0]
    hb = tb // HALO
    last = T // HALO - 1
    nblk = T // tb

    def body(d_ref, dn_ref, cv_ref, cg_ref, w_ref, o_ref, pad_ref):
        i = pl.program_id(0)
        pad_ref[0:tb, :] = d_ref[...]
        pad_ref[tb:tb + HALO, :] = jnp.where(i < nblk - 1, dn_ref[...], 0.0)
        acc = jnp.zeros((tb, CW), F32)
        for kk in range(CK):
            off = CK - 1 - kk
            acc = acc + w_ref[kk:kk + 1, :] * pad_ref[off:off + tb, :]
        sg = _sigmoid(cg_ref[...])
        o_ref[:, 0:CW] = (acc * sg).astype(BF16)
        o_ref[:, CW:2 * CW] = (acc * cv_ref[...] * sg * (1.0 - sg)).astype(BF16)

    cur = pl.BlockSpec((tb, CW), lambda i: (i, 0))
    return _pcall(body, "conv_bwd2", grid=(nblk,),
                  in_specs=[cur, pl.BlockSpec((HALO, CW), lambda i: (jnp.minimum((i + 1) * hb, last), 0)),
                            pl.BlockSpec((tb, CW), lambda i: (i, 0)), pl.BlockSpec((tb, CW), lambda i: (i, 1)),
                            pl.BlockSpec((HALO, CW), lambda i: (0, 0))],
                  out_specs=pl.BlockSpec((tb, 2 * CW), lambda i: (i, 0)),
                  out_shape=jax.ShapeDtypeStruct((T, 2 * CW), BF16),
                  scratch_shapes=[pltpu.VMEM((tb + HALO, CW), F32)],
                  compiler_params=_cp(("parallel",)))(du1, du1, cvg, cvg, conv_w)


def _adam_math(w, g, m, v):
    nm = ADAM_B1 * m + (1.0 - ADAM_B1) * g
    nv = ADAM_B2 * v + (1.0 - ADAM_B2) * (g * g)
    delta = -ADAM_LR * ((nm * ADAM_C1) / (jnp.sqrt(nv * ADAM_C2) + ADAM_EPS) + ADAM_WD * w)
    return delta, nm, nv


def _adamw(w, gslots, m, v, name, tb):
    R, C = w.shape
    S = gslots.shape[0]

    def body(w_ref, gs_ref, m_ref, v_ref, g_ref, d_ref, nm_ref, nv_ref):
        g = gs_ref[0].astype(F32)
        for s in range(1, S):
            g = g + gs_ref[s].astype(F32)
        g_ref[...] = g
        d_ref[...], nm_ref[...], nv_ref[...] = _adam_math(w_ref[...], g, m_ref[...], v_ref[...])

    blk = pl.BlockSpec((tb, C), lambda i: (i, 0))
    return _pcall(body, name, grid=(R // tb,),
                  in_specs=[blk, pl.BlockSpec((S, tb, C), lambda i: (0, i, 0)), blk, blk],
                  out_specs=[blk] * 4, out_shape=[jax.ShapeDtypeStruct((R, C), F32)] * 4,
                  compiler_params=_cp(("parallel",)))(w, gslots, m, v)


def _adamw_small(gall, gattn, gconvw, ws, ms, vs):
    n = len(ws)

    def body(*refs):
        gall_ref, gattn_ref, gconvw_ref = refs[:3]
        w_refs, m_refs, v_refs = refs[3:3 + n], refs[3 + n:3 + 2 * n], refs[3 + 2 * n:3 + 3 * n]
        loss_ref = refs[3 + 3 * n]
        outs = refs[4 + 3 * n:]
        g_refs, d_refs, nm_refs, nv_refs = outs[:n], outs[n:2 * n], outs[2 * n:3 * n], outs[3 * n:]

        def total(ref):
            t = ref[0]
            for dev in range(1, NDEV):
                t = t + ref[dev]
            return t

        tot = total(gall_ref)
        grads = [tot[0:9, :]] + [tot[ROW_GAINS + k:ROW_GAINS + k + 1, :] for k in range(6)]
        grads += [total(gattn_ref), tot[ROW_ATTN_CB:ROW_ATTN_CB + 1, CW:2 * CW], tot[ROW_LN:ROW_LN + 1, 0:CW],
                  tot[ROW_LN:ROW_LN + 1, CW:2 * CW], total(gconvw_ref)]
        loss_ref[...] = tot[ROW_LOSS:ROW_LOSS + 1, 0:1]
        for k in range(n):
            g_refs[k][...] = grads[k]
            d_refs[k][...], nm_refs[k][...], nv_refs[k][...] = _adam_math(w_refs[k][...], grads[k], m_refs[k][...],
                                                                          v_refs[k][...])

    shapes = [jax.ShapeDtypeStruct(w.shape, F32) for w in ws]
    res = _pcall(body, "adamw_small", out_shape=[jax.ShapeDtypeStruct((1, 1), F32)] + shapes * 4,
                 compiler_params=_cp())(gall, gattn, gconvw, *ws, *ms, *vs)
    return res[0], [res[1 + k * n:1 + (k + 1) * n] for k in range(4)]


def _ffn_fwd(x, g_pre, g_post, shift, scale, gate, w_in, w_out4, tag, tm, comm=None):
    h = _pre_fwd(x, g_pre, shift, scale, "pre_fwd_" + tag)
    gu, a, got = _ffn_in(h, w_in, "ffn_in_" + tag, comm)
    f, out = _mm_post([a], [pl.BlockSpec((NSL, tm, SL), lambda i: (0, i, 0))],
                      [w_out4], [pl.BlockSpec((NSL, SL, D), lambda i: (0, 0, 0))],
                      x, g_post, gate, 0.5, "ffn_out_" + tag, tm)
    return out, (x, h, gu, a, f), got


def _ffn_bwd(dout, saved, g_pre, g_post, scale, gate, w_in, w_out4, tag, tmb, tmw):
    x, h, gu, a, f = saved
    T = x.shape[0]
    df, dgate, dg_post = _post_bwd(dout, f, g_post, gate, 0.5, "post_bwd_" + tag)
    dgu = _ffn_out_bwd(df, w_out4, gu, "ffn_out_bwd_" + tag)
    dw_out, _ = _mm_tn(a, pl.BlockSpec((1, tmw, SL), lambda j, i: (j, i, 0)),
                       df, pl.BlockSpec((tmw, D), lambda j, i: (i, 0)),
                       (NSL, SL, D), pl.BlockSpec((1, SL, D), lambda j, i: (j, 0, 0)), (NSL, T // tmw), "dw_out_" + tag)
    dw_in, (r_out,) = _mm_tn(h, pl.BlockSpec((tmw, D), lambda s, i: (i, 0)),
                             dgu, pl.BlockSpec((1, 1, tmw, SL), lambda s, i: (s % NSL, s // NSL, i, 0)),
                             (NDEV, D, SL), pl.BlockSpec((1, D, SL), lambda s, i: (s, 0, 0)), (NDEV, T // tmw),
                             "dw_in_" + tag, comm=("a2a", [dw_out.reshape(NDEV, SL // 2, D)]))

    def dh_fn(a_ref, w_ref):
        dh = None
        for p in range(2):
            for j in range(NSL):
                t = _dot_nt(a_ref[j, p], w_ref[NSL * p + j])
                dh = t if dh is None else dh + t
        return dh

    (dx, dshift, dscale, dg_pre), (r_in,) = _mm_prebwd(
        dgu, pl.BlockSpec((NSL, 2, tmb, SL), lambda i: (0, 0, i, 0)),
        w_in, pl.BlockSpec((NDEV, D, SL), lambda i: (0, 0, 0), pipeline_mode=pl.Buffered(1)),
        dh_fn, x, dout, g_pre, scale, "ffn_in_bwd_" + tag, tmb, comm=("a2a", [dw_in]))
    return dx, r_in, r_out, dg_pre, dg_post, (dshift, dscale, dgate)


def kernel(x, c, w_ada, b_ada, g_pre_ff1, g_post_ff1, ff1_w_in, ff1_w_out, g_pre_mix, g_post_mix, w_in_mix, g_attn_out, conv_w, conv_b, conv_ln_g, conv_ln_b, w_out_mix, g_pre_ff2, g_post_ff2, ff2_w_in, ff2_w_out, loss_target, m_w_ada, m_b_ada, m_g_pre_ff1, m_g_post_ff1, m_ff1_w_in, m_ff1_w_out, m_g_pre_mix, m_g_post_mix, m_w_in_mix, m_g_attn_out, m_conv_w, m_conv_b, m_conv_ln_g, m_conv_ln_b, m_w_out_mix, m_g_pre_ff2, m_g_post_ff2, m_ff2_w_in, m_ff2_w_out, v_w_ada, v_b_ada, v_g_pre_ff1, v_g_post_ff1, v_ff1_w_in, v_ff1_w_out, v_g_pre_mix, v_g_post_mix, v_w_in_mix, v_g_attn_out, v_conv_w, v_conv_b, v_conv_ln_g, v_conv_ln_b, v_w_out_mix, v_g_pre_ff2, v_g_post_ff2, v_ff2_w_in, v_ff2_w_out):
    me = 4 * lax.axis_index("x") + 2 * lax.axis_index("y") + lax.axis_index("c")
    T = x.shape[1]
    tq = min(256, T)
    tm = 512
    tmb = 512
    tmw = 2048
    x0 = x.reshape(T, D)
    tgt = loss_target.reshape(T, D)
    row = lambda a: a.reshape(1, -1)

    small_in = jnp.concatenate([c.reshape(-1), jnp.pad(conv_w.reshape(-1), (0, 2 * D - CK * 64)),
                                jnp.zeros((5 * D,), F32)]).reshape(8, D)
    small_all, = _all_gather([small_in], "gather_c_convw", True)
    c_all = small_all[:, 0, :]
    conv_w_full = small_all[:, 1:3, :].reshape(NDEV, 2 * D)[:, :CK * 64].reshape(NDEV, CK, 64)
    conv_w_full = conv_w_full.transpose(1, 0, 2).reshape(CK, CW)
    conv_w_pad = jnp.pad(conv_w_full, ((0, HALO - CK), (0, 0)))

    big = [ff1_w_in, ff1_w_out, w_in_mix, w_out_mix, ff2_w_in, ff2_w_out]
    shards = [w.astype(BF16) for w in big]
    w_in1, w_out1 = _all_gather(shards[0:2], "gather_weights_ff1", False)
    w_out1_4 = w_out1.reshape(NSL, SL, D)

    b_cols = lax.dynamic_slice(b_ada, (me * ADA_COLS,), (ADA_COLS,)).reshape(1, ADA_COLS)
    mod_cols = _ada_fwd(c_all, w_ada, b_cols)
    mod_all, = _all_gather([mod_cols], "gather_mod", True)
    mod = lax.dynamic_slice(mod_all, (0, me, 0), (NDEV, 1, ADA_COLS)).reshape(9, D)
    sh = lambda s: mod[3 * s:3 * s + 1]
    sc = lambda s: mod[3 * s + 1:3 * s + 2]
    gt = lambda s: mod[3 * s + 2:3 * s + 3]

    x1, sv1, (w_inm_s, w_outm_s) = _ffn_fwd(x0, row(g_pre_ff1), row(g_post_ff1), sh(0), sc(0), gt(0), w_in1, w_out1_4,
                                            "ff1", tm, comm=("gather", shards[2:4]))
    w_inm = w_inm_s.transpose(1, 0, 2).reshape(D, MIXIN)
    w_outm = w_outm_s.reshape(D, D)
    hm = _pre_fwd(x1, row(g_pre_mix), sh(1), sc(1), "pre_fwd_mix")
    qkv, cvg = _mix_in(hm, w_inm, "mix_in")
    g_attn_row = row(g_attn_out)
    o_att, an, (w_in2, w_out2) = _attn_fwd(qkv, g_attn_row, tq, comm=("gather", shards[4:6]))
    w_out2_4 = w_out2.reshape(NSL, SL, D)
    u0, u1, u3 = _conv_fwd(cvg, conv_w_pad, row(conv_b), row(conv_ln_g), row(conv_ln_b))
    half = lambda k: pl.BlockSpec((AW, D), lambda i: (k, 0))
    act = pl.BlockSpec((tm, AW), lambda i: (i, 0))
    fm, x2 = _mm_post([an, u3], [act, act], [w_outm, w_outm], [half(0), half(1)],
                      x1, row(g_post_mix), gt(1), 1.0, "mix_out", tm)
    x3, sv2, _ = _ffn_fwd(x2, row(g_pre_ff2), row(g_post_ff2), sh(2), sc(2), gt(2), w_in2, w_out2_4, "ff2", tm)
    dy, loss_part = _loss_head(x3, tgt, "loss_head")

    dx2, r_in2, r_out2, dgpre2, dgpost2, dmod2 = _ffn_bwd(
        dy, sv2, row(g_pre_ff2), row(g_post_ff2), sc(2), gt(2), w_in2, w_out2_4, "ff2", tmb, tmw)

    dfm, dgate1, dgpostm = _post_bwd(dx2, fm, row(g_post_mix), gt(1), 1.0, "post_bwd_mix")
    dcat = _mm_nt(dfm, w_outm, "mix_out_bwd")
    tok = pl.BlockSpec((tmw, AW), lambda j, i: (i, 0))
    tokd = pl.BlockSpec((tmw, D), lambda j, i: (i, 0))
    whole = pl.BlockSpec((AW, D), lambda j, i: (0, 0))
    dw_outm = jnp.concatenate([_mm_tn(an, tok, dfm, tokd, (AW, D), whole, (1, T // tmw), "dw_out_mix_a")[0],
                               _mm_tn(u3, tok, dfm, tokd, (AW, D), whole, (1, T // tmw), "dw_out_mix_c")[0]], axis=0)
    dq, dk, dv, dg_attn = _attn_bwd(qkv, o_att, dcat, g_attn_row, tq)
    du1, dconv_w, dconv_b, dln_g, dln_b = _conv_bwd1(dcat, u1, u0, row(conv_ln_g), row(conv_ln_b))
    dcvg = _conv_bwd2(du1, cvg, conv_w_pad)
    dproj = jnp.concatenate([dq, dk.astype(BF16), dv.astype(BF16), dcvg], axis=1)
    dw_inm, _ = _mm_tn(hm, pl.BlockSpec((tmw, D), lambda j, i: (i, 0)),
                       dproj, pl.BlockSpec((tmw, MIXIN // 2), lambda j, i: (i, j)),
                       (D, MIXIN), pl.BlockSpec((D, MIXIN // 2), lambda j, i: (0, j)), (2, T // tmw), "dw_in_mix")
    (dx1, dshift1, dscale1, dgprem), (r_inm, r_outm) = _mm_prebwd(
        dproj, pl.BlockSpec((tmb, MIXIN), lambda i: (i, 0)), w_inm, pl.BlockSpec((D, MIXIN), lambda i: (0, 0)),
        lambda a_ref, w_ref: _dot_nt(a_ref[...], w_ref[...]), x1, dx2, row(g_pre_mix), sc(1), "mix_in_bwd", tmb,
        comm=("a2a", [dw_inm.reshape(D, NDEV, 320).transpose(1, 0, 2), dw_outm.reshape(NDEV, 128, D)]))

    dx0, r_in1, r_out1, dgpre1, dgpost1, dmod0 = _ffn_bwd(
        dx1, sv1, row(g_pre_ff1), row(g_post_ff1), sc(0), gt(0), w_in1, w_out1_4, "ff1", tmb, tmw)
    recvs = [r_in1, r_out1, r_inm, r_outm, r_in2, r_out2]

    zrow = jnp.zeros((1, D), F32)
    small_g = jnp.concatenate(
        list(dmod0) + [dshift1, dscale1, dgate1] + list(dmod2)
        + [dgpre1, dgpost1, dgprem, dgpostm, dgpre2, dgpost2]
        + [jnp.concatenate([dg_attn, dconv_b], axis=1), jnp.concatenate([dln_g, dln_b], axis=1),
           jnp.pad(dconv_w[:CK].reshape(-1), (0, CONVW_ROWS * D - CK * CW)).reshape(CONVW_ROWS, D),
           jnp.pad(loss_part, ((0, 0), (0, D - 1)))] + [zrow] * (SMALL_R - ROW_LOSS - 1), axis=0)
    small_g_all, = _all_gather([small_g], "gather_small_grads", True)

    dmod_all = small_g_all[:, 0:9, :].reshape(NDEV, NMOD)
    dmod_cols = lax.dynamic_slice(dmod_all, (0, me * ADA_COLS), (NDEV, ADA_COLS))
    g_w_ada = _ada_bwd(c_all.T, dmod_cols)

    gattn = small_g_all[:, ROW_ATTN_CB, 0:AW].reshape(NDEV, 8, HD)
    gconvw = small_g_all[:, ROW_CONVW:ROW_CONVW + CONVW_ROWS, :].reshape(NDEV, CONVW_ROWS * D)[:, :CK * CW]
    gconvw = lax.dynamic_slice(gconvw.reshape(NDEV, CK, CW), (0, 0, me * 64), (NDEV, CK, 64))

    def small_list(b, g6, ga, cb, lg, lb, cw):
        return [b.reshape(9, D)] + [row(g) for g in g6] + [ga, row(cb), row(lg), row(lb), cw]

    sw = small_list(b_ada, [g_pre_ff1, g_post_ff1, g_pre_mix, g_post_mix, g_pre_ff2, g_post_ff2], g_attn_out,
                    conv_b, conv_ln_g, conv_ln_b, conv_w)
    sm = small_list(m_b_ada, [m_g_pre_ff1, m_g_post_ff1, m_g_pre_mix, m_g_post_mix, m_g_pre_ff2, m_g_post_ff2],
                    m_g_attn_out, m_conv_b, m_conv_ln_g, m_conv_ln_b, m_conv_w)
    sv = small_list(v_b_ada, [v_g_pre_ff1, v_g_post_ff1, v_g_pre_mix, v_g_post_mix, v_g_pre_ff2, v_g_post_ff2],
                    v_g_attn_out, v_conv_b, v_conv_ln_g, v_conv_ln_b, v_conv_w)
    loss, s_out = _adamw_small(small_g_all, gattn, gconvw, sw, sm, sv)
    s_out = [[o.reshape(w.shape) for o, w in zip(outs, [b_ada, g_pre_ff1, g_post_ff1, g_pre_mix, g_post_mix,
                                                        g_pre_ff2, g_post_ff2, g_attn_out, conv_b, conv_ln_g,
                                                        conv_ln_b, conv_w])] for outs in s_out]

    big_m = [m_ff1_w_in, m_ff1_w_out, m_w_in_mix, m_w_out_mix, m_ff2_w_in, m_ff2_w_out]
    big_v = [v_ff1_w_in, v_ff1_w_out, v_w_in_mix, v_w_out_mix, v_ff2_w_in, v_ff2_w_out]
    tbs = [256, 176, 256, 128, 256, 176]
    tags = ["ff1_w_in", "ff1_w_out", "w_in_mix", "w_out_mix", "ff2_w_in", "ff2_w_out"]
    b_out = [_adamw(big[k], recvs[k], big_m[k], big_v[k], "adamw_" + tags[k], tbs[k]) for k in range(6)]
    a_out = _adamw(w_ada, g_w_ada.reshape(1, D, ADA_COLS), m_w_ada, v_w_ada, "adamw_ada", 256)

    def leaves(k):
        s = s_out[k]
        b = [o[k] for o in b_out]
        return [a_out[k], s[0], s[1], s[2], b[0], b[1], s[3], s[4], b[2], s[7], s[11], s[8], s[9], s[10], b[3],
                s[5], s[6], b[4], b[5]]

    return (loss.reshape(()), dx0.reshape(1, T, D), *leaves(0), *leaves(1), *leaves(2), *leaves(3))
```

```python
import functools

import jax
import jax.numpy as jnp
from jax import lax
from jax.experimental import pallas as pl
from jax.experimental.pallas import tpu as pltpu

F32 = jnp.float32
BF16 = jnp.bfloat16
D = 1024
DFF = 2816
SL = 704
NSL = DFF // SL
AW = 512
HD = 64
CW = 512
CK = 31
HALO = 32
MIXIN = 2560
NDEV = 8
NMOD = 9 * D
ADA_COLS = NMOD // NDEV
RMS_EPS = 1e-6
LN_EPS = 1e-5
QK_SCALE = HD ** -0.5
W_ZERO_BELOW = -104.0
ADAM_LR, ADAM_B1, ADAM_B2, ADAM_EPS, ADAM_WD, ADAM_STEP = 0.001, 0.9, 0.999, 1e-08, 0.01, 10
ADAM_C1 = 1.0 / (1.0 - ADAM_B1 ** ADAM_STEP)
ADAM_C2 = 1.0 / (1.0 - ADAM_B2 ** ADAM_STEP)
MIB = 1024 * 1024
MESH = pl.DeviceIdType.MESH

ROW_GAINS = 9
ROW_ATTN_CB = 15
ROW_LN = 16
ROW_CONVW = 17
CONVW_ROWS = 16
ROW_LOSS = 33
SMALL_R = 40


def _pcall(body, name, **kw):
    return pl.pallas_call(body, name=name, **kw)


def _cp(sem=None, vmem_mib=48):
    if sem is None:
        return pltpu.CompilerParams(vmem_limit_bytes=vmem_mib * MIB)
    return pltpu.CompilerParams(dimension_semantics=sem, vmem_limit_bytes=vmem_mib * MIB)


def _dot(a, b):
    return jnp.dot(a, b, preferred_element_type=F32)


def _dot_nt(a, b):
    return lax.dot_general(a, b, (((1,), (1,)), ((), ())), preferred_element_type=F32)


def _dot_tn(a, b):
    return lax.dot_general(a, b, (((0,), (0,)), ((), ())), preferred_element_type=F32)


def _sigmoid(x):
    return 0.5 * jnp.tanh(0.5 * x) + 0.5


def _split2(x):
    hi = x.astype(BF16)
    mid = (x - hi.astype(F32)).astype(BF16)
    return hi, mid


def _mat(ref):
    lead = len(ref.shape) - 2
    return ref[(0,) * lead] if lead else ref[...]


def _all_gather(xs, name, in_vmem):
    n = len(xs)

    def body(*refs):
        x_refs, out_refs = refs[:n], refs[n:2 * n]
        send_sems, recv_sems, local_sems = refs[2 * n:]
        mx, my, mc = lax.axis_index("x"), lax.axis_index("y"), lax.axis_index("c")
        me, sibling = (mx, my, mc), (mx, my, 1 - mc)
        chips = [(1 - mx, my), (mx, 1 - my), (1 - mx, 1 - my)]

        def slab(a, px, py, pc):
            return out_refs[a].at[4 * px + 2 * py + pc]

        def copy(a, k, block, to, src=None):
            return pltpu.make_async_remote_copy(
                src_ref=slab(a, *block) if src is None else src, dst_ref=slab(a, *block),
                send_sem=send_sems.at[a, k], recv_sem=recv_sems.at[a, k], device_id=to, device_id_type=MESH)

        mine = [pltpu.make_async_copy(x_refs[a], slab(a, *me), local_sems.at[a]) for a in range(n)]
        for cp in mine:
            cp.start()
        first = []
        for a in range(n):
            first.append(copy(a, 0, me, sibling, src=x_refs[a]))
            first += [copy(a, 1 + j, me, (*chip, mc), src=x_refs[a]) for j, chip in enumerate(chips)]
        for cp in first:
            cp.start()
        passed = []
        for j, chip in enumerate(chips):
            for a in range(n):
                copy(a, 1 + j, (*chip, mc), me).wait_recv()
                passed.append(copy(a, 4 + j, (*chip, mc), sibling))
                passed[-1].start()
        for a in range(n):
            copy(a, 0, sibling, me).wait_recv()
            for j, chip in enumerate(chips):
                copy(a, 4 + j, (*chip, 1 - mc), me).wait_recv()
        for cp in first + passed:
            cp.wait_send()
        for cp in mine:
            cp.wait()

    space = pltpu.VMEM if in_vmem else pl.ANY
    return _pcall(
        body, name,
        out_shape=[jax.ShapeDtypeStruct((NDEV,) + x.shape, x.dtype) for x in xs],
        in_specs=[pl.BlockSpec(memory_space=space)] * n,
        out_specs=[pl.BlockSpec(memory_space=space)] * n,
        scratch_shapes=[pltpu.SemaphoreType.DMA((n, 7)), pltpu.SemaphoreType.DMA((n, 7)),
                        pltpu.SemaphoreType.DMA((n,))],
    )(*xs)


def _exchange_copies(kind, src, dst, send_sems, recv_sems, local_sems):
    mx, my, mc = lax.axis_index("x"), lax.axis_index("y"), lax.axis_index("c")
    me = 4 * mx + 2 * my + mc
    n = len(src)
    pick = (lambda a, p: src[a].at[p]) if kind == "a2a" else (lambda a, p: src[a])
    mine = [pltpu.make_async_copy(pick(a, me), dst[a].at[me], local_sems.at[a]) for a in range(n)]
    copies = []
    for r in range(1, NDEV):
        px = 1 - mx if r & 4 else mx
        py = 1 - my if r & 2 else my
        pc = 1 - mc if r & 1 else mc
        for a in range(n):
            copies.append(pltpu.make_async_remote_copy(
                src_ref=pick(a, 4 * px + 2 * py + pc), dst_ref=dst[a].at[me],
                send_sem=send_sems.at[a, r - 1], recv_sem=recv_sems.at[a, r - 1],
                device_id=(px, py, pc), device_id_type=MESH))
    return mine, copies


def _hosted_call(body, name, comm, grid, in_specs, out_specs, out_shape, scratch_shapes, sem, vmem_mib, args):
    if comm is None:
        outs = _pcall(body, name, grid=grid, in_specs=in_specs, out_specs=out_specs, out_shape=out_shape,
                      scratch_shapes=scratch_shapes, compiler_params=_cp(sem, vmem_mib))(*args)
        return outs, []
    kind, arrs = comm
    nc, n_in, n_out, n_scr = len(arrs), len(in_specs), len(out_specs), len(scratch_shapes)
    rank = len(grid)

    def wrapped(*refs):
        ins, csrc = refs[:n_in], refs[n_in:n_in + nc]
        outs, cdst = refs[n_in + nc:n_in + nc + n_out], refs[n_in + nc + n_out:n_in + 2 * nc + n_out]
        rest = refs[n_in + 2 * nc + n_out:]
        scr, sems = rest[:n_scr], rest[n_scr:]
        first = functools.reduce(jnp.logical_and, [pl.program_id(d) == 0 for d in range(rank)])
        last = functools.reduce(jnp.logical_and, [pl.program_id(d) == grid[d] - 1 for d in range(rank)])

        @pl.when(first)
        def _():
            mine, copies = _exchange_copies(kind, csrc, cdst, *sems)
            for cp in mine + copies:
                cp.start()

        body(*ins, *outs, *scr)

        @pl.when(last)
        def _():
            mine, copies = _exchange_copies(kind, csrc, cdst, *sems)
            for cp in copies:
                cp.wait_recv()
            for cp in copies:
                cp.wait_send()
            for cp in mine:
                cp.wait()

    hbm = pl.BlockSpec(memory_space=pl.ANY)
    cshape = [jax.ShapeDtypeStruct(a.shape if kind == "a2a" else (NDEV,) + a.shape, a.dtype) for a in arrs]
    res = _pcall(wrapped, name, grid=grid, in_specs=list(in_specs) + [hbm] * nc,
                 out_specs=list(out_specs) + [hbm] * nc, out_shape=list(out_shape) + cshape,
                 scratch_shapes=list(scratch_shapes) + [pltpu.SemaphoreType.DMA((nc, 7)),
                                                        pltpu.SemaphoreType.DMA((nc, 7)),
                                                        pltpu.SemaphoreType.DMA((nc,))],
                 compiler_params=_cp(("arbitrary",) * rank, vmem_mib))(*args, *arrs)
    return res[:n_out], res[n_out:]


def _ada_fwd(c_all, w, b):
    n = w.shape[1]

    def body(c_ref, w_ref, b_ref, o_ref):
        c = c_ref[...]
        s = c * _sigmoid(c)
        o_ref[...] = jnp.dot(s, w_ref[...], preferred_element_type=F32, precision=lax.Precision.HIGHEST) + b_ref[...]

    return _pcall(body, "ada_fwd", out_shape=jax.ShapeDtypeStruct((NDEV, n), F32), compiler_params=_cp())(c_all, w, b)


def _ada_bwd(c_all_t, dmod):
    n = dmod.shape[1]

    def body(ct_ref, d_ref, o_ref):
        ct = ct_ref[...]
        s = ct * _sigmoid(ct)
        acc = s[:, 0:1] * d_ref[0:1, :]
        for b in range(1, NDEV):
            acc = acc + s[:, b:b + 1] * d_ref[b:b + 1, :]
        o_ref[...] = acc

    return _pcall(body, "ada_bwd", out_shape=jax.ShapeDtypeStruct((D, n), F32), compiler_params=_cp())(c_all_t, dmod)


def _pre_fwd(x, g, shift, scale, name, tb=512):
    T = x.shape[0]

    def body(x_ref, g_ref, sh_ref, sc_ref, h_ref):
        xv = x_ref[...]
        r = lax.rsqrt(jnp.mean(xv * xv, axis=-1, keepdims=True) + RMS_EPS)
        h_ref[...] = ((xv * r) * g_ref[...] * (1.0 + sc_ref[...]) + sh_ref[...]).astype(BF16)

    row = pl.BlockSpec((tb, D), lambda i: (i, 0))
    vec = pl.BlockSpec((1, D), lambda i: (0, 0))
    return _pcall(body, name, grid=(T // tb,), in_specs=[row, vec, vec, vec], out_specs=row,
                  out_shape=jax.ShapeDtypeStruct((T, D), BF16), compiler_params=_cp(("parallel",)))(x, g, shift, scale)


def _ffn_in(h, w_in, name, comm=None, tm=512):
    T = h.shape[0]

    def body(h_ref, wg_ref, wu_ref, gu_ref, a_ref):
        hv = h_ref[...]
        g = _dot(hv, wg_ref[0])
        u = _dot(hv, wu_ref[0])
        gu_ref[0, 0] = g.astype(BF16)
        gu_ref[0, 1] = u.astype(BF16)
        a_ref[0] = (g * _sigmoid(g) * u).astype(BF16)

    (gu, a), got = _hosted_call(
        body, name, comm, (NSL, T // tm),
        [pl.BlockSpec((tm, D), lambda j, i: (i, 0)), pl.BlockSpec((1, D, SL), lambda j, i: (j, 0, 0)),
         pl.BlockSpec((1, D, SL), lambda j, i: (j + NSL, 0, 0))],
        [pl.BlockSpec((1, 2, tm, SL), lambda j, i: (j, 0, i, 0)), pl.BlockSpec((1, tm, SL), lambda j, i: (j, i, 0))],
        [jax.ShapeDtypeStruct((NSL, 2, T, SL), BF16), jax.ShapeDtypeStruct((NSL, T, SL), BF16)],
        [], ("parallel", "parallel"), 48, (h, w_in, w_in))
    return gu, a, got


def _mm_post(a_list, a_specs, w_list, w_specs, x, g_post, gate, res_w, name, tm):
    T = x.shape[0]
    n = len(a_list)

    def body(*refs):
        a_refs, w_refs = refs[:n], refs[n:2 * n]
        x_ref, g_ref, gt_ref, f_ref, o_ref = refs[2 * n:]
        f = None
        for a_ref, w_ref in zip(a_refs, w_refs):
            if len(a_ref.shape) == 3:
                terms = [_dot(a_ref[j], w_ref[j]) for j in range(a_ref.shape[0])]
            else:
                terms = [_dot(a_ref[...], w_ref[...])]
            for t in terms:
                f = t if f is None else f + t
        f_ref[...] = f
        r = lax.rsqrt(jnp.mean(f * f, axis=-1, keepdims=True) + RMS_EPS)
        y = (f * r) * g_ref[...]
        o_ref[...] = x_ref[...] + (res_w * (1.0 + gt_ref[...])) * y

    row = pl.BlockSpec((tm, D), lambda i: (i, 0))
    vec = pl.BlockSpec((1, D), lambda i: (0, 0))
    return _pcall(body, name, grid=(T // tm,),
                  in_specs=list(a_specs) + list(w_specs) + [row, vec, vec], out_specs=[row, row],
                  out_shape=[jax.ShapeDtypeStruct((T, D), F32), jax.ShapeDtypeStruct((T, D), F32)],
                  compiler_params=_cp(("parallel",)))(*a_list, *w_list, x, g_post, gate)


def _post_bwd(dout, f, g_post, gate, res_w, name, tb=512):
    T = f.shape[0]

    def body(do_ref, f_ref, g_ref, gt_ref, df_ref, dgate_ref, dg_ref):
        @pl.when(pl.program_id(0) == 0)
        def _():
            dgate_ref[...] = jnp.zeros_like(dgate_ref)
            dg_ref[...] = jnp.zeros_like(dg_ref)

        do = do_ref[...]
        f = f_ref[...]
        r = lax.rsqrt(jnp.mean(f * f, axis=-1, keepdims=True) + RMS_EPS)
        fn = f * r
        dgate_ref[...] += jnp.sum((res_w * do) * (fn * g_ref[...]), axis=0, keepdims=True)
        dy = (res_w * (1.0 + gt_ref[...])) * do
        dg_ref[...] += jnp.sum(dy * fn, axis=0, keepdims=True)
        dyg = dy * g_ref[...]
        df = r * (dyg - fn * jnp.mean(dyg * fn, axis=-1, keepdims=True))
        df_ref[...] = df.astype(BF16)

    row = pl.BlockSpec((tb, D), lambda i: (i, 0))
    vec = pl.BlockSpec((1, D), lambda i: (0, 0))
    return _pcall(body, name, grid=(T // tb,), in_specs=[row, row, vec, vec], out_specs=[row, vec, vec],
                  out_shape=[jax.ShapeDtypeStruct((T, D), BF16), jax.ShapeDtypeStruct((1, D), F32),
                             jax.ShapeDtypeStruct((1, D), F32)],
                  compiler_params=_cp(("arbitrary",)))(dout, f, g_post, gate)


def _ffn_out_bwd(df, w_out4, gu, name, tm=512):
    T = df.shape[0]

    def body(df_ref, w_ref, gu_ref, dgu_ref):
        da = _dot_nt(df_ref[...], w_ref[0])
        gv = gu_ref[0, 0].astype(F32)
        s = _sigmoid(gv)
        gs = gv * s
        dgu_ref[0, 0] = (da * gu_ref[0, 1].astype(F32) * (s + gs * (1.0 - s))).astype(BF16)
        dgu_ref[0, 1] = (da * gs).astype(BF16)

    gus = pl.BlockSpec((1, 2, tm, SL), lambda j, i: (j, 0, i, 0))
    return _pcall(body, name, grid=(NSL, T // tm),
                  in_specs=[pl.BlockSpec((tm, D), lambda j, i: (i, 0)), pl.BlockSpec((1, SL, D), lambda j, i: (j, 0, 0)), gus],
                  out_specs=gus, out_shape=jax.ShapeDtypeStruct((NSL, 2, T, SL), BF16),
                  compiler_params=_cp(("parallel", "parallel")))(df, w_out4, gu)


def _mm_tn(a, a_spec, b, b_spec, out_shape, out_spec, grid, name, comm=None):
    k, nn = out_spec.block_shape[-2:]
    steps = grid[1]

    def body(a_ref, b_ref, o_ref, acc_ref):
        i = pl.program_id(1)

        @pl.when(i == 0)
        def _():
            acc_ref[...] = jnp.zeros_like(acc_ref)

        acc_ref[...] += _dot_tn(_mat(a_ref), _mat(b_ref))

        @pl.when(i == steps - 1)
        def _():
            lead = len(o_ref.shape) - 2
            o_ref[(0,) * lead if lead else ...] = acc_ref[...].astype(BF16)

    (out,), got = _hosted_call(body, name, comm, grid, [a_spec, b_spec], [out_spec],
                               [jax.ShapeDtypeStruct(out_shape, BF16)], [pltpu.VMEM((k, nn), F32)],
                               ("parallel", "arbitrary"), 48, (a, b))
    return out, got


def _mm_prebwd(a, a_spec, w, w_spec, dh_fn, x, dout, g_pre, scale, name, tm=256, comm=None):
    T = x.shape[0]

    def body(a_ref, w_ref, x_ref, do_ref, g_ref, sc_ref, dx_ref, dsh_ref, dsc_ref, dg_ref):
        @pl.when(pl.program_id(0) == 0)
        def _():
            dsh_ref[...] = jnp.zeros_like(dsh_ref)
            dsc_ref[...] = jnp.zeros_like(dsc_ref)
            dg_ref[...] = jnp.zeros_like(dg_ref)

        dh = dh_fn(a_ref, w_ref)
        xv = x_ref[...]
        r = lax.rsqrt(jnp.mean(xv * xv, axis=-1, keepdims=True) + RMS_EPS)
        xn = xv * r
        dsh_ref[...] += jnp.sum(dh, axis=0, keepdims=True)
        dsc_ref[...] += jnp.sum(dh * (xn * g_ref[...]), axis=0, keepdims=True)
        dn = dh * (1.0 + sc_ref[...])
        dg_ref[...] += jnp.sum(dn * xn, axis=0, keepdims=True)
        dng = dn * g_ref[...]
        dx_ref[...] = do_ref[...] + r * (dng - xn * jnp.mean(dng * xn, axis=-1, keepdims=True))

    row = pl.BlockSpec((tm, D), lambda i: (i, 0))
    vec = pl.BlockSpec((1, D), lambda i: (0, 0))
    return _hosted_call(body, name, comm, (T // tm,), [a_spec, w_spec, row, row, vec, vec], [row, vec, vec, vec],
                        [jax.ShapeDtypeStruct((T, D), F32)] + [jax.ShapeDtypeStruct((1, D), F32)] * 3,
                        [], ("arbitrary",), 56, (a, w, x, dout, g_pre, scale))


def _loss_head(y, tgt, name, tb=512):
    T = y.shape[0]

    def body(y_ref, t_ref, dy_ref, l_ref):
        @pl.when(pl.program_id(0) == 0)
        def _():
            l_ref[...] = jnp.zeros_like(l_ref)

        e = y_ref[...] - t_ref[...]
        dy_ref[...] = e * (1.0 / D)
        l_ref[...] += 0.5 * jnp.sum(jnp.mean(e * e, axis=-1, keepdims=True), axis=0, keepdims=True)

    row = pl.BlockSpec((tb, D), lambda i: (i, 0))
    return _pcall(body, name, grid=(T // tb,), in_specs=[row, row],
                  out_specs=[row, pl.BlockSpec((1, 1), lambda i: (0, 0))],
                  out_shape=[jax.ShapeDtypeStruct((T, D), F32), jax.ShapeDtypeStruct((1, 1), F32)],
                  compiler_params=_cp(("arbitrary",)))(y, tgt)


def _mix_in(h, w, name, tm=512):
    T = h.shape[0]

    def body(h_ref, w_ref, qkv_ref, cvg_ref):
        p = _dot(h_ref[...], w_ref[...])
        qkv_ref[...] = p[:, :3 * AW].astype(BF16)
        cvg_ref[...] = p[:, 3 * AW:]

    return _pcall(body, name, grid=(T // tm,),
                  in_specs=[pl.BlockSpec((tm, D), lambda i: (i, 0)), pl.BlockSpec((D, MIXIN), lambda i: (0, 0))],
                  out_specs=[pl.BlockSpec((tm, 3 * AW), lambda i: (i, 0)), pl.BlockSpec((tm, 2 * CW), lambda i: (i, 0))],
                  out_shape=[jax.ShapeDtypeStruct((T, 3 * AW), BF16), jax.ShapeDtypeStruct((T, 2 * CW), F32)],
                  compiler_params=_cp(("parallel",)))(h, w)


def _mm_nt(a, w, name, tm=512):
    T, K = a.shape
    N = w.shape[0]

    def body(a_ref, w_ref, o_ref):
        o_ref[...] = _dot_nt(a_ref[...], w_ref[...])

    return _pcall(body, name, grid=(T // tm,),
                  in_specs=[pl.BlockSpec((tm, K), lambda i: (i, 0)), pl.BlockSpec((N, K), lambda i: (0, 0))],
                  out_specs=pl.BlockSpec((tm, N), lambda i: (i, 0)),
                  out_shape=jax.ShapeDtypeStruct((T, N), F32), compiler_params=_cp(("parallel",)))(a, w)


def _softplus_parts(z):
    ls = jnp.minimum(z, 0.0) - jnp.log(1.0 + jnp.exp(-jnp.abs(z)))
    return ls, ls - z


def _head_sum(x, first):
    sa = jnp.sum(jnp.where(first, x, 0.0), axis=-1, keepdims=True)
    sb = jnp.sum(jnp.where(first, 0.0, x), axis=-1, keepdims=True)
    return jnp.where(first, sa, sb)


def _attn_specs(T, tq):
    qs = pl.BlockSpec((tq, 128), lambda p, i: (i, p))
    ks = pl.BlockSpec((T, 128), lambda p, i: (0, 4 + p))
    vs = pl.BlockSpec((T, 128), lambda p, i: (0, 8 + p))
    gs = pl.BlockSpec((1, 128), lambda p, i: (0, p))
    return qs, ks, vs, gs


def _attn_fwd(qkv, g_attn, tq, comm=None):
    T = qkv.shape[0]

    def body(q_ref, k_ref, v_ref, g_ref, o_ref, an_ref):
        i = pl.program_id(1)
        first = lax.broadcasted_iota(jnp.int32, (tq, 128), 1) < HD
        q = (q_ref[...].astype(F32) * QK_SCALE).astype(BF16)
        zq = jnp.zeros_like(q)
        qs = (jnp.where(first, q, zq), jnp.where(first, zq, q))
        rows = lax.broadcasted_iota(jnp.int32, (tq, tq), 0)
        cols = lax.broadcasted_iota(jnp.int32, (tq, tq), 1)
        tri = (rows > cols).astype(BF16)
        strict = cols < rows

        def tile(j, Rs, acc, masked):
            start = j * tq if isinstance(j, int) else pl.multiple_of(j * tq, tq)
            kb = k_ref[pl.ds(start, tq), :]
            vb = v_ref[pl.ds(start, tq), :]
            zs = [_dot_nt(qs[hh], kb) for hh in range(2)]
            parts = [_softplus_parts(z) for z in zs]
            lsms = [jnp.where(strict, p[1], 0.0) if masked else p[1] for p in parts]
            splits = [_split2(x) for x in lsms]
            afters = [_dot(s[0], tri) + _dot(s[1], tri) for s in splits]
            ws = [jnp.exp(parts[hh][0] + afters[hh] + Rs[hh]) for hh in range(2)]
            if masked:
                ws = [jnp.where(strict, w, 0.0) for w in ws]
            outs = [_dot(w.astype(BF16), vb) for w in ws]
            new_r = [Rs[hh] + afters[hh][:, 0:1] + lsms[hh][:, 0:1] for hh in range(2)]
            return new_r[0], new_r[1], acc + jnp.where(first, outs[0], outs[1])

        zr = jnp.zeros((tq, 1), F32)

        def finish(acc):
            o_ref[...] = acc
            r = lax.rsqrt(_head_sum(acc * acc, first) * (1.0 / HD) + RMS_EPS)
            an_ref[...] = ((acc * r) * g_ref[...]).astype(BF16)

        @pl.when(i == 0)
        def _():
            finish(tile(0, (zr, zr), jnp.zeros((tq, 128), F32), True)[2])

        @pl.when(i > 0)
        def _():
            ra, rb, acc = tile(i, (zr, zr), jnp.zeros((tq, 128), F32), True)
            ra, rb, acc = tile(i - 1, (ra, rb), acc, False)

            def more(c):
                return jnp.logical_and(c[0] < i, jnp.maximum(jnp.max(c[1]), jnp.max(c[2])) > W_ZERO_BELOW)

            def step(c):
                ra, rb, acc = tile(i - 1 - c[0], (c[1], c[2]), c[3], False)
                return c[0] + 1, ra, rb, acc

            finish(lax.while_loop(more, step, (jnp.int32(1), ra, rb, acc))[3])

    qs, ks, vs, gs = _attn_specs(T, tq)
    (o, an), got = _hosted_call(body, "attn_fwd", comm, (AW // 128, T // tq), [qs, ks, vs, gs], [qs, qs],
                                [jax.ShapeDtypeStruct((T, AW), F32), jax.ShapeDtypeStruct((T, AW), BF16)],
                                [], ("parallel", "parallel"), 48, (qkv, qkv, qkv, g_attn))
    return o, an, got


def _attn_bwd(qkv, o, dcat, g_attn, tq):
    T = qkv.shape[0]

    def body(q_ref, k_ref, v_ref, o_ref, dan_ref, g_ref, dq_ref, dk_ref, dv_ref, dg_ref):
        i = pl.program_id(1)

        @pl.when(i == 0)
        def _():
            dk_ref[...] = jnp.zeros_like(dk_ref)
            dv_ref[...] = jnp.zeros_like(dv_ref)
            dg_ref[...] = jnp.zeros_like(dg_ref)

        first = lax.broadcasted_iota(jnp.int32, (tq, 128), 1) < HD
        q = (q_ref[...].astype(F32) * QK_SCALE).astype(BF16)
        zq = jnp.zeros_like(q)
        qs = (jnp.where(first, q, zq), jnp.where(first, zq, q))
        o = o_ref[...]
        dan = dan_ref[...]
        r = lax.rsqrt(_head_sum(o * o, first) * (1.0 / HD) + RMS_EPS)
        on = o * r
        dg_ref[...] += jnp.sum(dan * on, axis=0, keepdims=True)
        dyg = dan * g_ref[...]
        dO = r * (dyg - on * (_head_sum(dyg * on, first) * (1.0 / HD)))
        dOb = dO.astype(BF16)
        dOs = (jnp.where(first, dOb, zq), jnp.where(first, zq, dOb))
        ones = jnp.ones((8, 128), BF16)
        Ds = []
        for hh in range(2):
            prod = dOs[hh].astype(F32) * o
            p1 = prod.astype(BF16)
            rem = prod - p1.astype(F32)
            p2 = rem.astype(BF16)
            p3 = (rem - p2.astype(F32)).astype(BF16)
            Ds.append((_dot_nt(ones, p1) + _dot_nt(ones, p2) + _dot_nt(ones, p3))[0:1, :])

        rows = lax.broadcasted_iota(jnp.int32, (tq, tq), 0)
        cols = lax.broadcasted_iota(jnp.int32, (tq, tq), 1)
        tri_after = (cols > rows).astype(BF16)
        tri_incl = (cols >= rows).astype(BF16)
        strict = rows < cols

        def tile(j, Rs, Gs, dq, masked):
            start = j * tq if isinstance(j, int) else pl.multiple_of(j * tq, tq)
            kb = k_ref[pl.ds(start, tq), :]
            vb = v_ref[pl.ds(start, tq), :]
            H = range(2)
            parts = [_softplus_parts(_dot_nt(kb, qs[hh])) for hh in H]
            lsms = [jnp.where(strict, p[1], 0.0) if masked else p[1] for p in parts]
            splits = [_split2(x) for x in lsms]
            afters = [_dot(tri_after, s[0]) + _dot(tri_after, s[1]) for s in splits]
            ws = [jnp.exp(parts[hh][0] + afters[hh] + Rs[hh]) for hh in H]
            if masked:
                ws = [jnp.where(strict, w, 0.0) for w in ws]
            wbs = [w.astype(BF16) for w in ws]
            dlws = [_dot_nt(vb, dOs[hh]) * wbs[hh].astype(F32) for hh in H]
            splits2 = [_split2(x) for x in dlws]
            Cs = [_dot(tri_incl, s[0]) + _dot(tri_incl, s[1]) for s in splits2]
            dlsms = [Ds[hh] - Gs[hh] - Cs[hh] for hh in H]
            if masked:
                dlsms = [jnp.where(strict, x, 0.0) for x in dlsms]
            ps = [jnp.exp(p[0]) for p in parts]
            dzs = [(dlws[hh] * (1.0 - ps[hh]) - dlsms[hh] * ps[hh]).astype(BF16) for hh in H]
            dkp = _dot(dzs[0], qs[0]) + _dot(dzs[1], qs[1])
            dvp = _dot(wbs[0], dOs[0]) + _dot(wbs[1], dOs[1])
            dq = dq + jnp.where(first, _dot_tn(dzs[0], kb), _dot_tn(dzs[1], kb))
            new_r = [Rs[hh] + afters[hh][0:1, :] + lsms[hh][0:1, :] for hh in H]
            new_g = [Gs[hh] + Cs[hh][0:1, :] for hh in H]
            dk_ref[pl.ds(start, tq), :] += dkp
            dv_ref[pl.ds(start, tq), :] += dvp
            return new_r[0], new_r[1], new_g[0], new_g[1], dq

        zrow = jnp.zeros((1, tq), F32)

        @pl.when(i == 0)
        def _():
            dq0 = tile(0, (zrow, zrow), (zrow, zrow), jnp.zeros((tq, 128), F32), True)[4]
            dq_ref[...] = (dq0 * QK_SCALE).astype(BF16)

        @pl.when(i > 0)
        def _():
            st = tile(i, (zrow, zrow), (zrow, zrow), jnp.zeros((tq, 128), F32), True)
            st = tile(i - 1, st[0:2], st[2:4], st[4], False)

            def more(c):
                return jnp.logical_and(c[0] < i, jnp.maximum(jnp.max(c[1]), jnp.max(c[2])) > W_ZERO_BELOW)

            def step(c):
                return (c[0] + 1,) + tile(i - 1 - c[0], (c[1], c[2]), (c[3], c[4]), c[5], False)

            dq_ref[...] = (lax.while_loop(more, step, (jnp.int32(1),) + st)[5] * QK_SCALE).astype(BF16)

    qs, ks, vs, gs = _attn_specs(T, tq)
    kacc = pl.BlockSpec((T, 128), lambda p, i: (0, p))
    return _pcall(body, "attn_bwd", grid=(AW // 128, T // tq), in_specs=[qs, ks, vs, qs, qs, gs],
                  out_specs=[qs, kacc, kacc, gs],
                  out_shape=[jax.ShapeDtypeStruct((T, AW), BF16), jax.ShapeDtypeStruct((T, AW), F32),
                             jax.ShapeDtypeStruct((T, AW), F32), jax.ShapeDtypeStruct((1, AW), F32)],
                  compiler_params=_cp(("parallel", "arbitrary")))(qkv, qkv, qkv, o, dcat, g_attn)


def _taps_by_phase(offsets):
    groups = {}
    for k, off in enumerate(offsets):
        groups.setdefault(off % 8, []).append((k, off // 8))
    return sorted(groups.items())


def _shifted_tap_sum(w_ref, pad_ref, ph_ref, offsets, tb):
    acc = None
    for p, taps in _taps_by_phase(offsets):
        n = tb if p == 0 else tb + 8
        a = None
        for k, m in taps:
            t = w_ref[k:k + 1, :] * pad_ref[8 * m:8 * m + n, :]
            a = t if a is None else a + t
        if p:
            ph_ref[...] = a
            a = ph_ref[p:p + tb, :]
        acc = a if acc is None else acc + a
    return acc


def _conv_fwd(cvg, conv_w, conv_b, ln_g, ln_b, tb=512):
    T = cvg.shape[0]
    hb = tb // HALO

    def body(cv_ref, cg_ref, cvp_ref, cgp_ref, w_ref, b_ref, g_ref, be_ref, u0_ref, u1_ref, u3_ref, pad_ref, ph_ref):
        i = pl.program_id(0)
        u0 = cv_ref[...] * _sigmoid(cg_ref[...])
        prev = cvp_ref[...] * _sigmoid(cgp_ref[...])
        pad_ref[0:HALO, :] = jnp.where(i > 0, prev, 0.0)
        pad_ref[HALO:HALO + tb, :] = u0
        u0_ref[...] = u0
        acc = _shifted_tap_sum(w_ref, pad_ref, ph_ref, [HALO - (CK - 1) + kk for kk in range(CK)], tb) + b_ref[...]
        u1_ref[...] = acc
        mu = jnp.mean(acc, axis=-1, keepdims=True)
        xc = acc - mu
        var = jnp.mean(xc * xc, axis=-1, keepdims=True)
        u2 = (xc * lax.rsqrt(var + LN_EPS)) * g_ref[...] + be_ref[...]
        u3_ref[...] = (u2 * _sigmoid(u2)).astype(BF16)

    cur = lambda col: pl.BlockSpec((tb, CW), lambda i: (i, col))
    prv = lambda col: pl.BlockSpec((HALO, CW), lambda i: (jnp.maximum(i * hb - 1, 0), col))
    vec = pl.BlockSpec((1, CW), lambda i: (0, 0))
    out = pl.BlockSpec((tb, CW), lambda i: (i, 0))
    return _pcall(body, "conv_fwd", grid=(T // tb,),
                  in_specs=[cur(0), cur(1), prv(0), prv(1), pl.BlockSpec((HALO, CW), lambda i: (0, 0)), vec, vec, vec],
                  out_specs=[out, out, out],
                  out_shape=[jax.ShapeDtypeStruct((T, CW), F32), jax.ShapeDtypeStruct((T, CW), F32),
                             jax.ShapeDtypeStruct((T, CW), BF16)],
                  scratch_shapes=[pltpu.VMEM((tb + HALO, CW), F32), pltpu.VMEM((tb + 8, CW), F32)],
                  compiler_params=_cp(("parallel",)))(cvg, cvg, cvg, cvg, conv_w, conv_b, ln_g, ln_b)


def _conv_bwd1(dcat, u1, u0, ln_g, ln_b, tb=512):
    T = u1.shape[0]
    hb = tb // HALO

    def body(d3_ref, u1_ref, u0_ref, u0p_ref, g_ref, be_ref, du1_ref, dw_ref, db_ref, dlg_ref, dlb_ref, pad_ref, d_ref):
        i = pl.program_id(0)

        @pl.when(i == 0)
        def _():
            dw_ref[...] = jnp.zeros_like(dw_ref)
            db_ref[...] = jnp.zeros_like(db_ref)
            dlg_ref[...] = jnp.zeros_like(dlg_ref)
            dlb_ref[...] = jnp.zeros_like(dlb_ref)

        u1 = u1_ref[...]
        mu = jnp.mean(u1, axis=-1, keepdims=True)
        xc = u1 - mu
        rstd = lax.rsqrt(jnp.mean(xc * xc, axis=-1, keepdims=True) + LN_EPS)
        xh = xc * rstd
        u2 = xh * g_ref[...] + be_ref[...]
        s = _sigmoid(u2)
        du2 = d3_ref[...] * (s + u2 * s * (1.0 - s))
        dlg_ref[...] += jnp.sum(du2 * xh, axis=0, keepdims=True)
        dlb_ref[...] += jnp.sum(du2, axis=0, keepdims=True)
        dxh = du2 * g_ref[...]
        du1 = rstd * (dxh - jnp.mean(dxh, axis=-1, keepdims=True) - xh * jnp.mean(dxh * xh, axis=-1, keepdims=True))
        du1_ref[...] = du1
        db_ref[...] += jnp.sum(du1, axis=0, keepdims=True)
        pad_ref[0:HALO, :] = jnp.where(i > 0, u0p_ref[...], 0.0)
        pad_ref[HALO:HALO + tb, :] = u0_ref[...]
        d_ref[0:8, :] = jnp.zeros((8, CW), F32)
        d_ref[8:8 + tb, :] = du1
        d_ref[8 + tb:16 + tb, :] = jnp.zeros((8, CW), F32)
        for p, taps in _taps_by_phase([HALO - (CK - 1) + kk for kk in range(CK)]):
            n = tb if p == 0 else tb + 8
            qv = du1 if p == 0 else d_ref[8 - p:8 - p + n, :]
            for k, m in taps:
                dw_ref[k:k + 1, :] += jnp.sum(qv * pad_ref[8 * m:8 * m + n, :], axis=0, keepdims=True)

    cur = pl.BlockSpec((tb, CW), lambda i: (i, 0))
    vec = pl.BlockSpec((1, CW), lambda i: (0, 0))
    return _pcall(body, "conv_bwd1", grid=(T // tb,),
                  in_specs=[pl.BlockSpec((tb, CW), lambda i: (i, 1)), cur, cur,
                            pl.BlockSpec((HALO, CW), lambda i: (jnp.maximum(i * hb - 1, 0), 0)), vec, vec],
                  out_specs=[cur, pl.BlockSpec((HALO, CW), lambda i: (0, 0)), vec, vec, vec],
                  out_shape=[jax.ShapeDtypeStruct((T, CW), F32), jax.ShapeDtypeStruct((HALO, CW), F32)]
                  + [jax.ShapeDtypeStruct((1, CW), F32)] * 3,
                  scratch_shapes=[pltpu.VMEM((tb + HALO, CW), F32), pltpu.VMEM((tb + 16, CW), F32)],
                  compiler_params=_cp(("arbitrary",)))(dcat, u1, u0, u0, ln_g, ln_b)


def _conv_bwd2(du1, cvg, conv_w, tb=512):
    T = du1.shape[0]
    hb = tb // HALO
    last = T // HALO - 1
    nblk = T // tb

    def body(d_ref, dn_ref, cv_ref, cg_ref, w_ref, o_ref, pad_ref, ph_ref):
        i = pl.program_id(0)
        pad_ref[0:tb, :] = d_ref[...]
        pad_ref[tb:tb + HALO, :] = jnp.where(i < nblk - 1, dn_ref[...], 0.0)
        acc = _shifted_tap_sum(w_ref, pad_ref, ph_ref, [CK - 1 - kk for kk in range(CK)], tb)
        sg = _sigmoid(cg_ref[...])
        o_ref[:, 0:CW] = (acc * sg).astype(BF16)
        o_ref[:, CW:2 * CW] = (acc * cv_ref[...] * sg * (1.0 - sg)).astype(BF16)

    cur = pl.BlockSpec((tb, CW), lambda i: (i, 0))
    return _pcall(body, "conv_bwd2", grid=(nblk,),
                  in_specs=[cur, pl.BlockSpec((HALO, CW), lambda i: (jnp.minimum((i + 1) * hb, last), 0)),
                            pl.BlockSpec((tb, CW), lambda i: (i, 0)), pl.BlockSpec((tb, CW), lambda i: (i, 1)),
                            pl.BlockSpec((HALO, CW), lambda i: (0, 0))],
                  out_specs=pl.BlockSpec((tb, 2 * CW), lambda i: (i, 0)),
                  out_shape=jax.ShapeDtypeStruct((T, 2 * CW), BF16),
                  scratch_shapes=[pltpu.VMEM((tb + HALO, CW), F32), pltpu.VMEM((tb + 8, CW), F32)],
                  compiler_params=_cp(("parallel",)))(du1, du1, cvg, cvg, conv_w)


def _adam_math(w, g, m, v):
    nm = ADAM_B1 * m + (1.0 - ADAM_B1) * g
    nv = ADAM_B2 * v + (1.0 - ADAM_B2) * (g * g)
    delta = -ADAM_LR * ((nm * ADAM_C1) / (jnp.sqrt(nv * ADAM_C2) + ADAM_EPS) + ADAM_WD * w)
    return delta, nm, nv


def _adamw(w, gslots, m, v, name, tb):
    R, C = w.shape
    S = gslots.shape[0]

    def body(w_ref, gs_ref, m_ref, v_ref, g_ref, d_ref, nm_ref, nv_ref):
        g = gs_ref[0].astype(F32)
        for s in range(1, S):
            g = g + gs_ref[s].astype(F32)
        g_ref[...] = g
        d_ref[...], nm_ref[...], nv_ref[...] = _adam_math(w_ref[...], g, m_ref[...], v_ref[...])

    blk = pl.BlockSpec((tb, C), lambda i: (i, 0))
    return _pcall(body, name, grid=(R // tb,),
                  in_specs=[blk, pl.BlockSpec((S, tb, C), lambda i: (0, i, 0)), blk, blk],
                  out_specs=[blk] * 4, out_shape=[jax.ShapeDtypeStruct((R, C), F32)] * 4,
                  compiler_params=_cp(("parallel",)))(w, gslots, m, v)


def _adamw_small(gall, gattn, gconvw, ws, ms, vs):
    n = len(ws)

    def body(*refs):
        gall_ref, gattn_ref, gconvw_ref = refs[:3]
        w_refs, m_refs, v_refs = refs[3:3 + n], refs[3 + n:3 + 2 * n], refs[3 + 2 * n:3 + 3 * n]
        loss_ref = refs[3 + 3 * n]
        outs = refs[4 + 3 * n:]
        g_refs, d_refs, nm_refs, nv_refs = outs[:n], outs[n:2 * n], outs[2 * n:3 * n], outs[3 * n:]

        def total(ref):
            t = ref[0]
            for dev in range(1, NDEV):
                t = t + ref[dev]
            return t

        tot = total(gall_ref)
        grads = [tot[0:9, :]] + [tot[ROW_GAINS + k:ROW_GAINS + k + 1, :] for k in range(6)]
        grads += [total(gattn_ref), tot[ROW_ATTN_CB:ROW_ATTN_CB + 1, CW:2 * CW], tot[ROW_LN:ROW_LN + 1, 0:CW],
                  tot[ROW_LN:ROW_LN + 1, CW:2 * CW], total(gconvw_ref)]
        loss_ref[...] = tot[ROW_LOSS:ROW_LOSS + 1, 0:1]
        for k in range(n):
            g_refs[k][...] = grads[k]
            d_refs[k][...], nm_refs[k][...], nv_refs[k][...] = _adam_math(w_refs[k][...], grads[k], m_refs[k][...],
                                                                          v_refs[k][...])

    shapes = [jax.ShapeDtypeStruct(w.shape, F32) for w in ws]
    res = _pcall(body, "adamw_small", out_shape=[jax.ShapeDtypeStruct((1, 1), F32)] + shapes * 4,
                 compiler_params=_cp())(gall, gattn, gconvw, *ws, *ms, *vs)
    return res[0], [res[1 + k * n:1 + (k + 1) * n] for k in range(4)]


def _ffn_fwd(x, g_pre, g_post, shift, scale, gate, w_in, w_out4, tag, tm, comm=None):
    h = _pre_fwd(x, g_pre, shift, scale, "pre_fwd_" + tag)
    gu, a, got = _ffn_in(h, w_in, "ffn_in_" + tag, comm)
    f, out = _mm_post([a], [pl.BlockSpec((NSL, tm, SL), lambda i: (0, i, 0))],
                      [w_out4], [pl.BlockSpec((NSL, SL, D), lambda i: (0, 0, 0))],
                      x, g_post, gate, 0.5, "ffn_out_" + tag, tm)
    return out, (x, h, gu, a, f), got


def _ffn_bwd(dout, saved, g_pre, g_post, scale, gate, w_in, w_out4, tag, tmb, tmw):
    x, h, gu, a, f = saved
    T = x.shape[0]
    df, dgate, dg_post = _post_bwd(dout, f, g_post, gate, 0.5, "post_bwd_" + tag)
    dgu = _ffn_out_bwd(df, w_out4, gu, "ffn_out_bwd_" + tag)
    dw_out, _ = _mm_tn(a, pl.BlockSpec((1, tmw, SL), lambda j, i: (j, i, 0)),
                       df, pl.BlockSpec((tmw, D), lambda j, i: (i, 0)),
                       (NSL, SL, D), pl.BlockSpec((1, SL, D), lambda j, i: (j, 0, 0)), (NSL, T // tmw), "dw_out_" + tag)
    dw_in, (r_out,) = _mm_tn(h, pl.BlockSpec((tmw, D), lambda s, i: (i, 0)),
                             dgu, pl.BlockSpec((1, 1, tmw, SL), lambda s, i: (s % NSL, s // NSL, i, 0)),
                             (NDEV, D, SL), pl.BlockSpec((1, D, SL), lambda s, i: (s, 0, 0)), (NDEV, T // tmw),
                             "dw_in_" + tag, comm=("a2a", [dw_out.reshape(NDEV, SL // 2, D)]))

    def dh_fn(a_ref, w_ref):
        dh = None
        for p in range(2):
            for j in range(NSL):
                t = _dot_nt(a_ref[j, p], w_ref[NSL * p + j])
                dh = t if dh is None else dh + t
        return dh

    (dx, dshift, dscale, dg_pre), (r_in,) = _mm_prebwd(
        dgu, pl.BlockSpec((NSL, 2, tmb, SL), lambda i: (0, 0, i, 0)),
        w_in, pl.BlockSpec((NDEV, D, SL), lambda i: (0, 0, 0), pipeline_mode=pl.Buffered(1)),
        dh_fn, x, dout, g_pre, scale, "ffn_in_bwd_" + tag, tmb, comm=("a2a", [dw_in]))
    return dx, r_in, r_out, dg_pre, dg_post, (dshift, dscale, dgate)


def kernel(x, c, w_ada, b_ada, g_pre_ff1, g_post_ff1, ff1_w_in, ff1_w_out, g_pre_mix, g_post_mix, w_in_mix, g_attn_out, conv_w, conv_b, conv_ln_g, conv_ln_b, w_out_mix, g_pre_ff2, g_post_ff2, ff2_w_in, ff2_w_out, loss_target, m_w_ada, m_b_ada, m_g_pre_ff1, m_g_post_ff1, m_ff1_w_in, m_ff1_w_out, m_g_pre_mix, m_g_post_mix, m_w_in_mix, m_g_attn_out, m_conv_w, m_conv_b, m_conv_ln_g, m_conv_ln_b, m_w_out_mix, m_g_pre_ff2, m_g_post_ff2, m_ff2_w_in, m_ff2_w_out, v_w_ada, v_b_ada, v_g_pre_ff1, v_g_post_ff1, v_ff1_w_in, v_ff1_w_out, v_g_pre_mix, v_g_post_mix, v_w_in_mix, v_g_attn_out, v_conv_w, v_conv_b, v_conv_ln_g, v_conv_ln_b, v_w_out_mix, v_g_pre_ff2, v_g_post_ff2, v_ff2_w_in, v_ff2_w_out):
    me = 4 * lax.axis_index("x") + 2 * lax.axis_index("y") + lax.axis_index("c")
    T = x.shape[1]
    tq = min(256, T)
    tm = 512
    tmb = 512
    tmw = 2048
    x0 = x.reshape(T, D)
    tgt = loss_target.reshape(T, D)
    row = lambda a: a.reshape(1, -1)

    small_in = jnp.concatenate([c.reshape(-1), jnp.pad(conv_w.reshape(-1), (0, 2 * D - CK * 64)),
                                jnp.zeros((5 * D,), F32)]).reshape(8, D)
    small_all, = _all_gather([small_in], "gather_c_convw", True)
    c_all = small_all[:, 0, :]
    conv_w_full = small_all[:, 1:3, :].reshape(NDEV, 2 * D)[:, :CK * 64].reshape(NDEV, CK, 64)
    conv_w_full = conv_w_full.transpose(1, 0, 2).reshape(CK, CW)
    conv_w_pad = jnp.pad(conv_w_full, ((0, HALO - CK), (0, 0)))

    big = [ff1_w_in, ff1_w_out, w_in_mix, w_out_mix, ff2_w_in, ff2_w_out]
    shards = [w.astype(BF16) for w in big]
    w_in1, w_out1 = _all_gather(shards[0:2], "gather_weights_ff1", False)
    w_out1_4 = w_out1.reshape(NSL, SL, D)

    b_cols = lax.dynamic_slice(b_ada, (me * ADA_COLS,), (ADA_COLS,)).reshape(1, ADA_COLS)
    mod_cols = _ada_fwd(c_all, w_ada, b_cols)
    mod_all, = _all_gather([mod_cols], "gather_mod", True)
    mod = lax.dynamic_slice(mod_all, (0, me, 0), (NDEV, 1, ADA_COLS)).reshape(9, D)
    sh = lambda s: mod[3 * s:3 * s + 1]
    sc = lambda s: mod[3 * s + 1:3 * s + 2]
    gt = lambda s: mod[3 * s + 2:3 * s + 3]

    x1, sv1, (w_inm_s, w_outm_s) = _ffn_fwd(x0, row(g_pre_ff1), row(g_post_ff1), sh(0), sc(0), gt(0), w_in1, w_out1_4,
                                            "ff1", tm, comm=("gather", shards[2:4]))
    w_inm = w_inm_s.transpose(1, 0, 2).reshape(D, MIXIN)
    w_outm = w_outm_s.reshape(D, D)
    hm = _pre_fwd(x1, row(g_pre_mix), sh(1), sc(1), "pre_fwd_mix")
    qkv, cvg = _mix_in(hm, w_inm, "mix_in")
    g_attn_row = row(g_attn_out)
    o_att, an, (w_in2, w_out2) = _attn_fwd(qkv, g_attn_row, tq, comm=("gather", shards[4:6]))
    w_out2_4 = w_out2.reshape(NSL, SL, D)
    u0, u1, u3 = _conv_fwd(cvg, conv_w_pad, row(conv_b), row(conv_ln_g), row(conv_ln_b))
    half = lambda k: pl.BlockSpec((AW, D), lambda i: (k, 0))
    act = pl.BlockSpec((tm, AW), lambda i: (i, 0))
    fm, x2 = _mm_post([an, u3], [act, act], [w_outm, w_outm], [half(0), half(1)],
                      x1, row(g_post_mix), gt(1), 1.0, "mix_out", tm)
    x3, sv2, _ = _ffn_fwd(x2, row(g_pre_ff2), row(g_post_ff2), sh(2), sc(2), gt(2), w_in2, w_out2_4, "ff2", tm)
    dy, loss_part = _loss_head(x3, tgt, "loss_head")

    dx2, r_in2, r_out2, dgpre2, dgpost2, dmod2 = _ffn_bwd(
        dy, sv2, row(g_pre_ff2), row(g_post_ff2), sc(2), gt(2), w_in2, w_out2_4, "ff2", tmb, tmw)

    dfm, dgate1, dgpostm = _post_bwd(dx2, fm, row(g_post_mix), gt(1), 1.0, "post_bwd_mix")
    dcat = _mm_nt(dfm, w_outm, "mix_out_bwd")
    tok = pl.BlockSpec((tmw, AW), lambda j, i: (i, 0))
    tokd = pl.BlockSpec((tmw, D), lambda j, i: (i, 0))
    whole = pl.BlockSpec((AW, D), lambda j, i: (0, 0))
    dw_outm = jnp.concatenate([_mm_tn(an, tok, dfm, tokd, (AW, D), whole, (1, T // tmw), "dw_out_mix_a")[0],
                               _mm_tn(u3, tok, dfm, tokd, (AW, D), whole, (1, T // tmw), "dw_out_mix_c")[0]], axis=0)
    dq, dk, dv, dg_attn = _attn_bwd(qkv, o_att, dcat, g_attn_row, tq)
    du1, dconv_w, dconv_b, dln_g, dln_b = _conv_bwd1(dcat, u1, u0, row(conv_ln_g), row(conv_ln_b))
    dcvg = _conv_bwd2(du1, cvg, conv_w_pad)
    dproj = jnp.concatenate([dq, dk.astype(BF16), dv.astype(BF16), dcvg], axis=1)
    dw_inm, _ = _mm_tn(hm, pl.BlockSpec((tmw, D), lambda j, i: (i, 0)),
                       dproj, pl.BlockSpec((tmw, MIXIN // 2), lambda j, i: (i, j)),
                       (D, MIXIN), pl.BlockSpec((D, MIXIN // 2), lambda j, i: (0, j)), (2, T // tmw), "dw_in_mix")
    (dx1, dshift1, dscale1, dgprem), (r_inm, r_outm) = _mm_prebwd(
        dproj, pl.BlockSpec((tmb, MIXIN), lambda i: (i, 0)), w_inm, pl.BlockSpec((D, MIXIN), lambda i: (0, 0)),
        lambda a_ref, w_ref: _dot_nt(a_ref[...], w_ref[...]), x1, dx2, row(g_pre_mix), sc(1), "mix_in_bwd", tmb,
        comm=("a2a", [dw_inm.reshape(D, NDEV, 320).transpose(1, 0, 2), dw_outm.reshape(NDEV, 128, D)]))

    dx0, r_in1, r_out1, dgpre1, dgpost1, dmod0 = _ffn_bwd(
        dx1, sv1, row(g_pre_ff1), row(g_post_ff1), sc(0), gt(0), w_in1, w_out1_4, "ff1", tmb, tmw)
    recvs = [r_in1, r_out1, r_inm, r_outm, r_in2, r_out2]

    zrow = jnp.zeros((1, D), F32)
    small_g = jnp.concatenate(
        list(dmod0) + [dshift1, dscale1, dgate1] + list(dmod2)
        + [dgpre1, dgpost1, dgprem, dgpostm, dgpre2, dgpost2]
        + [jnp.concatenate([dg_attn, dconv_b], axis=1), jnp.concatenate([dln_g, dln_b], axis=1),
           jnp.pad(dconv_w[:CK].reshape(-1), (0, CONVW_ROWS * D - CK * CW)).reshape(CONVW_ROWS, D),
           jnp.pad(loss_part, ((0, 0), (0, D - 1)))] + [zrow] * (SMALL_R - ROW_LOSS - 1), axis=0)
    small_g_all, = _all_gather([small_g], "gather_small_grads", True)

    dmod_all = small_g_all[:, 0:9, :].reshape(NDEV, NMOD)
    dmod_cols = lax.dynamic_slice(dmod_all, (0, me * ADA_COLS), (NDEV, ADA_COLS))
    g_w_ada = _ada_bwd(c_all.T, dmod_cols)

    gattn = small_g_all[:, ROW_ATTN_CB, 0:AW].reshape(NDEV, 8, HD)
    gconvw = small_g_all[:, ROW_CONVW:ROW_CONVW + CONVW_ROWS, :].reshape(NDEV, CONVW_ROWS * D)[:, :CK * CW]
    gconvw = lax.dynamic_slice(gconvw.reshape(NDEV, CK, CW), (0, 0, me * 64), (NDEV, CK, 64))

    def small_list(b, g6, ga, cb, lg, lb, cw):
        return [b.reshape(9, D)] + [row(g) for g in g6] + [ga, row(cb), row(lg), row(lb), cw]

    sw = small_list(b_ada, [g_pre_ff1, g_post_ff1, g_pre_mix, g_post_mix, g_pre_ff2, g_post_ff2], g_attn_out,
                    conv_b, conv_ln_g, conv_ln_b, conv_w)
    sm = small_list(m_b_ada, [m_g_pre_ff1, m_g_post_ff1, m_g_pre_mix, m_g_post_mix, m_g_pre_ff2, m_g_post_ff2],
                    m_g_attn_out, m_conv_b, m_conv_ln_g, m_conv_ln_b, m_conv_w)
    sv = small_list(v_b_ada, [v_g_pre_ff1, v_g_post_ff1, v_g_pre_mix, v_g_post_mix, v_g_pre_ff2, v_g_post_ff2],
                    v_g_attn_out, v_conv_b, v_conv_ln_g, v_conv_ln_b, v_conv_w)
    loss, s_out = _adamw_small(small_g_all, gattn, gconvw, sw, sm, sv)
    s_out = [[o.reshape(w.shape) for o, w in zip(outs, [b_ada, g_pre_ff1, g_post_ff1, g_pre_mix, g_post_mix,
                                                        g_pre_ff2, g_post_ff2, g_attn_out, conv_b, conv_ln_g,
                                                        conv_ln_b, conv_w])] for outs in s_out]

    big_m = [m_ff1_w_in, m_ff1_w_out, m_w_in_mix, m_w_out_mix, m_ff2_w_in, m_ff2_w_out]
    big_v = [v_ff1_w_in, v_ff1_w_out, v_w_in_mix, v_w_out_mix, v_ff2_w_in, v_ff2_w_out]
    tbs = [256, 176, 256, 128, 256, 176]
    tags = ["ff1_w_in", "ff1_w_out", "w_in_mix", "w_out_mix", "ff2_w_in", "ff2_w_out"]
    b_out = [_adamw(big[k], recvs[k], big_m[k], big_v[k], "adamw_" + tags[k], tbs[k]) for k in range(6)]
    a_out = _adamw(w_ada, g_w_ada.reshape(1, D, ADA_COLS), m_w_ada, v_w_ada, "adamw_ada", 256)

    def leaves(k):
        s = s_out[k]
        b = [o[k] for o in b_out]
        return [a_out[k], s[0], s[1], s[2], b[0], b[1], s[3], s[4], b[2], s[7], s[11], s[8], s[9], s[10], b[3],
                s[5], s[6], b[4], b[5]]

    return (loss.reshape(()), dx0.reshape(1, T, D), *leaves(0), *leaves(1), *leaves(2), *leaves(3))
```

```python
import functools

import jax
import jax.numpy as jnp
from jax import lax
from jax.experimental import pallas as pl
from jax.experimental.pallas import tpu as pltpu

F32 = jnp.float32
BF16 = jnp.bfloat16
D = 1024
DFF = 2816
SL = 704
NSL = DFF // SL
AW = 512
HD = 64
CW = 512
CK = 31
HALO = 32
MIXIN = 2560
NDEV = 8
NMOD = 9 * D
ADA_COLS = NMOD // NDEV
RMS_EPS = 1e-6
LN_EPS = 1e-5
QK_SCALE = HD ** -0.5
W_ZERO_BELOW = -104.0
ADAM_LR, ADAM_B1, ADAM_B2, ADAM_EPS, ADAM_WD, ADAM_STEP = 0.001, 0.9, 0.999, 1e-08, 0.01, 10
ADAM_C1 = 1.0 / (1.0 - ADAM_B1 ** ADAM_STEP)
ADAM_C2 = 1.0 / (1.0 - ADAM_B2 ** ADAM_STEP)
MIB = 1024 * 1024
MESH = pl.DeviceIdType.MESH

ROW_GAINS = 9
ROW_ATTN_CB = 15
ROW_LN = 16
ROW_CONVW = 17
CONVW_ROWS = 16
ROW_LOSS = 33
SMALL_R = 40


def _pcall(body, name, **kw):
    return pl.pallas_call(body, name=name, **kw)


def _cp(sem=None, vmem_mib=48):
    if sem is None:
        return pltpu.CompilerParams(vmem_limit_bytes=vmem_mib * MIB)
    return pltpu.CompilerParams(dimension_semantics=sem, vmem_limit_bytes=vmem_mib * MIB)


def _dot(a, b):
    return jnp.dot(a, b, preferred_element_type=F32)


def _dot_nt(a, b):
    return lax.dot_general(a, b, (((1,), (1,)), ((), ())), preferred_element_type=F32)


def _dot_tn(a, b):
    return lax.dot_general(a, b, (((0,), (0,)), ((), ())), preferred_element_type=F32)


def _sigmoid(x):
    return 0.5 * jnp.tanh(0.5 * x) + 0.5


def _split2(x):
    hi = x.astype(BF16)
    mid = (x - hi.astype(F32)).astype(BF16)
    return hi, mid


def _mat(ref):
    lead = len(ref.shape) - 2
    return ref[(0,) * lead] if lead else ref[...]


def _all_gather(xs, name, in_vmem):
    n = len(xs)

    def body(*refs):
        x_refs, out_refs = refs[:n], refs[n:2 * n]
        send_sems, recv_sems, local_sems = refs[2 * n:]
        mx, my, mc = lax.axis_index("x"), lax.axis_index("y"), lax.axis_index("c")
        me, sibling = (mx, my, mc), (mx, my, 1 - mc)
        chips = [(1 - mx, my), (mx, 1 - my), (1 - mx, 1 - my)]

        def slab(a, px, py, pc):
            return out_refs[a].at[4 * px + 2 * py + pc]

        def copy(a, k, block, to, src=None):
            return pltpu.make_async_remote_copy(
                src_ref=slab(a, *block) if src is None else src, dst_ref=slab(a, *block),
                send_sem=send_sems.at[a, k], recv_sem=recv_sems.at[a, k], device_id=to, device_id_type=MESH)

        mine = [pltpu.make_async_copy(x_refs[a], slab(a, *me), local_sems.at[a]) for a in range(n)]
        for cp in mine:
            cp.start()
        first = []
        for a in range(n):
            first.append(copy(a, 0, me, sibling, src=x_refs[a]))
            first += [copy(a, 1 + j, me, (*chip, mc), src=x_refs[a]) for j, chip in enumerate(chips)]
        for cp in first:
            cp.start()
        passed = []
        for j, chip in enumerate(chips):
            for a in range(n):
                copy(a, 1 + j, (*chip, mc), me).wait_recv()
                passed.append(copy(a, 4 + j, (*chip, mc), sibling))
                passed[-1].start()
        for a in range(n):
            copy(a, 0, sibling, me).wait_recv()
            for j, chip in enumerate(chips):
                copy(a, 4 + j, (*chip, 1 - mc), me).wait_recv()
        for cp in first + passed:
            cp.wait_send()
        for cp in mine:
            cp.wait()

    space = pltpu.VMEM if in_vmem else pl.ANY
    return _pcall(
        body, name,
        out_shape=[jax.ShapeDtypeStruct((NDEV,) + x.shape, x.dtype) for x in xs],
        in_specs=[pl.BlockSpec(memory_space=space)] * n,
        out_specs=[pl.BlockSpec(memory_space=space)] * n,
        scratch_shapes=[pltpu.SemaphoreType.DMA((n, 7)), pltpu.SemaphoreType.DMA((n, 7)),
                        pltpu.SemaphoreType.DMA((n,))],
    )(*xs)


def _exchange_copies(kind, src, dst, send_sems, recv_sems, local_sems):
    mx, my, mc = lax.axis_index("x"), lax.axis_index("y"), lax.axis_index("c")
    me = 4 * mx + 2 * my + mc
    n = len(src)
    pick = (lambda a, p: src[a].at[p]) if kind == "a2a" else (lambda a, p: src[a])
    mine = [pltpu.make_async_copy(pick(a, me), dst[a].at[me], local_sems.at[a]) for a in range(n)]
    copies = []
    for r in range(1, NDEV):
        px = 1 - mx if r & 4 else mx
        py = 1 - my if r & 2 else my
        pc = 1 - mc if r & 1 else mc
        for a in range(n):
            copies.append(pltpu.make_async_remote_copy(
                src_ref=pick(a, 4 * px + 2 * py + pc), dst_ref=dst[a].at[me],
                send_sem=send_sems.at[a, r - 1], recv_sem=recv_sems.at[a, r - 1],
                device_id=(px, py, pc), device_id_type=MESH))
    return mine, copies


def _hosted_call(body, name, comm, grid, in_specs, out_specs, out_shape, scratch_shapes, sem, vmem_mib, args):
    if comm is None:
        outs = _pcall(body, name, grid=grid, in_specs=in_specs, out_specs=out_specs, out_shape=out_shape,
                      scratch_shapes=scratch_shapes, compiler_params=_cp(sem, vmem_mib))(*args)
        return outs, []
    kind, arrs = comm
    nc, n_in, n_out, n_scr = len(arrs), len(in_specs), len(out_specs), len(scratch_shapes)
    rank = len(grid)

    def wrapped(*refs):
        ins, csrc = refs[:n_in], refs[n_in:n_in + nc]
        outs, cdst = refs[n_in + nc:n_in + nc + n_out], refs[n_in + nc + n_out:n_in + 2 * nc + n_out]
        rest = refs[n_in + 2 * nc + n_out:]
        scr, sems = rest[:n_scr], rest[n_scr:]
        first = functools.reduce(jnp.logical_and, [pl.program_id(d) == 0 for d in range(rank)])
        last = functools.reduce(jnp.logical_and, [pl.program_id(d) == grid[d] - 1 for d in range(rank)])

        @pl.when(first)
        def _():
            mine, copies = _exchange_copies(kind, csrc, cdst, *sems)
            for cp in mine + copies:
                cp.start()

        body(*ins, *outs, *scr)

        @pl.when(last)
        def _():
            mine, copies = _exchange_copies(kind, csrc, cdst, *sems)
            for cp in copies:
                cp.wait_recv()
            for cp in copies:
                cp.wait_send()
            for cp in mine:
                cp.wait()

    hbm = pl.BlockSpec(memory_space=pl.ANY)
    cshape = [jax.ShapeDtypeStruct(a.shape if kind == "a2a" else (NDEV,) + a.shape, a.dtype) for a in arrs]
    res = _pcall(wrapped, name, grid=grid, in_specs=list(in_specs) + [hbm] * nc,
                 out_specs=list(out_specs) + [hbm] * nc, out_shape=list(out_shape) + cshape,
                 scratch_shapes=list(scratch_shapes) + [pltpu.SemaphoreType.DMA((nc, 7)),
                                                        pltpu.SemaphoreType.DMA((nc, 7)),
                                                        pltpu.SemaphoreType.DMA((nc,))],
                 compiler_params=_cp(("arbitrary",) * rank, vmem_mib))(*args, *arrs)
    return res[:n_out], res[n_out:]


def _ada_fwd(c_all, w, b):
    n = w.shape[1]

    def body(c_ref, w_ref, b_ref, o_ref):
        c = c_ref[...]
        s = c * _sigmoid(c)
        o_ref[...] = jnp.dot(s, w_ref[...], preferred_element_type=F32, precision=lax.Precision.HIGHEST) + b_ref[...]

    return _pcall(body, "ada_fwd", out_shape=jax.ShapeDtypeStruct((NDEV, n), F32), compiler_params=_cp())(c_all, w, b)


def _ada_bwd(c_all_t, dmod):
    n = dmod.shape[1]

    def body(ct_ref, d_ref, o_ref):
        ct = ct_ref[...]
        s = ct * _sigmoid(ct)
        acc = s[:, 0:1] * d_ref[0:1, :]
        for b in range(1, NDEV):
            acc = acc + s[:, b:b + 1] * d_ref[b:b + 1, :]
        o_ref[...] = acc

    return _pcall(body, "ada_bwd", out_shape=jax.ShapeDtypeStruct((D, n), F32), compiler_params=_cp())(c_all_t, dmod)


def _pre_fwd(x, g, shift, scale, name, tb=512):
    T = x.shape[0]

    def body(x_ref, g_ref, sh_ref, sc_ref, h_ref):
        xv = x_ref[...]
        r = lax.rsqrt(jnp.mean(xv * xv, axis=-1, keepdims=True) + RMS_EPS)
        h_ref[...] = ((xv * r) * g_ref[...] * (1.0 + sc_ref[...]) + sh_ref[...]).astype(BF16)

    row = pl.BlockSpec((tb, D), lambda i: (i, 0))
    vec = pl.BlockSpec((1, D), lambda i: (0, 0))
    return _pcall(body, name, grid=(T // tb,), in_specs=[row, vec, vec, vec], out_specs=row,
                  out_shape=jax.ShapeDtypeStruct((T, D), BF16), compiler_params=_cp(("parallel",)))(x, g, shift, scale)


def _ffn_in(h, w_in, name, comm=None, tm=512):
    T = h.shape[0]

    def body(h_ref, w_ref, gu_ref, a_ref):
        hv = h_ref[...]
        for j in range(NSL):
            g = _dot(hv, w_ref[j])
            u = _dot(hv, w_ref[j + NSL])
            gu_ref[j, 0] = g.astype(BF16)
            gu_ref[j, 1] = u.astype(BF16)
            a_ref[j] = (g * _sigmoid(g) * u).astype(BF16)

    (gu, a), got = _hosted_call(
        body, name, comm, (T // tm,),
        [pl.BlockSpec((tm, D), lambda i: (i, 0)),
         pl.BlockSpec((NDEV, D, SL), lambda i: (0, 0, 0), pipeline_mode=pl.Buffered(1))],
        [pl.BlockSpec((NSL, 2, tm, SL), lambda i: (0, 0, i, 0)), pl.BlockSpec((NSL, tm, SL), lambda i: (0, i, 0))],
        [jax.ShapeDtypeStruct((NSL, 2, T, SL), BF16), jax.ShapeDtypeStruct((NSL, T, SL), BF16)],
        [], ("parallel",), 48, (h, w_in))
    return gu, a, got


def _mm_post(a_list, a_specs, w_list, w_specs, x, g_post, gate, res_w, name, tm):
    T = x.shape[0]
    n = len(a_list)

    def body(*refs):
        a_refs, w_refs = refs[:n], refs[n:2 * n]
        x_ref, g_ref, gt_ref, f_ref, o_ref = refs[2 * n:]
        f = None
        for a_ref, w_ref in zip(a_refs, w_refs):
            if len(a_ref.shape) == 3:
                terms = [_dot(a_ref[j], w_ref[j]) for j in range(a_ref.shape[0])]
            else:
                terms = [_dot(a_ref[...], w_ref[...])]
            for t in terms:
                f = t if f is None else f + t
        f_ref[...] = f
        r = lax.rsqrt(jnp.mean(f * f, axis=-1, keepdims=True) + RMS_EPS)
        y = (f * r) * g_ref[...]
        o_ref[...] = x_ref[...] + (res_w * (1.0 + gt_ref[...])) * y

    row = pl.BlockSpec((tm, D), lambda i: (i, 0))
    vec = pl.BlockSpec((1, D), lambda i: (0, 0))
    return _pcall(body, name, grid=(T // tm,),
                  in_specs=list(a_specs) + list(w_specs) + [row, vec, vec], out_specs=[row, row],
                  out_shape=[jax.ShapeDtypeStruct((T, D), F32), jax.ShapeDtypeStruct((T, D), F32)],
                  compiler_params=_cp(("parallel",)))(*a_list, *w_list, x, g_post, gate)


def _post_bwd(dout, f, g_post, gate, res_w, name, tb=512):
    T = f.shape[0]

    def body(do_ref, f_ref, g_ref, gt_ref, df_ref, dgate_ref, dg_ref):
        @pl.when(pl.program_id(0) == 0)
        def _():
            dgate_ref[...] = jnp.zeros_like(dgate_ref)
            dg_ref[...] = jnp.zeros_like(dg_ref)

        do = do_ref[...]
        f = f_ref[...]
        r = lax.rsqrt(jnp.mean(f * f, axis=-1, keepdims=True) + RMS_EPS)
        fn = f * r
        dgate_ref[...] += jnp.sum((res_w * do) * (fn * g_ref[...]), axis=0, keepdims=True)
        dy = (res_w * (1.0 + gt_ref[...])) * do
        dg_ref[...] += jnp.sum(dy * fn, axis=0, keepdims=True)
        dyg = dy * g_ref[...]
        df = r * (dyg - fn * jnp.mean(dyg * fn, axis=-1, keepdims=True))
        df_ref[...] = df.astype(BF16)

    row = pl.BlockSpec((tb, D), lambda i: (i, 0))
    vec = pl.BlockSpec((1, D), lambda i: (0, 0))
    return _pcall(body, name, grid=(T // tb,), in_specs=[row, row, vec, vec], out_specs=[row, vec, vec],
                  out_shape=[jax.ShapeDtypeStruct((T, D), BF16), jax.ShapeDtypeStruct((1, D), F32),
                             jax.ShapeDtypeStruct((1, D), F32)],
                  compiler_params=_cp(("arbitrary",)))(dout, f, g_post, gate)


def _ffn_out_bwd(df, w_out4, gu, name, tm=512):
    T = df.shape[0]

    def body(df_ref, w_ref, gu_ref, dgu_ref):
        dfv = df_ref[...]
        for j in range(NSL):
            da = _dot_nt(dfv, w_ref[j])
            gv = gu_ref[j, 0].astype(F32)
            s = _sigmoid(gv)
            gs = gv * s
            dgu_ref[j, 0] = (da * gu_ref[j, 1].astype(F32) * (s + gs * (1.0 - s))).astype(BF16)
            dgu_ref[j, 1] = (da * gs).astype(BF16)

    gus = pl.BlockSpec((NSL, 2, tm, SL), lambda i: (0, 0, i, 0))
    return _pcall(body, name, grid=(T // tm,),
                  in_specs=[pl.BlockSpec((tm, D), lambda i: (i, 0)),
                            pl.BlockSpec((NSL, SL, D), lambda i: (0, 0, 0), pipeline_mode=pl.Buffered(1)), gus],
                  out_specs=gus, out_shape=jax.ShapeDtypeStruct((NSL, 2, T, SL), BF16),
                  compiler_params=_cp(("parallel",)))(df, w_out4, gu)


def _mm_tn(a, a_spec, b, b_spec, out_shape, out_spec, grid, name, comm=None):
    k, nn = out_spec.block_shape[-2:]
    steps = grid[1]

    def body(a_ref, b_ref, o_ref, acc_ref):
        i = pl.program_id(1)

        @pl.when(i == 0)
        def _():
            acc_ref[...] = jnp.zeros_like(acc_ref)

        acc_ref[...] += _dot_tn(_mat(a_ref), _mat(b_ref))

        @pl.when(i == steps - 1)
        def _():
            lead = len(o_ref.shape) - 2
            o_ref[(0,) * lead if lead else ...] = acc_ref[...].astype(BF16)

    (out,), got = _hosted_call(body, name, comm, grid, [a_spec, b_spec], [out_spec],
                               [jax.ShapeDtypeStruct(out_shape, BF16)], [pltpu.VMEM((k, nn), F32)],
                               ("parallel", "arbitrary"), 48, (a, b))
    return out, got


def _dw_in(h, dgu, name, tmw, comm=None):
    T = h.shape[0]
    steps = T // tmw

    def body(h_ref, b_ref, o_ref, acc_ref):
        i = pl.program_id(1)

        @pl.when(i == 0)
        def _():
            acc_ref[...] = jnp.zeros_like(acc_ref)

        hv = h_ref[...]
        for p in range(2):
            acc_ref[p] += _dot_tn(hv, b_ref[0, p])

        @pl.when(i == steps - 1)
        def _():
            o_ref[:, 0] = acc_ref[...].astype(BF16)

    (out,), got = _hosted_call(
        body, name, comm, (NSL, steps),
        [pl.BlockSpec((tmw, D), lambda j, i: (i, 0)), pl.BlockSpec((1, 2, tmw, SL), lambda j, i: (j, 0, i, 0))],
        [pl.BlockSpec((2, 1, D, SL), lambda j, i: (0, j, 0, 0))],
        [jax.ShapeDtypeStruct((2, NSL, D, SL), BF16)], [pltpu.VMEM((2, D, SL), F32)],
        ("parallel", "arbitrary"), 56, (h, dgu))
    return out, got


def _mm_prebwd(a, a_spec, w, w_spec, dh_fn, x, dout, g_pre, scale, name, tm=256, comm=None):
    T = x.shape[0]

    def body(a_ref, w_ref, x_ref, do_ref, g_ref, sc_ref, dx_ref, dsh_ref, dsc_ref, dg_ref):
        @pl.when(pl.program_id(0) == 0)
        def _():
            dsh_ref[...] = jnp.zeros_like(dsh_ref)
            dsc_ref[...] = jnp.zeros_like(dsc_ref)
            dg_ref[...] = jnp.zeros_like(dg_ref)

        dh = dh_fn(a_ref, w_ref)
        xv = x_ref[...]
        r = lax.rsqrt(jnp.mean(xv * xv, axis=-1, keepdims=True) + RMS_EPS)
        xn = xv * r
        dsh_ref[...] += jnp.sum(dh, axis=0, keepdims=True)
        dsc_ref[...] += jnp.sum(dh * (xn * g_ref[...]), axis=0, keepdims=True)
        dn = dh * (1.0 + sc_ref[...])
        dg_ref[...] += jnp.sum(dn * xn, axis=0, keepdims=True)
        dng = dn * g_ref[...]
        dx_ref[...] = do_ref[...] + r * (dng - xn * jnp.mean(dng * xn, axis=-1, keepdims=True))

    row = pl.BlockSpec((tm, D), lambda i: (i, 0))
    vec = pl.BlockSpec((1, D), lambda i: (0, 0))
    return _hosted_call(body, name, comm, (T // tm,), [a_spec, w_spec, row, row, vec, vec], [row, vec, vec, vec],
                        [jax.ShapeDtypeStruct((T, D), F32)] + [jax.ShapeDtypeStruct((1, D), F32)] * 3,
                        [], ("arbitrary",), 56, (a, w, x, dout, g_pre, scale))


def _loss_head(y, tgt, name, tb=512):
    T = y.shape[0]

    def body(y_ref, t_ref, dy_ref, l_ref):
        @pl.when(pl.program_id(0) == 0)
        def _():
            l_ref[...] = jnp.zeros_like(l_ref)

        e = y_ref[...] - t_ref[...]
        dy_ref[...] = e * (1.0 / D)
        l_ref[...] += 0.5 * jnp.sum(jnp.mean(e * e, axis=-1, keepdims=True), axis=0, keepdims=True)

    row = pl.BlockSpec((tb, D), lambda i: (i, 0))
    return _pcall(body, name, grid=(T // tb,), in_specs=[row, row],
                  out_specs=[row, pl.BlockSpec((1, 1), lambda i: (0, 0))],
                  out_shape=[jax.ShapeDtypeStruct((T, D), F32), jax.ShapeDtypeStruct((1, 1), F32)],
                  compiler_params=_cp(("arbitrary",)))(y, tgt)


def _mix_in(h, w, name, tm=512):
    T = h.shape[0]

    def body(h_ref, w_ref, qkv_ref, cvg_ref):
        p = _dot(h_ref[...], w_ref[...])
        qkv_ref[...] = p[:, :3 * AW].astype(BF16)
        cvg_ref[...] = p[:, 3 * AW:]

    return _pcall(body, name, grid=(T // tm,),
                  in_specs=[pl.BlockSpec((tm, D), lambda i: (i, 0)), pl.BlockSpec((D, MIXIN), lambda i: (0, 0))],
                  out_specs=[pl.BlockSpec((tm, 3 * AW), lambda i: (i, 0)), pl.BlockSpec((tm, 2 * CW), lambda i: (i, 0))],
                  out_shape=[jax.ShapeDtypeStruct((T, 3 * AW), BF16), jax.ShapeDtypeStruct((T, 2 * CW), F32)],
                  compiler_params=_cp(("parallel",)))(h, w)


def _mm_nt(a, w, name, tm=512):
    T, K = a.shape
    N = w.shape[0]

    def body(a_ref, w_ref, o_ref):
        o_ref[...] = _dot_nt(a_ref[...], w_ref[...])

    return _pcall(body, name, grid=(T // tm,),
                  in_specs=[pl.BlockSpec((tm, K), lambda i: (i, 0)), pl.BlockSpec((N, K), lambda i: (0, 0))],
                  out_specs=pl.BlockSpec((tm, N), lambda i: (i, 0)),
                  out_shape=jax.ShapeDtypeStruct((T, N), F32), compiler_params=_cp(("parallel",)))(a, w)


def _softplus_parts(z):
    ls = jnp.minimum(z, 0.0) - jnp.log(1.0 + jnp.exp(-jnp.abs(z)))
    return ls, ls - z


def _head_sum(x, first):
    sa = jnp.sum(jnp.where(first, x, 0.0), axis=-1, keepdims=True)
    sb = jnp.sum(jnp.where(first, 0.0, x), axis=-1, keepdims=True)
    return jnp.where(first, sa, sb)


def _attn_specs(T, tq):
    qs = pl.BlockSpec((tq, 128), lambda p, i: (i, p))
    ks = pl.BlockSpec((T, 128), lambda p, i: (0, 4 + p))
    vs = pl.BlockSpec((T, 128), lambda p, i: (0, 8 + p))
    gs = pl.BlockSpec((1, 128), lambda p, i: (0, p))
    return qs, ks, vs, gs


def _attn_fwd(qkv, g_attn, tq, comm=None):
    T = qkv.shape[0]

    def body(q_ref, k_ref, v_ref, g_ref, o_ref, an_ref):
        i = pl.program_id(1)
        first = lax.broadcasted_iota(jnp.int32, (tq, 128), 1) < HD
        q = (q_ref[...].astype(F32) * QK_SCALE).astype(BF16)
        zq = jnp.zeros_like(q)
        qs = (jnp.where(first, q, zq), jnp.where(first, zq, q))
        rows = lax.broadcasted_iota(jnp.int32, (tq, tq), 0)
        cols = lax.broadcasted_iota(jnp.int32, (tq, tq), 1)
        tri = (rows > cols).astype(BF16)
        tri2 = jnp.concatenate([tri, tri], axis=0)
        strict = cols < rows

        def tile(j, Rs, acc, masked):
            start = j * tq if isinstance(j, int) else pl.multiple_of(j * tq, tq)
            kb = k_ref[pl.ds(start, tq), :]
            vb = v_ref[pl.ds(start, tq), :]
            zs = [_dot_nt(qs[hh], kb) for hh in range(2)]
            parts = [_softplus_parts(z) for z in zs]
            lsms = [jnp.where(strict, p[1], 0.0) if masked else p[1] for p in parts]
            splits = [_split2(x) for x in lsms]
            afters = [_dot(jnp.concatenate(s, axis=1), tri2) for s in splits]
            ws = [jnp.exp(parts[hh][0] + afters[hh] + Rs[hh]) for hh in range(2)]
            if masked:
                ws = [jnp.where(strict, w, 0.0) for w in ws]
            outs = [_dot(w.astype(BF16), vb) for w in ws]
            new_r = [Rs[hh] + afters[hh][:, 0:1] + lsms[hh][:, 0:1] for hh in range(2)]
            return new_r[0], new_r[1], acc + jnp.where(first, outs[0], outs[1])

        zr = jnp.zeros((tq, 1), F32)

        def finish(acc):
            o_ref[...] = acc
            r = lax.rsqrt(_head_sum(acc * acc, first) * (1.0 / HD) + RMS_EPS)
            an_ref[...] = ((acc * r) * g_ref[...]).astype(BF16)

        @pl.when(i == 0)
        def _():
            finish(tile(0, (zr, zr), jnp.zeros((tq, 128), F32), True)[2])

        @pl.when(i > 0)
        def _():
            ra, rb, acc = tile(i, (zr, zr), jnp.zeros((tq, 128), F32), True)
            ra, rb, acc = tile(i - 1, (ra, rb), acc, False)

            def more(c):
                return jnp.logical_and(c[0] < i, jnp.maximum(jnp.max(c[1]), jnp.max(c[2])) > W_ZERO_BELOW)

            def step(c):
                ra, rb, acc = tile(i - 1 - c[0], (c[1], c[2]), c[3], False)
                return c[0] + 1, ra, rb, acc

            finish(lax.while_loop(more, step, (jnp.int32(1), ra, rb, acc))[3])

    qs, ks, vs, gs = _attn_specs(T, tq)
    (o, an), got = _hosted_call(body, "attn_fwd", comm, (AW // 128, T // tq), [qs, ks, vs, gs], [qs, qs],
                                [jax.ShapeDtypeStruct((T, AW), F32), jax.ShapeDtypeStruct((T, AW), BF16)],
                                [], ("parallel", "parallel"), 48, (qkv, qkv, qkv, g_attn))
    return o, an, got


def _attn_bwd(qkv, o, dcat, g_attn, tq, comm=None):
    T = qkv.shape[0]

    def body(q_ref, k_ref, v_ref, o_ref, dan_ref, g_ref, dq_ref, dk_ref, dv_ref, dg_ref):
        i = pl.program_id(1)

        @pl.when(i == 0)
        def _():
            dk_ref[...] = jnp.zeros_like(dk_ref)
            dv_ref[...] = jnp.zeros_like(dv_ref)
            dg_ref[...] = jnp.zeros_like(dg_ref)

        first = lax.broadcasted_iota(jnp.int32, (tq, 128), 1) < HD
        q = (q_ref[...].astype(F32) * QK_SCALE).astype(BF16)
        zq = jnp.zeros_like(q)
        qs = (jnp.where(first, q, zq), jnp.where(first, zq, q))
        o = o_ref[...]
        dan = dan_ref[...]
        r = lax.rsqrt(_head_sum(o * o, first) * (1.0 / HD) + RMS_EPS)
        on = o * r
        dg_ref[...] += jnp.sum(dan * on, axis=0, keepdims=True)
        dyg = dan * g_ref[...]
        dO = r * (dyg - on * (_head_sum(dyg * on, first) * (1.0 / HD)))
        dOb = dO.astype(BF16)
        dOs = (jnp.where(first, dOb, zq), jnp.where(first, zq, dOb))
        ones = jnp.ones((8, 128), BF16)
        Ds = []
        for hh in range(2):
            prod = dOs[hh].astype(F32) * o
            p1 = prod.astype(BF16)
            rem = prod - p1.astype(F32)
            p2 = rem.astype(BF16)
            p3 = (rem - p2.astype(F32)).astype(BF16)
            Ds.append((_dot_nt(ones, p1) + _dot_nt(ones, p2) + _dot_nt(ones, p3))[0:1, :])

        rows = lax.broadcasted_iota(jnp.int32, (tq, tq), 0)
        cols = lax.broadcasted_iota(jnp.int32, (tq, tq), 1)
        tri_after = (cols > rows).astype(BF16)
        tri_incl = (cols >= rows).astype(BF16)
        tri_after2 = jnp.concatenate([tri_after, tri_after], axis=1)
        tri_incl2 = jnp.concatenate([tri_incl, tri_incl], axis=1)
        strict = rows < cols

        def tile(j, Rs, Gs, dq, masked):
            start = j * tq if isinstance(j, int) else pl.multiple_of(j * tq, tq)
            kb = k_ref[pl.ds(start, tq), :]
            vb = v_ref[pl.ds(start, tq), :]
            H = range(2)
            parts = [_softplus_parts(_dot_nt(kb, qs[hh])) for hh in H]
            lsms = [jnp.where(strict, p[1], 0.0) if masked else p[1] for p in parts]
            splits = [_split2(x) for x in lsms]
            afters = [_dot(tri_after2, jnp.concatenate(s, axis=0)) for s in splits]
            ws = [jnp.exp(parts[hh][0] + afters[hh] + Rs[hh]) for hh in H]
            if masked:
                ws = [jnp.where(strict, w, 0.0) for w in ws]
            wbs = [w.astype(BF16) for w in ws]
            dlws = [_dot_nt(vb, dOs[hh]) * wbs[hh].astype(F32) for hh in H]
            splits2 = [_split2(x) for x in dlws]
            Cs = [_dot(tri_incl2, jnp.concatenate(s, axis=0)) for s in splits2]
            dlsms = [Ds[hh] - Gs[hh] - Cs[hh] for hh in H]
            if masked:
                dlsms = [jnp.where(strict, x, 0.0) for x in dlsms]
            ps = [jnp.exp(p[0]) for p in parts]
            dzs = [(dlws[hh] * (1.0 - ps[hh]) - dlsms[hh] * ps[hh]).astype(BF16) for hh in H]
            dkp = _dot(dzs[0], qs[0]) + _dot(dzs[1], qs[1])
            dvp = _dot(wbs[0], dOs[0]) + _dot(wbs[1], dOs[1])
            dq = dq + jnp.where(first, _dot_tn(dzs[0], kb), _dot_tn(dzs[1], kb))
            new_r = [Rs[hh] + afters[hh][0:1, :] + lsms[hh][0:1, :] for hh in H]
            new_g = [Gs[hh] + Cs[hh][0:1, :] for hh in H]
            dk_ref[pl.ds(start, tq), :] += dkp
            dv_ref[pl.ds(start, tq), :] += dvp
            return new_r[0], new_r[1], new_g[0], new_g[1], dq

        zrow = jnp.zeros((1, tq), F32)

        @pl.when(i == 0)
        def _():
            dq0 = tile(0, (zrow, zrow), (zrow, zrow), jnp.zeros((tq, 128), F32), True)[4]
            dq_ref[...] = (dq0 * QK_SCALE).astype(BF16)

        @pl.when(i > 0)
        def _():
            st = tile(i, (zrow, zrow), (zrow, zrow), jnp.zeros((tq, 128), F32), True)
            st = tile(i - 1, st[0:2], st[2:4], st[4], False)

            def more(c):
                return jnp.logical_and(c[0] < i, jnp.maximum(jnp.max(c[1]), jnp.max(c[2])) > W_ZERO_BELOW)

            def step(c):
                return (c[0] + 1,) + tile(i - 1 - c[0], (c[1], c[2]), (c[3], c[4]), c[5], False)

            dq_ref[...] = (lax.while_loop(more, step, (jnp.int32(1),) + st)[5] * QK_SCALE).astype(BF16)

    qs, ks, vs, gs = _attn_specs(T, tq)
    kacc = pl.BlockSpec((T, 128), lambda p, i: (0, p))
    return _hosted_call(body, "attn_bwd", comm, (AW // 128, T // tq), [qs, ks, vs, qs, qs, gs], [qs, kacc, kacc, gs],
                        [jax.ShapeDtypeStruct((T, AW), BF16), jax.ShapeDtypeStruct((T, AW), F32),
                         jax.ShapeDtypeStruct((T, AW), F32), jax.ShapeDtypeStruct((1, AW), F32)],
                        [], ("parallel", "arbitrary"), 48, (qkv, qkv, qkv, o, dcat, g_attn))


def _taps_by_phase(offsets):
    groups = {}
    for k, off in enumerate(offsets):
        groups.setdefault(off % 8, []).append((k, off // 8))
    return sorted(groups.items())


def _shifted_tap_sum(w_ref, pad_ref, ph_ref, offsets, tb):
    acc = None
    for p, taps in _taps_by_phase(offsets):
        n = tb if p == 0 else tb + 8
        a = None
        for k, m in taps:
            t = w_ref[k:k + 1, :] * pad_ref[8 * m:8 * m + n, :]
            a = t if a is None else a + t
        if p:
            ph_ref[...] = a
            a = ph_ref[p:p + tb, :]
        acc = a if acc is None else acc + a
    return acc


def _conv_fwd(cvg, conv_w, conv_b, ln_g, ln_b, tb=512):
    T = cvg.shape[0]
    hb = tb // HALO

    def body(cv_ref, cg_ref, cvp_ref, cgp_ref, w_ref, b_ref, g_ref, be_ref, u0_ref, u1_ref, u3_ref, pad_ref, ph_ref):
        i = pl.program_id(0)
        u0 = cv_ref[...] * _sigmoid(cg_ref[...])
        prev = cvp_ref[...] * _sigmoid(cgp_ref[...])
        pad_ref[0:HALO, :] = jnp.where(i > 0, prev, 0.0)
        pad_ref[HALO:HALO + tb, :] = u0
        u0_ref[...] = u0
        acc = _shifted_tap_sum(w_ref, pad_ref, ph_ref, [HALO - (CK - 1) + kk for kk in range(CK)], tb) + b_ref[...]
        u1_ref[...] = acc
        mu = jnp.mean(acc, axis=-1, keepdims=True)
        xc = acc - mu
        var = jnp.mean(xc * xc, axis=-1, keepdims=True)
        u2 = (xc * lax.rsqrt(var + LN_EPS)) * g_ref[...] + be_ref[...]
        u3_ref[...] = (u2 * _sigmoid(u2)).astype(BF16)

    cur = lambda col: pl.BlockSpec((tb, CW), lambda i: (i, col))
    prv = lambda col: pl.BlockSpec((HALO, CW), lambda i: (jnp.maximum(i * hb - 1, 0), col))
    vec = pl.BlockSpec((1, CW), lambda i: (0, 0))
    out = pl.BlockSpec((tb, CW), lambda i: (i, 0))
    return _pcall(body, "conv_fwd", grid=(T // tb,),
                  in_specs=[cur(0), cur(1), prv(0), prv(1), pl.BlockSpec((HALO, CW), lambda i: (0, 0)), vec, vec, vec],
                  out_specs=[out, out, out],
                  out_shape=[jax.ShapeDtypeStruct((T, CW), F32), jax.ShapeDtypeStruct((T, CW), F32),
                             jax.ShapeDtypeStruct((T, CW), BF16)],
                  scratch_shapes=[pltpu.VMEM((tb + HALO, CW), F32), pltpu.VMEM((tb + 8, CW), F32)],
                  compiler_params=_cp(("parallel",)))(cvg, cvg, cvg, cvg, conv_w, conv_b, ln_g, ln_b)


def _conv_bwd1(dcat, u1, u0, ln_g, ln_b, tb=512):
    T = u1.shape[0]
    hb = tb // HALO

    def body(d3_ref, u1_ref, u0_ref, u0p_ref, g_ref, be_ref, du1_ref, dw_ref, db_ref, dlg_ref, dlb_ref, pad_ref, d_ref,
             q_ref):
        i = pl.program_id(0)

        @pl.when(i == 0)
        def _():
            dw_ref[...] = jnp.zeros_like(dw_ref)
            db_ref[...] = jnp.zeros_like(db_ref)
            dlg_ref[...] = jnp.zeros_like(dlg_ref)
            dlb_ref[...] = jnp.zeros_like(dlb_ref)

        u1 = u1_ref[...]
        mu = jnp.mean(u1, axis=-1, keepdims=True)
        xc = u1 - mu
        rstd = lax.rsqrt(jnp.mean(xc * xc, axis=-1, keepdims=True) + LN_EPS)
        xh = xc * rstd
        u2 = xh * g_ref[...] + be_ref[...]
        s = _sigmoid(u2)
        du2 = d3_ref[...] * (s + u2 * s * (1.0 - s))
        dlg_ref[...] += jnp.sum(du2 * xh, axis=0, keepdims=True)
        dlb_ref[...] += jnp.sum(du2, axis=0, keepdims=True)
        dxh = du2 * g_ref[...]
        du1 = rstd * (dxh - jnp.mean(dxh, axis=-1, keepdims=True) - xh * jnp.mean(dxh * xh, axis=-1, keepdims=True))
        du1_ref[...] = du1
        db_ref[...] += jnp.sum(du1, axis=0, keepdims=True)
        pad_ref[0:HALO, :] = jnp.where(i > 0, u0p_ref[...], 0.0)
        pad_ref[HALO:HALO + tb, :] = u0_ref[...]
        d_ref[0:8, :] = jnp.zeros((8, CW), F32)
        d_ref[8:8 + tb, :] = du1
        d_ref[8 + tb:16 + tb, :] = jnp.zeros((8, CW), F32)
        for p, taps in _taps_by_phase([HALO - (CK - 1) + kk for kk in range(CK)]):
            n = tb + 8
            q_ref[...] = d_ref[8 - p:8 - p + n, :]
            for k, m in taps:
                if 8 * m + n <= tb + HALO:
                    dw_ref[k:k + 1, :] += jnp.sum(q_ref[...] * pad_ref[8 * m:8 * m + n, :], axis=0, keepdims=True)
                else:
                    dw_ref[k:k + 1, :] += jnp.sum(q_ref[0:tb, :] * pad_ref[8 * m:8 * m + tb, :], axis=0, keepdims=True)

    cur = pl.BlockSpec((tb, CW), lambda i: (i, 0))
    vec = pl.BlockSpec((1, CW), lambda i: (0, 0))
    return _pcall(body, "conv_bwd1", grid=(T // tb,),
                  in_specs=[pl.BlockSpec((tb, CW), lambda i: (i, 1)), cur, cur,
                            pl.BlockSpec((HALO, CW), lambda i: (jnp.maximum(i * hb - 1, 0), 0)), vec, vec],
                  out_specs=[cur, pl.BlockSpec((HALO, CW), lambda i: (0, 0)), vec, vec, vec],
                  out_shape=[jax.ShapeDtypeStruct((T, CW), F32), jax.ShapeDtypeStruct((HALO, CW), F32)]
                  + [jax.ShapeDtypeStruct((1, CW), F32)] * 3,
                  scratch_shapes=[pltpu.VMEM((tb + HALO, CW), F32), pltpu.VMEM((tb + 16, CW), F32),
                                  pltpu.VMEM((tb + 8, CW), F32)],
                  compiler_params=_cp(("arbitrary",)))(dcat, u1, u0, u0, ln_g, ln_b)


def _conv_bwd2(du1, cvg, conv_w, tb=512):
    T = du1.shape[0]
    hb = tb // HALO
    last = T // HALO - 1
    nblk = T // tb

    def body(d_ref, dn_ref, cv_ref, cg_ref, w_ref, o_ref, pad_ref, ph_ref):
        i = pl.program_id(0)
        pad_ref[0:tb, :] = d_ref[...]
        pad_ref[tb:tb + HALO, :] = jnp.where(i < nblk - 1, dn_ref[...], 0.0)
        acc = _shifted_tap_sum(w_ref, pad_ref, ph_ref, [CK - 1 - kk for kk in range(CK)], tb)
        sg = _sigmoid(cg_ref[...])
        o_ref[:, 0:CW] = (acc * sg).astype(BF16)
        o_ref[:, CW:2 * CW] = (acc * cv_ref[...] * sg * (1.0 - sg)).astype(BF16)

    cur = pl.BlockSpec((tb, CW), lambda i: (i, 0))
    return _pcall(body, "conv_bwd2", grid=(nblk,),
                  in_specs=[cur, pl.BlockSpec((HALO, CW), lambda i: (jnp.minimum((i + 1) * hb, last), 0)),
                            pl.BlockSpec((tb, CW), lambda i: (i, 0)), pl.BlockSpec((tb, CW), lambda i: (i, 1)),
                            pl.BlockSpec((HALO, CW), lambda i: (0, 0))],
                  out_specs=pl.BlockSpec((tb, 2 * CW), lambda i: (i, 0)),
                  out_shape=jax.ShapeDtypeStruct((T, 2 * CW), BF16),
                  scratch_shapes=[pltpu.VMEM((tb + HALO, CW), F32), pltpu.VMEM((tb + 8, CW), F32)],
                  compiler_params=_cp(("parallel",)))(du1, du1, cvg, cvg, conv_w)


def _adam_math(w, g, m, v):
    nm = ADAM_B1 * m + (1.0 - ADAM_B1) * g
    nv = ADAM_B2 * v + (1.0 - ADAM_B2) * (g * g)
    delta = -ADAM_LR * ((nm * ADAM_C1) / (jnp.sqrt(nv * ADAM_C2) + ADAM_EPS) + ADAM_WD * w)
    return delta, nm, nv


def _adamw(w, gslots, m, v, name, tb):
    R, C = w.shape
    S = gslots.shape[0]

    def body(w_ref, gs_ref, m_ref, v_ref, g_ref, d_ref, nm_ref, nv_ref):
        g = gs_ref[0].astype(F32)
        for s in range(1, S):
            g = g + gs_ref[s].astype(F32)
        g_ref[...] = g
        d_ref[...], nm_ref[...], nv_ref[...] = _adam_math(w_ref[...], g, m_ref[...], v_ref[...])

    blk = pl.BlockSpec((tb, C), lambda i: (i, 0))
    return _pcall(body, name, grid=(R // tb,),
                  in_specs=[blk, pl.BlockSpec((S, tb, C), lambda i: (0, i, 0)), blk, blk],
                  out_specs=[blk] * 4, out_shape=[jax.ShapeDtypeStruct((R, C), F32)] * 4,
                  compiler_params=_cp(("parallel",)))(w, gslots, m, v)


def _adamw_small(gall, gattn, gconvw, ws, ms, vs):
    n = len(ws)

    def body(*refs):
        gall_ref, gattn_ref, gconvw_ref = refs[:3]
        w_refs, m_refs, v_refs = refs[3:3 + n], refs[3 + n:3 + 2 * n], refs[3 + 2 * n:3 + 3 * n]
        loss_ref = refs[3 + 3 * n]
        outs = refs[4 + 3 * n:]
        g_refs, d_refs, nm_refs, nv_refs = outs[:n], outs[n:2 * n], outs[2 * n:3 * n], outs[3 * n:]

        def total(ref):
            t = ref[0]
            for dev in range(1, NDEV):
                t = t + ref[dev]
            return t

        tot = total(gall_ref)
        grads = [tot[0:9, :]] + [tot[ROW_GAINS + k:ROW_GAINS + k + 1, :] for k in range(6)]
        grads += [total(gattn_ref), tot[ROW_ATTN_CB:ROW_ATTN_CB + 1, CW:2 * CW], tot[ROW_LN:ROW_LN + 1, 0:CW],
                  tot[ROW_LN:ROW_LN + 1, CW:2 * CW], total(gconvw_ref)]
        loss_ref[...] = tot[ROW_LOSS:ROW_LOSS + 1, 0:1]
        for k in range(n):
            g_refs[k][...] = grads[k]
            d_refs[k][...], nm_refs[k][...], nv_refs[k][...] = _adam_math(w_refs[k][...], grads[k], m_refs[k][...],
                                                                          v_refs[k][...])

    shapes = [jax.ShapeDtypeStruct(w.shape, F32) for w in ws]
    res = _pcall(body, "adamw_small", out_shape=[jax.ShapeDtypeStruct((1, 1), F32)] + shapes * 4,
                 compiler_params=_cp())(gall, gattn, gconvw, *ws, *ms, *vs)
    return res[0], [res[1 + k * n:1 + (k + 1) * n] for k in range(4)]


def _ffn_fwd(x, g_pre, g_post, shift, scale, gate, w_in, w_out4, tag, tm, comm=None):
    h = _pre_fwd(x, g_pre, shift, scale, "pre_fwd_" + tag)
    gu, a, got = _ffn_in(h, w_in, "ffn_in_" + tag, comm)
    f, out = _mm_post([a], [pl.BlockSpec((NSL, tm, SL), lambda i: (0, i, 0))],
                      [w_out4], [pl.BlockSpec((NSL, SL, D), lambda i: (0, 0, 0))],
                      x, g_post, gate, 0.5, "ffn_out_" + tag, tm)
    return out, (x, h, gu, a, f), got


def _ffn_bwd(dout, saved, g_pre, g_post, scale, gate, w_in, w_out4, tag, tmb, tmw, send_in=True):
    x, h, gu, a, f = saved
    T = x.shape[0]
    df, dgate, dg_post = _post_bwd(dout, f, g_post, gate, 0.5, "post_bwd_" + tag)
    dgu = _ffn_out_bwd(df, w_out4, gu, "ffn_out_bwd_" + tag)
    dw_out, _ = _mm_tn(a, pl.BlockSpec((1, tmw, SL), lambda j, i: (j, i, 0)),
                       df, pl.BlockSpec((tmw, D), lambda j, i: (i, 0)),
                       (NSL, SL, D), pl.BlockSpec((1, SL, D), lambda j, i: (j, 0, 0)), (NSL, T // tmw), "dw_out_" + tag)
    dw_in, (r_out,) = _dw_in(h, dgu, "dw_in_" + tag, tmw, comm=("a2a", [dw_out.reshape(NDEV, SL // 2, D)]))
    dw_in = dw_in.reshape(NDEV, D, SL)

    def dh_fn(a_ref, w_ref):
        dh = None
        for p in range(2):
            for j in range(NSL):
                t = _dot_nt(a_ref[j, p], w_ref[NSL * p + j])
                dh = t if dh is None else dh + t
        return dh

    (dx, dshift, dscale, dg_pre), got = _mm_prebwd(
        dgu, pl.BlockSpec((NSL, 2, tmb, SL), lambda i: (0, 0, i, 0)),
        w_in, pl.BlockSpec((NDEV, D, SL), lambda i: (0, 0, 0), pipeline_mode=pl.Buffered(1)),
        dh_fn, x, dout, g_pre, scale, "ffn_in_bwd_" + tag, tmb, comm=("a2a", [dw_in]) if send_in else None)
    return dx, got[0] if send_in else dw_in, r_out, dg_pre, dg_post, (dshift, dscale, dgate)


def kernel(x, c, w_ada, b_ada, g_pre_ff1, g_post_ff1, ff1_w_in, ff1_w_out, g_pre_mix, g_post_mix, w_in_mix, g_attn_out, conv_w, conv_b, conv_ln_g, conv_ln_b, w_out_mix, g_pre_ff2, g_post_ff2, ff2_w_in, ff2_w_out, loss_target, m_w_ada, m_b_ada, m_g_pre_ff1, m_g_post_ff1, m_ff1_w_in, m_ff1_w_out, m_g_pre_mix, m_g_post_mix, m_w_in_mix, m_g_attn_out, m_conv_w, m_conv_b, m_conv_ln_g, m_conv_ln_b, m_w_out_mix, m_g_pre_ff2, m_g_post_ff2, m_ff2_w_in, m_ff2_w_out, v_w_ada, v_b_ada, v_g_pre_ff1, v_g_post_ff1, v_ff1_w_in, v_ff1_w_out, v_g_pre_mix, v_g_post_mix, v_w_in_mix, v_g_attn_out, v_conv_w, v_conv_b, v_conv_ln_g, v_conv_ln_b, v_w_out_mix, v_g_pre_ff2, v_g_post_ff2, v_ff2_w_in, v_ff2_w_out):
    me = 4 * lax.axis_index("x") + 2 * lax.axis_index("y") + lax.axis_index("c")
    T = x.shape[1]
    tq = min(256, T)
    tm = 512
    tmb = 512
    tmw = 2048
    x0 = x.reshape(T, D)
    tgt = loss_target.reshape(T, D)
    row = lambda a: a.reshape(1, -1)

    small_in = jnp.concatenate([c.reshape(-1), jnp.pad(conv_w.reshape(-1), (0, 2 * D - CK * 64)),
                                jnp.zeros((5 * D,), F32)]).reshape(8, D)
    small_all, = _all_gather([small_in], "gather_c_convw", True)
    c_all = small_all[:, 0, :]
    conv_w_full = small_all[:, 1:3, :].reshape(NDEV, 2 * D)[:, :CK * 64].reshape(NDEV, CK, 64)
    conv_w_full = conv_w_full.transpose(1, 0, 2).reshape(CK, CW)
    conv_w_pad = jnp.pad(conv_w_full, ((0, HALO - CK), (0, 0)))

    big = [ff1_w_in, ff1_w_out, w_in_mix, w_out_mix, ff2_w_in, ff2_w_out]
    shards = [w.astype(BF16) for w in big]
    w_in1, w_out1 = _all_gather(shards[0:2], "gather_weights_ff1", False)
    w_out1_4 = w_out1.reshape(NSL, SL, D)

    b_cols = lax.dynamic_slice(b_ada, (me * ADA_COLS,), (ADA_COLS,)).reshape(1, ADA_COLS)
    mod_cols = _ada_fwd(c_all, w_ada, b_cols)
    mod_all, = _all_gather([mod_cols], "gather_mod", True)
    mod = lax.dynamic_slice(mod_all, (0, me, 0), (NDEV, 1, ADA_COLS)).reshape(9, D)
    sh = lambda s: mod[3 * s:3 * s + 1]
    sc = lambda s: mod[3 * s + 1:3 * s + 2]
    gt = lambda s: mod[3 * s + 2:3 * s + 3]

    x1, sv1, (w_inm_s, w_outm_s) = _ffn_fwd(x0, row(g_pre_ff1), row(g_post_ff1), sh(0), sc(0), gt(0), w_in1, w_out1_4,
                                            "ff1", tm, comm=("gather", shards[2:4]))
    w_inm = w_inm_s.transpose(1, 0, 2).reshape(D, MIXIN)
    w_outm = w_outm_s.reshape(D, D)
    hm = _pre_fwd(x1, row(g_pre_mix), sh(1), sc(1), "pre_fwd_mix")
    qkv, cvg = _mix_in(hm, w_inm, "mix_in")
    g_attn_row = row(g_attn_out)
    o_att, an, (w_in2, w_out2) = _attn_fwd(qkv, g_attn_row, tq, comm=("gather", shards[4:6]))
    w_out2_4 = w_out2.reshape(NSL, SL, D)
    u0, u1, u3 = _conv_fwd(cvg, conv_w_pad, row(conv_b), row(conv_ln_g), row(conv_ln_b))
    half = lambda k: pl.BlockSpec((AW, D), lambda i: (k, 0))
    act = pl.BlockSpec((tm, AW), lambda i: (i, 0))
    fm, x2 = _mm_post([an, u3], [act, act], [w_outm, w_outm], [half(0), half(1)],
                      x1, row(g_post_mix), gt(1), 1.0, "mix_out", tm)
    x3, sv2, _ = _ffn_fwd(x2, row(g_pre_ff2), row(g_post_ff2), sh(2), sc(2), gt(2), w_in2, w_out2_4, "ff2", tm)
    dy, loss_part = _loss_head(x3, tgt, "loss_head")

    dx2, dw_in2, r_out2, dgpre2, dgpost2, dmod2 = _ffn_bwd(
        dy, sv2, row(g_pre_ff2), row(g_post_ff2), sc(2), gt(2), w_in2, w_out2_4, "ff2", tmb, tmw, send_in=False)

    dfm, dgate1, dgpostm = _post_bwd(dx2, fm, row(g_post_mix), gt(1), 1.0, "post_bwd_mix")
    dcat = _mm_nt(dfm, w_outm, "mix_out_bwd")
    tok = pl.BlockSpec((tmw, AW), lambda j, i: (i, 0))
    tokd = pl.BlockSpec((tmw, D), lambda j, i: (i, 0))
    whole = pl.BlockSpec((AW, D), lambda j, i: (0, 0))
    dw_outm = jnp.concatenate([_mm_tn(an, tok, dfm, tokd, (AW, D), whole, (1, T // tmw), "dw_out_mix_a")[0],
                               _mm_tn(u3, tok, dfm, tokd, (AW, D), whole, (1, T // tmw), "dw_out_mix_c")[0]], axis=0)
    (dq, dk, dv, dg_attn), (r_in2,) = _attn_bwd(qkv, o_att, dcat, g_attn_row, tq, comm=("a2a", [dw_in2]))
    du1, dconv_w, dconv_b, dln_g, dln_b = _conv_bwd1(dcat, u1, u0, row(conv_ln_g), row(conv_ln_b))
    dcvg = _conv_bwd2(du1, cvg, conv_w_pad)
    dproj = jnp.concatenate([dq, dk.astype(BF16), dv.astype(BF16), dcvg], axis=1)
    dw_inm, _ = _mm_tn(hm, pl.BlockSpec((tmw, D), lambda j, i: (i, 0)),
                       dproj, pl.BlockSpec((tmw, MIXIN // 2), lambda j, i: (i, j)),
                       (D, MIXIN), pl.BlockSpec((D, MIXIN // 2), lambda j, i: (0, j)), (2, T // tmw), "dw_in_mix")
    (dx1, dshift1, dscale1, dgprem), (r_inm, r_outm) = _mm_prebwd(
        dproj, pl.BlockSpec((tmb, MIXIN), lambda i: (i, 0)), w_inm, pl.BlockSpec((D, MIXIN), lambda i: (0, 0)),
        lambda a_ref, w_ref: _dot_nt(a_ref[...], w_ref[...]), x1, dx2, row(g_pre_mix), sc(1), "mix_in_bwd", tmb,
        comm=("a2a", [dw_inm.reshape(D, NDEV, 320).transpose(1, 0, 2), dw_outm.reshape(NDEV, 128, D)]))

    dx0, r_in1, r_out1, dgpre1, dgpost1, dmod0 = _ffn_bwd(
        dx1, sv1, row(g_pre_ff1), row(g_post_ff1), sc(0), gt(0), w_in1, w_out1_4, "ff1", tmb, tmw)
    recvs = [r_in1, r_out1, r_inm, r_outm, r_in2, r_out2]

    zrow = jnp.zeros((1, D), F32)
    small_g = jnp.concatenate(
        list(dmod0) + [dshift1, dscale1, dgate1] + list(dmod2)
        + [dgpre1, dgpost1, dgprem, dgpostm, dgpre2, dgpost2]
        + [jnp.concatenate([dg_attn, dconv_b], axis=1), jnp.concatenate([dln_g, dln_b], axis=1),
           jnp.pad(dconv_w[:CK].reshape(-1), (0, CONVW_ROWS * D - CK * CW)).reshape(CONVW_ROWS, D),
           jnp.pad(loss_part, ((0, 0), (0, D - 1)))] + [zrow] * (SMALL_R - ROW_LOSS - 1), axis=0)
    small_g_all, = _all_gather([small_g], "gather_small_grads", True)

    dmod_all = small_g_all[:, 0:9, :].reshape(NDEV, NMOD)
    dmod_cols = lax.dynamic_slice(dmod_all, (0, me * ADA_COLS), (NDEV, ADA_COLS))
    g_w_ada = _ada_bwd(c_all.T, dmod_cols)

    gattn = small_g_all[:, ROW_ATTN_CB, 0:AW].reshape(NDEV, 8, HD)
    gconvw = small_g_all[:, ROW_CONVW:ROW_CONVW + CONVW_ROWS, :].reshape(NDEV, CONVW_ROWS * D)[:, :CK * CW]
    gconvw = lax.dynamic_slice(gconvw.reshape(NDEV, CK, CW), (0, 0, me * 64), (NDEV, CK, 64))

    def small_list(b, g6, ga, cb, lg, lb, cw):
        return [b.reshape(9, D)] + [row(g) for g in g6] + [ga, row(cb), row(lg), row(lb), cw]

    sw = small_list(b_ada, [g_pre_ff1, g_post_ff1, g_pre_mix, g_post_mix, g_pre_ff2, g_post_ff2], g_attn_out,
                    conv_b, conv_ln_g, conv_ln_b, conv_w)
    sm = small_list(m_b_ada, [m_g_pre_ff1, m_g_post_ff1, m_g_pre_mix, m_g_post_mix, m_g_pre_ff2, m_g_post_ff2],
                    m_g_attn_out, m_conv_b, m_conv_ln_g, m_conv_ln_b, m_conv_w)
    sv = small_list(v_b_ada, [v_g_pre_ff1, v_g_post_ff1, v_g_pre_mix, v_g_post_mix, v_g_pre_ff2, v_g_post_ff2],
                    v_g_attn_out, v_conv_b, v_conv_ln_g, v_conv_ln_b, v_conv_w)
    loss, s_out = _adamw_small(small_g_all, gattn, gconvw, sw, sm, sv)
    s_out = [[o.reshape(w.shape) for o, w in zip(outs, [b_ada, g_pre_ff1, g_post_ff1, g_pre_mix, g_post_mix,
                                                        g_pre_ff2, g_post_ff2, g_attn_out, conv_b, conv_ln_g,
                                                        conv_ln_b, conv_w])] for outs in s_out]

    big_m = [m_ff1_w_in, m_ff1_w_out, m_w_in_mix, m_w_out_mix, m_ff2_w_in, m_ff2_w_out]
    big_v = [v_ff1_w_in, v_ff1_w_out, v_w_in_mix, v_w_out_mix, v_ff2_w_in, v_ff2_w_out]
    tbs = [256, 176, 256, 128, 256, 176]
    tags = ["ff1_w_in", "ff1_w_out", "w_in_mix", "w_out_mix", "ff2_w_in", "ff2_w_out"]
    b_out = [_adamw(big[k], recvs[k], big_m[k], big_v[k], "adamw_" + tags[k], tbs[k]) for k in range(6)]
    a_out = _adamw(w_ada, g_w_ada.reshape(1, D, ADA_COLS), m_w_ada, v_w_ada, "adamw_ada", 256)

    def leaves(k):
        s = s_out[k]
        b = [o[k] for o in b_out]
        return [a_out[k], s[0], s[1], s[2], b[0], b[1], s[3], s[4], b[2], s[7], s[11], s[8], s[9], s[10], b[3],
                s[5], s[6], b[4], b[5]]

    return (loss.reshape(()), dx0.reshape(1, T, D), *leaves(0), *leaves(1), *leaves(2), *leaves(3))
```

```python
import functools

import jax
import jax.numpy as jnp
from jax import lax
from jax.experimental import pallas as pl
from jax.experimental.pallas import tpu as pltpu

F32 = jnp.float32
BF16 = jnp.bfloat16
D = 1024
DFF = 2816
SL = 704
NSL = DFF // SL
AW = 512
HD = 64
CW = 512
CK = 31
HALO = 32
MIXIN = 2560
NDEV = 8
NMOD = 9 * D
ADA_COLS = NMOD // NDEV
RMS_EPS = 1e-6
LN_EPS = 1e-5
QK_SCALE = HD ** -0.5
W_ZERO_BELOW = -104.0
ADAM_LR, ADAM_B1, ADAM_B2, ADAM_EPS, ADAM_WD, ADAM_STEP = 0.001, 0.9, 0.999, 1e-08, 0.01, 10
ADAM_C1 = 1.0 / (1.0 - ADAM_B1 ** ADAM_STEP)
ADAM_C2 = 1.0 / (1.0 - ADAM_B2 ** ADAM_STEP)
MIB = 1024 * 1024
MESH = pl.DeviceIdType.MESH

ROW_GAINS = 9
ROW_ATTN_CB = 15
ROW_LN = 16
ROW_CONVW = 17
CONVW_ROWS = 16
ROW_LOSS = 33
SMALL_R = 40


def _pcall(body, name, **kw):
    return pl.pallas_call(body, name=name, **kw)


def _cp(sem=None, vmem_mib=48):
    if sem is None:
        return pltpu.CompilerParams(vmem_limit_bytes=vmem_mib * MIB)
    return pltpu.CompilerParams(dimension_semantics=sem, vmem_limit_bytes=vmem_mib * MIB)


def _dot(a, b):
    return jnp.dot(a, b, preferred_element_type=F32)


def _dot_nt(a, b):
    return lax.dot_general(a, b, (((1,), (1,)), ((), ())), preferred_element_type=F32)


def _dot_tn(a, b):
    return lax.dot_general(a, b, (((0,), (0,)), ((), ())), preferred_element_type=F32)


def _sigmoid(x):
    return 0.5 * jnp.tanh(0.5 * x) + 0.5


def _split2(x):
    hi = x.astype(BF16)
    mid = (x - hi.astype(F32)).astype(BF16)
    return hi, mid


def _mat(ref):
    lead = len(ref.shape) - 2
    return ref[(0,) * lead] if lead else ref[...]


def _all_gather(xs, name, in_vmem):
    n = len(xs)

    def body(*refs):
        x_refs, out_refs = refs[:n], refs[n:2 * n]
        send_sems, recv_sems, local_sems = refs[2 * n:]
        mx, my, mc = lax.axis_index("x"), lax.axis_index("y"), lax.axis_index("c")
        me, sibling = (mx, my, mc), (mx, my, 1 - mc)
        chips = [(1 - mx, my), (mx, 1 - my), (1 - mx, 1 - my)]

        def slab(a, px, py, pc):
            return out_refs[a].at[4 * px + 2 * py + pc]

        def copy(a, k, block, to, src=None):
            return pltpu.make_async_remote_copy(
                src_ref=slab(a, *block) if src is None else src, dst_ref=slab(a, *block),
                send_sem=send_sems.at[a, k], recv_sem=recv_sems.at[a, k], device_id=to, device_id_type=MESH)

        mine = [pltpu.make_async_copy(x_refs[a], slab(a, *me), local_sems.at[a]) for a in range(n)]
        for cp in mine:
            cp.start()
        first = []
        for a in range(n):
            first.append(copy(a, 0, me, sibling, src=x_refs[a]))
            first += [copy(a, 1 + j, me, (*chip, mc), src=x_refs[a]) for j, chip in enumerate(chips)]
        for cp in first:
            cp.start()
        passed = []
        for j, chip in enumerate(chips):
            for a in range(n):
                copy(a, 1 + j, (*chip, mc), me).wait_recv()
                passed.append(copy(a, 4 + j, (*chip, mc), sibling))
                passed[-1].start()
        for a in range(n):
            copy(a, 0, sibling, me).wait_recv()
            for j, chip in enumerate(chips):
                copy(a, 4 + j, (*chip, 1 - mc), me).wait_recv()
        for cp in first + passed:
            cp.wait_send()
        for cp in mine:
            cp.wait()

    space = pltpu.VMEM if in_vmem else pl.ANY
    return _pcall(
        body, name,
        out_shape=[jax.ShapeDtypeStruct((NDEV,) + x.shape, x.dtype) for x in xs],
        in_specs=[pl.BlockSpec(memory_space=space)] * n,
        out_specs=[pl.BlockSpec(memory_space=space)] * n,
        scratch_shapes=[pltpu.SemaphoreType.DMA((n, 7)), pltpu.SemaphoreType.DMA((n, 7)),
                        pltpu.SemaphoreType.DMA((n,))],
    )(*xs)


def _exchange_copies(kind, src, dst, send_sems, recv_sems, local_sems):
    mx, my, mc = lax.axis_index("x"), lax.axis_index("y"), lax.axis_index("c")
    me = 4 * mx + 2 * my + mc
    n = len(src)
    pick = (lambda a, p: src[a].at[p]) if kind == "a2a" else (lambda a, p: src[a])
    mine = [pltpu.make_async_copy(pick(a, me), dst[a].at[me], local_sems.at[a]) for a in range(n)]
    copies = []
    for r in range(1, NDEV):
        px = 1 - mx if r & 4 else mx
        py = 1 - my if r & 2 else my
        pc = 1 - mc if r & 1 else mc
        for a in range(n):
            copies.append(pltpu.make_async_remote_copy(
                src_ref=pick(a, 4 * px + 2 * py + pc), dst_ref=dst[a].at[me],
                send_sem=send_sems.at[a, r - 1], recv_sem=recv_sems.at[a, r - 1],
                device_id=(px, py, pc), device_id_type=MESH))
    return mine, copies


def _hosted_call(body, name, comm, grid, in_specs, out_specs, out_shape, scratch_shapes, sem, vmem_mib, args):
    if comm is None:
        outs = _pcall(body, name, grid=grid, in_specs=in_specs, out_specs=out_specs, out_shape=out_shape,
                      scratch_shapes=scratch_shapes, compiler_params=_cp(sem, vmem_mib))(*args)
        return outs, []
    kind, arrs = comm
    nc, n_in, n_out, n_scr = len(arrs), len(in_specs), len(out_specs), len(scratch_shapes)
    rank = len(grid)

    def wrapped(*refs):
        ins, csrc = refs[:n_in], refs[n_in:n_in + nc]
        outs, cdst = refs[n_in + nc:n_in + nc + n_out], refs[n_in + nc + n_out:n_in + 2 * nc + n_out]
        rest = refs[n_in + 2 * nc + n_out:]
        scr, sems = rest[:n_scr], rest[n_scr:]
        first = functools.reduce(jnp.logical_and, [pl.program_id(d) == 0 for d in range(rank)])
        last = functools.reduce(jnp.logical_and, [pl.program_id(d) == grid[d] - 1 for d in range(rank)])

        @pl.when(first)
        def _():
            mine, copies = _exchange_copies(kind, csrc, cdst, *sems)
            for cp in mine + copies:
                cp.start()

        body(*ins, *outs, *scr)

        @pl.when(last)
        def _():
            mine, copies = _exchange_copies(kind, csrc, cdst, *sems)
            for cp in copies:
                cp.wait_recv()
            for cp in copies:
                cp.wait_send()
            for cp in mine:
                cp.wait()

    hbm = pl.BlockSpec(memory_space=pl.ANY)
    cshape = [jax.ShapeDtypeStruct(a.shape if kind == "a2a" else (NDEV,) + a.shape, a.dtype) for a in arrs]
    res = _pcall(wrapped, name, grid=grid, in_specs=list(in_specs) + [hbm] * nc,
                 out_specs=list(out_specs) + [hbm] * nc, out_shape=list(out_shape) + cshape,
                 scratch_shapes=list(scratch_shapes) + [pltpu.SemaphoreType.DMA((nc, 7)),
                                                        pltpu.SemaphoreType.DMA((nc, 7)),
                                                        pltpu.SemaphoreType.DMA((nc,))],
                 compiler_params=_cp(("arbitrary",) * rank, vmem_mib))(*args, *arrs)
    return res[:n_out], res[n_out:]


def _ada_fwd(c_all, w, b):
    n = w.shape[1]

    def body(c_ref, w_ref, b_ref, o_ref):
        c = c_ref[...]
        s = c * _sigmoid(c)
        o_ref[...] = jnp.dot(s, w_ref[...], preferred_element_type=F32, precision=lax.Precision.HIGHEST) + b_ref[...]

    return _pcall(body, "ada_fwd", out_shape=jax.ShapeDtypeStruct((NDEV, n), F32), compiler_params=_cp())(c_all, w, b)


def _ada_bwd(c_all_t, dmod):
    n = dmod.shape[1]

    def body(ct_ref, d_ref, o_ref):
        ct = ct_ref[...]
        s = ct * _sigmoid(ct)
        acc = s[:, 0:1] * d_ref[0:1, :]
        for b in range(1, NDEV):
            acc = acc + s[:, b:b + 1] * d_ref[b:b + 1, :]
        o_ref[...] = acc

    return _pcall(body, "ada_bwd", out_shape=jax.ShapeDtypeStruct((D, n), F32), compiler_params=_cp())(c_all_t, dmod)


def _pre_fwd(x, g, shift, scale, name, tb=512):
    T = x.shape[0]

    def body(x_ref, g_ref, sh_ref, sc_ref, h_ref):
        xv = x_ref[...]
        r = lax.rsqrt(jnp.mean(xv * xv, axis=-1, keepdims=True) + RMS_EPS)
        h_ref[...] = ((xv * r) * g_ref[...] * (1.0 + sc_ref[...]) + sh_ref[...]).astype(BF16)

    row = pl.BlockSpec((tb, D), lambda i: (i, 0))
    vec = pl.BlockSpec((1, D), lambda i: (0, 0))
    return _pcall(body, name, grid=(T // tb,), in_specs=[row, vec, vec, vec], out_specs=row,
                  out_shape=jax.ShapeDtypeStruct((T, D), BF16), compiler_params=_cp(("parallel",)))(x, g, shift, scale)


def _ffn_in(h, w_in, name, comm=None, tm=512):
    T = h.shape[0]

    def body(h_ref, w_ref, gu_ref, a_ref):
        hv = h_ref[...]
        for j in range(NSL):
            g = _dot(hv, w_ref[j])
            u = _dot(hv, w_ref[j + NSL])
            gu_ref[j, 0] = g.astype(BF16)
            gu_ref[j, 1] = u.astype(BF16)
            a_ref[j] = (g * _sigmoid(g) * u).astype(BF16)

    (gu, a), got = _hosted_call(
        body, name, comm, (T // tm,),
        [pl.BlockSpec((tm, D), lambda i: (i, 0)),
         pl.BlockSpec((NDEV, D, SL), lambda i: (0, 0, 0), pipeline_mode=pl.Buffered(1))],
        [pl.BlockSpec((NSL, 2, tm, SL), lambda i: (0, 0, i, 0)), pl.BlockSpec((NSL, tm, SL), lambda i: (0, i, 0))],
        [jax.ShapeDtypeStruct((NSL, 2, T, SL), BF16), jax.ShapeDtypeStruct((NSL, T, SL), BF16)],
        [], ("parallel",), 48, (h, w_in))
    return gu, a, got


def _mm_post(a_list, a_specs, w_list, w_specs, x, g_post, gate, res_w, name, tm, tgt=None):
    T = x.shape[0]
    n = len(a_list)
    with_loss = tgt is not None

    def body(*refs):
        a_refs, w_refs = refs[:n], refs[n:2 * n]
        x_ref, g_ref, gt_ref = refs[2 * n:2 * n + 3]
        rest = refs[2 * n + 3:]
        f = None
        for a_ref, w_ref in zip(a_refs, w_refs):
            if len(a_ref.shape) == 3:
                terms = [_dot(a_ref[j], w_ref[j]) for j in range(a_ref.shape[0])]
            else:
                terms = [_dot(a_ref[...], w_ref[...])]
            for t in terms:
                f = t if f is None else f + t
        r = lax.rsqrt(jnp.mean(f * f, axis=-1, keepdims=True) + RMS_EPS)
        y = (f * r) * g_ref[...]
        out = x_ref[...] + (res_w * (1.0 + gt_ref[...])) * y
        if with_loss:
            t_ref, f_ref, dy_ref, l_ref = rest

            @pl.when(pl.program_id(0) == 0)
            def _():
                l_ref[...] = jnp.zeros_like(l_ref)

            e = out - t_ref[...]
            dy_ref[...] = e * (1.0 / D)
            l_ref[...] += 0.5 * jnp.sum(jnp.mean(e * e, axis=-1, keepdims=True), axis=0, keepdims=True)
        else:
            f_ref, o_ref = rest
            o_ref[...] = out
        f_ref[...] = f

    row = pl.BlockSpec((tm, D), lambda i: (i, 0))
    vec = pl.BlockSpec((1, D), lambda i: (0, 0))
    big = jax.ShapeDtypeStruct((T, D), F32)
    if with_loss:
        return _pcall(body, name, grid=(T // tm,),
                      in_specs=list(a_specs) + list(w_specs) + [row, vec, vec, row],
                      out_specs=[row, row, pl.BlockSpec((1, 1), lambda i: (0, 0))],
                      out_shape=[big, big, jax.ShapeDtypeStruct((1, 1), F32)],
                      compiler_params=_cp(("arbitrary",)))(*a_list, *w_list, x, g_post, gate, tgt)
    return _pcall(body, name, grid=(T // tm,),
                  in_specs=list(a_specs) + list(w_specs) + [row, vec, vec], out_specs=[row, row],
                  out_shape=[big, big], compiler_params=_cp(("parallel",)))(*a_list, *w_list, x, g_post, gate)


def _post_bwd(dout, f, g_post, gate, res_w, name, tb=512):
    T = f.shape[0]

    def body(do_ref, f_ref, g_ref, gt_ref, df_ref, dgate_ref, dg_ref):
        @pl.when(pl.program_id(0) == 0)
        def _():
            dgate_ref[...] = jnp.zeros_like(dgate_ref)
            dg_ref[...] = jnp.zeros_like(dg_ref)

        do = do_ref[...]
        f = f_ref[...]
        r = lax.rsqrt(jnp.mean(f * f, axis=-1, keepdims=True) + RMS_EPS)
        fn = f * r
        dgate_ref[...] += jnp.sum((res_w * do) * (fn * g_ref[...]), axis=0, keepdims=True)
        dy = (res_w * (1.0 + gt_ref[...])) * do
        dg_ref[...] += jnp.sum(dy * fn, axis=0, keepdims=True)
        dyg = dy * g_ref[...]
        df = r * (dyg - fn * jnp.mean(dyg * fn, axis=-1, keepdims=True))
        df_ref[...] = df.astype(BF16)

    row = pl.BlockSpec((tb, D), lambda i: (i, 0))
    vec = pl.BlockSpec((1, D), lambda i: (0, 0))
    return _pcall(body, name, grid=(T // tb,), in_specs=[row, row, vec, vec], out_specs=[row, vec, vec],
                  out_shape=[jax.ShapeDtypeStruct((T, D), BF16), jax.ShapeDtypeStruct((1, D), F32),
                             jax.ShapeDtypeStruct((1, D), F32)],
                  compiler_params=_cp(("arbitrary",)))(dout, f, g_post, gate)


def _ffn_out_bwd(df, w_out4, gu, name, comm=None, tm=512):
    T = df.shape[0]

    def body(df_ref, w_ref, gu_ref, dgu_ref):
        dfv = df_ref[...]
        for j in range(NSL):
            da = _dot_nt(dfv, w_ref[j])
            gv = gu_ref[j, 0].astype(F32)
            s = _sigmoid(gv)
            gs = gv * s
            dgu_ref[j, 0] = (da * gu_ref[j, 1].astype(F32) * (s + gs * (1.0 - s))).astype(BF16)
            dgu_ref[j, 1] = (da * gs).astype(BF16)

    gus = pl.BlockSpec((NSL, 2, tm, SL), lambda i: (0, 0, i, 0))
    (dgu,), got = _hosted_call(
        body, name, comm, (T // tm,),
        [pl.BlockSpec((tm, D), lambda i: (i, 0)),
         pl.BlockSpec((NSL, SL, D), lambda i: (0, 0, 0), pipeline_mode=pl.Buffered(1)), gus],
        [gus], [jax.ShapeDtypeStruct((NSL, 2, T, SL), BF16)], [], ("parallel",), 48, (df, w_out4, gu))
    return dgu, got


def _mm_tn(a, a_spec, b, b_spec, out_shape, out_spec, grid, name, comm=None):
    k, nn = out_spec.block_shape[-2:]
    steps = grid[1]

    def body(a_ref, b_ref, o_ref, acc_ref):
        i = pl.program_id(1)

        @pl.when(i == 0)
        def _():
            acc_ref[...] = jnp.zeros_like(acc_ref)

        acc_ref[...] += _dot_tn(_mat(a_ref), _mat(b_ref))

        @pl.when(i == steps - 1)
        def _():
            lead = len(o_ref.shape) - 2
            o_ref[(0,) * lead if lead else ...] = acc_ref[...].astype(BF16)

    (out,), got = _hosted_call(body, name, comm, grid, [a_spec, b_spec], [out_spec],
                               [jax.ShapeDtypeStruct(out_shape, BF16)], [pltpu.VMEM((k, nn), F32)],
                               ("parallel", "arbitrary"), 48, (a, b))
    return out, got


def _dw_in(h, dgu, name, tmw, comm=None):
    T = h.shape[0]
    steps = T // tmw

    def body(h_ref, b_ref, o_ref, acc_ref):
        i = pl.program_id(1)

        @pl.when(i == 0)
        def _():
            acc_ref[...] = jnp.zeros_like(acc_ref)

        hv = h_ref[...]
        for p in range(2):
            acc_ref[p] += _dot_tn(hv, b_ref[0, p])

        @pl.when(i == steps - 1)
        def _():
            o_ref[:, 0] = acc_ref[...].astype(BF16)

    (out,), got = _hosted_call(
        body, name, comm, (NSL, steps),
        [pl.BlockSpec((tmw, D), lambda j, i: (i, 0)), pl.BlockSpec((1, 2, tmw, SL), lambda j, i: (j, 0, i, 0))],
        [pl.BlockSpec((2, 1, D, SL), lambda j, i: (0, j, 0, 0))],
        [jax.ShapeDtypeStruct((2, NSL, D, SL), BF16)], [pltpu.VMEM((2, D, SL), F32)],
        ("parallel", "arbitrary"), 56, (h, dgu))
    return out, got


def _mm_prebwd(a, a_spec, w, w_spec, dh_fn, x, dout, g_pre, scale, name, tm=256, comm=None):
    T = x.shape[0]

    def body(a_ref, w_ref, x_ref, do_ref, g_ref, sc_ref, dx_ref, dsh_ref, dsc_ref, dg_ref):
        @pl.when(pl.program_id(0) == 0)
        def _():
            dsh_ref[...] = jnp.zeros_like(dsh_ref)
            dsc_ref[...] = jnp.zeros_like(dsc_ref)
            dg_ref[...] = jnp.zeros_like(dg_ref)

        dh = dh_fn(a_ref, w_ref)
        xv = x_ref[...]
        r = lax.rsqrt(jnp.mean(xv * xv, axis=-1, keepdims=True) + RMS_EPS)
        xn = xv * r
        dsh_ref[...] += jnp.sum(dh, axis=0, keepdims=True)
        dsc_ref[...] += jnp.sum(dh * (xn * g_ref[...]), axis=0, keepdims=True)
        dn = dh * (1.0 + sc_ref[...])
        dg_ref[...] += jnp.sum(dn * xn, axis=0, keepdims=True)
        dng = dn * g_ref[...]
        dx_ref[...] = do_ref[...] + r * (dng - xn * jnp.mean(dng * xn, axis=-1, keepdims=True))

    row = pl.BlockSpec((tm, D), lambda i: (i, 0))
    vec = pl.BlockSpec((1, D), lambda i: (0, 0))
    return _hosted_call(body, name, comm, (T // tm,), [a_spec, w_spec, row, row, vec, vec], [row, vec, vec, vec],
                        [jax.ShapeDtypeStruct((T, D), F32)] + [jax.ShapeDtypeStruct((1, D), F32)] * 3,
                        [], ("arbitrary",), 56, (a, w, x, dout, g_pre, scale))


def _mix_in(h, w, name, tm=512):
    T = h.shape[0]

    def body(h_ref, w_ref, qkv_ref, cvg_ref):
        p = _dot(h_ref[...], w_ref[...])
        qkv_ref[...] = p[:, :3 * AW].astype(BF16)
        cvg_ref[...] = p[:, 3 * AW:]

    return _pcall(body, name, grid=(T // tm,),
                  in_specs=[pl.BlockSpec((tm, D), lambda i: (i, 0)), pl.BlockSpec((D, MIXIN), lambda i: (0, 0))],
                  out_specs=[pl.BlockSpec((tm, 3 * AW), lambda i: (i, 0)), pl.BlockSpec((tm, 2 * CW), lambda i: (i, 0))],
                  out_shape=[jax.ShapeDtypeStruct((T, 3 * AW), BF16), jax.ShapeDtypeStruct((T, 2 * CW), F32)],
                  compiler_params=_cp(("parallel",)))(h, w)


def _mm_nt(a, w, name, tm=512):
    T, K = a.shape
    N = w.shape[0]

    def body(a_ref, w_ref, o_ref):
        o_ref[...] = _dot_nt(a_ref[...], w_ref[...])

    return _pcall(body, name, grid=(T // tm,),
                  in_specs=[pl.BlockSpec((tm, K), lambda i: (i, 0)), pl.BlockSpec((N, K), lambda i: (0, 0))],
                  out_specs=pl.BlockSpec((tm, N), lambda i: (i, 0)),
                  out_shape=jax.ShapeDtypeStruct((T, N), F32), compiler_params=_cp(("parallel",)))(a, w)


def _softplus_parts(z):
    ls = jnp.minimum(z, 0.0) - jnp.log(1.0 + jnp.exp(-jnp.abs(z)))
    return ls, ls - z


def _head_sum(x, first):
    sa = jnp.sum(jnp.where(first, x, 0.0), axis=-1, keepdims=True)
    sb = jnp.sum(jnp.where(first, 0.0, x), axis=-1, keepdims=True)
    return jnp.where(first, sa, sb)


def _attn_specs(T, tq):
    qs = pl.BlockSpec((tq, 128), lambda p, i: (i, p))
    ks = pl.BlockSpec((T, 128), lambda p, i: (0, 4 + p))
    vs = pl.BlockSpec((T, 128), lambda p, i: (0, 8 + p))
    gs = pl.BlockSpec((1, 128), lambda p, i: (0, p))
    return qs, ks, vs, gs


def _attn_fwd(qkv, g_attn, tq, comm=None):
    T = qkv.shape[0]

    def body(q_ref, k_ref, v_ref, g_ref, o_ref, an_ref):
        i = pl.program_id(1)
        first = lax.broadcasted_iota(jnp.int32, (tq, 128), 1) < HD
        q = (q_ref[...].astype(F32) * QK_SCALE).astype(BF16)
        zq = jnp.zeros_like(q)
        qs = (jnp.where(first, q, zq), jnp.where(first, zq, q))
        rows = lax.broadcasted_iota(jnp.int32, (tq, tq), 0)
        cols = lax.broadcasted_iota(jnp.int32, (tq, tq), 1)
        tri = (rows > cols).astype(BF16)
        tri2 = jnp.concatenate([tri, tri], axis=0)
        strict = cols < rows

        def tile(j, Rs, acc, masked):
            start = j * tq if isinstance(j, int) else pl.multiple_of(j * tq, tq)
            kb = k_ref[pl.ds(start, tq), :]
            vb = v_ref[pl.ds(start, tq), :]
            zs = [_dot_nt(qs[hh], kb) for hh in range(2)]
            parts = [_softplus_parts(z) for z in zs]
            lsms = [jnp.where(strict, p[1], 0.0) if masked else p[1] for p in parts]
            splits = [_split2(x) for x in lsms]
            afters = [_dot(jnp.concatenate(s, axis=1), tri2) for s in splits]
            ws = [jnp.exp(parts[hh][0] + afters[hh] + Rs[hh]) for hh in range(2)]
            if masked:
                ws = [jnp.where(strict, w, 0.0) for w in ws]
            outs = [_dot(w.astype(BF16), vb) for w in ws]
            new_r = [Rs[hh] + afters[hh][:, 0:1] + lsms[hh][:, 0:1] for hh in range(2)]
            return new_r[0], new_r[1], acc + jnp.where(first, outs[0], outs[1])

        zr = jnp.zeros((tq, 1), F32)

        def finish(acc):
            o_ref[...] = acc
            r = lax.rsqrt(_head_sum(acc * acc, first) * (1.0 / HD) + RMS_EPS)
            an_ref[...] = ((acc * r) * g_ref[...]).astype(BF16)

        @pl.when(i == 0)
        def _():
            finish(tile(0, (zr, zr), jnp.zeros((tq, 128), F32), True)[2])

        @pl.when(i > 0)
        def _():
            ra, rb, acc = tile(i, (zr, zr), jnp.zeros((tq, 128), F32), True)
            ra, rb, acc = tile(i - 1, (ra, rb), acc, False)

            def more(c):
                return jnp.logical_and(c[0] < i, jnp.maximum(jnp.max(c[1]), jnp.max(c[2])) > W_ZERO_BELOW)

            def step(c):
                ra, rb, acc = tile(i - 1 - c[0], (c[1], c[2]), c[3], False)
                return c[0] + 1, ra, rb, acc

            finish(lax.while_loop(more, step, (jnp.int32(1), ra, rb, acc))[3])

    qs, ks, vs, gs = _attn_specs(T, tq)
    (o, an), got = _hosted_call(body, "attn_fwd", comm, (AW // 128, T // tq), [qs, ks, vs, gs], [qs, qs],
                                [jax.ShapeDtypeStruct((T, AW), F32), jax.ShapeDtypeStruct((T, AW), BF16)],
                                [], ("parallel", "parallel"), 48, (qkv, qkv, qkv, g_attn))
    return o, an, got


def _attn_bwd(qkv, o, dcat, g_attn, tq, comm=None):
    T = qkv.shape[0]

    def body(q_ref, k_ref, v_ref, o_ref, dan_ref, g_ref, dq_ref, dk_ref, dv_ref, dg_ref):
        i = pl.program_id(1)

        @pl.when(i == 0)
        def _():
            dk_ref[...] = jnp.zeros_like(dk_ref)
            dv_ref[...] = jnp.zeros_like(dv_ref)
            dg_ref[...] = jnp.zeros_like(dg_ref)

        first = lax.broadcasted_iota(jnp.int32, (tq, 128), 1) < HD
        q = (q_ref[...].astype(F32) * QK_SCALE).astype(BF16)
        zq = jnp.zeros_like(q)
        qs = (jnp.where(first, q, zq), jnp.where(first, zq, q))
        o = o_ref[...]
        dan = dan_ref[...]
        r = lax.rsqrt(_head_sum(o * o, first) * (1.0 / HD) + RMS_EPS)
        on = o * r
        dg_ref[...] += jnp.sum(dan * on, axis=0, keepdims=True)
        dyg = dan * g_ref[...]
        dO = r * (dyg - on * (_head_sum(dyg * on, first) * (1.0 / HD)))
        dOb = dO.astype(BF16)
        dOs = (jnp.where(first, dOb, zq), jnp.where(first, zq, dOb))
        ones = jnp.ones((8, 128), BF16)
        Ds = []
        for hh in range(2):
            prod = dOs[hh].astype(F32) * o
            p1 = prod.astype(BF16)
            rem = prod - p1.astype(F32)
            p2 = rem.astype(BF16)
            p3 = (rem - p2.astype(F32)).astype(BF16)
            Ds.append((_dot_nt(ones, p1) + _dot_nt(ones, p2) + _dot_nt(ones, p3))[0:1, :])

        rows = lax.broadcasted_iota(jnp.int32, (tq, tq), 0)
        cols = lax.broadcasted_iota(jnp.int32, (tq, tq), 1)
        tri_after = (cols > rows).astype(BF16)
        tri_incl = (cols >= rows).astype(BF16)
        tri_after2 = jnp.concatenate([tri_after, tri_after], axis=1)
        tri_incl2 = jnp.concatenate([tri_incl, tri_incl], axis=1)
        strict = rows < cols

        def tile(j, Rs, Gs, dq, masked):
            start = j * tq if isinstance(j, int) else pl.multiple_of(j * tq, tq)
            kb = k_ref[pl.ds(start, tq), :]
            vb = v_ref[pl.ds(start, tq), :]
            H = range(2)
            parts = [_softplus_parts(_dot_nt(kb, qs[hh])) for hh in H]
            lsms = [jnp.where(strict, p[1], 0.0) if masked else p[1] for p in parts]
            splits = [_split2(x) for x in lsms]
            afters = [_dot(tri_after2, jnp.concatenate(s, axis=0)) for s in splits]
            ws = [jnp.exp(parts[hh][0] + afters[hh] + Rs[hh]) for hh in H]
            if masked:
                ws = [jnp.where(strict, w, 0.0) for w in ws]
            wbs = [w.astype(BF16) for w in ws]
            dlws = [_dot_nt(vb, dOs[hh]) * wbs[hh].astype(F32) for hh in H]
            splits2 = [_split2(x) for x in dlws]
            Cs = [_dot(tri_incl2, jnp.concatenate(s, axis=0)) for s in splits2]
            dlsms = [Ds[hh] - Gs[hh] - Cs[hh] for hh in H]
            if masked:
                dlsms = [jnp.where(strict, x, 0.0) for x in dlsms]
            ps = [jnp.exp(p[0]) for p in parts]
            dzs = [(dlws[hh] * (1.0 - ps[hh]) - dlsms[hh] * ps[hh]).astype(BF16) for hh in H]
            dkp = _dot(dzs[0], qs[0]) + _dot(dzs[1], qs[1])
            dvp = _dot(wbs[0], dOs[0]) + _dot(wbs[1], dOs[1])
            dq = dq + jnp.where(first, _dot_tn(dzs[0], kb), _dot_tn(dzs[1], kb))
            new_r = [Rs[hh] + afters[hh][0:1, :] + lsms[hh][0:1, :] for hh in H]
            new_g = [Gs[hh] + Cs[hh][0:1, :] for hh in H]
            dk_ref[pl.ds(start, tq), :] += dkp
            dv_ref[pl.ds(start, tq), :] += dvp
            return new_r[0], new_r[1], new_g[0], new_g[1], dq

        zrow = jnp.zeros((1, tq), F32)

        @pl.when(i == 0)
        def _():
            dq0 = tile(0, (zrow, zrow), (zrow, zrow), jnp.zeros((tq, 128), F32), True)[4]
            dq_ref[...] = (dq0 * QK_SCALE).astype(BF16)

        @pl.when(i > 0)
        def _():
            st = tile(i, (zrow, zrow), (zrow, zrow), jnp.zeros((tq, 128), F32), True)
            st = tile(i - 1, st[0:2], st[2:4], st[4], False)

            def more(c):
                return jnp.logical_and(c[0] < i, jnp.maximum(jnp.max(c[1]), jnp.max(c[2])) > W_ZERO_BELOW)

            def step(c):
                return (c[0] + 1,) + tile(i - 1 - c[0], (c[1], c[2]), (c[3], c[4]), c[5], False)

            dq_ref[...] = (lax.while_loop(more, step, (jnp.int32(1),) + st)[5] * QK_SCALE).astype(BF16)

    qs, ks, vs, gs = _attn_specs(T, tq)
    kacc = pl.BlockSpec((T, 128), lambda p, i: (0, p))
    return _hosted_call(body, "attn_bwd", comm, (AW // 128, T // tq), [qs, ks, vs, qs, qs, gs], [qs, kacc, kacc, gs],
                        [jax.ShapeDtypeStruct((T, AW), BF16), jax.ShapeDtypeStruct((T, AW), F32),
                         jax.ShapeDtypeStruct((T, AW), F32), jax.ShapeDtypeStruct((1, AW), F32)],
                        [], ("parallel", "arbitrary"), 48, (qkv, qkv, qkv, o, dcat, g_attn))


def _taps_by_phase(offsets):
    groups = {}
    for k, off in enumerate(offsets):
        groups.setdefault(off % 8, []).append((k, off // 8))
    return sorted(groups.items())


def _shifted_tap_sum(w_ref, pad_ref, ph_ref, offsets, tb):
    acc = None
    for p, taps in _taps_by_phase(offsets):
        n = tb if p == 0 else tb + 8
        a = None
        for k, m in taps:
            t = w_ref[k:k + 1, :] * pad_ref[8 * m:8 * m + n, :]
            a = t if a is None else a + t
        if p:
            ph_ref[...] = a
            a = ph_ref[p:p + tb, :]
        acc = a if acc is None else acc + a
    return acc


def _conv_fwd(cvg, conv_w, conv_b, ln_g, ln_b, tb=512):
    T = cvg.shape[0]
    hb = tb // HALO

    def body(cv_ref, cg_ref, cvp_ref, cgp_ref, w_ref, b_ref, g_ref, be_ref, u0_ref, u1_ref, u3_ref, pad_ref, ph_ref):
        i = pl.program_id(0)
        u0 = cv_ref[...] * _sigmoid(cg_ref[...])
        prev = cvp_ref[...] * _sigmoid(cgp_ref[...])
        pad_ref[0:HALO, :] = jnp.where(i > 0, prev, 0.0)
        pad_ref[HALO:HALO + tb, :] = u0
        u0_ref[...] = u0
        acc = _shifted_tap_sum(w_ref, pad_ref, ph_ref, [HALO - (CK - 1) + kk for kk in range(CK)], tb) + b_ref[...]
        u1_ref[...] = acc
        mu = jnp.mean(acc, axis=-1, keepdims=True)
        xc = acc - mu
        var = jnp.mean(xc * xc, axis=-1, keepdims=True)
        u2 = (xc * lax.rsqrt(var + LN_EPS)) * g_ref[...] + be_ref[...]
        u3_ref[...] = (u2 * _sigmoid(u2)).astype(BF16)

    cur = lambda col: pl.BlockSpec((tb, CW), lambda i: (i, col))
    prv = lambda col: pl.BlockSpec((HALO, CW), lambda i: (jnp.maximum(i * hb - 1, 0), col))
    vec = pl.BlockSpec((1, CW), lambda i: (0, 0))
    out = pl.BlockSpec((tb, CW), lambda i: (i, 0))
    return _pcall(body, "conv_fwd", grid=(T // tb,),
                  in_specs=[cur(0), cur(1), prv(0), prv(1), pl.BlockSpec((HALO, CW), lambda i: (0, 0)), vec, vec, vec],
                  out_specs=[out, out, out],
                  out_shape=[jax.ShapeDtypeStruct((T, CW), F32), jax.ShapeDtypeStruct((T, CW), F32),
                             jax.ShapeDtypeStruct((T, CW), BF16)],
                  scratch_shapes=[pltpu.VMEM((tb + HALO, CW), F32), pltpu.VMEM((tb + 8, CW), F32)],
                  compiler_params=_cp(("parallel",)))(cvg, cvg, cvg, cvg, conv_w, conv_b, ln_g, ln_b)


def _conv_bwd1(dcat, u1, u0, ln_g, ln_b, tb=512):
    T = u1.shape[0]
    hb = tb // HALO

    def body(d3_ref, u1_ref, u0_ref, u0p_ref, g_ref, be_ref, du1_ref, dw_ref, db_ref, dlg_ref, dlb_ref, pad_ref, d_ref,
             q_ref):
        i = pl.program_id(0)

        @pl.when(i == 0)
        def _():
            dw_ref[...] = jnp.zeros_like(dw_ref)
            db_ref[...] = jnp.zeros_like(db_ref)
            dlg_ref[...] = jnp.zeros_like(dlg_ref)
            dlb_ref[...] = jnp.zeros_like(dlb_ref)

        u1 = u1_ref[...]
        mu = jnp.mean(u1, axis=-1, keepdims=True)
        xc = u1 - mu
        rstd = lax.rsqrt(jnp.mean(xc * xc, axis=-1, keepdims=True) + LN_EPS)
        xh = xc * rstd
        u2 = xh * g_ref[...] + be_ref[...]
        s = _sigmoid(u2)
        du2 = d3_ref[...] * (s + u2 * s * (1.0 - s))
        dlg_ref[...] += jnp.sum(du2 * xh, axis=0, keepdims=True)
        dlb_ref[...] += jnp.sum(du2, axis=0, keepdims=True)
        dxh = du2 * g_ref[...]
        du1 = rstd * (dxh - jnp.mean(dxh, axis=-1, keepdims=True) - xh * jnp.mean(dxh * xh, axis=-1, keepdims=True))
        du1_ref[...] = du1
        db_ref[...] += jnp.sum(du1, axis=0, keepdims=True)
        pad_ref[0:HALO, :] = jnp.where(i > 0, u0p_ref[...], 0.0)
        pad_ref[HALO:HALO + tb, :] = u0_ref[...]
        d_ref[0:8, :] = jnp.zeros((8, CW), F32)
        d_ref[8:8 + tb, :] = du1
        d_ref[8 + tb:16 + tb, :] = jnp.zeros((8, CW), F32)
        for p, taps in _taps_by_phase([HALO - (CK - 1) + kk for kk in range(CK)]):
            n = tb + 8
            q_ref[...] = d_ref[8 - p:8 - p + n, :]
            for k, m in taps:
                if 8 * m + n <= tb + HALO:
                    dw_ref[k:k + 1, :] += jnp.sum(q_ref[...] * pad_ref[8 * m:8 * m + n, :], axis=0, keepdims=True)
                else:
                    dw_ref[k:k + 1, :] += jnp.sum(q_ref[0:tb, :] * pad_ref[8 * m:8 * m + tb, :], axis=0, keepdims=True)

    cur = pl.BlockSpec((tb, CW), lambda i: (i, 0))
    vec = pl.BlockSpec((1, CW), lambda i: (0, 0))
    return _pcall(body, "conv_bwd1", grid=(T // tb,),
                  in_specs=[pl.BlockSpec((tb, CW), lambda i: (i, 1)), cur, cur,
                            pl.BlockSpec((HALO, CW), lambda i: (jnp.maximum(i * hb - 1, 0), 0)), vec, vec],
                  out_specs=[cur, pl.BlockSpec((HALO, CW), lambda i: (0, 0)), vec, vec, vec],
                  out_shape=[jax.ShapeDtypeStruct((T, CW), F32), jax.ShapeDtypeStruct((HALO, CW), F32)]
                  + [jax.ShapeDtypeStruct((1, CW), F32)] * 3,
                  scratch_shapes=[pltpu.VMEM((tb + HALO, CW), F32), pltpu.VMEM((tb + 16, CW), F32),
                                  pltpu.VMEM((tb + 8, CW), F32)],
                  compiler_params=_cp(("arbitrary",)))(dcat, u1, u0, u0, ln_g, ln_b)


def _conv_bwd2(du1, cvg, conv_w, tb=512):
    T = du1.shape[0]
    hb = tb // HALO
    last = T // HALO - 1
    nblk = T // tb

    def body(d_ref, dn_ref, cv_ref, cg_ref, w_ref, o_ref, pad_ref, ph_ref):
        i = pl.program_id(0)
        pad_ref[0:tb, :] = d_ref[...]
        pad_ref[tb:tb + HALO, :] = jnp.where(i < nblk - 1, dn_ref[...], 0.0)
        acc = _shifted_tap_sum(w_ref, pad_ref, ph_ref, [CK - 1 - kk for kk in range(CK)], tb)
        sg = _sigmoid(cg_ref[...])
        o_ref[:, 0:CW] = (acc * sg).astype(BF16)
        o_ref[:, CW:2 * CW] = (acc * cv_ref[...] * sg * (1.0 - sg)).astype(BF16)

    cur = pl.BlockSpec((tb, CW), lambda i: (i, 0))
    return _pcall(body, "conv_bwd2", grid=(nblk,),
                  in_specs=[cur, pl.BlockSpec((HALO, CW), lambda i: (jnp.minimum((i + 1) * hb, last), 0)),
                            pl.BlockSpec((tb, CW), lambda i: (i, 0)), pl.BlockSpec((tb, CW), lambda i: (i, 1)),
                            pl.BlockSpec((HALO, CW), lambda i: (0, 0))],
                  out_specs=pl.BlockSpec((tb, 2 * CW), lambda i: (i, 0)),
                  out_shape=jax.ShapeDtypeStruct((T, 2 * CW), BF16),
                  scratch_shapes=[pltpu.VMEM((tb + HALO, CW), F32), pltpu.VMEM((tb + 8, CW), F32)],
                  compiler_params=_cp(("parallel",)))(du1, du1, cvg, cvg, conv_w)


def _adam_math(w, g, m, v):
    nm = ADAM_B1 * m + (1.0 - ADAM_B1) * g
    nv = ADAM_B2 * v + (1.0 - ADAM_B2) * (g * g)
    delta = -ADAM_LR * ((nm * ADAM_C1) / (jnp.sqrt(nv * ADAM_C2) + ADAM_EPS) + ADAM_WD * w)
    return delta, nm, nv


def _adamw(w, gslots, m, v, name, tb):
    R, C = w.shape
    S = gslots.shape[0]

    def body(w_ref, gs_ref, m_ref, v_ref, g_ref, d_ref, nm_ref, nv_ref):
        g = gs_ref[0].astype(F32)
        for s in range(1, S):
            g = g + gs_ref[s].astype(F32)
        g_ref[...] = g
        d_ref[...], nm_ref[...], nv_ref[...] = _adam_math(w_ref[...], g, m_ref[...], v_ref[...])

    blk = pl.BlockSpec((tb, C), lambda i: (i, 0))
    return _pcall(body, name, grid=(R // tb,),
                  in_specs=[blk, pl.BlockSpec((S, tb, C), lambda i: (0, i, 0)), blk, blk],
                  out_specs=[blk] * 4, out_shape=[jax.ShapeDtypeStruct((R, C), F32)] * 4,
                  compiler_params=_cp(("parallel",)))(w, gslots, m, v)


def _adamw_small(gall, gattn, gconvw, ws, ms, vs):
    n = len(ws)

    def body(*refs):
        gall_ref, gattn_ref, gconvw_ref = refs[:3]
        w_refs, m_refs, v_refs = refs[3:3 + n], refs[3 + n:3 + 2 * n], refs[3 + 2 * n:3 + 3 * n]
        loss_ref = refs[3 + 3 * n]
        outs = refs[4 + 3 * n:]
        g_refs, d_refs, nm_refs, nv_refs = outs[:n], outs[n:2 * n], outs[2 * n:3 * n], outs[3 * n:]

        def total(ref):
            t = ref[0]
            for dev in range(1, NDEV):
                t = t + ref[dev]
            return t

        tot = total(gall_ref)
        grads = [tot[0:9, :]] + [tot[ROW_GAINS + k:ROW_GAINS + k + 1, :] for k in range(6)]
        grads += [total(gattn_ref), tot[ROW_ATTN_CB:ROW_ATTN_CB + 1, CW:2 * CW], tot[ROW_LN:ROW_LN + 1, 0:CW],
                  tot[ROW_LN:ROW_LN + 1, CW:2 * CW], total(gconvw_ref)]
        loss_ref[...] = tot[ROW_LOSS:ROW_LOSS + 1, 0:1]
        for k in range(n):
            g_refs[k][...] = grads[k]
            d_refs[k][...], nm_refs[k][...], nv_refs[k][...] = _adam_math(w_refs[k][...], grads[k], m_refs[k][...],
                                                                          v_refs[k][...])

    shapes = [jax.ShapeDtypeStruct(w.shape, F32) for w in ws]
    res = _pcall(body, "adamw_small", out_shape=[jax.ShapeDtypeStruct((1, 1), F32)] + shapes * 4,
                 compiler_params=_cp())(gall, gattn, gconvw, *ws, *ms, *vs)
    return res[0], [res[1 + k * n:1 + (k + 1) * n] for k in range(4)]


def _ffn_fwd(x, g_pre, g_post, shift, scale, gate, w_in, w_out4, tag, tm, comm=None, tgt=None):
    h = _pre_fwd(x, g_pre, shift, scale, "pre_fwd_" + tag)
    gu, a, got = _ffn_in(h, w_in, "ffn_in_" + tag, comm)
    res = _mm_post([a], [pl.BlockSpec((NSL, tm, SL), lambda i: (0, i, 0))],
                   [w_out4], [pl.BlockSpec((NSL, SL, D), lambda i: (0, 0, 0))],
                   x, g_post, gate, 0.5, "ffn_out_" + tag, tm, tgt)
    return (res[1] if tgt is None else (res[1], res[2])), (x, h, gu, a, res[0]), got


def _ffn_bwd(dout, saved, g_pre, g_post, scale, gate, w_in, w_out4, tag, tmb, tmw, send_in=True, carry=None):
    x, h, gu, a, f = saved
    T = x.shape[0]
    df, dgate, dg_post = _post_bwd(dout, f, g_post, gate, 0.5, "post_bwd_" + tag)
    dgu, carried = _ffn_out_bwd(df, w_out4, gu, "ffn_out_bwd_" + tag, carry)
    dw_out, _ = _mm_tn(a, pl.BlockSpec((1, tmw, SL), lambda j, i: (j, i, 0)),
                       df, pl.BlockSpec((tmw, D), lambda j, i: (i, 0)),
                       (NSL, SL, D), pl.BlockSpec((1, SL, D), lambda j, i: (j, 0, 0)), (NSL, T // tmw), "dw_out_" + tag)
    dw_in, (r_out,) = _dw_in(h, dgu, "dw_in_" + tag, tmw, comm=("a2a", [dw_out.reshape(NDEV, SL // 2, D)]))
    dw_in = dw_in.reshape(NDEV, D, SL)

    def dh_fn(a_ref, w_ref):
        dh = None
        for p in range(2):
            for j in range(NSL):
                t = _dot_nt(a_ref[j, p], w_ref[NSL * p + j])
                dh = t if dh is None else dh + t
        return dh

    (dx, dshift, dscale, dg_pre), got = _mm_prebwd(
        dgu, pl.BlockSpec((NSL, 2, tmb, SL), lambda i: (0, 0, i, 0)),
        w_in, pl.BlockSpec((NDEV, D, SL), lambda i: (0, 0, 0), pipeline_mode=pl.Buffered(1)),
        dh_fn, x, dout, g_pre, scale, "ffn_in_bwd_" + tag, tmb, comm=("a2a", [dw_in]) if send_in else None)
    return dx, got[0] if send_in else dw_in, r_out, dg_pre, dg_post, (dshift, dscale, dgate), carried


def kernel(x, c, w_ada, b_ada, g_pre_ff1, g_post_ff1, ff1_w_in, ff1_w_out, g_pre_mix, g_post_mix, w_in_mix, g_attn_out, conv_w, conv_b, conv_ln_g, conv_ln_b, w_out_mix, g_pre_ff2, g_post_ff2, ff2_w_in, ff2_w_out, loss_target, m_w_ada, m_b_ada, m_g_pre_ff1, m_g_post_ff1, m_ff1_w_in, m_ff1_w_out, m_g_pre_mix, m_g_post_mix, m_w_in_mix, m_g_attn_out, m_conv_w, m_conv_b, m_conv_ln_g, m_conv_ln_b, m_w_out_mix, m_g_pre_ff2, m_g_post_ff2, m_ff2_w_in, m_ff2_w_out, v_w_ada, v_b_ada, v_g_pre_ff1, v_g_post_ff1, v_ff1_w_in, v_ff1_w_out, v_g_pre_mix, v_g_post_mix, v_w_in_mix, v_g_attn_out, v_conv_w, v_conv_b, v_conv_ln_g, v_conv_ln_b, v_w_out_mix, v_g_pre_ff2, v_g_post_ff2, v_ff2_w_in, v_ff2_w_out):
    me = 4 * lax.axis_index("x") + 2 * lax.axis_index("y") + lax.axis_index("c")
    T = x.shape[1]
    tq = min(256, T)
    tm = 512
    tmb = 512
    tmw = 2048
    x0 = x.reshape(T, D)
    tgt = loss_target.reshape(T, D)
    row = lambda a: a.reshape(1, -1)

    small_in = jnp.concatenate([c.reshape(-1), jnp.pad(conv_w.reshape(-1), (0, 2 * D - CK * 64)),
                                jnp.zeros((5 * D,), F32)]).reshape(8, D)
    small_all, = _all_gather([small_in], "gather_c_convw", True)
    c_all = small_all[:, 0, :]
    conv_w_full = small_all[:, 1:3, :].reshape(NDEV, 2 * D)[:, :CK * 64].reshape(NDEV, CK, 64)
    conv_w_full = conv_w_full.transpose(1, 0, 2).reshape(CK, CW)
    conv_w_pad = jnp.pad(conv_w_full, ((0, HALO - CK), (0, 0)))

    big = [ff1_w_in, ff1_w_out, w_in_mix, w_out_mix, ff2_w_in, ff2_w_out]
    shards = [w.astype(BF16) for w in big]
    w_in1, w_out1 = _all_gather(shards[0:2], "gather_weights_ff1", False)
    w_out1_4 = w_out1.reshape(NSL, SL, D)

    b_cols = lax.dynamic_slice(b_ada, (me * ADA_COLS,), (ADA_COLS,)).reshape(1, ADA_COLS)
    mod_cols = _ada_fwd(c_all, w_ada, b_cols)
    mod_all, = _all_gather([mod_cols], "gather_mod", True)
    mod = lax.dynamic_slice(mod_all, (0, me, 0), (NDEV, 1, ADA_COLS)).reshape(9, D)
    sh = lambda s: mod[3 * s:3 * s + 1]
    sc = lambda s: mod[3 * s + 1:3 * s + 2]
    gt = lambda s: mod[3 * s + 2:3 * s + 3]

    x1, sv1, (w_inm_s, w_outm_s) = _ffn_fwd(x0, row(g_pre_ff1), row(g_post_ff1), sh(0), sc(0), gt(0), w_in1, w_out1_4,
                                            "ff1", tm, comm=("gather", shards[2:4]))
    w_inm = w_inm_s.transpose(1, 0, 2).reshape(D, MIXIN)
    w_outm = w_outm_s.reshape(D, D)
    hm = _pre_fwd(x1, row(g_pre_mix), sh(1), sc(1), "pre_fwd_mix")
    qkv, cvg = _mix_in(hm, w_inm, "mix_in")
    g_attn_row = row(g_attn_out)
    o_att, an, (w_in2, w_out2) = _attn_fwd(qkv, g_attn_row, tq, comm=("gather", shards[4:6]))
    w_out2_4 = w_out2.reshape(NSL, SL, D)
    u0, u1, u3 = _conv_fwd(cvg, conv_w_pad, row(conv_b), row(conv_ln_g), row(conv_ln_b))
    half = lambda k: pl.BlockSpec((AW, D), lambda i: (k, 0))
    act = pl.BlockSpec((tm, AW), lambda i: (i, 0))
    fm, x2 = _mm_post([an, u3], [act, act], [w_outm, w_outm], [half(0), half(1)],
                      x1, row(g_post_mix), gt(1), 1.0, "mix_out", tm)
    (dy, loss_part), sv2, _ = _ffn_fwd(x2, row(g_pre_ff2), row(g_post_ff2), sh(2), sc(2), gt(2), w_in2, w_out2_4,
                                       "ff2", tm, tgt=tgt)

    dx2, dw_in2, r_out2, dgpre2, dgpost2, dmod2, _ = _ffn_bwd(
        dy, sv2, row(g_pre_ff2), row(g_post_ff2), sc(2), gt(2), w_in2, w_out2_4, "ff2", tmb, tmw, send_in=False)

    dfm, dgate1, dgpostm = _post_bwd(dx2, fm, row(g_post_mix), gt(1), 1.0, "post_bwd_mix")
    dcat = _mm_nt(dfm, w_outm, "mix_out_bwd")
    tok = pl.BlockSpec((tmw, AW), lambda j, i: (i, 0))
    tokd = pl.BlockSpec((tmw, D), lambda j, i: (i, 0))
    whole = pl.BlockSpec((AW, D), lambda j, i: (0, 0))
    dw_outm = jnp.concatenate([_mm_tn(an, tok, dfm, tokd, (AW, D), whole, (1, T // tmw), "dw_out_mix_a")[0],
                               _mm_tn(u3, tok, dfm, tokd, (AW, D), whole, (1, T // tmw), "dw_out_mix_c")[0]], axis=0)
    (dq, dk, dv, dg_attn), (r_in2,) = _attn_bwd(qkv, o_att, dcat, g_attn_row, tq, comm=("a2a", [dw_in2]))
    du1, dconv_w, dconv_b, dln_g, dln_b = _conv_bwd1(dcat, u1, u0, row(conv_ln_g), row(conv_ln_b))
    dcvg = _conv_bwd2(du1, cvg, conv_w_pad)
    dproj = jnp.concatenate([dq, dk.astype(BF16), dv.astype(BF16), dcvg], axis=1)
    dw_inm, _ = _mm_tn(hm, pl.BlockSpec((tmw, D), lambda j, i: (i, 0)),
                       dproj, pl.BlockSpec((tmw, MIXIN // 2), lambda j, i: (i, j)),
                       (D, MIXIN), pl.BlockSpec((D, MIXIN // 2), lambda j, i: (0, j)), (2, T // tmw), "dw_in_mix")
    (dx1, dshift1, dscale1, dgprem), _ = _mm_prebwd(
        dproj, pl.BlockSpec((tmb, MIXIN), lambda i: (i, 0)), w_inm, pl.BlockSpec((D, MIXIN), lambda i: (0, 0)),
        lambda a_ref, w_ref: _dot_nt(a_ref[...], w_ref[...]), x1, dx2, row(g_pre_mix), sc(1), "mix_in_bwd", tmb)

    dx0, r_in1, r_out1, dgpre1, dgpost1, dmod0, (r_inm, r_outm) = _ffn_bwd(
        dx1, sv1, row(g_pre_ff1), row(g_post_ff1), sc(0), gt(0), w_in1, w_out1_4, "ff1", tmb, tmw,
        carry=("a2a", [dw_inm.reshape(D, NDEV, 320).transpose(1, 0, 2), dw_outm.reshape(NDEV, 128, D)]))
    recvs = [r_in1, r_out1, r_inm, r_outm, r_in2, r_out2]

    zrow = jnp.zeros((1, D), F32)
    small_g = jnp.concatenate(
        list(dmod0) + [dshift1, dscale1, dgate1] + list(dmod2)
        + [dgpre1, dgpost1, dgprem, dgpostm, dgpre2, dgpost2]
        + [jnp.concatenate([dg_attn, dconv_b], axis=1), jnp.concatenate([dln_g, dln_b], axis=1),
           jnp.pad(dconv_w[:CK].reshape(-1), (0, CONVW_ROWS * D - CK * CW)).reshape(CONVW_ROWS, D),
           jnp.pad(loss_part, ((0, 0), (0, D - 1)))] + [zrow] * (SMALL_R - ROW_LOSS - 1), axis=0)
    small_g_all, = _all_gather([small_g], "gather_small_grads", True)

    dmod_all = small_g_all[:, 0:9, :].reshape(NDEV, NMOD)
    dmod_cols = lax.dynamic_slice(dmod_all, (0, me * ADA_COLS), (NDEV, ADA_COLS))
    g_w_ada = _ada_bwd(c_all.T, dmod_cols)

    gattn = small_g_all[:, ROW_ATTN_CB, 0:AW].reshape(NDEV, 8, HD)
    gconvw = small_g_all[:, ROW_CONVW:ROW_CONVW + CONVW_ROWS, :].reshape(NDEV, CONVW_ROWS * D)[:, :CK * CW]
    gconvw = lax.dynamic_slice(gconvw.reshape(NDEV, CK, CW), (0, 0, me * 64), (NDEV, CK, 64))

    def small_list(b, g6, ga, cb, lg, lb, cw):
        return [b.reshape(9, D)] + [row(g) for g in g6] + [ga, row(cb), row(lg), row(lb), cw]

    sw = small_list(b_ada, [g_pre_ff1, g_post_ff1, g_pre_mix, g_post_mix, g_pre_ff2, g_post_ff2], g_attn_out,
                    conv_b, conv_ln_g, conv_ln_b, conv_w)
    sm = small_list(m_b_ada, [m_g_pre_ff1, m_g_post_ff1, m_g_pre_mix, m_g_post_mix, m_g_pre_ff2, m_g_post_ff2],
                    m_g_attn_out, m_conv_b, m_conv_ln_g, m_conv_ln_b, m_conv_w)
    sv = small_list(v_b_ada, [v_g_pre_ff1, v_g_post_ff1, v_g_pre_mix, v_g_post_mix, v_g_pre_ff2, v_g_post_ff2],
                    v_g_attn_out, v_conv_b, v_conv_ln_g, v_conv_ln_b, v_conv_w)
    loss, s_out = _adamw_small(small_g_all, gattn, gconvw, sw, sm, sv)
    s_out = [[o.reshape(w.shape) for o, w in zip(outs, [b_ada, g_pre_ff1, g_post_ff1, g_pre_mix, g_post_mix,
                                                        g_pre_ff2, g_post_ff2, g_attn_out, conv_b, conv_ln_g,
                                                        conv_ln_b, conv_w])] for outs in s_out]

    big_m = [m_ff1_w_in, m_ff1_w_out, m_w_in_mix, m_w_out_mix, m_ff2_w_in, m_ff2_w_out]
    big_v = [v_ff1_w_in, v_ff1_w_out, v_w_in_mix, v_w_out_mix, v_ff2_w_in, v_ff2_w_out]
    tbs = [256, 176, 256, 128, 256, 176]
    tags = ["ff1_w_in", "ff1_w_out", "w_in_mix", "w_out_mix", "ff2_w_in", "ff2_w_out"]
    b_out = [_adamw(big[k], recvs[k], big_m[k], big_v[k], "adamw_" + tags[k], tbs[k]) for k in range(6)]
    a_out = _adamw(w_ada, g_w_ada.reshape(1, D, ADA_COLS), m_w_ada, v_w_ada, "adamw_ada", 256)

    def leaves(k):
        s = s_out[k]
        b = [o[k] for o in b_out]
        return [a_out[k], s[0], s[1], s[2], b[0], b[1], s[3], s[4], b[2], s[7], s[11], s[8], s[9], s[10], b[3],
                s[5], s[6], b[4], b[5]]

    return (loss.reshape(()), dx0.reshape(1, T, D), *leaves(0), *leaves(1), *leaves(2), *leaves(3))
```

```python
import functools

import jax
import jax.numpy as jnp
from jax import lax
from jax.experimental import pallas as pl
from jax.experimental.pallas import tpu as pltpu

F32 = jnp.float32
BF16 = jnp.bfloat16
D = 1024
DFF = 2816
SL = 704
NSL = DFF // SL
AW = 512
HD = 64
CW = 512
CK = 31
HALO = 32
MIXIN = 2560
NDEV = 8
NMOD = 9 * D
ADA_COLS = NMOD // NDEV
RMS_EPS = 1e-6
LN_EPS = 1e-5
QK_SCALE = HD ** -0.5
W_ZERO_BELOW = -104.0
ADAM_LR, ADAM_B1, ADAM_B2, ADAM_EPS, ADAM_WD, ADAM_STEP = 0.001, 0.9, 0.999, 1e-08, 0.01, 10
ADAM_C1 = 1.0 / (1.0 - ADAM_B1 ** ADAM_STEP)
ADAM_C2 = 1.0 / (1.0 - ADAM_B2 ** ADAM_STEP)
MIB = 1024 * 1024
MESH = pl.DeviceIdType.MESH

ROW_GAINS = 9
ROW_ATTN_CB = 15
ROW_LN = 16
ROW_CONVW = 17
CONVW_ROWS = 16
ROW_LOSS = 33
SMALL_R = 40


def _pcall(body, name, **kw):
    return pl.pallas_call(body, name=name, **kw)


def _cp(sem=None, vmem_mib=48):
    if sem is None:
        return pltpu.CompilerParams(vmem_limit_bytes=vmem_mib * MIB)
    return pltpu.CompilerParams(dimension_semantics=sem, vmem_limit_bytes=vmem_mib * MIB)


def _dot(a, b):
    return jnp.dot(a, b, preferred_element_type=F32)


def _dot_nt(a, b):
    return lax.dot_general(a, b, (((1,), (1,)), ((), ())), preferred_element_type=F32)


def _dot_tn(a, b):
    return lax.dot_general(a, b, (((0,), (0,)), ((), ())), preferred_element_type=F32)


def _sigmoid(x):
    return 0.5 * jnp.tanh(0.5 * x) + 0.5


def _split2(x):
    hi = x.astype(BF16)
    mid = (x - hi.astype(F32)).astype(BF16)
    return hi, mid


def _mat(ref):
    lead = len(ref.shape) - 2
    return ref[(0,) * lead] if lead else ref[...]


def _all_gather(xs, name, in_vmem):
    n = len(xs)

    def body(*refs):
        x_refs, out_refs = refs[:n], refs[n:2 * n]
        send_sems, recv_sems, local_sems = refs[2 * n:]
        mx, my, mc = lax.axis_index("x"), lax.axis_index("y"), lax.axis_index("c")
        me, sibling = (mx, my, mc), (mx, my, 1 - mc)
        chips = [(1 - mx, my), (mx, 1 - my), (1 - mx, 1 - my)]

        def slab(a, px, py, pc):
            return out_refs[a].at[4 * px + 2 * py + pc]

        def copy(a, k, block, to, src=None):
            return pltpu.make_async_remote_copy(
                src_ref=slab(a, *block) if src is None else src, dst_ref=slab(a, *block),
                send_sem=send_sems.at[a, k], recv_sem=recv_sems.at[a, k], device_id=to, device_id_type=MESH)

        mine = [pltpu.make_async_copy(x_refs[a], slab(a, *me), local_sems.at[a]) for a in range(n)]
        for cp in mine:
            cp.start()
        first = []
        for a in range(n):
            first.append(copy(a, 0, me, sibling, src=x_refs[a]))
            first += [copy(a, 1 + j, me, (*chip, mc), src=x_refs[a]) for j, chip in enumerate(chips)]
        for cp in first:
            cp.start()
        passed = []
        for j, chip in enumerate(chips):
            for a in range(n):
                copy(a, 1 + j, (*chip, mc), me).wait_recv()
                passed.append(copy(a, 4 + j, (*chip, mc), sibling))
                passed[-1].start()
        for a in range(n):
            copy(a, 0, sibling, me).wait_recv()
            for j, chip in enumerate(chips):
                copy(a, 4 + j, (*chip, 1 - mc), me).wait_recv()
        for cp in first + passed:
            cp.wait_send()
        for cp in mine:
            cp.wait()

    space = pltpu.VMEM if in_vmem else pl.ANY
    return _pcall(
        body, name,
        out_shape=[jax.ShapeDtypeStruct((NDEV,) + x.shape, x.dtype) for x in xs],
        in_specs=[pl.BlockSpec(memory_space=space)] * n,
        out_specs=[pl.BlockSpec(memory_space=space)] * n,
        scratch_shapes=[pltpu.SemaphoreType.DMA((n, 7)), pltpu.SemaphoreType.DMA((n, 7)),
                        pltpu.SemaphoreType.DMA((n,))],
    )(*xs)


def _exchange_copies(kind, src, dst, send_sems, recv_sems, local_sems):
    mx, my, mc = lax.axis_index("x"), lax.axis_index("y"), lax.axis_index("c")
    me = 4 * mx + 2 * my + mc
    n = len(src)
    pick = (lambda a, p: src[a].at[p]) if kind == "a2a" else (lambda a, p: src[a])
    mine = [pltpu.make_async_copy(pick(a, me), dst[a].at[me], local_sems.at[a]) for a in range(n)]
    copies = []
    for r in range(1, NDEV):
        px = 1 - mx if r & 4 else mx
        py = 1 - my if r & 2 else my
        pc = 1 - mc if r & 1 else mc
        for a in range(n):
            copies.append(pltpu.make_async_remote_copy(
                src_ref=pick(a, 4 * px + 2 * py + pc), dst_ref=dst[a].at[me],
                send_sem=send_sems.at[a, r - 1], recv_sem=recv_sems.at[a, r - 1],
                device_id=(px, py, pc), device_id_type=MESH))
    return mine, copies


def _hosted_call(body, name, comm, grid, in_specs, out_specs, out_shape, scratch_shapes, sem, vmem_mib, args):
    if comm is None:
        outs = _pcall(body, name, grid=grid, in_specs=in_specs, out_specs=out_specs, out_shape=out_shape,
                      scratch_shapes=scratch_shapes, compiler_params=_cp(sem, vmem_mib))(*args)
        return outs, []
    kind, arrs = comm
    nc, n_in, n_out, n_scr = len(arrs), len(in_specs), len(out_specs), len(scratch_shapes)
    rank = len(grid)

    def wrapped(*refs):
        ins, csrc = refs[:n_in], refs[n_in:n_in + nc]
        outs, cdst = refs[n_in + nc:n_in + nc + n_out], refs[n_in + nc + n_out:n_in + 2 * nc + n_out]
        rest = refs[n_in + 2 * nc + n_out:]
        scr, sems = rest[:n_scr], rest[n_scr:]
        first = functools.reduce(jnp.logical_and, [pl.program_id(d) == 0 for d in range(rank)])
        last = functools.reduce(jnp.logical_and, [pl.program_id(d) == grid[d] - 1 for d in range(rank)])

        @pl.when(first)
        def _():
            mine, copies = _exchange_copies(kind, csrc, cdst, *sems)
            for cp in mine + copies:
                cp.start()

        body(*ins, *outs, *scr)

        @pl.when(last)
        def _():
            mine, copies = _exchange_copies(kind, csrc, cdst, *sems)
            for cp in copies:
                cp.wait_recv()
            for cp in copies:
                cp.wait_send()
            for cp in mine:
                cp.wait()

    hbm = pl.BlockSpec(memory_space=pl.ANY)
    cshape = [jax.ShapeDtypeStruct(a.shape if kind == "a2a" else (NDEV,) + a.shape, a.dtype) for a in arrs]
    res = _pcall(wrapped, name, grid=grid, in_specs=list(in_specs) + [hbm] * nc,
                 out_specs=list(out_specs) + [hbm] * nc, out_shape=list(out_shape) + cshape,
                 scratch_shapes=list(scratch_shapes) + [pltpu.SemaphoreType.DMA((nc, 7)),
                                                        pltpu.SemaphoreType.DMA((nc, 7)),
                                                        pltpu.SemaphoreType.DMA((nc,))],
                 compiler_params=_cp(("arbitrary",) * rank, vmem_mib))(*args, *arrs)
    return res[:n_out], res[n_out:]


def _ada_fwd(c_all, w, b):
    n = w.shape[1]

    def body(c_ref, w_ref, b_ref, o_ref):
        c = c_ref[...]
        s = c * _sigmoid(c)
        o_ref[...] = jnp.dot(s, w_ref[...], preferred_element_type=F32, precision=lax.Precision.HIGHEST) + b_ref[...]

    return _pcall(body, "ada_fwd", out_shape=jax.ShapeDtypeStruct((NDEV, n), F32), compiler_params=_cp())(c_all, w, b)


def _ada_bwd(c_all_t, dmod):
    n = dmod.shape[1]

    def body(ct_ref, d_ref, o_ref):
        ct = ct_ref[...]
        s = ct * _sigmoid(ct)
        acc = s[:, 0:1] * d_ref[0:1, :]
        for b in range(1, NDEV):
            acc = acc + s[:, b:b + 1] * d_ref[b:b + 1, :]
        o_ref[...] = acc

    return _pcall(body, "ada_bwd", out_shape=jax.ShapeDtypeStruct((D, n), F32), compiler_params=_cp())(c_all_t, dmod)


def _ffn_in(x, g_pre, shift, scale, w_in, name, comm=None, tm=512):
    T = x.shape[0]

    def body(x_ref, g_ref, sh_ref, sc_ref, w_ref, h_ref, gu_ref, a_ref):
        xv = x_ref[...]
        r = lax.rsqrt(jnp.mean(xv * xv, axis=-1, keepdims=True) + RMS_EPS)
        hv = ((xv * r) * g_ref[...] * (1.0 + sc_ref[...]) + sh_ref[...]).astype(BF16)
        h_ref[...] = hv
        for j in range(NSL):
            g = _dot(hv, w_ref[j])
            u = _dot(hv, w_ref[j + NSL])
            gu_ref[j, 0] = g.astype(BF16)
            gu_ref[j, 1] = u.astype(BF16)
            a_ref[j] = (g * _sigmoid(g) * u).astype(BF16)

    row = pl.BlockSpec((tm, D), lambda i: (i, 0))
    vec = pl.BlockSpec((1, D), lambda i: (0, 0))
    (h, gu, a), got = _hosted_call(
        body, name, comm, (T // tm,),
        [row, vec, vec, vec, pl.BlockSpec((NDEV, D, SL), lambda i: (0, 0, 0), pipeline_mode=pl.Buffered(1))],
        [row, pl.BlockSpec((NSL, 2, tm, SL), lambda i: (0, 0, i, 0)), pl.BlockSpec((NSL, tm, SL), lambda i: (0, i, 0))],
        [jax.ShapeDtypeStruct((T, D), BF16), jax.ShapeDtypeStruct((NSL, 2, T, SL), BF16),
         jax.ShapeDtypeStruct((NSL, T, SL), BF16)],
        [], ("parallel",), 48, (x, g_pre, shift, scale, w_in))
    return h, gu, a, got


def _mm_post(a_list, a_specs, w_list, w_specs, x, g_post, gate, res_w, name, tm, tgt=None):
    T = x.shape[0]
    n = len(a_list)
    with_loss = tgt is not None

    def body(*refs):
        a_refs, w_refs = refs[:n], refs[n:2 * n]
        x_ref, g_ref, gt_ref = refs[2 * n:2 * n + 3]
        rest = refs[2 * n + 3:]
        f = None
        for a_ref, w_ref in zip(a_refs, w_refs):
            if len(a_ref.shape) == 3:
                terms = [_dot(a_ref[j], w_ref[j]) for j in range(a_ref.shape[0])]
            else:
                terms = [_dot(a_ref[...], w_ref[...])]
            for t in terms:
                f = t if f is None else f + t
        r = lax.rsqrt(jnp.mean(f * f, axis=-1, keepdims=True) + RMS_EPS)
        y = (f * r) * g_ref[...]
        out = x_ref[...] + (res_w * (1.0 + gt_ref[...])) * y
        if with_loss:
            t_ref, f_ref, dy_ref, l_ref = rest

            @pl.when(pl.program_id(0) == 0)
            def _():
                l_ref[...] = jnp.zeros_like(l_ref)

            e = out - t_ref[...]
            dy_ref[...] = e * (1.0 / D)
            l_ref[...] += 0.5 * jnp.sum(jnp.mean(e * e, axis=-1, keepdims=True), axis=0, keepdims=True)
        else:
            f_ref, o_ref = rest
            o_ref[...] = out
        f_ref[...] = f

    row = pl.BlockSpec((tm, D), lambda i: (i, 0))
    vec = pl.BlockSpec((1, D), lambda i: (0, 0))
    big = jax.ShapeDtypeStruct((T, D), F32)
    if with_loss:
        return _pcall(body, name, grid=(T // tm,),
                      in_specs=list(a_specs) + list(w_specs) + [row, vec, vec, row],
                      out_specs=[row, row, pl.BlockSpec((1, 1), lambda i: (0, 0))],
                      out_shape=[big, big, jax.ShapeDtypeStruct((1, 1), F32)],
                      compiler_params=_cp(("arbitrary",)))(*a_list, *w_list, x, g_post, gate, tgt)
    return _pcall(body, name, grid=(T // tm,),
                  in_specs=list(a_specs) + list(w_specs) + [row, vec, vec], out_specs=[row, row],
                  out_shape=[big, big], compiler_params=_cp(("parallel",)))(*a_list, *w_list, x, g_post, gate)


def _ffn_out_bwd(dout, f, g_post, gate, res_w, w_out4, gu, name, comm=None, tm=512):
    T = f.shape[0]

    def body(do_ref, f_ref, g_ref, gt_ref, w_ref, gu_ref, df_ref, dgate_ref, dg_ref, dgu_ref):
        @pl.when(pl.program_id(0) == 0)
        def _():
            dgate_ref[...] = jnp.zeros_like(dgate_ref)
            dg_ref[...] = jnp.zeros_like(dg_ref)

        do = do_ref[...]
        f = f_ref[...]
        r = lax.rsqrt(jnp.mean(f * f, axis=-1, keepdims=True) + RMS_EPS)
        fn = f * r
        dgate_ref[...] += jnp.sum((res_w * do) * (fn * g_ref[...]), axis=0, keepdims=True)
        dy = (res_w * (1.0 + gt_ref[...])) * do
        dg_ref[...] += jnp.sum(dy * fn, axis=0, keepdims=True)
        dyg = dy * g_ref[...]
        dfv = (r * (dyg - fn * jnp.mean(dyg * fn, axis=-1, keepdims=True))).astype(BF16)
        df_ref[...] = dfv
        for j in range(NSL):
            da = _dot_nt(dfv, w_ref[j])
            gv = gu_ref[j, 0].astype(F32)
            s = _sigmoid(gv)
            gs = gv * s
            dgu_ref[j, 0] = (da * gu_ref[j, 1].astype(F32) * (s + gs * (1.0 - s))).astype(BF16)
            dgu_ref[j, 1] = (da * gs).astype(BF16)

    row = pl.BlockSpec((tm, D), lambda i: (i, 0))
    vec = pl.BlockSpec((1, D), lambda i: (0, 0))
    gus = pl.BlockSpec((NSL, 2, tm, SL), lambda i: (0, 0, i, 0))
    return _hosted_call(
        body, name, comm, (T // tm,),
        [row, row, vec, vec, pl.BlockSpec((NSL, SL, D), lambda i: (0, 0, 0), pipeline_mode=pl.Buffered(1)), gus],
        [row, vec, vec, gus],
        [jax.ShapeDtypeStruct((T, D), BF16), jax.ShapeDtypeStruct((1, D), F32), jax.ShapeDtypeStruct((1, D), F32),
         jax.ShapeDtypeStruct((NSL, 2, T, SL), BF16)], [], ("arbitrary",), 48, (dout, f, g_post, gate, w_out4, gu))


def _mm_tn(a, a_spec, b, b_spec, out_shape, out_spec, grid, name, comm=None):
    k, nn = out_spec.block_shape[-2:]
    steps = grid[1]

    def body(a_ref, b_ref, o_ref, acc_ref):
        i = pl.program_id(1)

        @pl.when(i == 0)
        def _():
            acc_ref[...] = jnp.zeros_like(acc_ref)

        acc_ref[...] += _dot_tn(_mat(a_ref), _mat(b_ref))

        @pl.when(i == steps - 1)
        def _():
            lead = len(o_ref.shape) - 2
            o_ref[(0,) * lead if lead else ...] = acc_ref[...].astype(BF16)

    (out,), got = _hosted_call(body, name, comm, grid, [a_spec, b_spec], [out_spec],
                               [jax.ShapeDtypeStruct(out_shape, BF16)], [pltpu.VMEM((k, nn), F32)],
                               ("parallel", "arbitrary"), 48, (a, b))
    return out, got


def _dw_in(h, dgu, name, tmw, comm=None):
    T = h.shape[0]
    steps = T // tmw

    def body(h_ref, b_ref, o_ref, acc_ref):
        i = pl.program_id(1)

        @pl.when(i == 0)
        def _():
            acc_ref[...] = jnp.zeros_like(acc_ref)

        hv = h_ref[...]
        for p in range(2):
            acc_ref[p] += _dot_tn(hv, b_ref[0, p])

        @pl.when(i == steps - 1)
        def _():
            o_ref[:, 0] = acc_ref[...].astype(BF16)

    (out,), got = _hosted_call(
        body, name, comm, (NSL, steps),
        [pl.BlockSpec((tmw, D), lambda j, i: (i, 0)), pl.BlockSpec((1, 2, tmw, SL), lambda j, i: (j, 0, i, 0))],
        [pl.BlockSpec((2, 1, D, SL), lambda j, i: (0, j, 0, 0))],
        [jax.ShapeDtypeStruct((2, NSL, D, SL), BF16)], [pltpu.VMEM((2, D, SL), F32)],
        ("parallel", "arbitrary"), 56, (h, dgu))
    return out, got


def _mm_prebwd(a, a_spec, w, w_spec, dh_fn, x, dout, g_pre, scale, name, tm=256, comm=None):
    T = x.shape[0]

    def body(a_ref, w_ref, x_ref, do_ref, g_ref, sc_ref, dx_ref, dsh_ref, dsc_ref, dg_ref):
        @pl.when(pl.program_id(0) == 0)
        def _():
            dsh_ref[...] = jnp.zeros_like(dsh_ref)
            dsc_ref[...] = jnp.zeros_like(dsc_ref)
            dg_ref[...] = jnp.zeros_like(dg_ref)

        dh = dh_fn(a_ref, w_ref)
        xv = x_ref[...]
        r = lax.rsqrt(jnp.mean(xv * xv, axis=-1, keepdims=True) + RMS_EPS)
        xn = xv * r
        dsh_ref[...] += jnp.sum(dh, axis=0, keepdims=True)
        dsc_ref[...] += jnp.sum(dh * (xn * g_ref[...]), axis=0, keepdims=True)
        dn = dh * (1.0 + sc_ref[...])
        dg_ref[...] += jnp.sum(dn * xn, axis=0, keepdims=True)
        dng = dn * g_ref[...]
        dx_ref[...] = do_ref[...] + r * (dng - xn * jnp.mean(dng * xn, axis=-1, keepdims=True))

    row = pl.BlockSpec((tm, D), lambda i: (i, 0))
    vec = pl.BlockSpec((1, D), lambda i: (0, 0))
    return _hosted_call(body, name, comm, (T // tm,), [a_spec, w_spec, row, row, vec, vec], [row, vec, vec, vec],
                        [jax.ShapeDtypeStruct((T, D), F32)] + [jax.ShapeDtypeStruct((1, D), F32)] * 3,
                        [], ("arbitrary",), 56, (a, w, x, dout, g_pre, scale))


def _mix_in(x, g_pre, shift, scale, w, name, tm=512):
    T = x.shape[0]

    def body(x_ref, g_ref, sh_ref, sc_ref, w_ref, h_ref, qkv_ref, cvg_ref):
        xv = x_ref[...]
        r = lax.rsqrt(jnp.mean(xv * xv, axis=-1, keepdims=True) + RMS_EPS)
        hv = ((xv * r) * g_ref[...] * (1.0 + sc_ref[...]) + sh_ref[...]).astype(BF16)
        h_ref[...] = hv
        p = _dot(hv, w_ref[...])
        qkv_ref[...] = p[:, :3 * AW].astype(BF16)
        cvg_ref[...] = p[:, 3 * AW:]

    row = pl.BlockSpec((tm, D), lambda i: (i, 0))
    vec = pl.BlockSpec((1, D), lambda i: (0, 0))
    return _pcall(body, name, grid=(T // tm,),
                  in_specs=[row, vec, vec, vec, pl.BlockSpec((D, MIXIN), lambda i: (0, 0))],
                  out_specs=[row, pl.BlockSpec((tm, 3 * AW), lambda i: (i, 0)), pl.BlockSpec((tm, 2 * CW), lambda i: (i, 0))],
                  out_shape=[jax.ShapeDtypeStruct((T, D), BF16), jax.ShapeDtypeStruct((T, 3 * AW), BF16),
                             jax.ShapeDtypeStruct((T, 2 * CW), F32)],
                  compiler_params=_cp(("parallel",)))(x, g_pre, shift, scale, w)


def _mix_out_bwd(dout, f, g_post, gate, res_w, w, name, tm=512):
    T = f.shape[0]
    N = w.shape[0]

    def body(do_ref, f_ref, g_ref, gt_ref, w_ref, df_ref, dgate_ref, dg_ref, o_ref):
        @pl.when(pl.program_id(0) == 0)
        def _():
            dgate_ref[...] = jnp.zeros_like(dgate_ref)
            dg_ref[...] = jnp.zeros_like(dg_ref)

        do = do_ref[...]
        f = f_ref[...]
        r = lax.rsqrt(jnp.mean(f * f, axis=-1, keepdims=True) + RMS_EPS)
        fn = f * r
        dgate_ref[...] += jnp.sum((res_w * do) * (fn * g_ref[...]), axis=0, keepdims=True)
        dy = (res_w * (1.0 + gt_ref[...])) * do
        dg_ref[...] += jnp.sum(dy * fn, axis=0, keepdims=True)
        dyg = dy * g_ref[...]
        dfv = (r * (dyg - fn * jnp.mean(dyg * fn, axis=-1, keepdims=True))).astype(BF16)
        df_ref[...] = dfv
        o_ref[...] = _dot_nt(dfv, w_ref[...])

    row = pl.BlockSpec((tm, D), lambda i: (i, 0))
    vec = pl.BlockSpec((1, D), lambda i: (0, 0))
    return _pcall(body, name, grid=(T // tm,),
                  in_specs=[row, row, vec, vec, pl.BlockSpec((N, D), lambda i: (0, 0))],
                  out_specs=[row, vec, vec, pl.BlockSpec((tm, N), lambda i: (i, 0))],
                  out_shape=[jax.ShapeDtypeStruct((T, D), BF16), jax.ShapeDtypeStruct((1, D), F32),
                             jax.ShapeDtypeStruct((1, D), F32), jax.ShapeDtypeStruct((T, N), F32)],
                  compiler_params=_cp(("arbitrary",)))(dout, f, g_post, gate, w)


def _softplus_parts(z):
    ls = jnp.minimum(z, 0.0) - jnp.log(1.0 + jnp.exp(-jnp.abs(z)))
    return ls, ls - z


def _head_sum(x, first):
    sa = jnp.sum(jnp.where(first, x, 0.0), axis=-1, keepdims=True)
    sb = jnp.sum(jnp.where(first, 0.0, x), axis=-1, keepdims=True)
    return jnp.where(first, sa, sb)


def _attn_specs(T, tq):
    qs = pl.BlockSpec((tq, 128), lambda p, i: (i, p))
    ks = pl.BlockSpec((T, 128), lambda p, i: (0, 4 + p))
    vs = pl.BlockSpec((T, 128), lambda p, i: (0, 8 + p))
    gs = pl.BlockSpec((1, 128), lambda p, i: (0, p))
    return qs, ks, vs, gs


def _attn_fwd(qkv, g_attn, tq, comm=None):
    T = qkv.shape[0]

    def body(q_ref, k_ref, v_ref, g_ref, o_ref, an_ref):
        i = pl.program_id(1)
        first = lax.broadcasted_iota(jnp.int32, (tq, 128), 1) < HD
        q = (q_ref[...].astype(F32) * QK_SCALE).astype(BF16)
        zq = jnp.zeros_like(q)
        qs = (jnp.where(first, q, zq), jnp.where(first, zq, q))
        rows = lax.broadcasted_iota(jnp.int32, (tq, tq), 0)
        cols = lax.broadcasted_iota(jnp.int32, (tq, tq), 1)
        tri = (rows > cols).astype(BF16)
        tri2 = jnp.concatenate([tri, tri], axis=0)
        strict = cols < rows

        def tile(j, Rs, acc, masked):
            start = j * tq if isinstance(j, int) else pl.multiple_of(j * tq, tq)
            kb = k_ref[pl.ds(start, tq), :]
            vb = v_ref[pl.ds(start, tq), :]
            zs = [_dot_nt(qs[hh], kb) for hh in range(2)]
            parts = [_softplus_parts(z) for z in zs]
            lsms = [jnp.where(strict, p[1], 0.0) if masked else p[1] for p in parts]
            splits = [_split2(x) for x in lsms]
            afters = [_dot(jnp.concatenate(s, axis=1), tri2) for s in splits]
            ws = [jnp.exp(parts[hh][0] + afters[hh] + Rs[hh]) for hh in range(2)]
            if masked:
                ws = [jnp.where(strict, w, 0.0) for w in ws]
            outs = [_dot(w.astype(BF16), vb) for w in ws]
            new_r = [Rs[hh] + afters[hh][:, 0:1] + lsms[hh][:, 0:1] for hh in range(2)]
            return new_r[0], new_r[1], acc + jnp.where(first, outs[0], outs[1])

        zr = jnp.zeros((tq, 1), F32)

        def finish(acc):
            o_ref[...] = acc
            r = lax.rsqrt(_head_sum(acc * acc, first) * (1.0 / HD) + RMS_EPS)
            an_ref[...] = ((acc * r) * g_ref[...]).astype(BF16)

        @pl.when(i == 0)
        def _():
            finish(tile(0, (zr, zr), jnp.zeros((tq, 128), F32), True)[2])

        @pl.when(i > 0)
        def _():
            ra, rb, acc = tile(i, (zr, zr), jnp.zeros((tq, 128), F32), True)
            ra, rb, acc = tile(i - 1, (ra, rb), acc, False)

            def more(c):
                return jnp.logical_and(c[0] < i, jnp.maximum(jnp.max(c[1]), jnp.max(c[2])) > W_ZERO_BELOW)

            def step(c):
                ra, rb, acc = tile(i - 1 - c[0], (c[1], c[2]), c[3], False)
                return c[0] + 1, ra, rb, acc

            finish(lax.while_loop(more, step, (jnp.int32(1), ra, rb, acc))[3])

    qs, ks, vs, gs = _attn_specs(T, tq)
    (o, an), got = _hosted_call(body, "attn_fwd", comm, (AW // 128, T // tq), [qs, ks, vs, gs], [qs, qs],
                                [jax.ShapeDtypeStruct((T, AW), F32), jax.ShapeDtypeStruct((T, AW), BF16)],
                                [], ("parallel", "parallel"), 48, (qkv, qkv, qkv, g_attn))
    return o, an, got


def _attn_bwd(qkv, o, dcat, g_attn, tq, comm=None):
    T = qkv.shape[0]

    def body(q_ref, k_ref, v_ref, o_ref, dan_ref, g_ref, dq_ref, dk_ref, dv_ref, dg_ref):
        i = pl.program_id(1)

        @pl.when(i == 0)
        def _():
            dk_ref[...] = jnp.zeros_like(dk_ref)
            dv_ref[...] = jnp.zeros_like(dv_ref)
            dg_ref[...] = jnp.zeros_like(dg_ref)

        first = lax.broadcasted_iota(jnp.int32, (tq, 128), 1) < HD
        q = (q_ref[...].astype(F32) * QK_SCALE).astype(BF16)
        zq = jnp.zeros_like(q)
        qs = (jnp.where(first, q, zq), jnp.where(first, zq, q))
        o = o_ref[...]
        dan = dan_ref[...]
        r = lax.rsqrt(_head_sum(o * o, first) * (1.0 / HD) + RMS_EPS)
        on = o * r
        dg_ref[...] += jnp.sum(dan * on, axis=0, keepdims=True)
        dyg = dan * g_ref[...]
        dO = r * (dyg - on * (_head_sum(dyg * on, first) * (1.0 / HD)))
        dOb = dO.astype(BF16)
        dOs = (jnp.where(first, dOb, zq), jnp.where(first, zq, dOb))
        ones = jnp.ones((8, 128), BF16)
        Ds = []
        for hh in range(2):
            prod = dOs[hh].astype(F32) * o
            p1 = prod.astype(BF16)
            rem = prod - p1.astype(F32)
            p2 = rem.astype(BF16)
            p3 = (rem - p2.astype(F32)).astype(BF16)
            Ds.append((_dot_nt(ones, p1) + _dot_nt(ones, p2) + _dot_nt(ones, p3))[0:1, :])

        rows = lax.broadcasted_iota(jnp.int32, (tq, tq), 0)
        cols = lax.broadcasted_iota(jnp.int32, (tq, tq), 1)
        tri_after = (cols > rows).astype(BF16)
        tri_incl = (cols >= rows).astype(BF16)
        tri_after2 = jnp.concatenate([tri_after, tri_after], axis=1)
        tri_incl2 = jnp.concatenate([tri_incl, tri_incl], axis=1)
        strict = rows < cols

        def tile(j, Rs, Gs, dq, masked):
            start = j * tq if isinstance(j, int) else pl.multiple_of(j * tq, tq)
            kb = k_ref[pl.ds(start, tq), :]
            vb = v_ref[pl.ds(start, tq), :]
            H = range(2)
            parts = [_softplus_parts(_dot_nt(kb, qs[hh])) for hh in H]
            lsms = [jnp.where(strict, p[1], 0.0) if masked else p[1] for p in parts]
            splits = [_split2(x) for x in lsms]
            afters = [_dot(tri_after2, jnp.concatenate(s, axis=0)) for s in splits]
            ws = [jnp.exp(parts[hh][0] + afters[hh] + Rs[hh]) for hh in H]
            if masked:
                ws = [jnp.where(strict, w, 0.0) for w in ws]
            wbs = [w.astype(BF16) for w in ws]
            dlws = [_dot_nt(vb, dOs[hh]) * wbs[hh].astype(F32) for hh in H]
            splits2 = [_split2(x) for x in dlws]
            Cs = [_dot(tri_incl2, jnp.concatenate(s, axis=0)) for s in splits2]
            dlsms = [Ds[hh] - Gs[hh] - Cs[hh] for hh in H]
            if masked:
                dlsms = [jnp.where(strict, x, 0.0) for x in dlsms]
            ps = [jnp.exp(p[0]) for p in parts]
            dzs = [(dlws[hh] * (1.0 - ps[hh]) - dlsms[hh] * ps[hh]).astype(BF16) for hh in H]
            dkp = _dot(dzs[0], qs[0]) + _dot(dzs[1], qs[1])
            dvp = _dot(wbs[0], dOs[0]) + _dot(wbs[1], dOs[1])
            dq = dq + jnp.where(first, _dot_tn(dzs[0], kb), _dot_tn(dzs[1], kb))
            new_r = [Rs[hh] + afters[hh][0:1, :] + lsms[hh][0:1, :] for hh in H]
            new_g = [Gs[hh] + Cs[hh][0:1, :] for hh in H]
            dk_ref[pl.ds(start, tq), :] += dkp
            dv_ref[pl.ds(start, tq), :] += dvp
            return new_r[0], new_r[1], new_g[0], new_g[1], dq

        zrow = jnp.zeros((1, tq), F32)

        @pl.when(i == 0)
        def _():
            dq0 = tile(0, (zrow, zrow), (zrow, zrow), jnp.zeros((tq, 128), F32), True)[4]
            dq_ref[...] = (dq0 * QK_SCALE).astype(BF16)

        @pl.when(i > 0)
        def _():
            st = tile(i, (zrow, zrow), (zrow, zrow), jnp.zeros((tq, 128), F32), True)
            st = tile(i - 1, st[0:2], st[2:4], st[4], False)

            def more(c):
                return jnp.logical_and(c[0] < i, jnp.maximum(jnp.max(c[1]), jnp.max(c[2])) > W_ZERO_BELOW)

            def step(c):
                return (c[0] + 1,) + tile(i - 1 - c[0], (c[1], c[2]), (c[3], c[4]), c[5], False)

            dq_ref[...] = (lax.while_loop(more, step, (jnp.int32(1),) + st)[5] * QK_SCALE).astype(BF16)

    qs, ks, vs, gs = _attn_specs(T, tq)
    kacc = pl.BlockSpec((T, 128), lambda p, i: (0, p))
    return _hosted_call(body, "attn_bwd", comm, (AW // 128, T // tq), [qs, ks, vs, qs, qs, gs], [qs, kacc, kacc, gs],
                        [jax.ShapeDtypeStruct((T, AW), BF16), jax.ShapeDtypeStruct((T, AW), F32),
                         jax.ShapeDtypeStruct((T, AW), F32), jax.ShapeDtypeStruct((1, AW), F32)],
                        [], ("parallel", "arbitrary"), 48, (qkv, qkv, qkv, o, dcat, g_attn))


def _taps_by_phase(offsets):
    groups = {}
    for k, off in enumerate(offsets):
        groups.setdefault(off % 8, []).append((k, off // 8))
    return sorted(groups.items())


def _shifted_tap_sum(w_ref, pad_ref, ph_ref, offsets, tb):
    acc = None
    for p, taps in _taps_by_phase(offsets):
        n = tb if p == 0 else tb + 8
        a = None
        for k, m in taps:
            t = w_ref[k:k + 1, :] * pad_ref[8 * m:8 * m + n, :]
            a = t if a is None else a + t
        if p:
            ph_ref[...] = a
            a = ph_ref[p:p + tb, :]
        acc = a if acc is None else acc + a
    return acc


def _conv_fwd(cvg, conv_w, conv_b, ln_g, ln_b, tb=512):
    T = cvg.shape[0]
    hb = tb // HALO

    def body(cv_ref, cg_ref, cvp_ref, cgp_ref, w_ref, b_ref, g_ref, be_ref, u0_ref, u1_ref, u3_ref, pad_ref, ph_ref):
        i = pl.program_id(0)
        u0 = cv_ref[...] * _sigmoid(cg_ref[...])
        prev = cvp_ref[...] * _sigmoid(cgp_ref[...])
        pad_ref[0:HALO, :] = jnp.where(i > 0, prev, 0.0)
        pad_ref[HALO:HALO + tb, :] = u0
        u0_ref[...] = u0
        acc = _shifted_tap_sum(w_ref, pad_ref, ph_ref, [HALO - (CK - 1) + kk for kk in range(CK)], tb) + b_ref[...]
        u1_ref[...] = acc
        mu = jnp.mean(acc, axis=-1, keepdims=True)
        xc = acc - mu
        var = jnp.mean(xc * xc, axis=-1, keepdims=True)
        u2 = (xc * lax.rsqrt(var + LN_EPS)) * g_ref[...] + be_ref[...]
        u3_ref[...] = (u2 * _sigmoid(u2)).astype(BF16)

    cur = lambda col: pl.BlockSpec((tb, CW), lambda i: (i, col))
    prv = lambda col: pl.BlockSpec((HALO, CW), lambda i: (jnp.maximum(i * hb - 1, 0), col))
    vec = pl.BlockSpec((1, CW), lambda i: (0, 0))
    out = pl.BlockSpec((tb, CW), lambda i: (i, 0))
    return _pcall(body, "conv_fwd", grid=(T // tb,),
                  in_specs=[cur(0), cur(1), prv(0), prv(1), pl.BlockSpec((HALO, CW), lambda i: (0, 0)), vec, vec, vec],
                  out_specs=[out, out, out],
                  out_shape=[jax.ShapeDtypeStruct((T, CW), F32), jax.ShapeDtypeStruct((T, CW), F32),
                             jax.ShapeDtypeStruct((T, CW), BF16)],
                  scratch_shapes=[pltpu.VMEM((tb + HALO, CW), F32), pltpu.VMEM((tb + 8, CW), F32)],
                  compiler_params=_cp(("parallel",)))(cvg, cvg, cvg, cvg, conv_w, conv_b, ln_g, ln_b)


def _conv_bwd1(dcat, u1, u0, ln_g, ln_b, tb=512):
    T = u1.shape[0]
    hb = tb // HALO

    def body(d3_ref, u1_ref, u0_ref, u0p_ref, g_ref, be_ref, du1_ref, dw_ref, db_ref, dlg_ref, dlb_ref, pad_ref, d_ref,
             q_ref):
        i = pl.program_id(0)

        @pl.when(i == 0)
        def _():
            dw_ref[...] = jnp.zeros_like(dw_ref)
            db_ref[...] = jnp.zeros_like(db_ref)
            dlg_ref[...] = jnp.zeros_like(dlg_ref)
            dlb_ref[...] = jnp.zeros_like(dlb_ref)

        u1 = u1_ref[...]
        mu = jnp.mean(u1, axis=-1, keepdims=True)
        xc = u1 - mu
        rstd = lax.rsqrt(jnp.mean(xc * xc, axis=-1, keepdims=True) + LN_EPS)
        xh = xc * rstd
        u2 = xh * g_ref[...] + be_ref[...]
        s = _sigmoid(u2)
        du2 = d3_ref[...] * (s + u2 * s * (1.0 - s))
        dlg_ref[...] += jnp.sum(du2 * xh, axis=0, keepdims=True)
        dlb_ref[...] += jnp.sum(du2, axis=0, keepdims=True)
        dxh = du2 * g_ref[...]
        du1 = rstd * (dxh - jnp.mean(dxh, axis=-1, keepdims=True) - xh * jnp.mean(dxh * xh, axis=-1, keepdims=True))
        du1_ref[...] = du1
        db_ref[...] += jnp.sum(du1, axis=0, keepdims=True)
        pad_ref[0:HALO, :] = jnp.where(i > 0, u0p_ref[...], 0.0)
        pad_ref[HALO:HALO + tb, :] = u0_ref[...]
        d_ref[0:8, :] = jnp.zeros((8, CW), F32)
        d_ref[8:8 + tb, :] = du1
        d_ref[8 + tb:16 + tb, :] = jnp.zeros((8, CW), F32)
        for p, taps in _taps_by_phase([HALO - (CK - 1) + kk for kk in range(CK)]):
            n = tb + 8
            q_ref[...] = d_ref[8 - p:8 - p + n, :]
            for k, m in taps:
                if 8 * m + n <= tb + HALO:
                    dw_ref[k:k + 1, :] += jnp.sum(q_ref[...] * pad_ref[8 * m:8 * m + n, :], axis=0, keepdims=True)
                else:
                    dw_ref[k:k + 1, :] += jnp.sum(q_ref[0:tb, :] * pad_ref[8 * m:8 * m + tb, :], axis=0, keepdims=True)

    cur = pl.BlockSpec((tb, CW), lambda i: (i, 0))
    vec = pl.BlockSpec((1, CW), lambda i: (0, 0))
    return _pcall(body, "conv_bwd1", grid=(T // tb,),
                  in_specs=[pl.BlockSpec((tb, CW), lambda i: (i, 1)), cur, cur,
                            pl.BlockSpec((HALO, CW), lambda i: (jnp.maximum(i * hb - 1, 0), 0)), vec, vec],
                  out_specs=[cur, pl.BlockSpec((HALO, CW), lambda i: (0, 0)), vec, vec, vec],
                  out_shape=[jax.ShapeDtypeStruct((T, CW), F32), jax.ShapeDtypeStruct((HALO, CW), F32)]
                  + [jax.ShapeDtypeStruct((1, CW), F32)] * 3,
                  scratch_shapes=[pltpu.VMEM((tb + HALO, CW), F32), pltpu.VMEM((tb + 16, CW), F32),
                                  pltpu.VMEM((tb + 8, CW), F32)],
                  compiler_params=_cp(("arbitrary",)))(dcat, u1, u0, u0, ln_g, ln_b)


def _conv_bwd2(du1, cvg, conv_w, tb=512):
    T = du1.shape[0]
    hb = tb // HALO
    last = T // HALO - 1
    nblk = T // tb

    def body(d_ref, dn_ref, cv_ref, cg_ref, w_ref, o_ref, pad_ref, ph_ref):
        i = pl.program_id(0)
        pad_ref[0:tb, :] = d_ref[...]
        pad_ref[tb:tb + HALO, :] = jnp.where(i < nblk - 1, dn_ref[...], 0.0)
        acc = _shifted_tap_sum(w_ref, pad_ref, ph_ref, [CK - 1 - kk for kk in range(CK)], tb)
        sg = _sigmoid(cg_ref[...])
        o_ref[:, 0:CW] = (acc * sg).astype(BF16)
        o_ref[:, CW:2 * CW] = (acc * cv_ref[...] * sg * (1.0 - sg)).astype(BF16)

    cur = pl.BlockSpec((tb, CW), lambda i: (i, 0))
    return _pcall(body, "conv_bwd2", grid=(nblk,),
                  in_specs=[cur, pl.BlockSpec((HALO, CW), lambda i: (jnp.minimum((i + 1) * hb, last), 0)),
                            pl.BlockSpec((tb, CW), lambda i: (i, 0)), pl.BlockSpec((tb, CW), lambda i: (i, 1)),
                            pl.BlockSpec((HALO, CW), lambda i: (0, 0))],
                  out_specs=pl.BlockSpec((tb, 2 * CW), lambda i: (i, 0)),
                  out_shape=jax.ShapeDtypeStruct((T, 2 * CW), BF16),
                  scratch_shapes=[pltpu.VMEM((tb + HALO, CW), F32), pltpu.VMEM((tb + 8, CW), F32)],
                  compiler_params=_cp(("parallel",)))(du1, du1, cvg, cvg, conv_w)


def _adam_math(w, g, m, v):
    nm = ADAM_B1 * m + (1.0 - ADAM_B1) * g
    nv = ADAM_B2 * v + (1.0 - ADAM_B2) * (g * g)
    delta = -ADAM_LR * ((nm * ADAM_C1) / (jnp.sqrt(nv * ADAM_C2) + ADAM_EPS) + ADAM_WD * w)
    return delta, nm, nv


def _adamw(w, gslots, m, v, name, tb):
    R, C = w.shape
    S = gslots.shape[0]

    def body(w_ref, gs_ref, m_ref, v_ref, g_ref, d_ref, nm_ref, nv_ref):
        g = gs_ref[0].astype(F32)
        for s in range(1, S):
            g = g + gs_ref[s].astype(F32)
        g_ref[...] = g
        d_ref[...], nm_ref[...], nv_ref[...] = _adam_math(w_ref[...], g, m_ref[...], v_ref[...])

    blk = pl.BlockSpec((tb, C), lambda i: (i, 0))
    return _pcall(body, name, grid=(R // tb,),
                  in_specs=[blk, pl.BlockSpec((S, tb, C), lambda i: (0, i, 0)), blk, blk],
                  out_specs=[blk] * 4, out_shape=[jax.ShapeDtypeStruct((R, C), F32)] * 4,
                  compiler_params=_cp(("parallel",)))(w, gslots, m, v)


def _adamw_small(gall, gattn, gconvw, ws, ms, vs):
    n = len(ws)

    def body(*refs):
        gall_ref, gattn_ref, gconvw_ref = refs[:3]
        w_refs, m_refs, v_refs = refs[3:3 + n], refs[3 + n:3 + 2 * n], refs[3 + 2 * n:3 + 3 * n]
        loss_ref = refs[3 + 3 * n]
        outs = refs[4 + 3 * n:]
        g_refs, d_refs, nm_refs, nv_refs = outs[:n], outs[n:2 * n], outs[2 * n:3 * n], outs[3 * n:]

        def total(ref):
            t = ref[0]
            for dev in range(1, NDEV):
                t = t + ref[dev]
            return t

        tot = total(gall_ref)
        grads = [tot[0:9, :]] + [tot[ROW_GAINS + k:ROW_GAINS + k + 1, :] for k in range(6)]
        grads += [total(gattn_ref), tot[ROW_ATTN_CB:ROW_ATTN_CB + 1, CW:2 * CW], tot[ROW_LN:ROW_LN + 1, 0:CW],
                  tot[ROW_LN:ROW_LN + 1, CW:2 * CW], total(gconvw_ref)]
        loss_ref[...] = tot[ROW_LOSS:ROW_LOSS + 1, 0:1]
        for k in range(n):
            g_refs[k][...] = grads[k]
            d_refs[k][...], nm_refs[k][...], nv_refs[k][...] = _adam_math(w_refs[k][...], grads[k], m_refs[k][...],
                                                                          v_refs[k][...])

    shapes = [jax.ShapeDtypeStruct(w.shape, F32) for w in ws]
    res = _pcall(body, "adamw_small", out_shape=[jax.ShapeDtypeStruct((1, 1), F32)] + shapes * 4,
                 compiler_params=_cp())(gall, gattn, gconvw, *ws, *ms, *vs)
    return res[0], [res[1 + k * n:1 + (k + 1) * n] for k in range(4)]


def _ffn_fwd(x, g_pre, g_post, shift, scale, gate, w_in, w_out4, tag, tm, comm=None, tgt=None):
    h, gu, a, got = _ffn_in(x, g_pre, shift, scale, w_in, "ffn_in_" + tag, comm)
    res = _mm_post([a], [pl.BlockSpec((NSL, tm, SL), lambda i: (0, i, 0))],
                   [w_out4], [pl.BlockSpec((NSL, SL, D), lambda i: (0, 0, 0))],
                   x, g_post, gate, 0.5, "ffn_out_" + tag, tm, tgt)
    return (res[1] if tgt is None else (res[1], res[2])), (x, h, gu, a, res[0]), got


def _ffn_bwd(dout, saved, g_pre, g_post, scale, gate, w_in, w_out4, tag, tmb, tmw, send_in=True, carry=None):
    x, h, gu, a, f = saved
    T = x.shape[0]
    (df, dgate, dg_post, dgu), carried = _ffn_out_bwd(dout, f, g_post, gate, 0.5, w_out4, gu,
                                                      "ffn_out_bwd_" + tag, carry)
    dw_out, _ = _mm_tn(a, pl.BlockSpec((1, tmw, SL), lambda j, i: (j, i, 0)),
                       df, pl.BlockSpec((tmw, D), lambda j, i: (i, 0)),
                       (NSL, SL, D), pl.BlockSpec((1, SL, D), lambda j, i: (j, 0, 0)), (NSL, T // tmw), "dw_out_" + tag)
    dw_in, (r_out,) = _dw_in(h, dgu, "dw_in_" + tag, tmw, comm=("a2a", [dw_out.reshape(NDEV, SL // 2, D)]))
    dw_in = dw_in.reshape(NDEV, D, SL)

    def dh_fn(a_ref, w_ref):
        dh = None
        for p in range(2):
            for j in range(NSL):
                t = _dot_nt(a_ref[j, p], w_ref[NSL * p + j])
                dh = t if dh is None else dh + t
        return dh

    (dx, dshift, dscale, dg_pre), got = _mm_prebwd(
        dgu, pl.BlockSpec((NSL, 2, tmb, SL), lambda i: (0, 0, i, 0)),
        w_in, pl.BlockSpec((NDEV, D, SL), lambda i: (0, 0, 0), pipeline_mode=pl.Buffered(1)),
        dh_fn, x, dout, g_pre, scale, "ffn_in_bwd_" + tag, tmb, comm=("a2a", [dw_in]) if send_in else None)
    return dx, got[0] if send_in else dw_in, r_out, dg_pre, dg_post, (dshift, dscale, dgate), carried


def kernel(x, c, w_ada, b_ada, g_pre_ff1, g_post_ff1, ff1_w_in, ff1_w_out, g_pre_mix, g_post_mix, w_in_mix, g_attn_out, conv_w, conv_b, conv_ln_g, conv_ln_b, w_out_mix, g_pre_ff2, g_post_ff2, ff2_w_in, ff2_w_out, loss_target, m_w_ada, m_b_ada, m_g_pre_ff1, m_g_post_ff1, m_ff1_w_in, m_ff1_w_out, m_g_pre_mix, m_g_post_mix, m_w_in_mix, m_g_attn_out, m_conv_w, m_conv_b, m_conv_ln_g, m_conv_ln_b, m_w_out_mix, m_g_pre_ff2, m_g_post_ff2, m_ff2_w_in, m_ff2_w_out, v_w_ada, v_b_ada, v_g_pre_ff1, v_g_post_ff1, v_ff1_w_in, v_ff1_w_out, v_g_pre_mix, v_g_post_mix, v_w_in_mix, v_g_attn_out, v_conv_w, v_conv_b, v_conv_ln_g, v_conv_ln_b, v_w_out_mix, v_g_pre_ff2, v_g_post_ff2, v_ff2_w_in, v_ff2_w_out):
    me = 4 * lax.axis_index("x") + 2 * lax.axis_index("y") + lax.axis_index("c")
    T = x.shape[1]
    tq = min(256, T)
    tm = 512
    tmb = 512
    tmw = 2048
    x0 = x.reshape(T, D)
    tgt = loss_target.reshape(T, D)
    row = lambda a: a.reshape(1, -1)

    small_in = jnp.concatenate([c.reshape(-1), jnp.pad(conv_w.reshape(-1), (0, 2 * D - CK * 64)),
                                jnp.zeros((5 * D,), F32)]).reshape(8, D)
    small_all, = _all_gather([small_in], "gather_c_convw", True)
    c_all = small_all[:, 0, :]
    conv_w_full = small_all[:, 1:3, :].reshape(NDEV, 2 * D)[:, :CK * 64].reshape(NDEV, CK, 64)
    conv_w_full = conv_w_full.transpose(1, 0, 2).reshape(CK, CW)
    conv_w_pad = jnp.pad(conv_w_full, ((0, HALO - CK), (0, 0)))

    big = [ff1_w_in, ff1_w_out, w_in_mix, w_out_mix, ff2_w_in, ff2_w_out]
    shards = [w.astype(BF16) for w in big]
    w_in1, w_out1 = _all_gather(shards[0:2], "gather_weights_ff1", False)
    w_out1_4 = w_out1.reshape(NSL, SL, D)

    b_cols = lax.dynamic_slice(b_ada, (me * ADA_COLS,), (ADA_COLS,)).reshape(1, ADA_COLS)
    mod_cols = _ada_fwd(c_all, w_ada, b_cols)
    mod_all, = _all_gather([mod_cols], "gather_mod", True)
    mod = lax.dynamic_slice(mod_all, (0, me, 0), (NDEV, 1, ADA_COLS)).reshape(9, D)
    sh = lambda s: mod[3 * s:3 * s + 1]
    sc = lambda s: mod[3 * s + 1:3 * s + 2]
    gt = lambda s: mod[3 * s + 2:3 * s + 3]

    x1, sv1, (w_inm_s, w_outm_s) = _ffn_fwd(x0, row(g_pre_ff1), row(g_post_ff1), sh(0), sc(0), gt(0), w_in1, w_out1_4,
                                            "ff1", tm, comm=("gather", shards[2:4]))
    w_inm = w_inm_s.transpose(1, 0, 2).reshape(D, MIXIN)
    w_outm = w_outm_s.reshape(D, D)
    hm, qkv, cvg = _mix_in(x1, row(g_pre_mix), sh(1), sc(1), w_inm, "mix_in")
    g_attn_row = row(g_attn_out)
    o_att, an, (w_in2, w_out2) = _attn_fwd(qkv, g_attn_row, tq, comm=("gather", shards[4:6]))
    w_out2_4 = w_out2.reshape(NSL, SL, D)
    u0, u1, u3 = _conv_fwd(cvg, conv_w_pad, row(conv_b), row(conv_ln_g), row(conv_ln_b))
    half = lambda k: pl.BlockSpec((AW, D), lambda i: (k, 0))
    act = pl.BlockSpec((tm, AW), lambda i: (i, 0))
    fm, x2 = _mm_post([an, u3], [act, act], [w_outm, w_outm], [half(0), half(1)],
                      x1, row(g_post_mix), gt(1), 1.0, "mix_out", tm)
    (dy, loss_part), sv2, _ = _ffn_fwd(x2, row(g_pre_ff2), row(g_post_ff2), sh(2), sc(2), gt(2), w_in2, w_out2_4,
                                       "ff2", tm, tgt=tgt)

    dx2, dw_in2, r_out2, dgpre2, dgpost2, dmod2, _ = _ffn_bwd(
        dy, sv2, row(g_pre_ff2), row(g_post_ff2), sc(2), gt(2), w_in2, w_out2_4, "ff2", tmb, tmw, send_in=False)

    dfm, dgate1, dgpostm, dcat = _mix_out_bwd(dx2, fm, row(g_post_mix), gt(1), 1.0, w_outm, "mix_out_bwd")
    tok = pl.BlockSpec((tmw, AW), lambda j, i: (i, 0))
    tokd = pl.BlockSpec((tmw, D), lambda j, i: (i, 0))
    whole = pl.BlockSpec((AW, D), lambda j, i: (0, 0))
    dw_outm = jnp.concatenate([_mm_tn(an, tok, dfm, tokd, (AW, D), whole, (1, T // tmw), "dw_out_mix_a")[0],
                               _mm_tn(u3, tok, dfm, tokd, (AW, D), whole, (1, T // tmw), "dw_out_mix_c")[0]], axis=0)
    (dq, dk, dv, dg_attn), (r_in2,) = _attn_bwd(qkv, o_att, dcat, g_attn_row, tq, comm=("a2a", [dw_in2]))
    du1, dconv_w, dconv_b, dln_g, dln_b = _conv_bwd1(dcat, u1, u0, row(conv_ln_g), row(conv_ln_b))
    dcvg = _conv_bwd2(du1, cvg, conv_w_pad)
    dproj = jnp.concatenate([dq, dk.astype(BF16), dv.astype(BF16), dcvg], axis=1)
    dw_inm, _ = _mm_tn(hm, pl.BlockSpec((tmw, D), lambda j, i: (i, 0)),
                       dproj, pl.BlockSpec((tmw, MIXIN // 2), lambda j, i: (i, j)),
                       (D, MIXIN), pl.BlockSpec((D, MIXIN // 2), lambda j, i: (0, j)), (2, T // tmw), "dw_in_mix")
    (dx1, dshift1, dscale1, dgprem), _ = _mm_prebwd(
        dproj, pl.BlockSpec((tmb, MIXIN), lambda i: (i, 0)), w_inm, pl.BlockSpec((D, MIXIN), lambda i: (0, 0)),
        lambda a_ref, w_ref: _dot_nt(a_ref[...], w_ref[...]), x1, dx2, row(g_pre_mix), sc(1), "mix_in_bwd", tmb)

    dx0, r_in1, r_out1, dgpre1, dgpost1, dmod0, (r_inm, r_outm) = _ffn_bwd(
        dx1, sv1, row(g_pre_ff1), row(g_post_ff1), sc(0), gt(0), w_in1, w_out1_4, "ff1", tmb, tmw,
        carry=("a2a", [dw_inm.reshape(D, NDEV, 320).transpose(1, 0, 2), dw_outm.reshape(NDEV, 128, D)]))
    recvs = [r_in1, r_out1, r_inm, r_outm, r_in2, r_out2]

    zrow = jnp.zeros((1, D), F32)
    small_g = jnp.concatenate(
        list(dmod0) + [dshift1, dscale1, dgate1] + list(dmod2)
        + [dgpre1, dgpost1, dgprem, dgpostm, dgpre2, dgpost2]
        + [jnp.concatenate([dg_attn, dconv_b], axis=1), jnp.concatenate([dln_g, dln_b], axis=1),
           jnp.pad(dconv_w[:CK].reshape(-1), (0, CONVW_ROWS * D - CK * CW)).reshape(CONVW_ROWS, D),
           jnp.pad(loss_part, ((0, 0), (0, D - 1)))] + [zrow] * (SMALL_R - ROW_LOSS - 1), axis=0)
    small_g_all, = _all_gather([small_g], "gather_small_grads", True)

    dmod_all = small_g_all[:, 0:9, :].reshape(NDEV, NMOD)
    dmod_cols = lax.dynamic_slice(dmod_all, (0, me * ADA_COLS), (NDEV, ADA_COLS))
    g_w_ada = _ada_bwd(c_all.T, dmod_cols)

    gattn = small_g_all[:, ROW_ATTN_CB, 0:AW].reshape(NDEV, 8, HD)
    gconvw = small_g_all[:, ROW_CONVW:ROW_CONVW + CONVW_ROWS, :].reshape(NDEV, CONVW_ROWS * D)[:, :CK * CW]
    gconvw = lax.dynamic_slice(gconvw.reshape(NDEV, CK, CW), (0, 0, me * 64), (NDEV, CK, 64))

    def small_list(b, g6, ga, cb, lg, lb, cw):
        return [b.reshape(9, D)] + [row(g) for g in g6] + [ga, row(cb), row(lg), row(lb), cw]

    sw = small_list(b_ada, [g_pre_ff1, g_post_ff1, g_pre_mix, g_post_mix, g_pre_ff2, g_post_ff2], g_attn_out,
                    conv_b, conv_ln_g, conv_ln_b, conv_w)
    sm = small_list(m_b_ada, [m_g_pre_ff1, m_g_post_ff1, m_g_pre_mix, m_g_post_mix, m_g_pre_ff2, m_g_post_ff2],
                    m_g_attn_out, m_conv_b, m_conv_ln_g, m_conv_ln_b, m_conv_w)
    sv = small_list(v_b_ada, [v_g_pre_ff1, v_g_post_ff1, v_g_pre_mix, v_g_post_mix, v_g_pre_ff2, v_g_post_ff2],
                    v_g_attn_out, v_conv_b, v_conv_ln_g, v_conv_ln_b, v_conv_w)
    loss, s_out = _adamw_small(small_g_all, gattn, gconvw, sw, sm, sv)
    s_out = [[o.reshape(w.shape) for o, w in zip(outs, [b_ada, g_pre_ff1, g_post_ff1, g_pre_mix, g_post_mix,
                                                        g_pre_ff2, g_post_ff2, g_attn_out, conv_b, conv_ln_g,
                                                        conv_ln_b, conv_w])] for outs in s_out]

    big_m = [m_ff1_w_in, m_ff1_w_out, m_w_in_mix, m_w_out_mix, m_ff2_w_in, m_ff2_w_out]
    big_v = [v_ff1_w_in, v_ff1_w_out, v_w_in_mix, v_w_out_mix, v_ff2_w_in, v_ff2_w_out]
    tbs = [256, 176, 256, 128, 256, 176]
    tags = ["ff1_w_in", "ff1_w_out", "w_in_mix", "w_out_mix", "ff2_w_in", "ff2_w_out"]
    b_out = [_adamw(big[k], recvs[k], big_m[k], big_v[k], "adamw_" + tags[k], tbs[k]) for k in range(6)]
    a_out = _adamw(w_ada, g_w_ada.reshape(1, D, ADA_COLS), m_w_ada, v_w_ada, "adamw_ada", 256)

    def leaves(k):
        s = s_out[k]
        b = [o[k] for o in b_out]
        return [a_out[k], s[0], s[1], s[2], b[0], b[1], s[3], s[4], b[2], s[7], s[11], s[8], s[9], s[10], b[3],
                s[5], s[6], b[4], b[5]]

    return (loss.reshape(()), dx0.reshape(1, T, D), *leaves(0), *leaves(1), *leaves(2), *leaves(3))
```

```python
import functools

import jax
import jax.numpy as jnp
from jax import lax
from jax.experimental import pallas as pl
from jax.experimental.pallas import tpu as pltpu

F32 = jnp.float32
BF16 = jnp.bfloat16
D = 1024
DFF = 2816
SL = 704
NSL = DFF // SL
AW = 512
HD = 64
CW = 512
CK = 31
HALO = 32
MIXIN = 2560
NDEV = 8
NMOD = 9 * D
ADA_COLS = NMOD // NDEV
RMS_EPS = 1e-6
LN_EPS = 1e-5
QK_SCALE = HD ** -0.5
W_ZERO_BELOW = -104.0
ADAM_LR, ADAM_B1, ADAM_B2, ADAM_EPS, ADAM_WD, ADAM_STEP = 0.001, 0.9, 0.999, 1e-08, 0.01, 10
ADAM_C1 = 1.0 / (1.0 - ADAM_B1 ** ADAM_STEP)
ADAM_C2 = 1.0 / (1.0 - ADAM_B2 ** ADAM_STEP)
MIB = 1024 * 1024
MESH = pl.DeviceIdType.MESH

ROW_GAINS = 9
ROW_ATTN_CB = 15
ROW_LN = 16
ROW_CONVW = 17
CONVW_ROWS = 16
ROW_LOSS = 33
SMALL_R = 40


def _pcall(body, name, **kw):
    return pl.pallas_call(body, name=name, **kw)


def _cp(sem=None, vmem_mib=48):
    if sem is None:
        return pltpu.CompilerParams(vmem_limit_bytes=vmem_mib * MIB)
    return pltpu.CompilerParams(dimension_semantics=sem, vmem_limit_bytes=vmem_mib * MIB)


def _dot(a, b):
    return jnp.dot(a, b, preferred_element_type=F32)


def _dot_nt(a, b):
    return lax.dot_general(a, b, (((1,), (1,)), ((), ())), preferred_element_type=F32)


def _dot_tn(a, b):
    return lax.dot_general(a, b, (((0,), (0,)), ((), ())), preferred_element_type=F32)


def _sigmoid(x):
    return 0.5 * jnp.tanh(0.5 * x) + 0.5


def _split2(x):
    hi = x.astype(BF16)
    mid = (x - hi.astype(F32)).astype(BF16)
    return hi, mid


def _mat(ref):
    lead = len(ref.shape) - 2
    return ref[(0,) * lead] if lead else ref[...]


def _all_gather(xs, name, in_vmem):
    n = len(xs)

    def body(*refs):
        x_refs, out_refs = refs[:n], refs[n:2 * n]
        send_sems, recv_sems, local_sems = refs[2 * n:]
        mx, my, mc = lax.axis_index("x"), lax.axis_index("y"), lax.axis_index("c")
        me, sibling = (mx, my, mc), (mx, my, 1 - mc)
        chips = [(1 - mx, my), (mx, 1 - my), (1 - mx, 1 - my)]

        def slab(a, px, py, pc):
            return out_refs[a].at[4 * px + 2 * py + pc]

        def copy(a, k, block, to, src=None):
            return pltpu.make_async_remote_copy(
                src_ref=slab(a, *block) if src is None else src, dst_ref=slab(a, *block),
                send_sem=send_sems.at[a, k], recv_sem=recv_sems.at[a, k], device_id=to, device_id_type=MESH)

        mine = [pltpu.make_async_copy(x_refs[a], slab(a, *me), local_sems.at[a]) for a in range(n)]
        for cp in mine:
            cp.start()
        first = []
        for a in range(n):
            first.append(copy(a, 0, me, sibling, src=x_refs[a]))
            first += [copy(a, 1 + j, me, (*chip, mc), src=x_refs[a]) for j, chip in enumerate(chips)]
        for cp in first:
            cp.start()
        passed = []
        for j, chip in enumerate(chips):
            for a in range(n):
                copy(a, 1 + j, (*chip, mc), me).wait_recv()
                passed.append(copy(a, 4 + j, (*chip, mc), sibling))
                passed[-1].start()
        for a in range(n):
            copy(a, 0, sibling, me).wait_recv()
            for j, chip in enumerate(chips):
                copy(a, 4 + j, (*chip, 1 - mc), me).wait_recv()
        for cp in first + passed:
            cp.wait_send()
        for cp in mine:
            cp.wait()

    space = pltpu.VMEM if in_vmem else pl.ANY
    return _pcall(
        body, name,
        out_shape=[jax.ShapeDtypeStruct((NDEV,) + x.shape, x.dtype) for x in xs],
        in_specs=[pl.BlockSpec(memory_space=space)] * n,
        out_specs=[pl.BlockSpec(memory_space=space)] * n,
        scratch_shapes=[pltpu.SemaphoreType.DMA((n, 7)), pltpu.SemaphoreType.DMA((n, 7)),
                        pltpu.SemaphoreType.DMA((n,))],
    )(*xs)


def _exchange_copies(kind, src, dst, send_sems, recv_sems, local_sems):
    mx, my, mc = lax.axis_index("x"), lax.axis_index("y"), lax.axis_index("c")
    me = 4 * mx + 2 * my + mc
    n = len(src)
    pick = (lambda a, p: src[a].at[p]) if kind == "a2a" else (lambda a, p: src[a])
    mine = [pltpu.make_async_copy(pick(a, me), dst[a].at[me], local_sems.at[a]) for a in range(n)]
    copies = []
    for r in range(1, NDEV):
        px = 1 - mx if r & 4 else mx
        py = 1 - my if r & 2 else my
        pc = 1 - mc if r & 1 else mc
        for a in range(n):
            copies.append(pltpu.make_async_remote_copy(
                src_ref=pick(a, 4 * px + 2 * py + pc), dst_ref=dst[a].at[me],
                send_sem=send_sems.at[a, r - 1], recv_sem=recv_sems.at[a, r - 1],
                device_id=(px, py, pc), device_id_type=MESH))
    return mine, copies


def _hosted_call(body, name, comm, grid, in_specs, out_specs, out_shape, scratch_shapes, sem, vmem_mib, args):
    if comm is None:
        outs = _pcall(body, name, grid=grid, in_specs=in_specs, out_specs=out_specs, out_shape=out_shape,
                      scratch_shapes=scratch_shapes, compiler_params=_cp(sem, vmem_mib))(*args)
        return outs, []
    kind, arrs = comm
    nc, n_in, n_out, n_scr = len(arrs), len(in_specs), len(out_specs), len(scratch_shapes)
    rank = len(grid)

    def wrapped(*refs):
        ins, csrc = refs[:n_in], refs[n_in:n_in + nc]
        outs, cdst = refs[n_in + nc:n_in + nc + n_out], refs[n_in + nc + n_out:n_in + 2 * nc + n_out]
        rest = refs[n_in + 2 * nc + n_out:]
        scr, sems = rest[:n_scr], rest[n_scr:]
        first = functools.reduce(jnp.logical_and, [pl.program_id(d) == 0 for d in range(rank)])
        last = functools.reduce(jnp.logical_and, [pl.program_id(d) == grid[d] - 1 for d in range(rank)])

        @pl.when(first)
        def _():
            mine, copies = _exchange_copies(kind, csrc, cdst, *sems)
            for cp in mine + copies:
                cp.start()

        body(*ins, *outs, *scr)

        @pl.when(last)
        def _():
            mine, copies = _exchange_copies(kind, csrc, cdst, *sems)
            for cp in copies:
                cp.wait_recv()
            for cp in copies:
                cp.wait_send()
            for cp in mine:
                cp.wait()

    hbm = pl.BlockSpec(memory_space=pl.ANY)
    cshape = [jax.ShapeDtypeStruct(a.shape if kind == "a2a" else (NDEV,) + a.shape, a.dtype) for a in arrs]
    res = _pcall(wrapped, name, grid=grid, in_specs=list(in_specs) + [hbm] * nc,
                 out_specs=list(out_specs) + [hbm] * nc, out_shape=list(out_shape) + cshape,
                 scratch_shapes=list(scratch_shapes) + [pltpu.SemaphoreType.DMA((nc, 7)),
                                                        pltpu.SemaphoreType.DMA((nc, 7)),
                                                        pltpu.SemaphoreType.DMA((nc,))],
                 compiler_params=_cp(("arbitrary",) * rank, vmem_mib))(*args, *arrs)
    return res[:n_out], res[n_out:]


def _ada_fwd(c_all, w, b):
    n = w.shape[1]

    def body(c_ref, w_ref, b_ref, o_ref):
        c = c_ref[...]
        s = c * _sigmoid(c)
        o_ref[...] = jnp.dot(s, w_ref[...], preferred_element_type=F32, precision=lax.Precision.HIGHEST) + b_ref[...]

    return _pcall(body, "ada_fwd", out_shape=jax.ShapeDtypeStruct((NDEV, n), F32), compiler_params=_cp())(c_all, w, b)


def _ada_bwd(c_all_t, dmod):
    n = dmod.shape[1]

    def body(ct_ref, d_ref, o_ref):
        ct = ct_ref[...]
        s = ct * _sigmoid(ct)
        acc = s[:, 0:1] * d_ref[0:1, :]
        for b in range(1, NDEV):
            acc = acc + s[:, b:b + 1] * d_ref[b:b + 1, :]
        o_ref[...] = acc

    return _pcall(body, "ada_bwd", out_shape=jax.ShapeDtypeStruct((D, n), F32), compiler_params=_cp())(c_all_t, dmod)


def _ffn_in(x, g_pre, shift, scale, w_in, name, comm=None, tm=512):
    T = x.shape[0]

    def body(x_ref, g_ref, sh_ref, sc_ref, w_ref, h_ref, gu_ref, a_ref):
        xv = x_ref[...]
        r = lax.rsqrt(jnp.mean(xv * xv, axis=-1, keepdims=True) + RMS_EPS)
        hv = ((xv * r) * g_ref[...] * (1.0 + sc_ref[...]) + sh_ref[...]).astype(BF16)
        h_ref[...] = hv
        for j in range(NSL):
            g = _dot_nt(hv, w_ref[j])
            u = _dot_nt(hv, w_ref[j + NSL])
            gu_ref[j, 0] = g.astype(BF16)
            gu_ref[j, 1] = u.astype(BF16)
            a_ref[j] = (g * _sigmoid(g) * u).astype(BF16)

    row = pl.BlockSpec((tm, D), lambda i: (i, 0))
    vec = pl.BlockSpec((1, D), lambda i: (0, 0))
    (h, gu, a), got = _hosted_call(
        body, name, comm, (T // tm,),
        [row, vec, vec, vec, pl.BlockSpec((NDEV, SL, D), lambda i: (0, 0, 0), pipeline_mode=pl.Buffered(1))],
        [row, pl.BlockSpec((NSL, 2, tm, SL), lambda i: (0, 0, i, 0)), pl.BlockSpec((NSL, tm, SL), lambda i: (0, i, 0))],
        [jax.ShapeDtypeStruct((T, D), BF16), jax.ShapeDtypeStruct((NSL, 2, T, SL), BF16),
         jax.ShapeDtypeStruct((NSL, T, SL), BF16)],
        [], ("parallel",), 48, (x, g_pre, shift, scale, w_in))
    return h, gu, a, got


def _mm_post(a_list, a_specs, w_list, w_specs, x, g_post, gate, res_w, name, tm, tgt=None):
    T = x.shape[0]
    n = len(a_list)
    with_loss = tgt is not None

    def body(*refs):
        a_refs, w_refs = refs[:n], refs[n:2 * n]
        x_ref, g_ref, gt_ref = refs[2 * n:2 * n + 3]
        rest = refs[2 * n + 3:]
        f = None
        for a_ref, w_ref in zip(a_refs, w_refs):
            if len(a_ref.shape) == 3:
                terms = [_dot(a_ref[j], w_ref[j]) for j in range(a_ref.shape[0])]
            else:
                terms = [_dot(a_ref[...], w_ref[...])]
            for t in terms:
                f = t if f is None else f + t
        r = lax.rsqrt(jnp.mean(f * f, axis=-1, keepdims=True) + RMS_EPS)
        y = (f * r) * g_ref[...]
        out = x_ref[...] + (res_w * (1.0 + gt_ref[...])) * y
        if with_loss:
            t_ref, f_ref, dy_ref, l_ref = rest

            @pl.when(pl.program_id(0) == 0)
            def _():
                l_ref[...] = jnp.zeros_like(l_ref)

            e = out - t_ref[...]
            dy_ref[...] = e * (1.0 / D)
            l_ref[...] += 0.5 * jnp.sum(jnp.mean(e * e, axis=-1, keepdims=True), axis=0, keepdims=True)
        else:
            f_ref, o_ref = rest
            o_ref[...] = out
        f_ref[...] = f

    row = pl.BlockSpec((tm, D), lambda i: (i, 0))
    vec = pl.BlockSpec((1, D), lambda i: (0, 0))
    big = jax.ShapeDtypeStruct((T, D), F32)
    if with_loss:
        return _pcall(body, name, grid=(T // tm,),
                      in_specs=list(a_specs) + list(w_specs) + [row, vec, vec, row],
                      out_specs=[row, row, pl.BlockSpec((1, 1), lambda i: (0, 0))],
                      out_shape=[big, big, jax.ShapeDtypeStruct((1, 1), F32)],
                      compiler_params=_cp(("arbitrary",)))(*a_list, *w_list, x, g_post, gate, tgt)
    return _pcall(body, name, grid=(T // tm,),
                  in_specs=list(a_specs) + list(w_specs) + [row, vec, vec], out_specs=[row, row],
                  out_shape=[big, big], compiler_params=_cp(("parallel",)))(*a_list, *w_list, x, g_post, gate)


def _ffn_out_bwd(dout, f, g_post, gate, res_w, w_out4, gu, name, comm=None, tm=512):
    T = f.shape[0]

    def body(do_ref, f_ref, g_ref, gt_ref, w_ref, gu_ref, df_ref, dgate_ref, dg_ref, dgu_ref):
        @pl.when(pl.program_id(0) == 0)
        def _():
            dgate_ref[...] = jnp.zeros_like(dgate_ref)
            dg_ref[...] = jnp.zeros_like(dg_ref)

        do = do_ref[...]
        f = f_ref[...]
        r = lax.rsqrt(jnp.mean(f * f, axis=-1, keepdims=True) + RMS_EPS)
        fn = f * r
        dgate_ref[...] += jnp.sum((res_w * do) * (fn * g_ref[...]), axis=0, keepdims=True)
        dy = (res_w * (1.0 + gt_ref[...])) * do
        dg_ref[...] += jnp.sum(dy * fn, axis=0, keepdims=True)
        dyg = dy * g_ref[...]
        dfv = (r * (dyg - fn * jnp.mean(dyg * fn, axis=-1, keepdims=True))).astype(BF16)
        df_ref[...] = dfv
        for j in range(NSL):
            da = _dot_nt(dfv, w_ref[j])
            gv = gu_ref[j, 0].astype(F32)
            s = _sigmoid(gv)
            gs = gv * s
            dgu_ref[j, 0] = (da * gu_ref[j, 1].astype(F32) * (s + gs * (1.0 - s))).astype(BF16)
            dgu_ref[j, 1] = (da * gs).astype(BF16)

    row = pl.BlockSpec((tm, D), lambda i: (i, 0))
    vec = pl.BlockSpec((1, D), lambda i: (0, 0))
    gus = pl.BlockSpec((NSL, 2, tm, SL), lambda i: (0, 0, i, 0))
    return _hosted_call(
        body, name, comm, (T // tm,),
        [row, row, vec, vec, pl.BlockSpec((NSL, SL, D), lambda i: (0, 0, 0), pipeline_mode=pl.Buffered(1)), gus],
        [row, vec, vec, gus],
        [jax.ShapeDtypeStruct((T, D), BF16), jax.ShapeDtypeStruct((1, D), F32), jax.ShapeDtypeStruct((1, D), F32),
         jax.ShapeDtypeStruct((NSL, 2, T, SL), BF16)], [], ("arbitrary",), 48, (dout, f, g_post, gate, w_out4, gu))


def _mm_tn(a, a_spec, b, b_spec, out_shape, out_spec, grid, name, comm=None):
    k, nn = out_spec.block_shape[-2:]
    steps = grid[1]

    def body(a_ref, b_ref, o_ref, acc_ref):
        i = pl.program_id(1)

        @pl.when(i == 0)
        def _():
            acc_ref[...] = jnp.zeros_like(acc_ref)

        acc_ref[...] += _dot_tn(_mat(a_ref), _mat(b_ref))

        @pl.when(i == steps - 1)
        def _():
            lead = len(o_ref.shape) - 2
            o_ref[(0,) * lead if lead else ...] = acc_ref[...].astype(BF16)

    (out,), got = _hosted_call(body, name, comm, grid, [a_spec, b_spec], [out_spec],
                               [jax.ShapeDtypeStruct(out_shape, BF16)], [pltpu.VMEM((k, nn), F32)],
                               ("parallel", "arbitrary"), 48, (a, b))
    return out, got


def _dw_in(h, dgu, name, tmw, comm=None):
    T = h.shape[0]
    steps = T // tmw

    def body(h_ref, b_ref, o_ref, acc_ref):
        i = pl.program_id(1)

        @pl.when(i == 0)
        def _():
            acc_ref[...] = jnp.zeros_like(acc_ref)

        hv = h_ref[...]
        for p in range(2):
            acc_ref[p] += _dot_tn(b_ref[0, p], hv)

        @pl.when(i == steps - 1)
        def _():
            o_ref[:, 0] = acc_ref[...].astype(BF16)

    (out,), got = _hosted_call(
        body, name, comm, (NSL, steps),
        [pl.BlockSpec((tmw, D), lambda j, i: (i, 0)), pl.BlockSpec((1, 2, tmw, SL), lambda j, i: (j, 0, i, 0))],
        [pl.BlockSpec((2, 1, SL, D), lambda j, i: (0, j, 0, 0))],
        [jax.ShapeDtypeStruct((2, NSL, SL, D), BF16)], [pltpu.VMEM((2, SL, D), F32)],
        ("parallel", "arbitrary"), 56, (h, dgu))
    return out, got


def _mm_prebwd(a, a_spec, w, w_spec, dh_fn, x, dout, g_pre, scale, name, tm=256, comm=None):
    T = x.shape[0]

    def body(a_ref, w_ref, x_ref, do_ref, g_ref, sc_ref, dx_ref, dsh_ref, dsc_ref, dg_ref):
        @pl.when(pl.program_id(0) == 0)
        def _():
            dsh_ref[...] = jnp.zeros_like(dsh_ref)
            dsc_ref[...] = jnp.zeros_like(dsc_ref)
            dg_ref[...] = jnp.zeros_like(dg_ref)

        dh = dh_fn(a_ref, w_ref)
        xv = x_ref[...]
        r = lax.rsqrt(jnp.mean(xv * xv, axis=-1, keepdims=True) + RMS_EPS)
        xn = xv * r
        dsh_ref[...] += jnp.sum(dh, axis=0, keepdims=True)
        dsc_ref[...] += jnp.sum(dh * (xn * g_ref[...]), axis=0, keepdims=True)
        dn = dh * (1.0 + sc_ref[...])
        dg_ref[...] += jnp.sum(dn * xn, axis=0, keepdims=True)
        dng = dn * g_ref[...]
        dx_ref[...] = do_ref[...] + r * (dng - xn * jnp.mean(dng * xn, axis=-1, keepdims=True))

    row = pl.BlockSpec((tm, D), lambda i: (i, 0))
    vec = pl.BlockSpec((1, D), lambda i: (0, 0))
    return _hosted_call(body, name, comm, (T // tm,), [a_spec, w_spec, row, row, vec, vec], [row, vec, vec, vec],
                        [jax.ShapeDtypeStruct((T, D), F32)] + [jax.ShapeDtypeStruct((1, D), F32)] * 3,
                        [], ("arbitrary",), 56, (a, w, x, dout, g_pre, scale))


def _mix_in(x, g_pre, shift, scale, w, name, tm=512):
    T = x.shape[0]

    def body(x_ref, g_ref, sh_ref, sc_ref, w_ref, h_ref, qkv_ref, cvg_ref):
        xv = x_ref[...]
        r = lax.rsqrt(jnp.mean(xv * xv, axis=-1, keepdims=True) + RMS_EPS)
        hv = ((xv * r) * g_ref[...] * (1.0 + sc_ref[...]) + sh_ref[...]).astype(BF16)
        h_ref[...] = hv
        p = _dot_nt(hv, w_ref[...])
        qkv_ref[...] = p[:, :3 * AW].astype(BF16)
        cvg_ref[...] = p[:, 3 * AW:]

    row = pl.BlockSpec((tm, D), lambda i: (i, 0))
    vec = pl.BlockSpec((1, D), lambda i: (0, 0))
    return _pcall(body, name, grid=(T // tm,),
                  in_specs=[row, vec, vec, vec, pl.BlockSpec((MIXIN, D), lambda i: (0, 0))],
                  out_specs=[row, pl.BlockSpec((tm, 3 * AW), lambda i: (i, 0)), pl.BlockSpec((tm, 2 * CW), lambda i: (i, 0))],
                  out_shape=[jax.ShapeDtypeStruct((T, D), BF16), jax.ShapeDtypeStruct((T, 3 * AW), BF16),
                             jax.ShapeDtypeStruct((T, 2 * CW), F32)],
                  compiler_params=_cp(("parallel",)))(x, g_pre, shift, scale, w)


def _mix_out_bwd(dout, f, g_post, gate, res_w, w, name, tm=512):
    T = f.shape[0]
    N = w.shape[0]

    def body(do_ref, f_ref, g_ref, gt_ref, w_ref, df_ref, dgate_ref, dg_ref, o_ref):
        @pl.when(pl.program_id(0) == 0)
        def _():
            dgate_ref[...] = jnp.zeros_like(dgate_ref)
            dg_ref[...] = jnp.zeros_like(dg_ref)

        do = do_ref[...]
        f = f_ref[...]
        r = lax.rsqrt(jnp.mean(f * f, axis=-1, keepdims=True) + RMS_EPS)
        fn = f * r
        dgate_ref[...] += jnp.sum((res_w * do) * (fn * g_ref[...]), axis=0, keepdims=True)
        dy = (res_w * (1.0 + gt_ref[...])) * do
        dg_ref[...] += jnp.sum(dy * fn, axis=0, keepdims=True)
        dyg = dy * g_ref[...]
        dfv = (r * (dyg - fn * jnp.mean(dyg * fn, axis=-1, keepdims=True))).astype(BF16)
        df_ref[...] = dfv
        o_ref[...] = _dot_nt(dfv, w_ref[...])

    row = pl.BlockSpec((tm, D), lambda i: (i, 0))
    vec = pl.BlockSpec((1, D), lambda i: (0, 0))
    return _pcall(body, name, grid=(T // tm,),
                  in_specs=[row, row, vec, vec, pl.BlockSpec((N, D), lambda i: (0, 0))],
                  out_specs=[row, vec, vec, pl.BlockSpec((tm, N), lambda i: (i, 0))],
                  out_shape=[jax.ShapeDtypeStruct((T, D), BF16), jax.ShapeDtypeStruct((1, D), F32),
                             jax.ShapeDtypeStruct((1, D), F32), jax.ShapeDtypeStruct((T, N), F32)],
                  compiler_params=_cp(("arbitrary",)))(dout, f, g_post, gate, w)


def _softplus_parts(z):
    ls = jnp.minimum(z, 0.0) - jnp.log(1.0 + jnp.exp(-jnp.abs(z)))
    return ls, ls - z


def _head_sum(x, first):
    sa = jnp.sum(jnp.where(first, x, 0.0), axis=-1, keepdims=True)
    sb = jnp.sum(jnp.where(first, 0.0, x), axis=-1, keepdims=True)
    return jnp.where(first, sa, sb)


def _attn_specs(T, tq):
    qs = pl.BlockSpec((tq, 128), lambda p, i: (i, p))
    ks = pl.BlockSpec((T, 128), lambda p, i: (0, 4 + p))
    vs = pl.BlockSpec((T, 128), lambda p, i: (0, 8 + p))
    gs = pl.BlockSpec((1, 128), lambda p, i: (0, p))
    return qs, ks, vs, gs


def _attn_fwd(qkv, g_attn, tq, comm=None):
    T = qkv.shape[0]

    def body(q_ref, k_ref, v_ref, g_ref, o_ref, an_ref):
        i = pl.program_id(1)
        first = lax.broadcasted_iota(jnp.int32, (tq, 128), 1) < HD
        q = (q_ref[...].astype(F32) * QK_SCALE).astype(BF16)
        zq = jnp.zeros_like(q)
        qs = (jnp.where(first, q, zq), jnp.where(first, zq, q))
        rows = lax.broadcasted_iota(jnp.int32, (tq, tq), 0)
        cols = lax.broadcasted_iota(jnp.int32, (tq, tq), 1)
        tri = (rows > cols).astype(BF16)
        tri2 = jnp.concatenate([tri, tri], axis=0)
        strict = cols < rows

        def tile(j, Rs, acc, masked):
            start = j * tq if isinstance(j, int) else pl.multiple_of(j * tq, tq)
            kb = k_ref[pl.ds(start, tq), :]
            vb = v_ref[pl.ds(start, tq), :]
            zs = [_dot_nt(qs[hh], kb) for hh in range(2)]
            parts = [_softplus_parts(z) for z in zs]
            lsms = [jnp.where(strict, p[1], 0.0) if masked else p[1] for p in parts]
            splits = [_split2(x) for x in lsms]
            afters = [_dot(jnp.concatenate(s, axis=1), tri2) for s in splits]
            ws = [jnp.exp(parts[hh][0] + afters[hh] + Rs[hh]) for hh in range(2)]
            if masked:
                ws = [jnp.where(strict, w, 0.0) for w in ws]
            outs = [_dot(w.astype(BF16), vb) for w in ws]
            new_r = [Rs[hh] + afters[hh][:, 0:1] + lsms[hh][:, 0:1] for hh in range(2)]
            return new_r[0], new_r[1], acc + jnp.where(first, outs[0], outs[1])

        zr = jnp.zeros((tq, 1), F32)

        def finish(acc):
            o_ref[...] = acc
            r = lax.rsqrt(_head_sum(acc * acc, first) * (1.0 / HD) + RMS_EPS)
            an_ref[...] = ((acc * r) * g_ref[...]).astype(BF16)

        @pl.when(i == 0)
        def _():
            finish(tile(0, (zr, zr), jnp.zeros((tq, 128), F32), True)[2])

        @pl.when(i > 0)
        def _():
            ra, rb, acc = tile(i, (zr, zr), jnp.zeros((tq, 128), F32), True)
            ra, rb, acc = tile(i - 1, (ra, rb), acc, False)

            def more(c):
                return jnp.logical_and(c[0] < i, jnp.maximum(jnp.max(c[1]), jnp.max(c[2])) > W_ZERO_BELOW)

            def step(c):
                ra, rb, acc = tile(i - 1 - c[0], (c[1], c[2]), c[3], False)
                return c[0] + 1, ra, rb, acc

            finish(lax.while_loop(more, step, (jnp.int32(1), ra, rb, acc))[3])

    qs, ks, vs, gs = _attn_specs(T, tq)
    (o, an), got = _hosted_call(body, "attn_fwd", comm, (AW // 128, T // tq), [qs, ks, vs, gs], [qs, qs],
                                [jax.ShapeDtypeStruct((T, AW), F32), jax.ShapeDtypeStruct((T, AW), BF16)],
                                [], ("parallel", "parallel"), 48, (qkv, qkv, qkv, g_attn))
    return o, an, got


def _attn_bwd(qkv, o, dcat, g_attn, tq, comm=None):
    T = qkv.shape[0]

    def body(q_ref, k_ref, v_ref, o_ref, dan_ref, g_ref, dq_ref, dk_ref, dv_ref, dg_ref):
        i = pl.program_id(1)

        @pl.when(i == 0)
        def _():
            dk_ref[...] = jnp.zeros_like(dk_ref)
            dv_ref[...] = jnp.zeros_like(dv_ref)
            dg_ref[...] = jnp.zeros_like(dg_ref)

        first = lax.broadcasted_iota(jnp.int32, (tq, 128), 1) < HD
        q = (q_ref[...].astype(F32) * QK_SCALE).astype(BF16)
        zq = jnp.zeros_like(q)
        qs = (jnp.where(first, q, zq), jnp.where(first, zq, q))
        o = o_ref[...]
        dan = dan_ref[...]
        r = lax.rsqrt(_head_sum(o * o, first) * (1.0 / HD) + RMS_EPS)
        on = o * r
        dg_ref[...] += jnp.sum(dan * on, axis=0, keepdims=True)
        dyg = dan * g_ref[...]
        dO = r * (dyg - on * (_head_sum(dyg * on, first) * (1.0 / HD)))
        dOb = dO.astype(BF16)
        dOs = (jnp.where(first, dOb, zq), jnp.where(first, zq, dOb))
        ones = jnp.ones((8, 128), BF16)
        Ds = []
        for hh in range(2):
            prod = dOs[hh].astype(F32) * o
            p1 = prod.astype(BF16)
            rem = prod - p1.astype(F32)
            p2 = rem.astype(BF16)
            p3 = (rem - p2.astype(F32)).astype(BF16)
            Ds.append((_dot_nt(ones, p1) + _dot_nt(ones, p2) + _dot_nt(ones, p3))[0:1, :])

        rows = lax.broadcasted_iota(jnp.int32, (tq, tq), 0)
        cols = lax.broadcasted_iota(jnp.int32, (tq, tq), 1)
        tri_after = (cols > rows).astype(BF16)
        tri_incl = (cols >= rows).astype(BF16)
        tri_after2 = jnp.concatenate([tri_after, tri_after], axis=1)
        tri_incl2 = jnp.concatenate([tri_incl, tri_incl], axis=1)
        strict = rows < cols

        def tile(j, Rs, Gs, dq, masked):
            start = j * tq if isinstance(j, int) else pl.multiple_of(j * tq, tq)
            kb = k_ref[pl.ds(start, tq), :]
            vb = v_ref[pl.ds(start, tq), :]
            H = range(2)
            parts = [_softplus_parts(_dot_nt(kb, qs[hh])) for hh in H]
            lsms = [jnp.where(strict, p[1], 0.0) if masked else p[1] for p in parts]
            splits = [_split2(x) for x in lsms]
            afters = [_dot(tri_after2, jnp.concatenate(s, axis=0)) for s in splits]
            ws = [jnp.exp(parts[hh][0] + afters[hh] + Rs[hh]) for hh in H]
            if masked:
                ws = [jnp.where(strict, w, 0.0) for w in ws]
            wbs = [w.astype(BF16) for w in ws]
            dlws = [_dot_nt(vb, dOs[hh]) * wbs[hh].astype(F32) for hh in H]
            splits2 = [_split2(x) for x in dlws]
            Cs = [_dot(tri_incl2, jnp.concatenate(s, axis=0)) for s in splits2]
            dlsms = [Ds[hh] - Gs[hh] - Cs[hh] for hh in H]
            if masked:
                dlsms = [jnp.where(strict, x, 0.0) for x in dlsms]
            ps = [jnp.exp(p[0]) for p in parts]
            dzs = [(dlws[hh] * (1.0 - ps[hh]) - dlsms[hh] * ps[hh]).astype(BF16) for hh in H]
            dkp = _dot(dzs[0], qs[0]) + _dot(dzs[1], qs[1])
            dvp = _dot(wbs[0], dOs[0]) + _dot(wbs[1], dOs[1])
            dq = dq + jnp.where(first, _dot_tn(dzs[0], kb), _dot_tn(dzs[1], kb))
            new_r = [Rs[hh] + afters[hh][0:1, :] + lsms[hh][0:1, :] for hh in H]
            new_g = [Gs[hh] + Cs[hh][0:1, :] for hh in H]
            dk_ref[pl.ds(start, tq), :] += dkp
            dv_ref[pl.ds(start, tq), :] += dvp
            return new_r[0], new_r[1], new_g[0], new_g[1], dq

        zrow = jnp.zeros((1, tq), F32)

        @pl.when(i == 0)
        def _():
            dq0 = tile(0, (zrow, zrow), (zrow, zrow), jnp.zeros((tq, 128), F32), True)[4]
            dq_ref[...] = (dq0 * QK_SCALE).astype(BF16)

        @pl.when(i > 0)
        def _():
            st = tile(i, (zrow, zrow), (zrow, zrow), jnp.zeros((tq, 128), F32), True)
            st = tile(i - 1, st[0:2], st[2:4], st[4], False)

            def more(c):
                return jnp.logical_and(c[0] < i, jnp.maximum(jnp.max(c[1]), jnp.max(c[2])) > W_ZERO_BELOW)

            def step(c):
                return (c[0] + 1,) + tile(i - 1 - c[0], (c[1], c[2]), (c[3], c[4]), c[5], False)

            dq_ref[...] = (lax.while_loop(more, step, (jnp.int32(1),) + st)[5] * QK_SCALE).astype(BF16)

    qs, ks, vs, gs = _attn_specs(T, tq)
    kacc = pl.BlockSpec((T, 128), lambda p, i: (0, p))
    return _hosted_call(body, "attn_bwd", comm, (AW // 128, T // tq), [qs, ks, vs, qs, qs, gs], [qs, kacc, kacc, gs],
                        [jax.ShapeDtypeStruct((T, AW), BF16), jax.ShapeDtypeStruct((T, AW), F32),
                         jax.ShapeDtypeStruct((T, AW), F32), jax.ShapeDtypeStruct((1, AW), F32)],
                        [], ("parallel", "arbitrary"), 48, (qkv, qkv, qkv, o, dcat, g_attn))


def _taps_by_phase(offsets):
    groups = {}
    for k, off in enumerate(offsets):
        groups.setdefault(off % 8, []).append((k, off // 8))
    return sorted(groups.items())


def _shifted_tap_sum(w_ref, pad_ref, ph_ref, offsets, tb):
    acc = None
    for p, taps in _taps_by_phase(offsets):
        n = tb if p == 0 else tb + 8
        a = None
        for k, m in taps:
            t = w_ref[k:k + 1, :] * pad_ref[8 * m:8 * m + n, :]
            a = t if a is None else a + t
        if p:
            ph_ref[...] = a
            a = ph_ref[p:p + tb, :]
        acc = a if acc is None else acc + a
    return acc


def _conv_fwd(cvg, conv_w, conv_b, ln_g, ln_b, tb=512):
    T = cvg.shape[0]
    hb = tb // HALO

    def body(cv_ref, cg_ref, cvp_ref, cgp_ref, w_ref, b_ref, g_ref, be_ref, u0_ref, u1_ref, u3_ref, pad_ref, ph_ref):
        i = pl.program_id(0)
        u0 = cv_ref[...] * _sigmoid(cg_ref[...])
        prev = cvp_ref[...] * _sigmoid(cgp_ref[...])
        pad_ref[0:HALO, :] = jnp.where(i > 0, prev, 0.0)
        pad_ref[HALO:HALO + tb, :] = u0
        u0_ref[...] = u0
        acc = _shifted_tap_sum(w_ref, pad_ref, ph_ref, [HALO - (CK - 1) + kk for kk in range(CK)], tb) + b_ref[...]
        u1_ref[...] = acc
        mu = jnp.mean(acc, axis=-1, keepdims=True)
        xc = acc - mu
        var = jnp.mean(xc * xc, axis=-1, keepdims=True)
        u2 = (xc * lax.rsqrt(var + LN_EPS)) * g_ref[...] + be_ref[...]
        u3_ref[...] = (u2 * _sigmoid(u2)).astype(BF16)

    cur = lambda col: pl.BlockSpec((tb, CW), lambda i: (i, col))
    prv = lambda col: pl.BlockSpec((HALO, CW), lambda i: (jnp.maximum(i * hb - 1, 0), col))
    vec = pl.BlockSpec((1, CW), lambda i: (0, 0))
    out = pl.BlockSpec((tb, CW), lambda i: (i, 0))
    return _pcall(body, "conv_fwd", grid=(T // tb,),
                  in_specs=[cur(0), cur(1), prv(0), prv(1), pl.BlockSpec((HALO, CW), lambda i: (0, 0)), vec, vec, vec],
                  out_specs=[out, out, out],
                  out_shape=[jax.ShapeDtypeStruct((T, CW), F32), jax.ShapeDtypeStruct((T, CW), F32),
                             jax.ShapeDtypeStruct((T, CW), BF16)],
                  scratch_shapes=[pltpu.VMEM((tb + HALO, CW), F32), pltpu.VMEM((tb + 8, CW), F32)],
                  compiler_params=_cp(("parallel",)))(cvg, cvg, cvg, cvg, conv_w, conv_b, ln_g, ln_b)


def _conv_bwd1(dcat, u1, u0, ln_g, ln_b, tb=512):
    T = u1.shape[0]
    hb = tb // HALO

    def body(d3_ref, u1_ref, u0_ref, u0p_ref, g_ref, be_ref, du1_ref, dw_ref, db_ref, dlg_ref, dlb_ref, pad_ref, d_ref,
             q_ref):
        i = pl.program_id(0)

        @pl.when(i == 0)
        def _():
            dw_ref[...] = jnp.zeros_like(dw_ref)
            db_ref[...] = jnp.zeros_like(db_ref)
            dlg_ref[...] = jnp.zeros_like(dlg_ref)
            dlb_ref[...] = jnp.zeros_like(dlb_ref)

        u1 = u1_ref[...]
        mu = jnp.mean(u1, axis=-1, keepdims=True)
        xc = u1 - mu
        rstd = lax.rsqrt(jnp.mean(xc * xc, axis=-1, keepdims=True) + LN_EPS)
        xh = xc * rstd
        u2 = xh * g_ref[...] + be_ref[...]
        s = _sigmoid(u2)
        du2 = d3_ref[...] * (s + u2 * s * (1.0 - s))
        dlg_ref[...] += jnp.sum(du2 * xh, axis=0, keepdims=True)
        dlb_ref[...] += jnp.sum(du2, axis=0, keepdims=True)
        dxh = du2 * g_ref[...]
        du1 = rstd * (dxh - jnp.mean(dxh, axis=-1, keepdims=True) - xh * jnp.mean(dxh * xh, axis=-1, keepdims=True))
        du1_ref[...] = du1
        db_ref[...] += jnp.sum(du1, axis=0, keepdims=True)
        pad_ref[0:HALO, :] = jnp.where(i > 0, u0p_ref[...], 0.0)
        pad_ref[HALO:HALO + tb, :] = u0_ref[...]
        d_ref[0:8, :] = jnp.zeros((8, CW), F32)
        d_ref[8:8 + tb, :] = du1
        d_ref[8 + tb:16 + tb, :] = jnp.zeros((8, CW), F32)
        for p, taps in _taps_by_phase([HALO - (CK - 1) + kk for kk in range(CK)]):
            n = tb + 8
            q_ref[...] = d_ref[8 - p:8 - p + n, :]
            for k, m in taps:
                if 8 * m + n <= tb + HALO:
                    dw_ref[k:k + 1, :] += jnp.sum(q_ref[...] * pad_ref[8 * m:8 * m + n, :], axis=0, keepdims=True)
                else:
                    dw_ref[k:k + 1, :] += jnp.sum(q_ref[0:tb, :] * pad_ref[8 * m:8 * m + tb, :], axis=0, keepdims=True)

    cur = pl.BlockSpec((tb, CW), lambda i: (i, 0))
    vec = pl.BlockSpec((1, CW), lambda i: (0, 0))
    return _pcall(body, "conv_bwd1", grid=(T // tb,),
                  in_specs=[pl.BlockSpec((tb, CW), lambda i: (i, 1)), cur, cur,
                            pl.BlockSpec((HALO, CW), lambda i: (jnp.maximum(i * hb - 1, 0), 0)), vec, vec],
                  out_specs=[cur, pl.BlockSpec((HALO, CW), lambda i: (0, 0)), vec, vec, vec],
                  out_shape=[jax.ShapeDtypeStruct((T, CW), F32), jax.ShapeDtypeStruct((HALO, CW), F32)]
                  + [jax.ShapeDtypeStruct((1, CW), F32)] * 3,
                  scratch_shapes=[pltpu.VMEM((tb + HALO, CW), F32), pltpu.VMEM((tb + 16, CW), F32),
                                  pltpu.VMEM((tb + 8, CW), F32)],
                  compiler_params=_cp(("arbitrary",)))(dcat, u1, u0, u0, ln_g, ln_b)


def _conv_bwd2(du1, cvg, conv_w, tb=512):
    T = du1.shape[0]
    hb = tb // HALO
    last = T // HALO - 1
    nblk = T // tb

    def body(d_ref, dn_ref, cv_ref, cg_ref, w_ref, o_ref, pad_ref, ph_ref):
        i = pl.program_id(0)
        pad_ref[0:tb, :] = d_ref[...]
        pad_ref[tb:tb + HALO, :] = jnp.where(i < nblk - 1, dn_ref[...], 0.0)
        acc = _shifted_tap_sum(w_ref, pad_ref, ph_ref, [CK - 1 - kk for kk in range(CK)], tb)
        sg = _sigmoid(cg_ref[...])
        o_ref[:, 0:CW] = (acc * sg).astype(BF16)
        o_ref[:, CW:2 * CW] = (acc * cv_ref[...] * sg * (1.0 - sg)).astype(BF16)

    cur = pl.BlockSpec((tb, CW), lambda i: (i, 0))
    return _pcall(body, "conv_bwd2", grid=(nblk,),
                  in_specs=[cur, pl.BlockSpec((HALO, CW), lambda i: (jnp.minimum((i + 1) * hb, last), 0)),
                            pl.BlockSpec((tb, CW), lambda i: (i, 0)), pl.BlockSpec((tb, CW), lambda i: (i, 1)),
                            pl.BlockSpec((HALO, CW), lambda i: (0, 0))],
                  out_specs=pl.BlockSpec((tb, 2 * CW), lambda i: (i, 0)),
                  out_shape=jax.ShapeDtypeStruct((T, 2 * CW), BF16),
                  scratch_shapes=[pltpu.VMEM((tb + HALO, CW), F32), pltpu.VMEM((tb + 8, CW), F32)],
                  compiler_params=_cp(("parallel",)))(du1, du1, cvg, cvg, conv_w)


def _adam_math(w, g, m, v):
    nm = ADAM_B1 * m + (1.0 - ADAM_B1) * g
    nv = ADAM_B2 * v + (1.0 - ADAM_B2) * (g * g)
    delta = -ADAM_LR * ((nm * ADAM_C1) / (jnp.sqrt(nv * ADAM_C2) + ADAM_EPS) + ADAM_WD * w)
    return delta, nm, nv


def _adamw(w, gslots, m, v, name, tb):
    R, C = w.shape
    S = gslots.shape[0]

    def body(w_ref, gs_ref, m_ref, v_ref, g_ref, d_ref, nm_ref, nv_ref):
        g = gs_ref[0].astype(F32)
        for s in range(1, S):
            g = g + gs_ref[s].astype(F32)
        g_ref[...] = g
        d_ref[...], nm_ref[...], nv_ref[...] = _adam_math(w_ref[...], g, m_ref[...], v_ref[...])

    blk = pl.BlockSpec((tb, C), lambda i: (i, 0))
    return _pcall(body, name, grid=(R // tb,),
                  in_specs=[blk, pl.BlockSpec((S, tb, C), lambda i: (0, i, 0)), blk, blk],
                  out_specs=[blk] * 4, out_shape=[jax.ShapeDtypeStruct((R, C), F32)] * 4,
                  compiler_params=_cp(("parallel",)))(w, gslots, m, v)


def _adamw_small(gall, gattn, gconvw, ws, ms, vs):
    n = len(ws)

    def body(*refs):
        gall_ref, gattn_ref, gconvw_ref = refs[:3]
        w_refs, m_refs, v_refs = refs[3:3 + n], refs[3 + n:3 + 2 * n], refs[3 + 2 * n:3 + 3 * n]
        loss_ref = refs[3 + 3 * n]
        outs = refs[4 + 3 * n:]
        g_refs, d_refs, nm_refs, nv_refs = outs[:n], outs[n:2 * n], outs[2 * n:3 * n], outs[3 * n:]

        def total(ref):
            t = ref[0]
            for dev in range(1, NDEV):
                t = t + ref[dev]
            return t

        tot = total(gall_ref)
        grads = [tot[0:9, :]] + [tot[ROW_GAINS + k:ROW_GAINS + k + 1, :] for k in range(6)]
        grads += [total(gattn_ref), tot[ROW_ATTN_CB:ROW_ATTN_CB + 1, CW:2 * CW], tot[ROW_LN:ROW_LN + 1, 0:CW],
                  tot[ROW_LN:ROW_LN + 1, CW:2 * CW], total(gconvw_ref)]
        loss_ref[...] = tot[ROW_LOSS:ROW_LOSS + 1, 0:1]
        for k in range(n):
            g_refs[k][...] = grads[k]
            d_refs[k][...], nm_refs[k][...], nv_refs[k][...] = _adam_math(w_refs[k][...], grads[k], m_refs[k][...],
                                                                          v_refs[k][...])

    shapes = [jax.ShapeDtypeStruct(w.shape, F32) for w in ws]
    res = _pcall(body, "adamw_small", out_shape=[jax.ShapeDtypeStruct((1, 1), F32)] + shapes * 4,
                 compiler_params=_cp())(gall, gattn, gconvw, *ws, *ms, *vs)
    return res[0], [res[1 + k * n:1 + (k + 1) * n] for k in range(4)]


def _ffn_fwd(x, g_pre, g_post, shift, scale, gate, w_in, w_out4, tag, tm, comm=None, tgt=None):
    h, gu, a, got = _ffn_in(x, g_pre, shift, scale, w_in, "ffn_in_" + tag, comm)
    res = _mm_post([a], [pl.BlockSpec((NSL, tm, SL), lambda i: (0, i, 0))],
                   [w_out4], [pl.BlockSpec((NSL, SL, D), lambda i: (0, 0, 0))],
                   x, g_post, gate, 0.5, "ffn_out_" + tag, tm, tgt)
    return (res[1] if tgt is None else (res[1], res[2])), (x, h, gu, a, res[0]), got


def _ffn_bwd(dout, saved, g_pre, g_post, scale, gate, w_in, w_out4, tag, tmb, tmw, send_in=True, carry=None):
    x, h, gu, a, f = saved
    T = x.shape[0]
    (df, dgate, dg_post, dgu), carried = _ffn_out_bwd(dout, f, g_post, gate, 0.5, w_out4, gu,
                                                      "ffn_out_bwd_" + tag, carry)
    dw_out, _ = _mm_tn(a, pl.BlockSpec((1, tmw, SL), lambda j, i: (j, i, 0)),
                       df, pl.BlockSpec((tmw, D), lambda j, i: (i, 0)),
                       (NSL, SL, D), pl.BlockSpec((1, SL, D), lambda j, i: (j, 0, 0)), (NSL, T // tmw), "dw_out_" + tag)
    dw_in, (r_out,) = _dw_in(h, dgu, "dw_in_" + tag, tmw, comm=("a2a", [dw_out.reshape(NDEV, SL // 2, D)]))
    dw_in = dw_in.reshape(NDEV, SL, D)

    def dh_fn(a_ref, w_ref):
        dh = None
        for p in range(2):
            for j in range(NSL):
                t = _dot(a_ref[j, p], w_ref[NSL * p + j])
                dh = t if dh is None else dh + t
        return dh

    (dx, dshift, dscale, dg_pre), got = _mm_prebwd(
        dgu, pl.BlockSpec((NSL, 2, tmb, SL), lambda i: (0, 0, i, 0)),
        w_in, pl.BlockSpec((NDEV, SL, D), lambda i: (0, 0, 0), pipeline_mode=pl.Buffered(1)),
        dh_fn, x, dout, g_pre, scale, "ffn_in_bwd_" + tag, tmb, comm=("a2a", [dw_in]) if send_in else None)
    return dx, got[0] if send_in else dw_in, r_out, dg_pre, dg_post, (dshift, dscale, dgate), carried


def kernel(x, c, w_ada, b_ada, g_pre_ff1, g_post_ff1, ff1_w_in, ff1_w_out, g_pre_mix, g_post_mix, w_in_mix, g_attn_out, conv_w, conv_b, conv_ln_g, conv_ln_b, w_out_mix, g_pre_ff2, g_post_ff2, ff2_w_in, ff2_w_out, loss_target, m_w_ada, m_b_ada, m_g_pre_ff1, m_g_post_ff1, m_ff1_w_in, m_ff1_w_out, m_g_pre_mix, m_g_post_mix, m_w_in_mix, m_g_attn_out, m_conv_w, m_conv_b, m_conv_ln_g, m_conv_ln_b, m_w_out_mix, m_g_pre_ff2, m_g_post_ff2, m_ff2_w_in, m_ff2_w_out, v_w_ada, v_b_ada, v_g_pre_ff1, v_g_post_ff1, v_ff1_w_in, v_ff1_w_out, v_g_pre_mix, v_g_post_mix, v_w_in_mix, v_g_attn_out, v_conv_w, v_conv_b, v_conv_ln_g, v_conv_ln_b, v_w_out_mix, v_g_pre_ff2, v_g_post_ff2, v_ff2_w_in, v_ff2_w_out):
    me = 4 * lax.axis_index("x") + 2 * lax.axis_index("y") + lax.axis_index("c")
    T = x.shape[1]
    tq = min(256, T)
    tm = 512
    tmb = 512
    tmw = 2048
    x0 = x.reshape(T, D)
    tgt = loss_target.reshape(T, D)
    row = lambda a: a.reshape(1, -1)

    small_in = jnp.concatenate([c.reshape(-1), jnp.pad(conv_w.reshape(-1), (0, 2 * D - CK * 64)),
                                jnp.zeros((5 * D,), F32)]).reshape(8, D)
    small_all, = _all_gather([small_in], "gather_c_convw", True)
    c_all = small_all[:, 0, :]
    conv_w_full = small_all[:, 1:3, :].reshape(NDEV, 2 * D)[:, :CK * 64].reshape(NDEV, CK, 64)
    conv_w_full = conv_w_full.transpose(1, 0, 2).reshape(CK, CW)
    conv_w_pad = jnp.pad(conv_w_full, ((0, HALO - CK), (0, 0)))

    big = [ff1_w_in.T, ff1_w_out, w_in_mix.T, w_out_mix, ff2_w_in.T, ff2_w_out]
    shards = [w.astype(BF16) for w in big]
    w_in1, w_out1 = _all_gather(shards[0:2], "gather_weights_ff1", False)
    w_out1_4 = w_out1.reshape(NSL, SL, D)

    b_cols = lax.dynamic_slice(b_ada, (me * ADA_COLS,), (ADA_COLS,)).reshape(1, ADA_COLS)
    mod_cols = _ada_fwd(c_all, w_ada, b_cols)
    mod_all, = _all_gather([mod_cols], "gather_mod", True)
    mod = lax.dynamic_slice(mod_all, (0, me, 0), (NDEV, 1, ADA_COLS)).reshape(9, D)
    sh = lambda s: mod[3 * s:3 * s + 1]
    sc = lambda s: mod[3 * s + 1:3 * s + 2]
    gt = lambda s: mod[3 * s + 2:3 * s + 3]

    x1, sv1, (w_inm_s, w_outm_s) = _ffn_fwd(x0, row(g_pre_ff1), row(g_post_ff1), sh(0), sc(0), gt(0), w_in1, w_out1_4,
                                            "ff1", tm, comm=("gather", shards[2:4]))
    w_inm = w_inm_s.reshape(MIXIN, D)
    w_outm = w_outm_s.reshape(D, D)
    hm, qkv, cvg = _mix_in(x1, row(g_pre_mix), sh(1), sc(1), w_inm, "mix_in")
    g_attn_row = row(g_attn_out)
    o_att, an, (w_in2, w_out2) = _attn_fwd(qkv, g_attn_row, tq, comm=("gather", shards[4:6]))
    w_out2_4 = w_out2.reshape(NSL, SL, D)
    u0, u1, u3 = _conv_fwd(cvg, conv_w_pad, row(conv_b), row(conv_ln_g), row(conv_ln_b))
    half = lambda k: pl.BlockSpec((AW, D), lambda i: (k, 0))
    act = pl.BlockSpec((tm, AW), lambda i: (i, 0))
    fm, x2 = _mm_post([an, u3], [act, act], [w_outm, w_outm], [half(0), half(1)],
                      x1, row(g_post_mix), gt(1), 1.0, "mix_out", tm)
    (dy, loss_part), sv2, _ = _ffn_fwd(x2, row(g_pre_ff2), row(g_post_ff2), sh(2), sc(2), gt(2), w_in2, w_out2_4,
                                       "ff2", tm, tgt=tgt)

    dx2, dw_in2, r_out2, dgpre2, dgpost2, dmod2, _ = _ffn_bwd(
        dy, sv2, row(g_pre_ff2), row(g_post_ff2), sc(2), gt(2), w_in2, w_out2_4, "ff2", tmb, tmw, send_in=False)

    dfm, dgate1, dgpostm, dcat = _mix_out_bwd(dx2, fm, row(g_post_mix), gt(1), 1.0, w_outm, "mix_out_bwd")
    tok = pl.BlockSpec((tmw, AW), lambda j, i: (i, 0))
    tokd = pl.BlockSpec((tmw, D), lambda j, i: (i, 0))
    whole = pl.BlockSpec((AW, D), lambda j, i: (0, 0))
    dw_outm = jnp.concatenate([_mm_tn(an, tok, dfm, tokd, (AW, D), whole, (1, T // tmw), "dw_out_mix_a")[0],
                               _mm_tn(u3, tok, dfm, tokd, (AW, D), whole, (1, T // tmw), "dw_out_mix_c")[0]], axis=0)
    (dq, dk, dv, dg_attn), (r_in2,) = _attn_bwd(qkv, o_att, dcat, g_attn_row, tq, comm=("a2a", [dw_in2]))
    du1, dconv_w, dconv_b, dln_g, dln_b = _conv_bwd1(dcat, u1, u0, row(conv_ln_g), row(conv_ln_b))
    dcvg = _conv_bwd2(du1, cvg, conv_w_pad)
    dproj = jnp.concatenate([dq, dk.astype(BF16), dv.astype(BF16), dcvg], axis=1)
    dw_inm, _ = _mm_tn(dproj, pl.BlockSpec((tmw, MIXIN // 2), lambda j, i: (i, j)),
                       hm, pl.BlockSpec((tmw, D), lambda j, i: (i, 0)),
                       (MIXIN, D), pl.BlockSpec((MIXIN // 2, D), lambda j, i: (j, 0)), (2, T // tmw), "dw_in_mix")
    (dx1, dshift1, dscale1, dgprem), _ = _mm_prebwd(
        dproj, pl.BlockSpec((tmb, MIXIN), lambda i: (i, 0)), w_inm, pl.BlockSpec((MIXIN, D), lambda i: (0, 0)),
        lambda a_ref, w_ref: _dot(a_ref[...], w_ref[...]), x1, dx2, row(g_pre_mix), sc(1), "mix_in_bwd", tmb)

    dx0, r_in1, r_out1, dgpre1, dgpost1, dmod0, (r_inm, r_outm) = _ffn_bwd(
        dx1, sv1, row(g_pre_ff1), row(g_post_ff1), sc(0), gt(0), w_in1, w_out1_4, "ff1", tmb, tmw,
        carry=("a2a", [dw_inm.reshape(NDEV, 320, D), dw_outm.reshape(NDEV, 128, D)]))
    recvs = [r_in1, r_out1, r_inm, r_outm, r_in2, r_out2]

    zrow = jnp.zeros((1, D), F32)
    small_g = jnp.concatenate(
        list(dmod0) + [dshift1, dscale1, dgate1] + list(dmod2)
        + [dgpre1, dgpost1, dgprem, dgpostm, dgpre2, dgpost2]
        + [jnp.concatenate([dg_attn, dconv_b], axis=1), jnp.concatenate([dln_g, dln_b], axis=1),
           jnp.pad(dconv_w[:CK].reshape(-1), (0, CONVW_ROWS * D - CK * CW)).reshape(CONVW_ROWS, D),
           jnp.pad(loss_part, ((0, 0), (0, D - 1)))] + [zrow] * (SMALL_R - ROW_LOSS - 1), axis=0)
    small_g_all, = _all_gather([small_g], "gather_small_grads", True)

    dmod_all = small_g_all[:, 0:9, :].reshape(NDEV, NMOD)
    dmod_cols = lax.dynamic_slice(dmod_all, (0, me * ADA_COLS), (NDEV, ADA_COLS))
    g_w_ada = _ada_bwd(c_all.T, dmod_cols)

    gattn = small_g_all[:, ROW_ATTN_CB, 0:AW].reshape(NDEV, 8, HD)
    gconvw = small_g_all[:, ROW_CONVW:ROW_CONVW + CONVW_ROWS, :].reshape(NDEV, CONVW_ROWS * D)[:, :CK * CW]
    gconvw = lax.dynamic_slice(gconvw.reshape(NDEV, CK, CW), (0, 0, me * 64), (NDEV, CK, 64))

    def small_list(b, g6, ga, cb, lg, lb, cw):
        return [b.reshape(9, D)] + [row(g) for g in g6] + [ga, row(cb), row(lg), row(lb), cw]

    sw = small_list(b_ada, [g_pre_ff1, g_post_ff1, g_pre_mix, g_post_mix, g_pre_ff2, g_post_ff2], g_attn_out,
                    conv_b, conv_ln_g, conv_ln_b, conv_w)
    sm = small_list(m_b_ada, [m_g_pre_ff1, m_g_post_ff1, m_g_pre_mix, m_g_post_mix, m_g_pre_ff2, m_g_post_ff2],
                    m_g_attn_out, m_conv_b, m_conv_ln_g, m_conv_ln_b, m_conv_w)
    sv = small_list(v_b_ada, [v_g_pre_ff1, v_g_post_ff1, v_g_pre_mix, v_g_post_mix, v_g_pre_ff2, v_g_post_ff2],
                    v_g_attn_out, v_conv_b, v_conv_ln_g, v_conv_ln_b, v_conv_w)
    loss, s_out = _adamw_small(small_g_all, gattn, gconvw, sw, sm, sv)
    s_out = [[o.reshape(w.shape) for o, w in zip(outs, [b_ada, g_pre_ff1, g_post_ff1, g_pre_mix, g_post_mix,
                                                        g_pre_ff2, g_post_ff2, g_attn_out, conv_b, conv_ln_g,
                                                        conv_ln_b, conv_w])] for outs in s_out]

    big_m = [m_ff1_w_in.T, m_ff1_w_out, m_w_in_mix.T, m_w_out_mix, m_ff2_w_in.T, m_ff2_w_out]
    big_v = [v_ff1_w_in.T, v_ff1_w_out, v_w_in_mix.T, v_w_out_mix, v_ff2_w_in.T, v_ff2_w_out]
    tbs = [352, 176, 160, 128, 352, 176]
    tags = ["ff1_w_in", "ff1_w_out", "w_in_mix", "w_out_mix", "ff2_w_in", "ff2_w_out"]
    b_out = [_adamw(big[k], recvs[k], big_m[k], big_v[k], "adamw_" + tags[k], tbs[k]) for k in range(6)]
    b_out = [[o.T for o in outs] if k % 2 == 0 else outs for k, outs in enumerate(b_out)]
    a_out = _adamw(w_ada, g_w_ada.reshape(1, D, ADA_COLS), m_w_ada, v_w_ada, "adamw_ada", 256)

    def leaves(k):
        s = s_out[k]
        b = [o[k] for o in b_out]
        return [a_out[k], s[0], s[1], s[2], b[0], b[1], s[3], s[4], b[2], s[7], s[11], s[8], s[9], s[10], b[3],
                s[5], s[6], b[4], b[5]]

    return (loss.reshape(()), dx0.reshape(1, T, D), *leaves(0), *leaves(1), *leaves(2), *leaves(3))
```

```python
import functools

import jax
import jax.numpy as jnp
from jax import lax
from jax.experimental import pallas as pl
from jax.experimental.pallas import tpu as pltpu

F32 = jnp.float32
BF16 = jnp.bfloat16
D = 1024
DFF = 2816
SL = 704
NSL = DFF // SL
AW = 512
HD = 64
CW = 512
CK = 31
HALO = 32
MIXIN = 2560
NDEV = 8
NMOD = 9 * D
ADA_COLS = NMOD // NDEV
RMS_EPS = 1e-6
LN_EPS = 1e-5
QK_SCALE = HD ** -0.5
W_ZERO_BELOW = -104.0
ADAM_LR, ADAM_B1, ADAM_B2, ADAM_EPS, ADAM_WD, ADAM_STEP = 0.001, 0.9, 0.999, 1e-08, 0.01, 10
ADAM_C1 = 1.0 / (1.0 - ADAM_B1 ** ADAM_STEP)
ADAM_C2 = 1.0 / (1.0 - ADAM_B2 ** ADAM_STEP)
MIB = 1024 * 1024
MESH = pl.DeviceIdType.MESH

ROW_GAINS = 9
ROW_ATTN_CB = 15
ROW_LN = 16
ROW_CONVW = 17
CONVW_ROWS = 16
ROW_LOSS = 33
SMALL_R = 40


def _pcall(body, name, **kw):
    return pl.pallas_call(body, name=name, **kw)


def _cp(sem=None, vmem_mib=48):
    if sem is None:
        return pltpu.CompilerParams(vmem_limit_bytes=vmem_mib * MIB)
    return pltpu.CompilerParams(dimension_semantics=sem, vmem_limit_bytes=vmem_mib * MIB)


def _dot(a, b):
    return jnp.dot(a, b, preferred_element_type=F32)


def _dot_nt(a, b):
    return lax.dot_general(a, b, (((1,), (1,)), ((), ())), preferred_element_type=F32)


def _dot_tn(a, b):
    return lax.dot_general(a, b, (((0,), (0,)), ((), ())), preferred_element_type=F32)


def _sigmoid(x):
    return 0.5 * jnp.tanh(0.5 * x) + 0.5


def _split2(x):
    hi = x.astype(BF16)
    mid = (x - hi.astype(F32)).astype(BF16)
    return hi, mid


def _mat(ref):
    lead = len(ref.shape) - 2
    return ref[(0,) * lead] if lead else ref[...]


def _all_gather(xs, name, in_vmem):
    n = len(xs)

    def body(*refs):
        x_refs, out_refs = refs[:n], refs[n:2 * n]
        send_sems, recv_sems, local_sems = refs[2 * n:]
        mx, my, mc = lax.axis_index("x"), lax.axis_index("y"), lax.axis_index("c")
        me, sibling = (mx, my, mc), (mx, my, 1 - mc)
        chips = [(1 - mx, my), (mx, 1 - my), (1 - mx, 1 - my)]

        def slab(a, px, py, pc):
            return out_refs[a].at[4 * px + 2 * py + pc]

        def copy(a, k, block, to, src=None):
            return pltpu.make_async_remote_copy(
                src_ref=slab(a, *block) if src is None else src, dst_ref=slab(a, *block),
                send_sem=send_sems.at[a, k], recv_sem=recv_sems.at[a, k], device_id=to, device_id_type=MESH)

        mine = [pltpu.make_async_copy(x_refs[a], slab(a, *me), local_sems.at[a]) for a in range(n)]
        for cp in mine:
            cp.start()
        first = []
        for a in range(n):
            first.append(copy(a, 0, me, sibling, src=x_refs[a]))
            first += [copy(a, 1 + j, me, (*chip, mc), src=x_refs[a]) for j, chip in enumerate(chips)]
        for cp in first:
            cp.start()
        passed = []
        for j, chip in enumerate(chips):
            for a in range(n):
                copy(a, 1 + j, (*chip, mc), me).wait_recv()
                passed.append(copy(a, 4 + j, (*chip, mc), sibling))
                passed[-1].start()
        for a in range(n):
            copy(a, 0, sibling, me).wait_recv()
            for j, chip in enumerate(chips):
                copy(a, 4 + j, (*chip, 1 - mc), me).wait_recv()
        for cp in first + passed:
            cp.wait_send()
        for cp in mine:
            cp.wait()

    space = pltpu.VMEM if in_vmem else pl.ANY
    return _pcall(
        body, name,
        out_shape=[jax.ShapeDtypeStruct((NDEV,) + x.shape, x.dtype) for x in xs],
        in_specs=[pl.BlockSpec(memory_space=space)] * n,
        out_specs=[pl.BlockSpec(memory_space=space)] * n,
        scratch_shapes=[pltpu.SemaphoreType.DMA((n, 7)), pltpu.SemaphoreType.DMA((n, 7)),
                        pltpu.SemaphoreType.DMA((n,))],
    )(*xs)


def _exchange_copies(kind, src, dst, send_sems, recv_sems, local_sems):
    mx, my, mc = lax.axis_index("x"), lax.axis_index("y"), lax.axis_index("c")
    me = 4 * mx + 2 * my + mc
    n = len(src)
    pick = (lambda a, p: src[a].at[p]) if kind == "a2a" else (lambda a, p: src[a])
    mine = [pltpu.make_async_copy(pick(a, me), dst[a].at[me], local_sems.at[a]) for a in range(n)]
    copies = []
    for r in range(1, NDEV):
        px = 1 - mx if r & 4 else mx
        py = 1 - my if r & 2 else my
        pc = 1 - mc if r & 1 else mc
        for a in range(n):
            copies.append(pltpu.make_async_remote_copy(
                src_ref=pick(a, 4 * px + 2 * py + pc), dst_ref=dst[a].at[me],
                send_sem=send_sems.at[a, r - 1], recv_sem=recv_sems.at[a, r - 1],
                device_id=(px, py, pc), device_id_type=MESH))
    return mine, copies


def _hosted_call(body, name, comm, grid, in_specs, out_specs, out_shape, scratch_shapes, sem, vmem_mib, args):
    if comm is None:
        outs = _pcall(body, name, grid=grid, in_specs=in_specs, out_specs=out_specs, out_shape=out_shape,
                      scratch_shapes=scratch_shapes, compiler_params=_cp(sem, vmem_mib))(*args)
        return outs, []
    kind, arrs = comm
    nc, n_in, n_out, n_scr = len(arrs), len(in_specs), len(out_specs), len(scratch_shapes)
    rank = len(grid)

    def wrapped(*refs):
        ins, csrc = refs[:n_in], refs[n_in:n_in + nc]
        outs, cdst = refs[n_in + nc:n_in + nc + n_out], refs[n_in + nc + n_out:n_in + 2 * nc + n_out]
        rest = refs[n_in + 2 * nc + n_out:]
        scr, sems = rest[:n_scr], rest[n_scr:]
        first = functools.reduce(jnp.logical_and, [pl.program_id(d) == 0 for d in range(rank)])
        last = functools.reduce(jnp.logical_and, [pl.program_id(d) == grid[d] - 1 for d in range(rank)])

        @pl.when(first)
        def _():
            mine, copies = _exchange_copies(kind, csrc, cdst, *sems)
            for cp in mine + copies:
                cp.start()

        body(*ins, *outs, *scr)

        @pl.when(last)
        def _():
            mine, copies = _exchange_copies(kind, csrc, cdst, *sems)
            for cp in copies:
                cp.wait_recv()
            for cp in copies:
                cp.wait_send()
            for cp in mine:
                cp.wait()

    hbm = pl.BlockSpec(memory_space=pl.ANY)
    cshape = [jax.ShapeDtypeStruct(a.shape if kind == "a2a" else (NDEV,) + a.shape, a.dtype) for a in arrs]
    res = _pcall(wrapped, name, grid=grid, in_specs=list(in_specs) + [hbm] * nc,
                 out_specs=list(out_specs) + [hbm] * nc, out_shape=list(out_shape) + cshape,
                 scratch_shapes=list(scratch_shapes) + [pltpu.SemaphoreType.DMA((nc, 7)),
                                                        pltpu.SemaphoreType.DMA((nc, 7)),
                                                        pltpu.SemaphoreType.DMA((nc,))],
                 compiler_params=_cp(("arbitrary",) * rank, vmem_mib))(*args, *arrs)
    return res[:n_out], res[n_out:]


def _ada_fwd(c_all, w, b):
    n = w.shape[1]

    def body(c_ref, w_ref, b_ref, o_ref):
        c = c_ref[...]
        s = c * _sigmoid(c)
        o_ref[...] = jnp.dot(s, w_ref[...], preferred_element_type=F32, precision=lax.Precision.HIGHEST) + b_ref[...]

    return _pcall(body, "ada_fwd", out_shape=jax.ShapeDtypeStruct((NDEV, n), F32), compiler_params=_cp())(c_all, w, b)


def _ada_bwd(c_all_t, dmod):
    n = dmod.shape[1]

    def body(ct_ref, d_ref, o_ref):
        ct = ct_ref[...]
        s = ct * _sigmoid(ct)
        acc = s[:, 0:1] * d_ref[0:1, :]
        for b in range(1, NDEV):
            acc = acc + s[:, b:b + 1] * d_ref[b:b + 1, :]
        o_ref[...] = acc

    return _pcall(body, "ada_bwd", out_shape=jax.ShapeDtypeStruct((D, n), F32), compiler_params=_cp())(c_all_t, dmod)


def _ffn_in(x, g_pre, shift, scale, w_in, name, comm=None, tm=512):
    T = x.shape[0]

    def body(x_ref, g_ref, sh_ref, sc_ref, w_ref, h_ref, gu_ref, a_ref):
        xv = x_ref[...]
        r = lax.rsqrt(jnp.mean(xv * xv, axis=-1, keepdims=True) + RMS_EPS)
        hv = ((xv * r) * g_ref[...] * (1.0 + sc_ref[...]) + sh_ref[...]).astype(BF16)
        h_ref[...] = hv
        for j in range(NSL):
            g = _dot_nt(hv, w_ref[j])
            u = _dot_nt(hv, w_ref[j + NSL])
            gu_ref[j, 0] = g.astype(BF16)
            gu_ref[j, 1] = u.astype(BF16)
            a_ref[j] = (g * _sigmoid(g) * u).astype(BF16)

    row = pl.BlockSpec((tm, D), lambda i: (i, 0))
    vec = pl.BlockSpec((1, D), lambda i: (0, 0))
    (h, gu, a), got = _hosted_call(
        body, name, comm, (T // tm,),
        [row, vec, vec, vec, pl.BlockSpec((NDEV, SL, D), lambda i: (0, 0, 0), pipeline_mode=pl.Buffered(1))],
        [row, pl.BlockSpec((NSL, 2, tm, SL), lambda i: (0, 0, i, 0)), pl.BlockSpec((NSL, tm, SL), lambda i: (0, i, 0))],
        [jax.ShapeDtypeStruct((T, D), BF16), jax.ShapeDtypeStruct((NSL, 2, T, SL), BF16),
         jax.ShapeDtypeStruct((NSL, T, SL), BF16)],
        [], ("parallel",), 48, (x, g_pre, shift, scale, w_in))
    return h, gu, a, got


def _mm_post(a_list, a_specs, w_list, w_specs, x, g_post, gate, res_w, name, tm, tgt=None):
    T = x.shape[0]
    n = len(a_list)
    with_loss = tgt is not None

    def body(*refs):
        a_refs, w_refs = refs[:n], refs[n:2 * n]
        x_ref, g_ref, gt_ref = refs[2 * n:2 * n + 3]
        rest = refs[2 * n + 3:]
        f = None
        for a_ref, w_ref in zip(a_refs, w_refs):
            if len(a_ref.shape) == 3:
                terms = [_dot(a_ref[j], w_ref[j]) for j in range(a_ref.shape[0])]
            else:
                terms = [_dot(a_ref[...], w_ref[...])]
            for t in terms:
                f = t if f is None else f + t
        r = lax.rsqrt(jnp.mean(f * f, axis=-1, keepdims=True) + RMS_EPS)
        y = (f * r) * g_ref[...]
        out = x_ref[...] + (res_w * (1.0 + gt_ref[...])) * y
        if with_loss:
            t_ref, f_ref, dy_ref, l_ref = rest

            @pl.when(pl.program_id(0) == 0)
            def _():
                l_ref[...] = jnp.zeros_like(l_ref)

            e = out - t_ref[...]
            dy_ref[...] = e * (1.0 / D)
            l_ref[...] += 0.5 * jnp.sum(jnp.mean(e * e, axis=-1, keepdims=True), axis=0, keepdims=True)
        else:
            f_ref, o_ref = rest
            o_ref[...] = out
        f_ref[...] = f

    row = pl.BlockSpec((tm, D), lambda i: (i, 0))
    vec = pl.BlockSpec((1, D), lambda i: (0, 0))
    big = jax.ShapeDtypeStruct((T, D), F32)
    if with_loss:
        return _pcall(body, name, grid=(T // tm,),
                      in_specs=list(a_specs) + list(w_specs) + [row, vec, vec, row],
                      out_specs=[row, row, pl.BlockSpec((1, 1), lambda i: (0, 0))],
                      out_shape=[big, big, jax.ShapeDtypeStruct((1, 1), F32)],
                      compiler_params=_cp(("arbitrary",)))(*a_list, *w_list, x, g_post, gate, tgt)
    return _pcall(body, name, grid=(T // tm,),
                  in_specs=list(a_specs) + list(w_specs) + [row, vec, vec], out_specs=[row, row],
                  out_shape=[big, big], compiler_params=_cp(("parallel",)))(*a_list, *w_list, x, g_post, gate)


def _ffn_out_bwd(dout, f, g_post, gate, res_w, w_out4, gu, name, comm=None, tm=512):
    T = f.shape[0]

    def body(do_ref, f_ref, g_ref, gt_ref, w_ref, gu_ref, df_ref, dgate_ref, dg_ref, dgu_ref):
        @pl.when(pl.program_id(0) == 0)
        def _():
            dgate_ref[...] = jnp.zeros_like(dgate_ref)
            dg_ref[...] = jnp.zeros_like(dg_ref)

        do = do_ref[...]
        f = f_ref[...]
        r = lax.rsqrt(jnp.mean(f * f, axis=-1, keepdims=True) + RMS_EPS)
        fn = f * r
        dgate_ref[...] += jnp.sum((res_w * do) * (fn * g_ref[...]), axis=0, keepdims=True)
        dy = (res_w * (1.0 + gt_ref[...])) * do
        dg_ref[...] += jnp.sum(dy * fn, axis=0, keepdims=True)
        dyg = dy * g_ref[...]
        dfv = (r * (dyg - fn * jnp.mean(dyg * fn, axis=-1, keepdims=True))).astype(BF16)
        df_ref[...] = dfv
        for j in range(NSL):
            da = _dot_nt(dfv, w_ref[j])
            gv = gu_ref[j, 0].astype(F32)
            s = _sigmoid(gv)
            gs = gv * s
            dgu_ref[j, 0] = (da * gu_ref[j, 1].astype(F32) * (s + gs * (1.0 - s))).astype(BF16)
            dgu_ref[j, 1] = (da * gs).astype(BF16)

    row = pl.BlockSpec((tm, D), lambda i: (i, 0))
    vec = pl.BlockSpec((1, D), lambda i: (0, 0))
    gus = pl.BlockSpec((NSL, 2, tm, SL), lambda i: (0, 0, i, 0))
    return _hosted_call(
        body, name, comm, (T // tm,),
        [row, row, vec, vec, pl.BlockSpec((NSL, SL, D), lambda i: (0, 0, 0), pipeline_mode=pl.Buffered(1)), gus],
        [row, vec, vec, gus],
        [jax.ShapeDtypeStruct((T, D), BF16), jax.ShapeDtypeStruct((1, D), F32), jax.ShapeDtypeStruct((1, D), F32),
         jax.ShapeDtypeStruct((NSL, 2, T, SL), BF16)], [], ("arbitrary",), 48, (dout, f, g_post, gate, w_out4, gu))


def _mm_tn(a, a_spec, b, b_spec, out_shape, out_spec, grid, name, comm=None):
    k, nn = out_spec.block_shape[-2:]
    steps = grid[1]

    def body(a_ref, b_ref, o_ref, acc_ref):
        i = pl.program_id(1)

        @pl.when(i == 0)
        def _():
            acc_ref[...] = jnp.zeros_like(acc_ref)

        acc_ref[...] += _dot_tn(_mat(a_ref), _mat(b_ref))

        @pl.when(i == steps - 1)
        def _():
            lead = len(o_ref.shape) - 2
            o_ref[(0,) * lead if lead else ...] = acc_ref[...].astype(BF16)

    (out,), got = _hosted_call(body, name, comm, grid, [a_spec, b_spec], [out_spec],
                               [jax.ShapeDtypeStruct(out_shape, BF16)], [pltpu.VMEM((k, nn), F32)],
                               ("parallel", "arbitrary"), 48, (a, b))
    return out, got


def _dw_in(h, dgu, name, tmw, comm=None):
    T = h.shape[0]
    steps = T // tmw

    def body(h_ref, b_ref, o_ref, acc_ref):
        i = pl.program_id(1)

        @pl.when(i == 0)
        def _():
            acc_ref[...] = jnp.zeros_like(acc_ref)

        hv = h_ref[...]
        for p in range(2):
            acc_ref[p] += _dot_tn(b_ref[0, p], hv)

        @pl.when(i == steps - 1)
        def _():
            o_ref[:, 0] = acc_ref[...].astype(BF16)

    (out,), got = _hosted_call(
        body, name, comm, (NSL, steps),
        [pl.BlockSpec((tmw, D), lambda j, i: (i, 0)), pl.BlockSpec((1, 2, tmw, SL), lambda j, i: (j, 0, i, 0))],
        [pl.BlockSpec((2, 1, SL, D), lambda j, i: (0, j, 0, 0))],
        [jax.ShapeDtypeStruct((2, NSL, SL, D), BF16)], [pltpu.VMEM((2, SL, D), F32)],
        ("parallel", "arbitrary"), 56, (h, dgu))
    return out, got


def _mm_prebwd(a, a_spec, w, w_spec, dh_fn, x, dout, g_pre, scale, name, tm=256, comm=None):
    T = x.shape[0]

    def body(a_ref, w_ref, x_ref, do_ref, g_ref, sc_ref, dx_ref, dsh_ref, dsc_ref, dg_ref):
        @pl.when(pl.program_id(0) == 0)
        def _():
            dsh_ref[...] = jnp.zeros_like(dsh_ref)
            dsc_ref[...] = jnp.zeros_like(dsc_ref)
            dg_ref[...] = jnp.zeros_like(dg_ref)

        dh = dh_fn(a_ref, w_ref)
        xv = x_ref[...]
        r = lax.rsqrt(jnp.mean(xv * xv, axis=-1, keepdims=True) + RMS_EPS)
        xn = xv * r
        dsh_ref[...] += jnp.sum(dh, axis=0, keepdims=True)
        dsc_ref[...] += jnp.sum(dh * (xn * g_ref[...]), axis=0, keepdims=True)
        dn = dh * (1.0 + sc_ref[...])
        dg_ref[...] += jnp.sum(dn * xn, axis=0, keepdims=True)
        dng = dn * g_ref[...]
        dx_ref[...] = do_ref[...] + r * (dng - xn * jnp.mean(dng * xn, axis=-1, keepdims=True))

    row = pl.BlockSpec((tm, D), lambda i: (i, 0))
    vec = pl.BlockSpec((1, D), lambda i: (0, 0))
    return _hosted_call(body, name, comm, (T // tm,), [a_spec, w_spec, row, row, vec, vec], [row, vec, vec, vec],
                        [jax.ShapeDtypeStruct((T, D), F32)] + [jax.ShapeDtypeStruct((1, D), F32)] * 3,
                        [], ("arbitrary",), 56, (a, w, x, dout, g_pre, scale))


def _mix_in(x, g_pre, shift, scale, w, name, tm=512):
    T = x.shape[0]

    def body(x_ref, g_ref, sh_ref, sc_ref, w_ref, h_ref, qkv_ref, cvg_ref):
        xv = x_ref[...]
        r = lax.rsqrt(jnp.mean(xv * xv, axis=-1, keepdims=True) + RMS_EPS)
        hv = ((xv * r) * g_ref[...] * (1.0 + sc_ref[...]) + sh_ref[...]).astype(BF16)
        h_ref[...] = hv
        p = _dot_nt(hv, w_ref[...])
        qkv_ref[...] = p[:, :3 * AW].astype(BF16)
        cvg_ref[...] = p[:, 3 * AW:]

    row = pl.BlockSpec((tm, D), lambda i: (i, 0))
    vec = pl.BlockSpec((1, D), lambda i: (0, 0))
    return _pcall(body, name, grid=(T // tm,),
                  in_specs=[row, vec, vec, vec, pl.BlockSpec((MIXIN, D), lambda i: (0, 0))],
                  out_specs=[row, pl.BlockSpec((tm, 3 * AW), lambda i: (i, 0)), pl.BlockSpec((tm, 2 * CW), lambda i: (i, 0))],
                  out_shape=[jax.ShapeDtypeStruct((T, D), BF16), jax.ShapeDtypeStruct((T, 3 * AW), BF16),
                             jax.ShapeDtypeStruct((T, 2 * CW), F32)],
                  compiler_params=_cp(("parallel",)))(x, g_pre, shift, scale, w)


def _mix_out_bwd(dout, f, g_post, gate, res_w, w, name, tm=512):
    T = f.shape[0]
    N = w.shape[0]

    def body(do_ref, f_ref, g_ref, gt_ref, w_ref, df_ref, dgate_ref, dg_ref, o_ref):
        @pl.when(pl.program_id(0) == 0)
        def _():
            dgate_ref[...] = jnp.zeros_like(dgate_ref)
            dg_ref[...] = jnp.zeros_like(dg_ref)

        do = do_ref[...]
        f = f_ref[...]
        r = lax.rsqrt(jnp.mean(f * f, axis=-1, keepdims=True) + RMS_EPS)
        fn = f * r
        dgate_ref[...] += jnp.sum((res_w * do) * (fn * g_ref[...]), axis=0, keepdims=True)
        dy = (res_w * (1.0 + gt_ref[...])) * do
        dg_ref[...] += jnp.sum(dy * fn, axis=0, keepdims=True)
        dyg = dy * g_ref[...]
        dfv = (r * (dyg - fn * jnp.mean(dyg * fn, axis=-1, keepdims=True))).astype(BF16)
        df_ref[...] = dfv
        o_ref[...] = _dot_nt(dfv, w_ref[...])

    row = pl.BlockSpec((tm, D), lambda i: (i, 0))
    vec = pl.BlockSpec((1, D), lambda i: (0, 0))
    return _pcall(body, name, grid=(T // tm,),
                  in_specs=[row, row, vec, vec, pl.BlockSpec((N, D), lambda i: (0, 0))],
                  out_specs=[row, vec, vec, pl.BlockSpec((tm, N), lambda i: (i, 0))],
                  out_shape=[jax.ShapeDtypeStruct((T, D), BF16), jax.ShapeDtypeStruct((1, D), F32),
                             jax.ShapeDtypeStruct((1, D), F32), jax.ShapeDtypeStruct((T, N), F32)],
                  compiler_params=_cp(("arbitrary",)))(dout, f, g_post, gate, w)


def _softplus_parts(z):
    ls = jnp.minimum(z, 0.0) - jnp.log(1.0 + jnp.exp(-jnp.abs(z)))
    return ls, ls - z


def _head_sum(x, first):
    sa = jnp.sum(jnp.where(first, x, 0.0), axis=-1, keepdims=True)
    sb = jnp.sum(jnp.where(first, 0.0, x), axis=-1, keepdims=True)
    return jnp.where(first, sa, sb)


def _attn_specs(T, tq):
    qs = pl.BlockSpec((tq, 128), lambda p, i: (i, p))
    ks = pl.BlockSpec((T, 128), lambda p, i: (0, 4 + p))
    vs = pl.BlockSpec((T, 128), lambda p, i: (0, 8 + p))
    gs = pl.BlockSpec((1, 128), lambda p, i: (0, p))
    return qs, ks, vs, gs


def _attn_fwd(qkv, g_attn, tq, comm=None):
    T = qkv.shape[0]

    def body(q_ref, k_ref, v_ref, g_ref, o_ref, an_ref):
        i = pl.program_id(1)
        first = lax.broadcasted_iota(jnp.int32, (tq, 128), 1) < HD
        q = (q_ref[...].astype(F32) * QK_SCALE).astype(BF16)
        zq = jnp.zeros_like(q)
        qs = (jnp.where(first, q, zq), jnp.where(first, zq, q))
        rows = lax.broadcasted_iota(jnp.int32, (tq, tq), 0)
        cols = lax.broadcasted_iota(jnp.int32, (tq, tq), 1)
        tri = (rows > cols).astype(BF16)
        tri2 = jnp.concatenate([tri, tri], axis=0)
        strict = cols < rows

        def tile(j, Rs, acc, masked):
            start = j * tq if isinstance(j, int) else pl.multiple_of(j * tq, tq)
            kb = k_ref[pl.ds(start, tq), :]
            vb = v_ref[pl.ds(start, tq), :]
            zs = [_dot_nt(qs[hh], kb) for hh in range(2)]
            parts = [_softplus_parts(z) for z in zs]
            lsms = [jnp.where(strict, p[1], 0.0) if masked else p[1] for p in parts]
            splits = [_split2(x) for x in lsms]
            afters = [_dot(jnp.concatenate(s, axis=1), tri2) for s in splits]
            ws = [jnp.exp(parts[hh][0] + afters[hh] + Rs[hh]) for hh in range(2)]
            if masked:
                ws = [jnp.where(strict, w, 0.0) for w in ws]
            outs = [_dot(w.astype(BF16), vb) for w in ws]
            new_r = [Rs[hh] + afters[hh][:, 0:1] + lsms[hh][:, 0:1] for hh in range(2)]
            return new_r[0], new_r[1], acc + jnp.where(first, outs[0], outs[1])

        zr = jnp.zeros((tq, 1), F32)

        def finish(acc):
            o_ref[...] = acc
            r = lax.rsqrt(_head_sum(acc * acc, first) * (1.0 / HD) + RMS_EPS)
            an_ref[...] = ((acc * r) * g_ref[...]).astype(BF16)

        @pl.when(i == 0)
        def _():
            finish(tile(0, (zr, zr), jnp.zeros((tq, 128), F32), True)[2])

        @pl.when(i > 0)
        def _():
            ra, rb, acc = tile(i, (zr, zr), jnp.zeros((tq, 128), F32), True)
            ra, rb, acc = tile(i - 1, (ra, rb), acc, False)

            def more(c):
                return jnp.logical_and(c[0] < i, jnp.maximum(jnp.max(c[1]), jnp.max(c[2])) > W_ZERO_BELOW)

            def step(c):
                ra, rb, acc = tile(i - 1 - c[0], (c[1], c[2]), c[3], False)
                return c[0] + 1, ra, rb, acc

            finish(lax.while_loop(more, step, (jnp.int32(1), ra, rb, acc))[3])

    qs, ks, vs, gs = _attn_specs(T, tq)
    (o, an), got = _hosted_call(body, "attn_fwd", comm, (AW // 128, T // tq), [qs, ks, vs, gs], [qs, qs],
                                [jax.ShapeDtypeStruct((T, AW), F32), jax.ShapeDtypeStruct((T, AW), BF16)],
                                [], ("parallel", "parallel"), 48, (qkv, qkv, qkv, g_attn))
    return o, an, got


def _attn_bwd(qkv, o, dcat, g_attn, tq, comm=None):
    T = qkv.shape[0]

    def body(q_ref, k_ref, v_ref, o_ref, dan_ref, g_ref, dq_ref, dk_ref, dv_ref, dg_ref):
        i = pl.program_id(1)

        @pl.when(i == 0)
        def _():
            dk_ref[...] = jnp.zeros_like(dk_ref)
            dv_ref[...] = jnp.zeros_like(dv_ref)
            dg_ref[...] = jnp.zeros_like(dg_ref)

        first = lax.broadcasted_iota(jnp.int32, (tq, 128), 1) < HD
        q = (q_ref[...].astype(F32) * QK_SCALE).astype(BF16)
        zq = jnp.zeros_like(q)
        qs = (jnp.where(first, q, zq), jnp.where(first, zq, q))
        o = o_ref[...]
        dan = dan_ref[...]
        r = lax.rsqrt(_head_sum(o * o, first) * (1.0 / HD) + RMS_EPS)
        on = o * r
        dg_ref[...] += jnp.sum(dan * on, axis=0, keepdims=True)
        dyg = dan * g_ref[...]
        dO = r * (dyg - on * (_head_sum(dyg * on, first) * (1.0 / HD)))
        dOb = dO.astype(BF16)
        dOs = (jnp.where(first, dOb, zq), jnp.where(first, zq, dOb))
        ones = jnp.ones((8, 128), BF16)
        Ds = []
        for hh in range(2):
            prod = dOs[hh].astype(F32) * o
            p1 = prod.astype(BF16)
            rem = prod - p1.astype(F32)
            p2 = rem.astype(BF16)
            p3 = (rem - p2.astype(F32)).astype(BF16)
            Ds.append((_dot_nt(ones, p1) + _dot_nt(ones, p2) + _dot_nt(ones, p3))[0:1, :])

        rows = lax.broadcasted_iota(jnp.int32, (tq, tq), 0)
        cols = lax.broadcasted_iota(jnp.int32, (tq, tq), 1)
        tri_after = (cols > rows).astype(BF16)
        tri_incl = (cols >= rows).astype(BF16)
        tri_after2 = jnp.concatenate([tri_after, tri_after], axis=1)
        tri_incl2 = jnp.concatenate([tri_incl, tri_incl], axis=1)
        strict = rows < cols

        def tile(j, Rs, Gs, dq, masked):
            start = j * tq if isinstance(j, int) else pl.multiple_of(j * tq, tq)
            kb = k_ref[pl.ds(start, tq), :]
            vb = v_ref[pl.ds(start, tq), :]
            H = range(2)
            parts = [_softplus_parts(_dot_nt(kb, qs[hh])) for hh in H]
            lsms = [jnp.where(strict, p[1], 0.0) if masked else p[1] for p in parts]
            splits = [_split2(x) for x in lsms]
            afters = [_dot(tri_after2, jnp.concatenate(s, axis=0)) for s in splits]
            ws = [jnp.exp(parts[hh][0] + afters[hh] + Rs[hh]) for hh in H]
            if masked:
                ws = [jnp.where(strict, w, 0.0) for w in ws]
            wbs = [w.astype(BF16) for w in ws]
            dlws = [_dot_nt(vb, dOs[hh]) * wbs[hh].astype(F32) for hh in H]
            splits2 = [_split2(x) for x in dlws]
            Cs = [_dot(tri_incl2, jnp.concatenate(s, axis=0)) for s in splits2]
            dlsms = [Ds[hh] - Gs[hh] - Cs[hh] for hh in H]
            if masked:
                dlsms = [jnp.where(strict, x, 0.0) for x in dlsms]
            ps = [jnp.exp(p[0]) for p in parts]
            dzs = [(dlws[hh] * (1.0 - ps[hh]) - dlsms[hh] * ps[hh]).astype(BF16) for hh in H]
            dkp = _dot(dzs[0], qs[0]) + _dot(dzs[1], qs[1])
            dvp = _dot(wbs[0], dOs[0]) + _dot(wbs[1], dOs[1])
            dq = dq + jnp.where(first, _dot_tn(dzs[0], kb), _dot_tn(dzs[1], kb))
            new_r = [Rs[hh] + afters[hh][0:1, :] + lsms[hh][0:1, :] for hh in H]
            new_g = [Gs[hh] + Cs[hh][0:1, :] for hh in H]
            dk_ref[pl.ds(start, tq), :] += dkp
            dv_ref[pl.ds(start, tq), :] += dvp
            return new_r[0], new_r[1], new_g[0], new_g[1], dq

        zrow = jnp.zeros((1, tq), F32)

        @pl.when(i == 0)
        def _():
            dq0 = tile(0, (zrow, zrow), (zrow, zrow), jnp.zeros((tq, 128), F32), True)[4]
            dq_ref[...] = (dq0 * QK_SCALE).astype(BF16)

        @pl.when(i > 0)
        def _():
            st = tile(i, (zrow, zrow), (zrow, zrow), jnp.zeros((tq, 128), F32), True)
            st = tile(i - 1, st[0:2], st[2:4], st[4], False)

            def more(c):
                return jnp.logical_and(c[0] < i, jnp.maximum(jnp.max(c[1]), jnp.max(c[2])) > W_ZERO_BELOW)

            def step(c):
                return (c[0] + 1,) + tile(i - 1 - c[0], (c[1], c[2]), (c[3], c[4]), c[5], False)

            dq_ref[...] = (lax.while_loop(more, step, (jnp.int32(1),) + st)[5] * QK_SCALE).astype(BF16)

    qs, ks, vs, gs = _attn_specs(T, tq)
    kacc = pl.BlockSpec((T, 128), lambda p, i: (0, p))
    return _hosted_call(body, "attn_bwd", comm, (AW // 128, T // tq), [qs, ks, vs, qs, qs, gs], [qs, kacc, kacc, gs],
                        [jax.ShapeDtypeStruct((T, AW), BF16), jax.ShapeDtypeStruct((T, AW), F32),
                         jax.ShapeDtypeStruct((T, AW), F32), jax.ShapeDtypeStruct((1, AW), F32)],
                        [], ("parallel", "arbitrary"), 48, (qkv, qkv, qkv, o, dcat, g_attn))


def _taps_by_phase(offsets):
    groups = {}
    for k, off in enumerate(offsets):
        groups.setdefault(off % 8, []).append((k, off // 8))
    return sorted(groups.items())


CONV_ROWS = 32


def _shifted_tap_sum(w_ref, pad_ref, ph_ref, out_ref, offsets, tb):
    groups = _taps_by_phase(offsets)

    def chunk(c, carry):
        r0 = pl.multiple_of(c * CONV_ROWS, CONV_ROWS)
        acc = None
        for p, taps in groups:
            n = CONV_ROWS if p == 0 else CONV_ROWS + 8
            a = None
            for k, m in taps:
                t = w_ref[k:k + 1, :] * pad_ref[pl.ds(pl.multiple_of(r0 + 8 * m, 8), n), :]
                a = t if a is None else a + t
            if p:
                ph_ref[0:n, :] = a
                a = ph_ref[p:p + CONV_ROWS, :]
            acc = a if acc is None else acc + a
        out_ref[pl.ds(r0, CONV_ROWS), :] = acc
        return carry

    lax.fori_loop(0, tb // CONV_ROWS, chunk, 0)


def _conv_fwd(cvg, conv_w, conv_b, ln_g, ln_b, tb=512):
    T = cvg.shape[0]
    hb = tb // HALO

    def body(cv_ref, cg_ref, cvp_ref, cgp_ref, w_ref, b_ref, g_ref, be_ref, u0_ref, u1_ref, u3_ref, pad_ref, ph_ref,
             sum_ref):
        i = pl.program_id(0)
        u0 = cv_ref[...] * _sigmoid(cg_ref[...])
        prev = cvp_ref[...] * _sigmoid(cgp_ref[...])
        pad_ref[0:HALO, :] = jnp.where(i > 0, prev, 0.0)
        pad_ref[HALO:HALO + tb, :] = u0
        u0_ref[...] = u0
        _shifted_tap_sum(w_ref, pad_ref, ph_ref, sum_ref, [HALO - (CK - 1) + kk for kk in range(CK)], tb)
        acc = sum_ref[...] + b_ref[...]
        u1_ref[...] = acc
        mu = jnp.mean(acc, axis=-1, keepdims=True)
        xc = acc - mu
        var = jnp.mean(xc * xc, axis=-1, keepdims=True)
        u2 = (xc * lax.rsqrt(var + LN_EPS)) * g_ref[...] + be_ref[...]
        u3_ref[...] = (u2 * _sigmoid(u2)).astype(BF16)

    cur = lambda col: pl.BlockSpec((tb, CW), lambda i: (i, col))
    prv = lambda col: pl.BlockSpec((HALO, CW), lambda i: (jnp.maximum(i * hb - 1, 0), col))
    vec = pl.BlockSpec((1, CW), lambda i: (0, 0))
    out = pl.BlockSpec((tb, CW), lambda i: (i, 0))
    return _pcall(body, "conv_fwd", grid=(T // tb,),
                  in_specs=[cur(0), cur(1), prv(0), prv(1), pl.BlockSpec((HALO, CW), lambda i: (0, 0)), vec, vec, vec],
                  out_specs=[out, out, out],
                  out_shape=[jax.ShapeDtypeStruct((T, CW), F32), jax.ShapeDtypeStruct((T, CW), F32),
                             jax.ShapeDtypeStruct((T, CW), BF16)],
                  scratch_shapes=[pltpu.VMEM((tb + HALO, CW), F32), pltpu.VMEM((CONV_ROWS + 8, CW), F32),
                                  pltpu.VMEM((tb, CW), F32)],
                  compiler_params=_cp(("parallel",)))(cvg, cvg, cvg, cvg, conv_w, conv_b, ln_g, ln_b)


def _conv_bwd1(dcat, u1, u0, ln_g, ln_b, tb=512):
    T = u1.shape[0]
    hb = tb // HALO

    def body(d3_ref, u1_ref, u0_ref, u0p_ref, g_ref, be_ref, du1_ref, dw_ref, db_ref, dlg_ref, dlb_ref, pad_ref, d_ref,
             q_ref):
        i = pl.program_id(0)

        @pl.when(i == 0)
        def _():
            dw_ref[...] = jnp.zeros_like(dw_ref)
            db_ref[...] = jnp.zeros_like(db_ref)
            dlg_ref[...] = jnp.zeros_like(dlg_ref)
            dlb_ref[...] = jnp.zeros_like(dlb_ref)

        u1 = u1_ref[...]
        mu = jnp.mean(u1, axis=-1, keepdims=True)
        xc = u1 - mu
        rstd = lax.rsqrt(jnp.mean(xc * xc, axis=-1, keepdims=True) + LN_EPS)
        xh = xc * rstd
        u2 = xh * g_ref[...] + be_ref[...]
        s = _sigmoid(u2)
        du2 = d3_ref[...] * (s + u2 * s * (1.0 - s))
        dlg_ref[...] += jnp.sum(du2 * xh, axis=0, keepdims=True)
        dlb_ref[...] += jnp.sum(du2, axis=0, keepdims=True)
        dxh = du2 * g_ref[...]
        du1 = rstd * (dxh - jnp.mean(dxh, axis=-1, keepdims=True) - xh * jnp.mean(dxh * xh, axis=-1, keepdims=True))
        du1_ref[...] = du1
        db_ref[...] += jnp.sum(du1, axis=0, keepdims=True)
        pad_ref[0:HALO, :] = jnp.where(i > 0, u0p_ref[...], 0.0)
        pad_ref[HALO:HALO + tb, :] = u0_ref[...]
        d_ref[0:8, :] = jnp.zeros((8, CW), F32)
        d_ref[8:8 + tb, :] = du1
        d_ref[8 + tb:16 + tb, :] = jnp.zeros((8, CW), F32)
        for p, taps in _taps_by_phase([HALO - (CK - 1) + kk for kk in range(CK)]):
            n = tb + 8
            q_ref[...] = d_ref[8 - p:8 - p + n, :]
            for k, m in taps:
                if 8 * m + n <= tb + HALO:
                    dw_ref[k:k + 1, :] += jnp.sum(q_ref[...] * pad_ref[8 * m:8 * m + n, :], axis=0, keepdims=True)
                else:
                    dw_ref[k:k + 1, :] += jnp.sum(q_ref[0:tb, :] * pad_ref[8 * m:8 * m + tb, :], axis=0, keepdims=True)

    cur = pl.BlockSpec((tb, CW), lambda i: (i, 0))
    vec = pl.BlockSpec((1, CW), lambda i: (0, 0))
    return _pcall(body, "conv_bwd1", grid=(T // tb,),
                  in_specs=[pl.BlockSpec((tb, CW), lambda i: (i, 1)), cur, cur,
                            pl.BlockSpec((HALO, CW), lambda i: (jnp.maximum(i * hb - 1, 0), 0)), vec, vec],
                  out_specs=[cur, pl.BlockSpec((HALO, CW), lambda i: (0, 0)), vec, vec, vec],
                  out_shape=[jax.ShapeDtypeStruct((T, CW), F32), jax.ShapeDtypeStruct((HALO, CW), F32)]
                  + [jax.ShapeDtypeStruct((1, CW), F32)] * 3,
                  scratch_shapes=[pltpu.VMEM((tb + HALO, CW), F32), pltpu.VMEM((tb + 16, CW), F32),
                                  pltpu.VMEM((tb + 8, CW), F32)],
                  compiler_params=_cp(("arbitrary",)))(dcat, u1, u0, u0, ln_g, ln_b)


def _conv_bwd2(du1, cvg, conv_w, tb=512):
    T = du1.shape[0]
    hb = tb // HALO
    last = T // HALO - 1
    nblk = T // tb

    def body(d_ref, dn_ref, cv_ref, cg_ref, w_ref, o_ref, pad_ref, ph_ref, sum_ref):
        i = pl.program_id(0)
        pad_ref[0:tb, :] = d_ref[...]
        pad_ref[tb:tb + HALO, :] = jnp.where(i < nblk - 1, dn_ref[...], 0.0)
        _shifted_tap_sum(w_ref, pad_ref, ph_ref, sum_ref, [CK - 1 - kk for kk in range(CK)], tb)
        acc = sum_ref[...]
        sg = _sigmoid(cg_ref[...])
        o_ref[:, 0:CW] = (acc * sg).astype(BF16)
        o_ref[:, CW:2 * CW] = (acc * cv_ref[...] * sg * (1.0 - sg)).astype(BF16)

    cur = pl.BlockSpec((tb, CW), lambda i: (i, 0))
    return _pcall(body, "conv_bwd2", grid=(nblk,),
                  in_specs=[cur, pl.BlockSpec((HALO, CW), lambda i: (jnp.minimum((i + 1) * hb, last), 0)),
                            pl.BlockSpec((tb, CW), lambda i: (i, 0)), pl.BlockSpec((tb, CW), lambda i: (i, 1)),
                            pl.BlockSpec((HALO, CW), lambda i: (0, 0))],
                  out_specs=pl.BlockSpec((tb, 2 * CW), lambda i: (i, 0)),
                  out_shape=jax.ShapeDtypeStruct((T, 2 * CW), BF16),
                  scratch_shapes=[pltpu.VMEM((tb + HALO, CW), F32), pltpu.VMEM((CONV_ROWS + 8, CW), F32),
                                  pltpu.VMEM((tb, CW), F32)],
                  compiler_params=_cp(("parallel",)))(du1, du1, cvg, cvg, conv_w)


def _adam_math(w, g, m, v):
    nm = ADAM_B1 * m + (1.0 - ADAM_B1) * g
    nv = ADAM_B2 * v + (1.0 - ADAM_B2) * (g * g)
    delta = -ADAM_LR * ((nm * ADAM_C1) / (jnp.sqrt(nv * ADAM_C2) + ADAM_EPS) + ADAM_WD * w)
    return delta, nm, nv


def _adamw(w, gslots, m, v, name, tb):
    R, C = w.shape
    S = gslots.shape[0]

    def body(w_ref, gs_ref, m_ref, v_ref, g_ref, d_ref, nm_ref, nv_ref):
        g = gs_ref[0].astype(F32)
        for s in range(1, S):
            g = g + gs_ref[s].astype(F32)
        g_ref[...] = g
        d_ref[...], nm_ref[...], nv_ref[...] = _adam_math(w_ref[...], g, m_ref[...], v_ref[...])

    blk = pl.BlockSpec((tb, C), lambda i: (i, 0))
    return _pcall(body, name, grid=(R // tb,),
                  in_specs=[blk, pl.BlockSpec((S, tb, C), lambda i: (0, i, 0)), blk, blk],
                  out_specs=[blk] * 4, out_shape=[jax.ShapeDtypeStruct((R, C), F32)] * 4,
                  compiler_params=_cp(("parallel",)))(w, gslots, m, v)


def _adamw_small(gall, gattn, gconvw, ws, ms, vs):
    n = len(ws)

    def body(*refs):
        gall_ref, gattn_ref, gconvw_ref = refs[:3]
        w_refs, m_refs, v_refs = refs[3:3 + n], refs[3 + n:3 + 2 * n], refs[3 + 2 * n:3 + 3 * n]
        loss_ref = refs[3 + 3 * n]
        outs = refs[4 + 3 * n:]
        g_refs, d_refs, nm_refs, nv_refs = outs[:n], outs[n:2 * n], outs[2 * n:3 * n], outs[3 * n:]

        def total(ref):
            t = ref[0]
            for dev in range(1, NDEV):
                t = t + ref[dev]
            return t

        tot = total(gall_ref)
        grads = [tot[0:9, :]] + [tot[ROW_GAINS + k:ROW_GAINS + k + 1, :] for k in range(6)]
        grads += [total(gattn_ref), tot[ROW_ATTN_CB:ROW_ATTN_CB + 1, CW:2 * CW], tot[ROW_LN:ROW_LN + 1, 0:CW],
                  tot[ROW_LN:ROW_LN + 1, CW:2 * CW], total(gconvw_ref)]
        loss_ref[...] = tot[ROW_LOSS:ROW_LOSS + 1, 0:1]
        for k in range(n):
            g_refs[k][...] = grads[k]
            d_refs[k][...], nm_refs[k][...], nv_refs[k][...] = _adam_math(w_refs[k][...], grads[k], m_refs[k][...],
                                                                          v_refs[k][...])

    shapes = [jax.ShapeDtypeStruct(w.shape, F32) for w in ws]
    res = _pcall(body, "adamw_small", out_shape=[jax.ShapeDtypeStruct((1, 1), F32)] + shapes * 4,
                 compiler_params=_cp())(gall, gattn, gconvw, *ws, *ms, *vs)
    return res[0], [res[1 + k * n:1 + (k + 1) * n] for k in range(4)]


def _ffn_fwd(x, g_pre, g_post, shift, scale, gate, w_in, w_out4, tag, tm, comm=None, tgt=None):
    h, gu, a, got = _ffn_in(x, g_pre, shift, scale, w_in, "ffn_in_" + tag, comm)
    res = _mm_post([a], [pl.BlockSpec((NSL, tm, SL), lambda i: (0, i, 0))],
                   [w_out4], [pl.BlockSpec((NSL, SL, D), lambda i: (0, 0, 0))],
                   x, g_post, gate, 0.5, "ffn_out_" + tag, tm, tgt)
    return (res[1] if tgt is None else (res[1], res[2])), (x, h, gu, a, res[0]), got


def _ffn_bwd(dout, saved, g_pre, g_post, scale, gate, w_in, w_out4, tag, tmb, tmw, send_in=True, carry=None):
    x, h, gu, a, f = saved
    T = x.shape[0]
    (df, dgate, dg_post, dgu), carried = _ffn_out_bwd(dout, f, g_post, gate, 0.5, w_out4, gu,
                                                      "ffn_out_bwd_" + tag, carry)
    dw_out, _ = _mm_tn(a, pl.BlockSpec((1, tmw, SL), lambda j, i: (j, i, 0)),
                       df, pl.BlockSpec((tmw, D), lambda j, i: (i, 0)),
                       (NSL, SL, D), pl.BlockSpec((1, SL, D), lambda j, i: (j, 0, 0)), (NSL, T // tmw), "dw_out_" + tag)
    dw_in, (r_out,) = _dw_in(h, dgu, "dw_in_" + tag, tmw, comm=("a2a", [dw_out.reshape(NDEV, SL // 2, D)]))
    dw_in = dw_in.reshape(NDEV, SL, D)

    def dh_fn(a_ref, w_ref):
        dh = None
        for p in range(2):
            for j in range(NSL):
                t = _dot(a_ref[j, p], w_ref[NSL * p + j])
                dh = t if dh is None else dh + t
        return dh

    (dx, dshift, dscale, dg_pre), got = _mm_prebwd(
        dgu, pl.BlockSpec((NSL, 2, tmb, SL), lambda i: (0, 0, i, 0)),
        w_in, pl.BlockSpec((NDEV, SL, D), lambda i: (0, 0, 0), pipeline_mode=pl.Buffered(1)),
        dh_fn, x, dout, g_pre, scale, "ffn_in_bwd_" + tag, tmb, comm=("a2a", [dw_in]) if send_in else None)
    return dx, got[0] if send_in else dw_in, r_out, dg_pre, dg_post, (dshift, dscale, dgate), carried


def kernel(x, c, w_ada, b_ada, g_pre_ff1, g_post_ff1, ff1_w_in, ff1_w_out, g_pre_mix, g_post_mix, w_in_mix, g_attn_out, conv_w, conv_b, conv_ln_g, conv_ln_b, w_out_mix, g_pre_ff2, g_post_ff2, ff2_w_in, ff2_w_out, loss_target, m_w_ada, m_b_ada, m_g_pre_ff1, m_g_post_ff1, m_ff1_w_in, m_ff1_w_out, m_g_pre_mix, m_g_post_mix, m_w_in_mix, m_g_attn_out, m_conv_w, m_conv_b, m_conv_ln_g, m_conv_ln_b, m_w_out_mix, m_g_pre_ff2, m_g_post_ff2, m_ff2_w_in, m_ff2_w_out, v_w_ada, v_b_ada, v_g_pre_ff1, v_g_post_ff1, v_ff1_w_in, v_ff1_w_out, v_g_pre_mix, v_g_post_mix, v_w_in_mix, v_g_attn_out, v_conv_w, v_conv_b, v_conv_ln_g, v_conv_ln_b, v_w_out_mix, v_g_pre_ff2, v_g_post_ff2, v_ff2_w_in, v_ff2_w_out):
    me = 4 * lax.axis_index("x") + 2 * lax.axis_index("y") + lax.axis_index("c")
    T = x.shape[1]
    tq = min(256, T)
    tm = 512
    tmb = 512
    tmw = 2048
    x0 = x.reshape(T, D)
    tgt = loss_target.reshape(T, D)
    row = lambda a: a.reshape(1, -1)

    small_in = jnp.concatenate([c.reshape(-1), jnp.pad(conv_w.reshape(-1), (0, 2 * D - CK * 64)),
                                jnp.zeros((5 * D,), F32)]).reshape(8, D)
    small_all, = _all_gather([small_in], "gather_c_convw", True)
    c_all = small_all[:, 0, :]
    conv_w_full = small_all[:, 1:3, :].reshape(NDEV, 2 * D)[:, :CK * 64].reshape(NDEV, CK, 64)
    conv_w_full = conv_w_full.transpose(1, 0, 2).reshape(CK, CW)
    conv_w_pad = jnp.pad(conv_w_full, ((0, HALO - CK), (0, 0)))

    big = [ff1_w_in.T, ff1_w_out, w_in_mix.T, w_out_mix, ff2_w_in.T, ff2_w_out]
    shards = [w.astype(BF16) for w in big]
    w_in1, w_out1 = _all_gather(shards[0:2], "gather_weights_ff1", False)
    w_out1_4 = w_out1.reshape(NSL, SL, D)

    b_cols = lax.dynamic_slice(b_ada, (me * ADA_COLS,), (ADA_COLS,)).reshape(1, ADA_COLS)
    mod_cols = _ada_fwd(c_all, w_ada, b_cols)
    mod_all, = _all_gather([mod_cols], "gather_mod", True)
    mod = lax.dynamic_slice(mod_all, (0, me, 0), (NDEV, 1, ADA_COLS)).reshape(9, D)
    sh = lambda s: mod[3 * s:3 * s + 1]
    sc = lambda s: mod[3 * s + 1:3 * s + 2]
    gt = lambda s: mod[3 * s + 2:3 * s + 3]

    x1, sv1, (w_inm_s, w_outm_s) = _ffn_fwd(x0, row(g_pre_ff1), row(g_post_ff1), sh(0), sc(0), gt(0), w_in1, w_out1_4,
                                            "ff1", tm, comm=("gather", shards[2:4]))
    w_inm = w_inm_s.reshape(MIXIN, D)
    w_outm = w_outm_s.reshape(D, D)
    hm, qkv, cvg = _mix_in(x1, row(g_pre_mix), sh(1), sc(1), w_inm, "mix_in")
    g_attn_row = row(g_attn_out)
    o_att, an, (w_in2, w_out2) = _attn_fwd(qkv, g_attn_row, tq, comm=("gather", shards[4:6]))
    w_out2_4 = w_out2.reshape(NSL, SL, D)
    u0, u1, u3 = _conv_fwd(cvg, conv_w_pad, row(conv_b), row(conv_ln_g), row(conv_ln_b))
    half = lambda k: pl.BlockSpec((AW, D), lambda i: (k, 0))
    act = pl.BlockSpec((tm, AW), lambda i: (i, 0))
    fm, x2 = _mm_post([an, u3], [act, act], [w_outm, w_outm], [half(0), half(1)],
                      x1, row(g_post_mix), gt(1), 1.0, "mix_out", tm)
    (dy, loss_part), sv2, _ = _ffn_fwd(x2, row(g_pre_ff2), row(g_post_ff2), sh(2), sc(2), gt(2), w_in2, w_out2_4,
                                       "ff2", tm, tgt=tgt)

    dx2, dw_in2, r_out2, dgpre2, dgpost2, dmod2, _ = _ffn_bwd(
        dy, sv2, row(g_pre_ff2), row(g_post_ff2), sc(2), gt(2), w_in2, w_out2_4, "ff2", tmb, tmw, send_in=False)

    dfm, dgate1, dgpostm, dcat = _mix_out_bwd(dx2, fm, row(g_post_mix), gt(1), 1.0, w_outm, "mix_out_bwd")
    tok = pl.BlockSpec((tmw, AW), lambda j, i: (i, 0))
    tokd = pl.BlockSpec((tmw, D), lambda j, i: (i, 0))
    whole = pl.BlockSpec((AW, D), lambda j, i: (0, 0))
    dw_outm = jnp.concatenate([_mm_tn(an, tok, dfm, tokd, (AW, D), whole, (1, T // tmw), "dw_out_mix_a")[0],
                               _mm_tn(u3, tok, dfm, tokd, (AW, D), whole, (1, T // tmw), "dw_out_mix_c")[0]], axis=0)
    (dq, dk, dv, dg_attn), (r_in2,) = _attn_bwd(qkv, o_att, dcat, g_attn_row, tq, comm=("a2a", [dw_in2]))
    du1, dconv_w, dconv_b, dln_g, dln_b = _conv_bwd1(dcat, u1, u0, row(conv_ln_g), row(conv_ln_b))
    dcvg = _conv_bwd2(du1, cvg, conv_w_pad)
    dproj = jnp.concatenate([dq, dk.astype(BF16), dv.astype(BF16), dcvg], axis=1)
    dw_inm, _ = _mm_tn(dproj, pl.BlockSpec((tmw, MIXIN // 2), lambda j, i: (i, j)),
                       hm, pl.BlockSpec((tmw, D), lambda j, i: (i, 0)),
                       (MIXIN, D), pl.BlockSpec((MIXIN // 2, D), lambda j, i: (j, 0)), (2, T // tmw), "dw_in_mix")
    (dx1, dshift1, dscale1, dgprem), _ = _mm_prebwd(
        dproj, pl.BlockSpec((tmb, MIXIN), lambda i: (i, 0)), w_inm, pl.BlockSpec((MIXIN, D), lambda i: (0, 0)),
        lambda a_ref, w_ref: _dot(a_ref[...], w_ref[...]), x1, dx2, row(g_pre_mix), sc(1), "mix_in_bwd", tmb)

    dx0, r_in1, r_out1, dgpre1, dgpost1, dmod0, (r_inm, r_outm) = _ffn_bwd(
        dx1, sv1, row(g_pre_ff1), row(g_post_ff1), sc(0), gt(0), w_in1, w_out1_4, "ff1", tmb, tmw,
        carry=("a2a", [dw_inm.reshape(NDEV, 320, D), dw_outm.reshape(NDEV, 128, D)]))
    recvs = [r_in1, r_out1, r_inm, r_outm, r_in2, r_out2]

    zrow = jnp.zeros((1, D), F32)
    small_g = jnp.concatenate(
        list(dmod0) + [dshift1, dscale1, dgate1] + list(dmod2)
        + [dgpre1, dgpost1, dgprem, dgpostm, dgpre2, dgpost2]
        + [jnp.concatenate([dg_attn, dconv_b], axis=1), jnp.concatenate([dln_g, dln_b], axis=1),
           jnp.pad(dconv_w[:CK].reshape(-1), (0, CONVW_ROWS * D - CK * CW)).reshape(CONVW_ROWS, D),
           jnp.pad(loss_part, ((0, 0), (0, D - 1)))] + [zrow] * (SMALL_R - ROW_LOSS - 1), axis=0)
    small_g_all, = _all_gather([small_g], "gather_small_grads", True)

    dmod_all = small_g_all[:, 0:9, :].reshape(NDEV, NMOD)
    dmod_cols = lax.dynamic_slice(dmod_all, (0, me * ADA_COLS), (NDEV, ADA_COLS))
    g_w_ada = _ada_bwd(c_all.T, dmod_cols)

    gattn = small_g_all[:, ROW_ATTN_CB, 0:AW].reshape(NDEV, 8, HD)
    gconvw = small_g_all[:, ROW_CONVW:ROW_CONVW + CONVW_ROWS, :].reshape(NDEV, CONVW_ROWS * D)[:, :CK * CW]
    gconvw = lax.dynamic_slice(gconvw.reshape(NDEV, CK, CW), (0, 0, me * 64), (NDEV, CK, 64))

    def small_list(b, g6, ga, cb, lg, lb, cw):
        return [b.reshape(9, D)] + [row(g) for g in g6] + [ga, row(cb), row(lg), row(lb), cw]

    sw = small_list(b_ada, [g_pre_ff1, g_post_ff1, g_pre_mix, g_post_mix, g_pre_ff2, g_post_ff2], g_attn_out,
                    conv_b, conv_ln_g, conv_ln_b, conv_w)
    sm = small_list(m_b_ada, [m_g_pre_ff1, m_g_post_ff1, m_g_pre_mix, m_g_post_mix, m_g_pre_ff2, m_g_post_ff2],
                    m_g_attn_out, m_conv_b, m_conv_ln_g, m_conv_ln_b, m_conv_w)
    sv = small_list(v_b_ada, [v_g_pre_ff1, v_g_post_ff1, v_g_pre_mix, v_g_post_mix, v_g_pre_ff2, v_g_post_ff2],
                    v_g_attn_out, v_conv_b, v_conv_ln_g, v_conv_ln_b, v_conv_w)
    loss, s_out = _adamw_small(small_g_all, gattn, gconvw, sw, sm, sv)
    s_out = [[o.reshape(w.shape) for o, w in zip(outs, [b_ada, g_pre_ff1, g_post_ff1, g_pre_mix, g_post_mix,
                                                        g_pre_ff2, g_post_ff2, g_attn_out, conv_b, conv_ln_g,
                                                        conv_ln_b, conv_w])] for outs in s_out]

    big_m = [m_ff1_w_in.T, m_ff1_w_out, m_w_in_mix.T, m_w_out_mix, m_ff2_w_in.T, m_ff2_w_out]
    big_v = [v_ff1_w_in.T, v_ff1_w_out, v_w_in_mix.T, v_w_out_mix, v_ff2_w_in.T, v_ff2_w_out]
    tbs = [352, 176, 160, 128, 352, 176]
    tags = ["ff1_w_in", "ff1_w_out", "w_in_mix", "w_out_mix", "ff2_w_in", "ff2_w_out"]
    b_out = [_adamw(big[k], recvs[k], big_m[k], big_v[k], "adamw_" + tags[k], tbs[k]) for k in range(6)]
    b_out = [[o.T for o in outs] if k % 2 == 0 else outs for k, outs in enumerate(b_out)]
    a_out = _adamw(w_ada, g_w_ada.reshape(1, D, ADA_COLS), m_w_ada, v_w_ada, "adamw_ada", 256)

    def leaves(k):
        s = s_out[k]
        b = [o[k] for o in b_out]
        return [a_out[k], s[0], s[1], s[2], b[0], b[1], s[3], s[4], b[2], s[7], s[11], s[8], s[9], s[10], b[3],
                s[5], s[6], b[4], b[5]]

    return (loss.reshape(()), dx0.reshape(1, T, D), *leaves(0), *leaves(1), *leaves(2), *leaves(3))
```

```python
import functools

import jax
import jax.numpy as jnp
from jax import lax
from jax.experimental import pallas as pl
from jax.experimental.pallas import tpu as pltpu

F32 = jnp.float32
BF16 = jnp.bfloat16
D = 1024
DFF = 2816
SL = 704
NSL = DFF // SL
AW = 512
HD = 64
CW = 512
CK = 31
HALO = 32
MIXIN = 2560
NDEV = 8
NMOD = 9 * D
ADA_COLS = NMOD // NDEV
RMS_EPS = 1e-6
LN_EPS = 1e-5
QK_SCALE = HD ** -0.5
W_ZERO_BELOW = -104.0
ADAM_LR, ADAM_B1, ADAM_B2, ADAM_EPS, ADAM_WD, ADAM_STEP = 0.001, 0.9, 0.999, 1e-08, 0.01, 10
ADAM_C1 = 1.0 / (1.0 - ADAM_B1 ** ADAM_STEP)
ADAM_C2 = 1.0 / (1.0 - ADAM_B2 ** ADAM_STEP)
MIB = 1024 * 1024
MESH = pl.DeviceIdType.MESH

ROW_GAINS = 9
ROW_ATTN_CB = 15
ROW_LN = 16
ROW_CONVW = 17
CONVW_ROWS = 16
ROW_LOSS = 33
SMALL_R = 40


def _pcall(body, name, **kw):
    return pl.pallas_call(body, name=name, **kw)


def _cp(sem=None, vmem_mib=48):
    if sem is None:
        return pltpu.CompilerParams(vmem_limit_bytes=vmem_mib * MIB)
    return pltpu.CompilerParams(dimension_semantics=sem, vmem_limit_bytes=vmem_mib * MIB)


def _dot(a, b):
    return jnp.dot(a, b, preferred_element_type=F32)


def _dot_nt(a, b):
    return lax.dot_general(a, b, (((1,), (1,)), ((), ())), preferred_element_type=F32)


def _dot_tn(a, b):
    return lax.dot_general(a, b, (((0,), (0,)), ((), ())), preferred_element_type=F32)


def _sigmoid(x):
    return 0.5 * jnp.tanh(0.5 * x) + 0.5


def _split2(x):
    hi = x.astype(BF16)
    mid = (x - hi.astype(F32)).astype(BF16)
    return hi, mid


def _mat(ref):
    lead = len(ref.shape) - 2
    return ref[(0,) * lead] if lead else ref[...]


def _all_gather(xs, name, in_vmem):
    n = len(xs)

    def body(*refs):
        x_refs, out_refs = refs[:n], refs[n:2 * n]
        send_sems, recv_sems, local_sems = refs[2 * n:]
        mx, my, mc = lax.axis_index("x"), lax.axis_index("y"), lax.axis_index("c")
        me, sibling = (mx, my, mc), (mx, my, 1 - mc)
        chips = [(1 - mx, my), (mx, 1 - my), (1 - mx, 1 - my)]

        def slab(a, px, py, pc):
            return out_refs[a].at[4 * px + 2 * py + pc]

        def copy(a, k, block, to, src=None):
            return pltpu.make_async_remote_copy(
                src_ref=slab(a, *block) if src is None else src, dst_ref=slab(a, *block),
                send_sem=send_sems.at[a, k], recv_sem=recv_sems.at[a, k], device_id=to, device_id_type=MESH)

        mine = [pltpu.make_async_copy(x_refs[a], slab(a, *me), local_sems.at[a]) for a in range(n)]
        for cp in mine:
            cp.start()
        first = []
        for a in range(n):
            first.append(copy(a, 0, me, sibling, src=x_refs[a]))
            first += [copy(a, 1 + j, me, (*chip, mc), src=x_refs[a]) for j, chip in enumerate(chips)]
        for cp in first:
            cp.start()
        passed = []
        for j, chip in enumerate(chips):
            for a in range(n):
                copy(a, 1 + j, (*chip, mc), me).wait_recv()
                passed.append(copy(a, 4 + j, (*chip, mc), sibling))
                passed[-1].start()
        for a in range(n):
            copy(a, 0, sibling, me).wait_recv()
            for j, chip in enumerate(chips):
                copy(a, 4 + j, (*chip, 1 - mc), me).wait_recv()
        for cp in first + passed:
            cp.wait_send()
        for cp in mine:
            cp.wait()

    space = pltpu.VMEM if in_vmem else pl.ANY
    return _pcall(
        body, name,
        out_shape=[jax.ShapeDtypeStruct((NDEV,) + x.shape, x.dtype) for x in xs],
        in_specs=[pl.BlockSpec(memory_space=space)] * n,
        out_specs=[pl.BlockSpec(memory_space=space)] * n,
        scratch_shapes=[pltpu.SemaphoreType.DMA((n, 7)), pltpu.SemaphoreType.DMA((n, 7)),
                        pltpu.SemaphoreType.DMA((n,))],
    )(*xs)


def _exchange_copies(kind, src, dst, send_sems, recv_sems, local_sems):
    mx, my, mc = lax.axis_index("x"), lax.axis_index("y"), lax.axis_index("c")
    me = 4 * mx + 2 * my + mc
    n = len(src)
    pick = (lambda a, p: src[a].at[p]) if kind == "a2a" else (lambda a, p: src[a])
    mine = [pltpu.make_async_copy(pick(a, me), dst[a].at[me], local_sems.at[a]) for a in range(n)]
    copies = []
    for r in range(1, NDEV):
        px = 1 - mx if r & 4 else mx
        py = 1 - my if r & 2 else my
        pc = 1 - mc if r & 1 else mc
        for a in range(n):
            copies.append(pltpu.make_async_remote_copy(
                src_ref=pick(a, 4 * px + 2 * py + pc), dst_ref=dst[a].at[me],
                send_sem=send_sems.at[a, r - 1], recv_sem=recv_sems.at[a, r - 1],
                device_id=(px, py, pc), device_id_type=MESH))
    return mine, copies


def _hosted_call(body, name, comm, grid, in_specs, out_specs, out_shape, scratch_shapes, sem, vmem_mib, args):
    if comm is None:
        outs = _pcall(body, name, grid=grid, in_specs=in_specs, out_specs=out_specs, out_shape=out_shape,
                      scratch_shapes=scratch_shapes, compiler_params=_cp(sem, vmem_mib))(*args)
        return outs, []
    kind, arrs = comm
    nc, n_in, n_out, n_scr = len(arrs), len(in_specs), len(out_specs), len(scratch_shapes)
    rank = len(grid)

    def wrapped(*refs):
        ins, csrc = refs[:n_in], refs[n_in:n_in + nc]
        outs, cdst = refs[n_in + nc:n_in + nc + n_out], refs[n_in + nc + n_out:n_in + 2 * nc + n_out]
        rest = refs[n_in + 2 * nc + n_out:]
        scr, sems = rest[:n_scr], rest[n_scr:]
        first = functools.reduce(jnp.logical_and, [pl.program_id(d) == 0 for d in range(rank)])
        last = functools.reduce(jnp.logical_and, [pl.program_id(d) == grid[d] - 1 for d in range(rank)])

        @pl.when(first)
        def _():
            mine, copies = _exchange_copies(kind, csrc, cdst, *sems)
            for cp in mine + copies:
                cp.start()

        body(*ins, *outs, *scr)

        @pl.when(last)
        def _():
            mine, copies = _exchange_copies(kind, csrc, cdst, *sems)
            for cp in copies:
                cp.wait_recv()
            for cp in copies:
                cp.wait_send()
            for cp in mine:
                cp.wait()

    hbm = pl.BlockSpec(memory_space=pl.ANY)
    cshape = [jax.ShapeDtypeStruct(a.shape if kind == "a2a" else (NDEV,) + a.shape, a.dtype) for a in arrs]
    res = _pcall(wrapped, name, grid=grid, in_specs=list(in_specs) + [hbm] * nc,
                 out_specs=list(out_specs) + [hbm] * nc, out_shape=list(out_shape) + cshape,
                 scratch_shapes=list(scratch_shapes) + [pltpu.SemaphoreType.DMA((nc, 7)),
                                                        pltpu.SemaphoreType.DMA((nc, 7)),
                                                        pltpu.SemaphoreType.DMA((nc,))],
                 compiler_params=_cp(("arbitrary",) * rank, vmem_mib))(*args, *arrs)
    return res[:n_out], res[n_out:]


def _ada_fwd(c_all, w, b):
    n = w.shape[1]

    def body(c_ref, w_ref, b_ref, o_ref):
        c = c_ref[...]
        s = c * _sigmoid(c)
        o_ref[...] = jnp.dot(s, w_ref[...], preferred_element_type=F32, precision=lax.Precision.HIGHEST) + b_ref[...]

    return _pcall(body, "ada_fwd", out_shape=jax.ShapeDtypeStruct((NDEV, n), F32), compiler_params=_cp())(c_all, w, b)


def _ada_bwd(c_all_t, dmod):
    n = dmod.shape[1]

    def body(ct_ref, d_ref, o_ref):
        ct = ct_ref[...]
        s = ct * _sigmoid(ct)
        acc = s[:, 0:1] * d_ref[0:1, :]
        for b in range(1, NDEV):
            acc = acc + s[:, b:b + 1] * d_ref[b:b + 1, :]
        o_ref[...] = acc

    return _pcall(body, "ada_bwd", out_shape=jax.ShapeDtypeStruct((D, n), F32), compiler_params=_cp())(c_all_t, dmod)


def _ffn_in(x, g_pre, shift, scale, w_in, name, comm=None, tm=512):
    T = x.shape[0]

    def body(x_ref, g_ref, sh_ref, sc_ref, w_ref, h_ref, gu_ref, a_ref):
        xv = x_ref[...]
        r = lax.rsqrt(jnp.mean(xv * xv, axis=-1, keepdims=True) + RMS_EPS)
        hv = ((xv * r) * g_ref[...] * (1.0 + sc_ref[...]) + sh_ref[...]).astype(BF16)
        h_ref[...] = hv
        for j in range(NSL):
            g = _dot_nt(hv, w_ref[j])
            u = _dot_nt(hv, w_ref[j + NSL])
            gu_ref[j, 0] = g.astype(BF16)
            gu_ref[j, 1] = u.astype(BF16)
            a_ref[j] = (g * _sigmoid(g) * u).astype(BF16)

    row = pl.BlockSpec((tm, D), lambda i: (i, 0))
    vec = pl.BlockSpec((1, D), lambda i: (0, 0))
    (h, gu, a), got = _hosted_call(
        body, name, comm, (T // tm,),
        [row, vec, vec, vec, pl.BlockSpec((NDEV, SL, D), lambda i: (0, 0, 0), pipeline_mode=pl.Buffered(1))],
        [row, pl.BlockSpec((NSL, 2, tm, SL), lambda i: (0, 0, i, 0)), pl.BlockSpec((NSL, tm, SL), lambda i: (0, i, 0))],
        [jax.ShapeDtypeStruct((T, D), BF16), jax.ShapeDtypeStruct((NSL, 2, T, SL), BF16),
         jax.ShapeDtypeStruct((NSL, T, SL), BF16)],
        [], ("parallel",), 48, (x, g_pre, shift, scale, w_in))
    return h, gu, a, got


def _mm_post(a_list, a_specs, w_list, w_specs, x, g_post, gate, res_w, name, tm, tgt=None):
    T = x.shape[0]
    n = len(a_list)
    with_loss = tgt is not None

    def body(*refs):
        a_refs, w_refs = refs[:n], refs[n:2 * n]
        x_ref, g_ref, gt_ref = refs[2 * n:2 * n + 3]
        rest = refs[2 * n + 3:]
        f = None
        for a_ref, w_ref in zip(a_refs, w_refs):
            if len(a_ref.shape) == 3:
                terms = [_dot(a_ref[j], w_ref[j]) for j in range(a_ref.shape[0])]
            else:
                terms = [_dot(a_ref[...], w_ref[...])]
            for t in terms:
                f = t if f is None else f + t
        r = lax.rsqrt(jnp.mean(f * f, axis=-1, keepdims=True) + RMS_EPS)
        y = (f * r) * g_ref[...]
        out = x_ref[...] + (res_w * (1.0 + gt_ref[...])) * y
        if with_loss:
            t_ref, f_ref, dy_ref, l_ref = rest

            @pl.when(pl.program_id(0) == 0)
            def _():
                l_ref[...] = jnp.zeros_like(l_ref)

            e = out - t_ref[...]
            dy_ref[...] = e * (1.0 / D)
            l_ref[...] += 0.5 * jnp.sum(jnp.mean(e * e, axis=-1, keepdims=True), axis=0, keepdims=True)
        else:
            f_ref, o_ref = rest
            o_ref[...] = out
        f_ref[...] = f

    row = pl.BlockSpec((tm, D), lambda i: (i, 0))
    vec = pl.BlockSpec((1, D), lambda i: (0, 0))
    big = jax.ShapeDtypeStruct((T, D), F32)
    if with_loss:
        return _pcall(body, name, grid=(T // tm,),
                      in_specs=list(a_specs) + list(w_specs) + [row, vec, vec, row],
                      out_specs=[row, row, pl.BlockSpec((1, 1), lambda i: (0, 0))],
                      out_shape=[big, big, jax.ShapeDtypeStruct((1, 1), F32)],
                      compiler_params=_cp(("arbitrary",)))(*a_list, *w_list, x, g_post, gate, tgt)
    return _pcall(body, name, grid=(T // tm,),
                  in_specs=list(a_specs) + list(w_specs) + [row, vec, vec], out_specs=[row, row],
                  out_shape=[big, big], compiler_params=_cp(("parallel",)))(*a_list, *w_list, x, g_post, gate)


def _ffn_out_bwd(dout, f, g_post, gate, res_w, w_out4, gu, name, comm=None, tm=512):
    T = f.shape[0]

    def body(do_ref, f_ref, g_ref, gt_ref, w_ref, gu_ref, df_ref, dgate_ref, dg_ref, dgu_ref):
        @pl.when(pl.program_id(0) == 0)
        def _():
            dgate_ref[...] = jnp.zeros_like(dgate_ref)
            dg_ref[...] = jnp.zeros_like(dg_ref)

        do = do_ref[...]
        f = f_ref[...]
        r = lax.rsqrt(jnp.mean(f * f, axis=-1, keepdims=True) + RMS_EPS)
        fn = f * r
        dgate_ref[...] += jnp.sum((res_w * do) * (fn * g_ref[...]), axis=0, keepdims=True)
        dy = (res_w * (1.0 + gt_ref[...])) * do
        dg_ref[...] += jnp.sum(dy * fn, axis=0, keepdims=True)
        dyg = dy * g_ref[...]
        dfv = (r * (dyg - fn * jnp.mean(dyg * fn, axis=-1, keepdims=True))).astype(BF16)
        df_ref[...] = dfv
        for j in range(NSL):
            da = _dot_nt(dfv, w_ref[j])
            gv = gu_ref[j, 0].astype(F32)
            s = _sigmoid(gv)
            gs = gv * s
            dgu_ref[j, 0] = (da * gu_ref[j, 1].astype(F32) * (s + gs * (1.0 - s))).astype(BF16)
            dgu_ref[j, 1] = (da * gs).astype(BF16)

    row = pl.BlockSpec((tm, D), lambda i: (i, 0))
    vec = pl.BlockSpec((1, D), lambda i: (0, 0))
    gus = pl.BlockSpec((NSL, 2, tm, SL), lambda i: (0, 0, i, 0))
    return _hosted_call(
        body, name, comm, (T // tm,),
        [row, row, vec, vec, pl.BlockSpec((NSL, SL, D), lambda i: (0, 0, 0), pipeline_mode=pl.Buffered(1)), gus],
        [row, vec, vec, gus],
        [jax.ShapeDtypeStruct((T, D), BF16), jax.ShapeDtypeStruct((1, D), F32), jax.ShapeDtypeStruct((1, D), F32),
         jax.ShapeDtypeStruct((NSL, 2, T, SL), BF16)], [], ("arbitrary",), 48, (dout, f, g_post, gate, w_out4, gu))


def _mm_tn(a, a_spec, b, b_spec, out_shape, out_spec, grid, name, comm=None):
    k, nn = out_spec.block_shape[-2:]
    steps = grid[1]

    def body(a_ref, b_ref, o_ref, acc_ref):
        i = pl.program_id(1)

        @pl.when(i == 0)
        def _():
            acc_ref[...] = jnp.zeros_like(acc_ref)

        acc_ref[...] += _dot_tn(_mat(a_ref), _mat(b_ref))

        @pl.when(i == steps - 1)
        def _():
            lead = len(o_ref.shape) - 2
            o_ref[(0,) * lead if lead else ...] = acc_ref[...].astype(BF16)

    (out,), got = _hosted_call(body, name, comm, grid, [a_spec, b_spec], [out_spec],
                               [jax.ShapeDtypeStruct(out_shape, BF16)], [pltpu.VMEM((k, nn), F32)],
                               ("parallel", "arbitrary"), 48, (a, b))
    return out, got


def _dw_in(h, dgu, name, tmw, comm=None):
    T = h.shape[0]
    steps = T // tmw

    def body(h_ref, b_ref, o_ref, acc_ref):
        i = pl.program_id(1)

        @pl.when(i == 0)
        def _():
            acc_ref[...] = jnp.zeros_like(acc_ref)

        hv = h_ref[...]
        for p in range(2):
            acc_ref[p] += _dot_tn(b_ref[0, p], hv)

        @pl.when(i == steps - 1)
        def _():
            o_ref[:, 0] = acc_ref[...].astype(BF16)

    (out,), got = _hosted_call(
        body, name, comm, (NSL, steps),
        [pl.BlockSpec((tmw, D), lambda j, i: (i, 0)), pl.BlockSpec((1, 2, tmw, SL), lambda j, i: (j, 0, i, 0))],
        [pl.BlockSpec((2, 1, SL, D), lambda j, i: (0, j, 0, 0))],
        [jax.ShapeDtypeStruct((2, NSL, SL, D), BF16)], [pltpu.VMEM((2, SL, D), F32)],
        ("parallel", "arbitrary"), 56, (h, dgu))
    return out, got


def _mm_prebwd(a, a_spec, w, w_spec, dh_fn, x, dout, g_pre, scale, name, tm=256, comm=None):
    T = x.shape[0]

    def body(a_ref, w_ref, x_ref, do_ref, g_ref, sc_ref, dx_ref, dsh_ref, dsc_ref, dg_ref):
        @pl.when(pl.program_id(0) == 0)
        def _():
            dsh_ref[...] = jnp.zeros_like(dsh_ref)
            dsc_ref[...] = jnp.zeros_like(dsc_ref)
            dg_ref[...] = jnp.zeros_like(dg_ref)

        dh = dh_fn(a_ref, w_ref)
        xv = x_ref[...]
        r = lax.rsqrt(jnp.mean(xv * xv, axis=-1, keepdims=True) + RMS_EPS)
        xn = xv * r
        dsh_ref[...] += jnp.sum(dh, axis=0, keepdims=True)
        dsc_ref[...] += jnp.sum(dh * (xn * g_ref[...]), axis=0, keepdims=True)
        dn = dh * (1.0 + sc_ref[...])
        dg_ref[...] += jnp.sum(dn * xn, axis=0, keepdims=True)
        dng = dn * g_ref[...]
        dx_ref[...] = do_ref[...] + r * (dng - xn * jnp.mean(dng * xn, axis=-1, keepdims=True))

    row = pl.BlockSpec((tm, D), lambda i: (i, 0))
    vec = pl.BlockSpec((1, D), lambda i: (0, 0))
    return _hosted_call(body, name, comm, (T // tm,), [a_spec, w_spec, row, row, vec, vec], [row, vec, vec, vec],
                        [jax.ShapeDtypeStruct((T, D), F32)] + [jax.ShapeDtypeStruct((1, D), F32)] * 3,
                        [], ("arbitrary",), 56, (a, w, x, dout, g_pre, scale))


def _mix_in(x, g_pre, shift, scale, w, name, tm=512):
    T = x.shape[0]

    def body(x_ref, g_ref, sh_ref, sc_ref, w_ref, h_ref, qkv_ref, cvg_ref):
        xv = x_ref[...]
        r = lax.rsqrt(jnp.mean(xv * xv, axis=-1, keepdims=True) + RMS_EPS)
        hv = ((xv * r) * g_ref[...] * (1.0 + sc_ref[...]) + sh_ref[...]).astype(BF16)
        h_ref[...] = hv
        p = _dot_nt(hv, w_ref[...])
        qkv_ref[...] = p[:, :3 * AW].astype(BF16)
        cvg_ref[...] = p[:, 3 * AW:]

    row = pl.BlockSpec((tm, D), lambda i: (i, 0))
    vec = pl.BlockSpec((1, D), lambda i: (0, 0))
    return _pcall(body, name, grid=(T // tm,),
                  in_specs=[row, vec, vec, vec, pl.BlockSpec((MIXIN, D), lambda i: (0, 0))],
                  out_specs=[row, pl.BlockSpec((tm, 3 * AW), lambda i: (i, 0)), pl.BlockSpec((tm, 2 * CW), lambda i: (i, 0))],
                  out_shape=[jax.ShapeDtypeStruct((T, D), BF16), jax.ShapeDtypeStruct((T, 3 * AW), BF16),
                             jax.ShapeDtypeStruct((T, 2 * CW), F32)],
                  compiler_params=_cp(("parallel",)))(x, g_pre, shift, scale, w)


def _mix_out_bwd(dout, f, g_post, gate, res_w, w, name, tm=512):
    T = f.shape[0]
    N = w.shape[0]

    def body(do_ref, f_ref, g_ref, gt_ref, w_ref, df_ref, dgate_ref, dg_ref, o_ref):
        @pl.when(pl.program_id(0) == 0)
        def _():
            dgate_ref[...] = jnp.zeros_like(dgate_ref)
            dg_ref[...] = jnp.zeros_like(dg_ref)

        do = do_ref[...]
        f = f_ref[...]
        r = lax.rsqrt(jnp.mean(f * f, axis=-1, keepdims=True) + RMS_EPS)
        fn = f * r
        dgate_ref[...] += jnp.sum((res_w * do) * (fn * g_ref[...]), axis=0, keepdims=True)
        dy = (res_w * (1.0 + gt_ref[...])) * do
        dg_ref[...] += jnp.sum(dy * fn, axis=0, keepdims=True)
        dyg = dy * g_ref[...]
        dfv = (r * (dyg - fn * jnp.mean(dyg * fn, axis=-1, keepdims=True))).astype(BF16)
        df_ref[...] = dfv
        o_ref[...] = _dot_nt(dfv, w_ref[...])

    row = pl.BlockSpec((tm, D), lambda i: (i, 0))
    vec = pl.BlockSpec((1, D), lambda i: (0, 0))
    return _pcall(body, name, grid=(T // tm,),
                  in_specs=[row, row, vec, vec, pl.BlockSpec((N, D), lambda i: (0, 0))],
                  out_specs=[row, vec, vec, pl.BlockSpec((tm, N), lambda i: (i, 0))],
                  out_shape=[jax.ShapeDtypeStruct((T, D), BF16), jax.ShapeDtypeStruct((1, D), F32),
                             jax.ShapeDtypeStruct((1, D), F32), jax.ShapeDtypeStruct((T, N), F32)],
                  compiler_params=_cp(("arbitrary",)))(dout, f, g_post, gate, w)


def _softplus_parts(z):
    ls = jnp.minimum(z, 0.0) - jnp.log(1.0 + jnp.exp(-jnp.abs(z)))
    return ls, ls - z


def _head_sum(x, first):
    sa = jnp.sum(jnp.where(first, x, 0.0), axis=-1, keepdims=True)
    sb = jnp.sum(jnp.where(first, 0.0, x), axis=-1, keepdims=True)
    return jnp.where(first, sa, sb)


def _attn_specs(T, tq):
    qs = pl.BlockSpec((tq, 128), lambda p, i: (i, p))
    ks = pl.BlockSpec((T, 128), lambda p, i: (0, 4 + p))
    vs = pl.BlockSpec((T, 128), lambda p, i: (0, 8 + p))
    gs = pl.BlockSpec((1, 128), lambda p, i: (0, p))
    return qs, ks, vs, gs


def _attn_fwd(qkv, g_attn, tq, comm=None):
    T = qkv.shape[0]

    def body(q_ref, k_ref, v_ref, g_ref, o_ref, an_ref):
        i = pl.program_id(1)
        first = lax.broadcasted_iota(jnp.int32, (tq, 128), 1) < HD
        q = (q_ref[...].astype(F32) * QK_SCALE).astype(BF16)
        zq = jnp.zeros_like(q)
        qs = (jnp.where(first, q, zq), jnp.where(first, zq, q))
        rows = lax.broadcasted_iota(jnp.int32, (tq, tq), 0)
        cols = lax.broadcasted_iota(jnp.int32, (tq, tq), 1)
        tri = (rows > cols).astype(BF16)
        tri2 = jnp.concatenate([tri, tri], axis=0)
        strict = cols < rows

        def tile(j, Rs, acc, masked):
            start = j * tq if isinstance(j, int) else pl.multiple_of(j * tq, tq)
            kb = k_ref[pl.ds(start, tq), :]
            vb = v_ref[pl.ds(start, tq), :]
            zs = [_dot_nt(qs[hh], kb) for hh in range(2)]
            parts = [_softplus_parts(z) for z in zs]
            lsms = [jnp.where(strict, p[1], 0.0) if masked else p[1] for p in parts]
            splits = [_split2(x) for x in lsms]
            afters = [_dot(jnp.concatenate(s, axis=1), tri2) for s in splits]
            ws = [jnp.exp(parts[hh][0] + afters[hh] + Rs[hh]) for hh in range(2)]
            if masked:
                ws = [jnp.where(strict, w, 0.0) for w in ws]
            outs = [_dot(w.astype(BF16), vb) for w in ws]
            new_r = [Rs[hh] + afters[hh][:, 0:1] + lsms[hh][:, 0:1] for hh in range(2)]
            return new_r[0], new_r[1], acc + jnp.where(first, outs[0], outs[1])

        zr = jnp.zeros((tq, 1), F32)

        def finish(acc):
            o_ref[...] = acc
            r = lax.rsqrt(_head_sum(acc * acc, first) * (1.0 / HD) + RMS_EPS)
            an_ref[...] = ((acc * r) * g_ref[...]).astype(BF16)

        @pl.when(i == 0)
        def _():
            finish(tile(0, (zr, zr), jnp.zeros((tq, 128), F32), True)[2])

        @pl.when(i > 0)
        def _():
            ra, rb, acc = tile(i, (zr, zr), jnp.zeros((tq, 128), F32), True)
            ra, rb, acc = tile(i - 1, (ra, rb), acc, False)

            def more(c):
                return jnp.logical_and(c[0] < i, jnp.maximum(jnp.max(c[1]), jnp.max(c[2])) > W_ZERO_BELOW)

            def step(c):
                ra, rb, acc = tile(i - 1 - c[0], (c[1], c[2]), c[3], False)
                return c[0] + 1, ra, rb, acc

            finish(lax.while_loop(more, step, (jnp.int32(1), ra, rb, acc))[3])

    qs, ks, vs, gs = _attn_specs(T, tq)
    (o, an), got = _hosted_call(body, "attn_fwd", comm, (AW // 128, T // tq), [qs, ks, vs, gs], [qs, qs],
                                [jax.ShapeDtypeStruct((T, AW), F32), jax.ShapeDtypeStruct((T, AW), BF16)],
                                [], ("parallel", "parallel"), 48, (qkv, qkv, qkv, g_attn))
    return o, an, got


def _attn_bwd(qkv, o, dcat, g_attn, tq, comm=None):
    T = qkv.shape[0]

    def body(q_ref, k_ref, v_ref, o_ref, dan_ref, g_ref, dq_ref, dk_ref, dv_ref, dg_ref):
        i = pl.program_id(1)

        @pl.when(i == 0)
        def _():
            dk_ref[...] = jnp.zeros_like(dk_ref)
            dv_ref[...] = jnp.zeros_like(dv_ref)
            dg_ref[...] = jnp.zeros_like(dg_ref)

        first = lax.broadcasted_iota(jnp.int32, (tq, 128), 1) < HD
        q = (q_ref[...].astype(F32) * QK_SCALE).astype(BF16)
        zq = jnp.zeros_like(q)
        qs = (jnp.where(first, q, zq), jnp.where(first, zq, q))
        o = o_ref[...]
        dan = dan_ref[...]
        r = lax.rsqrt(_head_sum(o * o, first) * (1.0 / HD) + RMS_EPS)
        on = o * r
        dg_ref[...] += jnp.sum(dan * on, axis=0, keepdims=True)
        dyg = dan * g_ref[...]
        dO = r * (dyg - on * (_head_sum(dyg * on, first) * (1.0 / HD)))
        dOb = dO.astype(BF16)
        dOs = (jnp.where(first, dOb, zq), jnp.where(first, zq, dOb))
        ones = jnp.ones((8, 128), BF16)
        Ds = []
        for hh in range(2):
            prod = dOs[hh].astype(F32) * o
            p1 = prod.astype(BF16)
            rem = prod - p1.astype(F32)
            p2 = rem.astype(BF16)
            p3 = (rem - p2.astype(F32)).astype(BF16)
            Ds.append((_dot_nt(ones, p1) + _dot_nt(ones, p2) + _dot_nt(ones, p3))[0:1, :])

        rows = lax.broadcasted_iota(jnp.int32, (tq, tq), 0)
        cols = lax.broadcasted_iota(jnp.int32, (tq, tq), 1)
        tri_after = (cols > rows).astype(BF16)
        tri_incl = (cols >= rows).astype(BF16)
        tri_after2 = jnp.concatenate([tri_after, tri_after], axis=1)
        tri_incl2 = jnp.concatenate([tri_incl, tri_incl], axis=1)
        strict = rows < cols

        def tile(j, Rs, Gs, dq, masked):
            start = j * tq if isinstance(j, int) else pl.multiple_of(j * tq, tq)
            kb = k_ref[pl.ds(start, tq), :]
            vb = v_ref[pl.ds(start, tq), :]
            H = range(2)
            parts = [_softplus_parts(_dot_nt(kb, qs[hh])) for hh in H]
            lsms = [jnp.where(strict, p[1], 0.0) if masked else p[1] for p in parts]
            splits = [_split2(x) for x in lsms]
            afters = [_dot(tri_after2, jnp.concatenate(s, axis=0)) for s in splits]
            ws = [jnp.exp(parts[hh][0] + afters[hh] + Rs[hh]) for hh in H]
            if masked:
                ws = [jnp.where(strict, w, 0.0) for w in ws]
            wbs = [w.astype(BF16) for w in ws]
            dlws = [_dot_nt(vb, dOs[hh]) * wbs[hh].astype(F32) for hh in H]
            splits2 = [_split2(x) for x in dlws]
            Cs = [_dot(tri_incl2, jnp.concatenate(s, axis=0)) for s in splits2]
            dlsms = [Ds[hh] - Gs[hh] - Cs[hh] for hh in H]
            if masked:
                dlsms = [jnp.where(strict, x, 0.0) for x in dlsms]
            ps = [jnp.exp(p[0]) for p in parts]
            dzs = [(dlws[hh] * (1.0 - ps[hh]) - dlsms[hh] * ps[hh]).astype(BF16) for hh in H]
            dkp = _dot(dzs[0], qs[0]) + _dot(dzs[1], qs[1])
            dvp = _dot(wbs[0], dOs[0]) + _dot(wbs[1], dOs[1])
            dq = dq + jnp.where(first, _dot_tn(dzs[0], kb), _dot_tn(dzs[1], kb))
            new_r = [Rs[hh] + afters[hh][0:1, :] + lsms[hh][0:1, :] for hh in H]
            new_g = [Gs[hh] + Cs[hh][0:1, :] for hh in H]
            dk_ref[pl.ds(start, tq), :] += dkp
            dv_ref[pl.ds(start, tq), :] += dvp
            return new_r[0], new_r[1], new_g[0], new_g[1], dq

        zrow = jnp.zeros((1, tq), F32)

        @pl.when(i == 0)
        def _():
            dq0 = tile(0, (zrow, zrow), (zrow, zrow), jnp.zeros((tq, 128), F32), True)[4]
            dq_ref[...] = (dq0 * QK_SCALE).astype(BF16)

        @pl.when(i > 0)
        def _():
            st = tile(i, (zrow, zrow), (zrow, zrow), jnp.zeros((tq, 128), F32), True)
            st = tile(i - 1, st[0:2], st[2:4], st[4], False)

            def more(c):
                return jnp.logical_and(c[0] < i, jnp.maximum(jnp.max(c[1]), jnp.max(c[2])) > W_ZERO_BELOW)

            def step(c):
                return (c[0] + 1,) + tile(i - 1 - c[0], (c[1], c[2]), (c[3], c[4]), c[5], False)

            dq_ref[...] = (lax.while_loop(more, step, (jnp.int32(1),) + st)[5] * QK_SCALE).astype(BF16)

    qs, ks, vs, gs = _attn_specs(T, tq)
    kacc = pl.BlockSpec((T, 128), lambda p, i: (0, p))
    return _hosted_call(body, "attn_bwd", comm, (AW // 128, T // tq), [qs, ks, vs, qs, qs, gs], [qs, kacc, kacc, gs],
                        [jax.ShapeDtypeStruct((T, AW), BF16), jax.ShapeDtypeStruct((T, AW), F32),
                         jax.ShapeDtypeStruct((T, AW), F32), jax.ShapeDtypeStruct((1, AW), F32)],
                        [], ("parallel", "arbitrary"), 48, (qkv, qkv, qkv, o, dcat, g_attn))


def _taps_by_phase(offsets):
    groups = {}
    for k, off in enumerate(offsets):
        groups.setdefault(off % 8, []).append((k, off // 8))
    return sorted(groups.items())


CONV_ROWS = 32


def _shifted_tap_sum(w_ref, pad_ref, ph_ref, out_ref, offsets, tb):
    groups = _taps_by_phase(offsets)

    def chunk(c, carry):
        r0 = pl.multiple_of(c * CONV_ROWS, CONV_ROWS)
        acc = None
        for p, taps in groups:
            n = CONV_ROWS if p == 0 else CONV_ROWS + 8
            a = None
            for k, m in taps:
                t = w_ref[k:k + 1, :] * pad_ref[pl.ds(pl.multiple_of(r0 + 8 * m, 8), n), :]
                a = t if a is None else a + t
            if p:
                ph_ref[0:n, :] = a
                a = ph_ref[p:p + CONV_ROWS, :]
            acc = a if acc is None else acc + a
        out_ref[pl.ds(r0, CONV_ROWS), :] = acc
        return carry

    lax.fori_loop(0, tb // CONV_ROWS, chunk, 0)


def _conv_fwd(cvg, conv_w, conv_b, ln_g, ln_b, tb=512):
    T = cvg.shape[0]
    hb = tb // HALO

    def body(cv_ref, cg_ref, cvp_ref, cgp_ref, w_ref, b_ref, g_ref, be_ref, u0_ref, u1_ref, u3_ref, pad_ref, ph_ref,
             sum_ref):
        i = pl.program_id(0)
        u0 = cv_ref[...] * _sigmoid(cg_ref[...])
        prev = cvp_ref[...] * _sigmoid(cgp_ref[...])
        pad_ref[0:HALO, :] = jnp.where(i > 0, prev, 0.0)
        pad_ref[HALO:HALO + tb, :] = u0
        u0_ref[...] = u0
        _shifted_tap_sum(w_ref, pad_ref, ph_ref, sum_ref, [HALO - (CK - 1) + kk for kk in range(CK)], tb)
        acc = sum_ref[...] + b_ref[...]
        u1_ref[...] = acc
        mu = jnp.mean(acc, axis=-1, keepdims=True)
        xc = acc - mu
        var = jnp.mean(xc * xc, axis=-1, keepdims=True)
        u2 = (xc * lax.rsqrt(var + LN_EPS)) * g_ref[...] + be_ref[...]
        u3_ref[...] = (u2 * _sigmoid(u2)).astype(BF16)

    cur = lambda col: pl.BlockSpec((tb, CW), lambda i: (i, col))
    prv = lambda col: pl.BlockSpec((HALO, CW), lambda i: (jnp.maximum(i * hb - 1, 0), col))
    vec = pl.BlockSpec((1, CW), lambda i: (0, 0))
    out = pl.BlockSpec((tb, CW), lambda i: (i, 0))
    return _pcall(body, "conv_fwd", grid=(T // tb,),
                  in_specs=[cur(0), cur(1), prv(0), prv(1), pl.BlockSpec((HALO, CW), lambda i: (0, 0)), vec, vec, vec],
                  out_specs=[out, out, out],
                  out_shape=[jax.ShapeDtypeStruct((T, CW), F32), jax.ShapeDtypeStruct((T, CW), F32),
                             jax.ShapeDtypeStruct((T, CW), BF16)],
                  scratch_shapes=[pltpu.VMEM((tb + HALO, CW), F32), pltpu.VMEM((CONV_ROWS + 8, CW), F32),
                                  pltpu.VMEM((tb, CW), F32)],
                  compiler_params=_cp(("parallel",)))(cvg, cvg, cvg, cvg, conv_w, conv_b, ln_g, ln_b)


def _conv_bwd1(dcat, u1, u0, ln_g, ln_b, tb=512):
    T = u1.shape[0]
    hb = tb // HALO

    def body(d3_ref, u1_ref, u0_ref, u0p_ref, g_ref, be_ref, du1_ref, dw_ref, db_ref, dlg_ref, dlb_ref, pad_ref, d_ref,
             q_ref):
        i = pl.program_id(0)

        @pl.when(i == 0)
        def _():
            dw_ref[...] = jnp.zeros_like(dw_ref)
            db_ref[...] = jnp.zeros_like(db_ref)
            dlg_ref[...] = jnp.zeros_like(dlg_ref)
            dlb_ref[...] = jnp.zeros_like(dlb_ref)

        u1 = u1_ref[...]
        mu = jnp.mean(u1, axis=-1, keepdims=True)
        xc = u1 - mu
        rstd = lax.rsqrt(jnp.mean(xc * xc, axis=-1, keepdims=True) + LN_EPS)
        xh = xc * rstd
        u2 = xh * g_ref[...] + be_ref[...]
        s = _sigmoid(u2)
        du2 = d3_ref[...] * (s + u2 * s * (1.0 - s))
        dlg_ref[...] += jnp.sum(du2 * xh, axis=0, keepdims=True)
        dlb_ref[...] += jnp.sum(du2, axis=0, keepdims=True)
        dxh = du2 * g_ref[...]
        du1 = rstd * (dxh - jnp.mean(dxh, axis=-1, keepdims=True) - xh * jnp.mean(dxh * xh, axis=-1, keepdims=True))
        du1_ref[...] = du1
        db_ref[...] += jnp.sum(du1, axis=0, keepdims=True)
        pad_ref[0:HALO, :] = jnp.where(i > 0, u0p_ref[...], 0.0)
        pad_ref[HALO:HALO + tb, :] = u0_ref[...]
        d_ref[0:8, :] = jnp.zeros((8, CW), F32)
        d_ref[8:8 + tb, :] = du1
        d_ref[8 + tb:16 + tb, :] = jnp.zeros((8, CW), F32)
        for p, taps in _taps_by_phase([HALO - (CK - 1) + kk for kk in range(CK)]):
            n = tb + 8
            q_ref[...] = d_ref[8 - p:8 - p + n, :]
            for k, m in taps:
                if 8 * m + n <= tb + HALO:
                    dw_ref[k:k + 1, :] += jnp.sum(q_ref[...] * pad_ref[8 * m:8 * m + n, :], axis=0, keepdims=True)
                else:
                    dw_ref[k:k + 1, :] += jnp.sum(q_ref[0:tb, :] * pad_ref[8 * m:8 * m + tb, :], axis=0, keepdims=True)

    cur = pl.BlockSpec((tb, CW), lambda i: (i, 0))
    vec = pl.BlockSpec((1, CW), lambda i: (0, 0))
    return _pcall(body, "conv_bwd1", grid=(T // tb,),
                  in_specs=[pl.BlockSpec((tb, CW), lambda i: (i, 1)), cur, cur,
                            pl.BlockSpec((HALO, CW), lambda i: (jnp.maximum(i * hb - 1, 0), 0)), vec, vec],
                  out_specs=[cur, pl.BlockSpec((HALO, CW), lambda i: (0, 0)), vec, vec, vec],
                  out_shape=[jax.ShapeDtypeStruct((T, CW), F32), jax.ShapeDtypeStruct((HALO, CW), F32)]
                  + [jax.ShapeDtypeStruct((1, CW), F32)] * 3,
                  scratch_shapes=[pltpu.VMEM((tb + HALO, CW), F32), pltpu.VMEM((tb + 16, CW), F32),
                                  pltpu.VMEM((tb + 8, CW), F32)],
                  compiler_params=_cp(("arbitrary",)))(dcat, u1, u0, u0, ln_g, ln_b)


def _conv_bwd2(du1, cvg, conv_w, tb=512):
    T = du1.shape[0]
    hb = tb // HALO
    last = T // HALO - 1
    nblk = T // tb

    def body(d_ref, dn_ref, cv_ref, cg_ref, w_ref, o_ref, pad_ref, ph_ref, sum_ref):
        i = pl.program_id(0)
        pad_ref[0:tb, :] = d_ref[...]
        pad_ref[tb:tb + HALO, :] = jnp.where(i < nblk - 1, dn_ref[...], 0.0)
        _shifted_tap_sum(w_ref, pad_ref, ph_ref, sum_ref, [CK - 1 - kk for kk in range(CK)], tb)
        acc = sum_ref[...]
        sg = _sigmoid(cg_ref[...])
        o_ref[:, 0:CW] = (acc * sg).astype(BF16)
        o_ref[:, CW:2 * CW] = (acc * cv_ref[...] * sg * (1.0 - sg)).astype(BF16)

    cur = pl.BlockSpec((tb, CW), lambda i: (i, 0))
    return _pcall(body, "conv_bwd2", grid=(nblk,),
                  in_specs=[cur, pl.BlockSpec((HALO, CW), lambda i: (jnp.minimum((i + 1) * hb, last), 0)),
                            pl.BlockSpec((tb, CW), lambda i: (i, 0)), pl.BlockSpec((tb, CW), lambda i: (i, 1)),
                            pl.BlockSpec((HALO, CW), lambda i: (0, 0))],
                  out_specs=pl.BlockSpec((tb, 2 * CW), lambda i: (i, 0)),
                  out_shape=jax.ShapeDtypeStruct((T, 2 * CW), BF16),
                  scratch_shapes=[pltpu.VMEM((tb + HALO, CW), F32), pltpu.VMEM((CONV_ROWS + 8, CW), F32),
                                  pltpu.VMEM((tb, CW), F32)],
                  compiler_params=_cp(("parallel",)))(du1, du1, cvg, cvg, conv_w)


def _adam_math(w, g, m, v):
    nm = ADAM_B1 * m + (1.0 - ADAM_B1) * g
    nv = ADAM_B2 * v + (1.0 - ADAM_B2) * (g * g)
    delta = -ADAM_LR * ((nm * ADAM_C1) / (jnp.sqrt(nv * ADAM_C2) + ADAM_EPS) + ADAM_WD * w)
    return delta, nm, nv


def _adamw(w, gslots, m, v, name, tb):
    R, C = w.shape
    S = gslots.shape[0]

    def body(w_ref, gs_ref, m_ref, v_ref, g_ref, d_ref, nm_ref, nv_ref):
        g = gs_ref[0].astype(F32)
        for s in range(1, S):
            g = g + gs_ref[s].astype(F32)
        g_ref[...] = g
        d_ref[...], nm_ref[...], nv_ref[...] = _adam_math(w_ref[...], g, m_ref[...], v_ref[...])

    blk = pl.BlockSpec((tb, C), lambda i: (i, 0))
    return _pcall(body, name, grid=(R // tb,),
                  in_specs=[blk, pl.BlockSpec((S, tb, C), lambda i: (0, i, 0)), blk, blk],
                  out_specs=[blk] * 4, out_shape=[jax.ShapeDtypeStruct((R, C), F32)] * 4,
                  compiler_params=_cp(("parallel",)))(w, gslots, m, v)


def _adamw_small(gall, gattn, gconvw, ws, ms, vs):
    n = len(ws)

    def body(*refs):
        gall_ref, gattn_ref, gconvw_ref = refs[:3]
        w_refs, m_refs, v_refs = refs[3:3 + n], refs[3 + n:3 + 2 * n], refs[3 + 2 * n:3 + 3 * n]
        loss_ref = refs[3 + 3 * n]
        outs = refs[4 + 3 * n:]
        g_refs, d_refs, nm_refs, nv_refs = outs[:n], outs[n:2 * n], outs[2 * n:3 * n], outs[3 * n:]

        def total(ref):
            t = ref[0]
            for dev in range(1, NDEV):
                t = t + ref[dev]
            return t

        tot = total(gall_ref)
        grads = [tot[0:9, :]] + [tot[ROW_GAINS + k:ROW_GAINS + k + 1, :] for k in range(6)]
        grads += [total(gattn_ref), tot[ROW_ATTN_CB:ROW_ATTN_CB + 1, CW:2 * CW], tot[ROW_LN:ROW_LN + 1, 0:CW],
                  tot[ROW_LN:ROW_LN + 1, CW:2 * CW], total(gconvw_ref)]
        loss_ref[...] = tot[ROW_LOSS:ROW_LOSS + 1, 0:1]
        for k in range(n):
            g_refs[k][...] = grads[k]
            d_refs[k][...], nm_refs[k][...], nv_refs[k][...] = _adam_math(w_refs[k][...], grads[k], m_refs[k][...],
                                                                          v_refs[k][...])

    shapes = [jax.ShapeDtypeStruct(w.shape, F32) for w in ws]
    res = _pcall(body, "adamw_small", out_shape=[jax.ShapeDtypeStruct((1, 1), F32)] + shapes * 4,
                 compiler_params=_cp())(gall, gattn, gconvw, *ws, *ms, *vs)
    return res[0], [res[1 + k * n:1 + (k + 1) * n] for k in range(4)]


def _ffn_fwd(x, g_pre, g_post, shift, scale, gate, w_in, w_out4, tag, tm, comm=None, tgt=None):
    h, gu, a, got = _ffn_in(x, g_pre, shift, scale, w_in, "ffn_in_" + tag, comm)
    if w_out4 is None:
        w_out4 = got[0].reshape(NSL, SL, D)
        got = [w_out4] + list(got[1:])
    res = _mm_post([a], [pl.BlockSpec((NSL, tm, SL), lambda i: (0, i, 0))],
                   [w_out4], [pl.BlockSpec((NSL, SL, D), lambda i: (0, 0, 0))],
                   x, g_post, gate, 0.5, "ffn_out_" + tag, tm, tgt)
    return (res[1] if tgt is None else (res[1], res[2])), (x, h, gu, a, res[0]), got


def _ffn_bwd(dout, saved, g_pre, g_post, scale, gate, w_in, w_out4, tag, tmb, tmw, send_in=True, carry=None):
    x, h, gu, a, f = saved
    T = x.shape[0]
    (df, dgate, dg_post, dgu), carried = _ffn_out_bwd(dout, f, g_post, gate, 0.5, w_out4, gu,
                                                      "ffn_out_bwd_" + tag, carry)
    dw_out, _ = _mm_tn(a, pl.BlockSpec((1, tmw, SL), lambda j, i: (j, i, 0)),
                       df, pl.BlockSpec((tmw, D), lambda j, i: (i, 0)),
                       (NSL, SL, D), pl.BlockSpec((1, SL, D), lambda j, i: (j, 0, 0)), (NSL, T // tmw), "dw_out_" + tag)
    dw_in, (r_out,) = _dw_in(h, dgu, "dw_in_" + tag, tmw, comm=("a2a", [dw_out.reshape(NDEV, SL // 2, D)]))
    dw_in = dw_in.reshape(NDEV, SL, D)

    def dh_fn(a_ref, w_ref):
        dh = None
        for p in range(2):
            for j in range(NSL):
                t = _dot(a_ref[j, p], w_ref[NSL * p + j])
                dh = t if dh is None else dh + t
        return dh

    (dx, dshift, dscale, dg_pre), got = _mm_prebwd(
        dgu, pl.BlockSpec((NSL, 2, tmb, SL), lambda i: (0, 0, i, 0)),
        w_in, pl.BlockSpec((NDEV, SL, D), lambda i: (0, 0, 0), pipeline_mode=pl.Buffered(1)),
        dh_fn, x, dout, g_pre, scale, "ffn_in_bwd_" + tag, tmb, comm=("a2a", [dw_in]) if send_in else None)
    return dx, got[0] if send_in else dw_in, r_out, dg_pre, dg_post, (dshift, dscale, dgate), carried


def kernel(x, c, w_ada, b_ada, g_pre_ff1, g_post_ff1, ff1_w_in, ff1_w_out, g_pre_mix, g_post_mix, w_in_mix, g_attn_out, conv_w, conv_b, conv_ln_g, conv_ln_b, w_out_mix, g_pre_ff2, g_post_ff2, ff2_w_in, ff2_w_out, loss_target, m_w_ada, m_b_ada, m_g_pre_ff1, m_g_post_ff1, m_ff1_w_in, m_ff1_w_out, m_g_pre_mix, m_g_post_mix, m_w_in_mix, m_g_attn_out, m_conv_w, m_conv_b, m_conv_ln_g, m_conv_ln_b, m_w_out_mix, m_g_pre_ff2, m_g_post_ff2, m_ff2_w_in, m_ff2_w_out, v_w_ada, v_b_ada, v_g_pre_ff1, v_g_post_ff1, v_ff1_w_in, v_ff1_w_out, v_g_pre_mix, v_g_post_mix, v_w_in_mix, v_g_attn_out, v_conv_w, v_conv_b, v_conv_ln_g, v_conv_ln_b, v_w_out_mix, v_g_pre_ff2, v_g_post_ff2, v_ff2_w_in, v_ff2_w_out):
    me = 4 * lax.axis_index("x") + 2 * lax.axis_index("y") + lax.axis_index("c")
    T = x.shape[1]
    tq = min(256, T)
    tm = 512
    tmb = 512
    tmw = 2048
    x0 = x.reshape(T, D)
    tgt = loss_target.reshape(T, D)
    row = lambda a: a.reshape(1, -1)

    small_in = jnp.concatenate([c.reshape(-1), jnp.pad(conv_w.reshape(-1), (0, 2 * D - CK * 64)),
                                jnp.zeros((5 * D,), F32)]).reshape(8, D)
    small_all, = _all_gather([small_in], "gather_c_convw", True)
    c_all = small_all[:, 0, :]
    conv_w_full = small_all[:, 1:3, :].reshape(NDEV, 2 * D)[:, :CK * 64].reshape(NDEV, CK, 64)
    conv_w_full = conv_w_full.transpose(1, 0, 2).reshape(CK, CW)
    conv_w_pad = jnp.pad(conv_w_full, ((0, HALO - CK), (0, 0)))

    big = [ff1_w_in.T, ff1_w_out, w_in_mix.T, w_out_mix, ff2_w_in.T, ff2_w_out]
    shards = [w.astype(BF16) for w in big]
    w_in1, = _all_gather(shards[0:1], "gather_weights_ff1", False)

    b_cols = lax.dynamic_slice(b_ada, (me * ADA_COLS,), (ADA_COLS,)).reshape(1, ADA_COLS)
    mod_cols = _ada_fwd(c_all, w_ada, b_cols)
    mod_all, = _all_gather([mod_cols], "gather_mod", True)
    mod = lax.dynamic_slice(mod_all, (0, me, 0), (NDEV, 1, ADA_COLS)).reshape(9, D)
    sh = lambda s: mod[3 * s:3 * s + 1]
    sc = lambda s: mod[3 * s + 1:3 * s + 2]
    gt = lambda s: mod[3 * s + 2:3 * s + 3]

    x1, sv1, (w_out1_4, w_inm_s, w_outm_s) = _ffn_fwd(x0, row(g_pre_ff1), row(g_post_ff1), sh(0), sc(0), gt(0), w_in1,
                                                      None, "ff1", tm, comm=("gather", shards[1:4]))
    w_inm = w_inm_s.reshape(MIXIN, D)
    w_outm = w_outm_s.reshape(D, D)
    hm, qkv, cvg = _mix_in(x1, row(g_pre_mix), sh(1), sc(1), w_inm, "mix_in")
    g_attn_row = row(g_attn_out)
    o_att, an, (w_in2, w_out2) = _attn_fwd(qkv, g_attn_row, tq, comm=("gather", shards[4:6]))
    w_out2_4 = w_out2.reshape(NSL, SL, D)
    u0, u1, u3 = _conv_fwd(cvg, conv_w_pad, row(conv_b), row(conv_ln_g), row(conv_ln_b))
    half = lambda k: pl.BlockSpec((AW, D), lambda i: (k, 0))
    act = pl.BlockSpec((tm, AW), lambda i: (i, 0))
    fm, x2 = _mm_post([an, u3], [act, act], [w_outm, w_outm], [half(0), half(1)],
                      x1, row(g_post_mix), gt(1), 1.0, "mix_out", tm)
    (dy, loss_part), sv2, _ = _ffn_fwd(x2, row(g_pre_ff2), row(g_post_ff2), sh(2), sc(2), gt(2), w_in2, w_out2_4,
                                       "ff2", tm, tgt=tgt)

    dx2, dw_in2, r_out2, dgpre2, dgpost2, dmod2, _ = _ffn_bwd(
        dy, sv2, row(g_pre_ff2), row(g_post_ff2), sc(2), gt(2), w_in2, w_out2_4, "ff2", tmb, tmw, send_in=False)

    dfm, dgate1, dgpostm, dcat = _mix_out_bwd(dx2, fm, row(g_post_mix), gt(1), 1.0, w_outm, "mix_out_bwd")
    tok = pl.BlockSpec((tmw, AW), lambda j, i: (i, 0))
    tokd = pl.BlockSpec((tmw, D), lambda j, i: (i, 0))
    whole = pl.BlockSpec((AW, D), lambda j, i: (0, 0))
    dw_outm = jnp.concatenate([_mm_tn(an, tok, dfm, tokd, (AW, D), whole, (1, T // tmw), "dw_out_mix_a")[0],
                               _mm_tn(u3, tok, dfm, tokd, (AW, D), whole, (1, T // tmw), "dw_out_mix_c")[0]], axis=0)
    (dq, dk, dv, dg_attn), (r_in2,) = _attn_bwd(qkv, o_att, dcat, g_attn_row, tq, comm=("a2a", [dw_in2]))
    du1, dconv_w, dconv_b, dln_g, dln_b = _conv_bwd1(dcat, u1, u0, row(conv_ln_g), row(conv_ln_b))
    dcvg = _conv_bwd2(du1, cvg, conv_w_pad)
    dproj = jnp.concatenate([dq, dk.astype(BF16), dv.astype(BF16), dcvg], axis=1)
    dw_inm, _ = _mm_tn(dproj, pl.BlockSpec((tmw, MIXIN // 2), lambda j, i: (i, j)),
                       hm, pl.BlockSpec((tmw, D), lambda j, i: (i, 0)),
                       (MIXIN, D), pl.BlockSpec((MIXIN // 2, D), lambda j, i: (j, 0)), (2, T // tmw), "dw_in_mix")
    (dx1, dshift1, dscale1, dgprem), _ = _mm_prebwd(
        dproj, pl.BlockSpec((tmb, MIXIN), lambda i: (i, 0)), w_inm, pl.BlockSpec((MIXIN, D), lambda i: (0, 0)),
        lambda a_ref, w_ref: _dot(a_ref[...], w_ref[...]), x1, dx2, row(g_pre_mix), sc(1), "mix_in_bwd", tmb)

    dx0, r_in1, r_out1, dgpre1, dgpost1, dmod0, (r_inm, r_outm) = _ffn_bwd(
        dx1, sv1, row(g_pre_ff1), row(g_post_ff1), sc(0), gt(0), w_in1, w_out1_4, "ff1", tmb, tmw,
        carry=("a2a", [dw_inm.reshape(NDEV, 320, D), dw_outm.reshape(NDEV, 128, D)]))
    recvs = [r_in1, r_out1, r_inm, r_outm, r_in2, r_out2]

    zrow = jnp.zeros((1, D), F32)
    small_g = jnp.concatenate(
        list(dmod0) + [dshift1, dscale1, dgate1] + list(dmod2)
        + [dgpre1, dgpost1, dgprem, dgpostm, dgpre2, dgpost2]
        + [jnp.concatenate([dg_attn, dconv_b], axis=1), jnp.concatenate([dln_g, dln_b], axis=1),
           jnp.pad(dconv_w[:CK].reshape(-1), (0, CONVW_ROWS * D - CK * CW)).reshape(CONVW_ROWS, D),
           jnp.pad(loss_part, ((0, 0), (0, D - 1)))] + [zrow] * (SMALL_R - ROW_LOSS - 1), axis=0)
    small_g_all, = _all_gather([small_g], "gather_small_grads", True)

    dmod_all = small_g_all[:, 0:9, :].reshape(NDEV, NMOD)
    dmod_cols = lax.dynamic_slice(dmod_all, (0, me * ADA_COLS), (NDEV, ADA_COLS))
    g_w_ada = _ada_bwd(c_all.T, dmod_cols)

    gattn = small_g_all[:, ROW_ATTN_CB, 0:AW].reshape(NDEV, 8, HD)
    gconvw = small_g_all[:, ROW_CONVW:ROW_CONVW + CONVW_ROWS, :].reshape(NDEV, CONVW_ROWS * D)[:, :CK * CW]
    gconvw = lax.dynamic_slice(gconvw.reshape(NDEV, CK, CW), (0, 0, me * 64), (NDEV, CK, 64))

    def small_list(b, g6, ga, cb, lg, lb, cw):
        return [b.reshape(9, D)] + [row(g) for g in g6] + [ga, row(cb), row(lg), row(lb), cw]

    sw = small_list(b_ada, [g_pre_ff1, g_post_ff1, g_pre_mix, g_post_mix, g_pre_ff2, g_post_ff2], g_attn_out,
                    conv_b, conv_ln_g, conv_ln_b, conv_w)
    sm = small_list(m_b_ada, [m_g_pre_ff1, m_g_post_ff1, m_g_pre_mix, m_g_post_mix, m_g_pre_ff2, m_g_post_ff2],
                    m_g_attn_out, m_conv_b, m_conv_ln_g, m_conv_ln_b, m_conv_w)
    sv = small_list(v_b_ada, [v_g_pre_ff1, v_g_post_ff1, v_g_pre_mix, v_g_post_mix, v_g_pre_ff2, v_g_post_ff2],
                    v_g_attn_out, v_conv_b, v_conv_ln_g, v_conv_ln_b, v_conv_w)
    loss, s_out = _adamw_small(small_g_all, gattn, gconvw, sw, sm, sv)
    s_out = [[o.reshape(w.shape) for o, w in zip(outs, [b_ada, g_pre_ff1, g_post_ff1, g_pre_mix, g_post_mix,
                                                        g_pre_ff2, g_post_ff2, g_attn_out, conv_b, conv_ln_g,
                                                        conv_ln_b, conv_w])] for outs in s_out]

    big_m = [m_ff1_w_in.T, m_ff1_w_out, m_w_in_mix.T, m_w_out_mix, m_ff2_w_in.T, m_ff2_w_out]
    big_v = [v_ff1_w_in.T, v_ff1_w_out, v_w_in_mix.T, v_w_out_mix, v_ff2_w_in.T, v_ff2_w_out]
    tbs = [352, 176, 160, 128, 352, 176]
    tags = ["ff1_w_in", "ff1_w_out", "w_in_mix", "w_out_mix", "ff2_w_in", "ff2_w_out"]
    b_out = [_adamw(big[k], recvs[k], big_m[k], big_v[k], "adamw_" + tags[k], tbs[k]) for k in range(6)]
    b_out = [[o.T for o in outs] if k % 2 == 0 else outs for k, outs in enumerate(b_out)]
    a_out = _adamw(w_ada, g_w_ada.reshape(1, D, ADA_COLS), m_w_ada, v_w_ada, "adamw_ada", 256)

    def leaves(k):
        s = s_out[k]
        b = [o[k] for o in b_out]
        return [a_out[k], s[0], s[1], s[2], b[0], b[1], s[3], s[4], b[2], s[7], s[11], s[8], s[9], s[10], b[3],
                s[5], s[6], b[4], b[5]]

    return (loss.reshape(()), dx0.reshape(1, T, D), *leaves(0), *leaves(1), *leaves(2), *leaves(3))
```
